```python
import jax
import jax.numpy as jnp
from jax import lax
import numpy as np


D_MODEL = 1024
BATCH = 8
SEQ = 4096
DEPTH = 2

CHUNK = 64
Q_BLOCK = 128
D_PLE = 256
D_FF = 4 * D_MODEL
NORM_EPS = 1e-6

MLA_HEADS = 4
MLA_NOPE = 128
MLA_ROPE = 64
MLA_V = 128
MLA_Q_RANK = 256
MLA_KV_RANK = 128
ROPE_THETA = 10000.0

RW_HEADS = 8
RW_HEAD = 64
RW_DIM = RW_HEADS * RW_HEAD
RW_DECAY_LORA = 64
RW_AAA_LORA = 64
RW_GATE_LORA = 128
RW_LN_EPS = 64e-5

CA_HEADS = 8
CA_HEAD = 64
CA_DIM = CA_HEADS * CA_HEAD
CA_LEFT_CHUNKS = 8
CA_BAND = (CA_LEFT_CHUNKS + 1) * CHUNK
REL_MIN = -(CHUNK - 1)
REL_MAX = 256
REL_SIZE = REL_MAX - REL_MIN + 1

N_BRANCH = 3
BRANCH_DIM = 512
MIX_DIM = MLA_HEADS * MLA_V + RW_DIM + CA_DIM

MLA_COLS = MLA_Q_RANK + MLA_KV_RANK + MLA_ROPE
RW_COLS = 3 * RW_DIM + RW_DECAY_LORA + RW_AAA_LORA + RW_GATE_LORA
CA_COLS = 3 * CA_DIM
GATE_COLS = N_BRANCH * D_MODEL
IN_COLS = MLA_COLS + RW_COLS + CA_COLS + GATE_COLS
IN_SPLITS = (MLA_COLS, MLA_COLS + RW_COLS, MLA_COLS + RW_COLS + CA_COLS)
RW_SPLITS = (RW_DIM, 2 * RW_DIM, 3 * RW_DIM, 3 * RW_DIM + RW_DECAY_LORA,
             3 * RW_DIM + RW_DECAY_LORA + RW_AAA_LORA)

kernel_name = 'hybrid_mla_rwkv7_chunkattn_stream_block'

F32 = jnp.float32


def rms_norm(x, g):
    xf = x.astype(F32)
    y = xf * lax.rsqrt(jnp.mean(xf * xf, axis=-1, keepdims=True) + NORM_EPS)
    return (y * g.astype(F32)).astype(x.dtype)


def rope(x, positions):
    half = x.shape[-1] // 2
    inv_freq = 1.0 / (ROPE_THETA ** (jnp.arange(half, dtype=F32) / half))
    ang = positions.astype(F32)[..., None] * inv_freq
    ang = ang.reshape(ang.shape[:2] + (1,) * (x.ndim - 3) + (half,))
    cos, sin = jnp.cos(ang), jnp.sin(ang)
    x1 = x[..., :half].astype(F32)
    x2 = x[..., half:].astype(F32)
    return jnp.concatenate([x1 * cos - x2 * sin, x2 * cos + x1 * sin], axis=-1).astype(x.dtype)


def chunk_causal_attention(q, k, v):
    b, s, h, dk = q.shape
    n_blk = s // Q_BLOCK
    scale = dk ** -0.5
    key_chunk = jnp.arange(s) // CHUNK
    q_blocks = q.reshape(b, n_blk, Q_BLOCK, h, dk).transpose(1, 0, 2, 3, 4)

    def one_block(args):
        qb, blk = args
        sc = jnp.einsum('bqhd,bkhd->bhqk', qb, k, preferred_element_type=F32) * scale
        q_chunk = (blk * Q_BLOCK + jnp.arange(Q_BLOCK)) // CHUNK
        sc = jnp.where(key_chunk[None, :] <= q_chunk[:, None], sc, -jnp.inf)
        pr = jax.nn.softmax(sc, axis=-1).astype(v.dtype)
        return jnp.einsum('bhqk,bkhd->bqhd', pr, v)

    out = lax.map(one_block, (q_blocks, jnp.arange(n_blk)))
    return out.transpose(1, 0, 2, 3, 4).reshape(b, s, h, v.shape[-1])


def mla_branch(z, positions, q_norm_g, kv_norm_g, w_uq, w_ukv):
    b, s, _ = z.shape
    z_q, z_kv, z_kr = jnp.split(z, [MLA_Q_RANK, MLA_Q_RANK + MLA_KV_RANK], axis=-1)
    q = (rms_norm(z_q, q_norm_g) @ w_uq).reshape(b, s, MLA_HEADS, MLA_NOPE + MLA_ROPE)
    q = jnp.concatenate([q[..., :MLA_NOPE], rope(q[..., MLA_NOPE:], positions)], axis=-1)
    kv = (rms_norm(z_kv, kv_norm_g) @ w_ukv).reshape(b, s, MLA_HEADS, MLA_NOPE + MLA_V)
    k_rope = jnp.broadcast_to(rope(z_kr, positions)[:, :, None, :], (b, s, MLA_HEADS, MLA_ROPE))
    k = jnp.concatenate([kv[..., :MLA_NOPE], k_rope], axis=-1)
    o = chunk_causal_attention(q, k, kv[..., MLA_NOPE:])
    return o.reshape(b, s, MLA_HEADS * MLA_V)


def token_shift(z, mu):
    prev = jnp.pad(z, ((0, 0), (1, 0), (0, 0)))[:, :-1]
    return z + (prev - z) * mu


def wkv7_scan(r, w, k, v, kk, a):
    b, s, h, n = r.shape

    def step(state, inp):
        r_t, w_t, k_t, v_t, kk_t, a_t = inp
        sa = jnp.einsum('bhvk,bhk->bhv', state, -kk_t)
        state = (state * w_t[:, :, None, :] + sa[..., None] * (kk_t * a_t)[:, :, None, :]
                 + v_t[..., None] * k_t[:, :, None, :])
        return state, jnp.einsum('bhvk,bhk->bhv', state, r_t)

    xs = tuple(t.astype(F32).transpose(1, 0, 2, 3) for t in (r, w, k, v, kk, a))
    _, ys = lax.scan(step, jnp.zeros((b, h, n, n), F32), xs)
    return ys.transpose(1, 0, 2, 3)


def rwkv7_branch(z, mu, w0, w_up, a0, a_up, g_up, k_k, k_a, r_k, ln_w, ln_b):
    b, s, _ = z.shape
    r, k, v, xw, xa, xg = jnp.split(token_shift(z, mu), RW_SPLITS, axis=-1)
    w_log = -jax.nn.softplus(-(w0 + jnp.tanh(xw) @ w_up).astype(F32)) - 0.5
    decay = jnp.exp(-jnp.exp(w_log))
    a = jax.nn.sigmoid(a0 + xa @ a_up)
    g = jax.nn.sigmoid(xg) @ g_up
    heads = lambda t: t.reshape(b, s, RW_HEADS, RW_HEAD)
    kk = heads(k * k_k).astype(F32)
    kk = kk * lax.rsqrt(jnp.maximum(jnp.sum(kk * kk, axis=-1, keepdims=True), 1e-24))
    k = k * (1.0 + (a - 1.0) * k_a)
    y = wkv7_scan(heads(r), heads(decay), heads(k), heads(v), kk, heads(a))
    mean = jnp.mean(y, axis=-1, keepdims=True)
    var = jnp.mean(jnp.square(y - mean), axis=-1, keepdims=True)
    yn = ((y - mean) * lax.rsqrt(var + RW_LN_EPS)).reshape(b, s, RW_DIM)
    yn = yn * ln_w.astype(F32) + ln_b.astype(F32)
    bonus = jnp.sum(heads(r * k).astype(F32) * r_k.astype(F32), axis=-1, keepdims=True) * heads(v).astype(F32)
    out = (yn + bonus.reshape(b, s, RW_DIM)) * g.astype(F32)
    return out.astype(z.dtype)


def chunk_band_attention(z, rel_bias):
    b, s, _ = z.shape
    q, k, v = [t.reshape(b, s, CA_HEADS, CA_HEAD) for t in jnp.split(z, [CA_DIM, 2 * CA_DIM], axis=-1)]
    n_chunks = s // CHUNK
    pad = CA_LEFT_CHUNKS * CHUNK
    kp = jnp.pad(k, ((0, 0), (pad, 0), (0, 0), (0, 0)))
    vp = jnp.pad(v, ((0, 0), (pad, 0), (0, 0), (0, 0)))
    band = jnp.arange(CA_BAND)
    dist = pad + jnp.arange(CHUNK)[:, None] - band[None, :]
    rel_idx = jnp.clip(dist, REL_MIN, REL_MAX) - REL_MIN
    bias = rel_bias.astype(F32)[rel_idx].transpose(2, 0, 1)
    q_chunks = q.reshape(b, n_chunks, CHUNK, CA_HEADS, CA_HEAD).transpose(1, 0, 2, 3, 4)
    scale = CA_HEAD ** -0.5

    def one_chunk(args):
        qc, c = args
        kb = lax.dynamic_slice_in_dim(kp, c * CHUNK, CA_BAND, axis=1)
        vb = lax.dynamic_slice_in_dim(vp, c * CHUNK, CA_BAND, axis=1)
        sc = jnp.einsum('bqhd,bkhd->bhqk', qc, kb, preferred_element_type=F32) * scale + bias
        valid = band >= (CA_LEFT_CHUNKS - c) * CHUNK
        sc = jnp.where(valid, sc, -jnp.inf)
        pr = jax.nn.softmax(sc, axis=-1).astype(vb.dtype)
        return jnp.einsum('bhqk,bkhd->bqhd', pr, vb)

    out = lax.map(one_chunk, (q_chunks, jnp.arange(n_chunks)))
    return out.transpose(1, 0, 2, 3, 4).reshape(b, s, CA_DIM)


def setup_inputs(seed: int = 0) -> dict:
    key = jax.random.key(seed)
    ks = iter(jax.random.split(key, 40))

    def nrm(shape, scale):
        return scale * jax.random.normal(next(ks), shape, F32)

    def gain(shape):
        return 1.0 + nrm(shape, 0.05)

    L = DEPTH
    x = nrm((BATCH, SEQ, D_MODEL), 1.0)
    p = nrm((L, BATCH, SEQ, D_PLE), 1.0)
    offsets = jax.random.randint(next(ks), (BATCH,), 0, 1024, dtype=jnp.int32) * CHUNK
    positions = offsets[:, None] + jnp.arange(SEQ, dtype=jnp.int32)[None, :]
    return {
        'x': x,
        'p': p,
        'positions': positions,
        'pre_mix_g': gain((L, D_MODEL)),
        'w_in': nrm((L, D_MODEL, IN_COLS), D_MODEL ** -0.5),
        'mla_q_norm_g': gain((L, MLA_Q_RANK)),
        'mla_kv_norm_g': gain((L, MLA_KV_RANK)),
        'mla_w_uq': nrm((L, MLA_Q_RANK, MLA_HEADS * (MLA_NOPE + MLA_ROPE)), MLA_Q_RANK ** -0.5),
        'mla_w_ukv': nrm((L, MLA_KV_RANK, MLA_HEADS * (MLA_NOPE + MLA_V)), MLA_KV_RANK ** -0.5),
        'rw_mu': jax.random.uniform(next(ks), (L, RW_COLS), F32),
        'rw_w0': jax.random.uniform(next(ks), (L, RW_DIM), F32, -4.0, 0.0),
        'rw_w_up': nrm((L, RW_DECAY_LORA, RW_DIM), 0.1),
        'rw_a0': nrm((L, RW_DIM), 0.5),
        'rw_a_up': nrm((L, RW_AAA_LORA, RW_DIM), 0.1),
        'rw_g_up': nrm((L, RW_GATE_LORA, RW_DIM), RW_GATE_LORA ** -0.5),
        'rw_k_k': 0.85 + nrm((L, RW_DIM), 0.05),
        'rw_k_a': 1.0 + nrm((L, RW_DIM), 0.05),
        'rw_r_k': nrm((L, RW_HEADS, RW_HEAD), 0.1),
        'rw_ln_w': gain((L, RW_DIM)),
        'rw_ln_b': nrm((L, RW_DIM), 0.02),
        'ca_rel_bias': nrm((L, REL_SIZE, CA_HEADS), 0.5),
        'w_branch': nrm((L, MIX_DIM, D_MODEL), BRANCH_DIM ** -0.5),
        'w_out': nrm((L, D_MODEL, D_MODEL), D_MODEL ** -0.5),
        'post_mix_g': gain((L, D_MODEL)),
        'pre_ff_g': gain((L, D_MODEL)),
        'w_ff1': nrm((L, D_MODEL, D_FF), D_MODEL ** -0.5),
        'w_ff2': nrm((L, D_FF, D_MODEL), D_FF ** -0.5),
        'post_ff_g': gain((L, D_MODEL)),
        'w_ple_gate': nrm((L, D_MODEL, D_MODEL), D_MODEL ** -0.5),
        'w_ple_proj': nrm((L, D_PLE, D_MODEL), D_PLE ** -0.5),
    }


def reference(x, p, positions, pre_mix_g, w_in, mla_q_norm_g, mla_kv_norm_g, mla_w_uq, mla_w_ukv,
              rw_mu, rw_w0, rw_w_up, rw_a0, rw_a_up, rw_g_up, rw_k_k, rw_k_a, rw_r_k, rw_ln_w, rw_ln_b,
              ca_rel_bias, w_branch, w_out, post_mix_g, pre_ff_g, w_ff1, w_ff2, post_ff_g,
              w_ple_gate, w_ple_proj):
    b, s, _ = x.shape
    h = x
    for i in range(DEPTH):
        u = rms_norm(h, pre_mix_g[i])
        z = u @ w_in[i]
        z_mla, z_rw, z_ca, z_gate = jnp.split(z, IN_SPLITS, axis=-1)
        o_mla = mla_branch(z_mla, positions, mla_q_norm_g[i], mla_kv_norm_g[i], mla_w_uq[i], mla_w_ukv[i])
        o_rw = rwkv7_branch(z_rw, rw_mu[i], rw_w0[i], rw_w_up[i], rw_a0[i], rw_a_up[i], rw_g_up[i],
                            rw_k_k[i], rw_k_a[i], rw_r_k[i], rw_ln_w[i], rw_ln_b[i])
        o_ca = chunk_band_attention(z_ca, ca_rel_bias[i])
        branches = jnp.stack([o_mla, o_rw, o_ca], axis=2)
        y = jnp.einsum('bsnc,ncd->bsnd', branches, w_branch[i].reshape(N_BRANCH, BRANCH_DIM, D_MODEL))
        gates = jax.nn.sigmoid(z_gate).reshape(b, s, N_BRANCH, D_MODEL)
        merged = jnp.sum(gates * y, axis=2)
        h = h + rms_norm(merged @ w_out[i], post_mix_g[i])
        f = rms_norm(h, pre_ff_g[i])
        f = jnp.square(jax.nn.relu(f @ w_ff1[i])) @ w_ff2[i]
        h = h + rms_norm(f, post_ff_g[i])
        h = h + jax.nn.sigmoid(h @ w_ple_gate[i]) * (p[i] @ w_ple_proj[i])
    return h
```

```python
import functools
import math

import jax
import jax.numpy as jnp
from jax import lax
from jax.experimental import pallas as pl
from jax.experimental.pallas import tpu as pltpu

F32 = jnp.float32
BF16 = jnp.bfloat16

D_MODEL = 1024
D_PLE = 256
D_FF = 4 * D_MODEL
NORM_EPS = 1e-6
CHUNK = 64

MLA_HEADS = 4
MLA_NOPE = 128
MLA_ROPE = 64
MLA_V = 128
MLA_Q_RANK = 256
MLA_KV_RANK = 128
ROPE_THETA = 10000.0
MLA_QK = MLA_NOPE + MLA_ROPE
MLA_HEAD_PAD = 256

RW_HEADS = 8
RW_HEAD = 64
RW_DIM = RW_HEADS * RW_HEAD
RW_DECAY_LORA = 64
RW_AAA_LORA = 64
RW_GATE_LORA = 128
RW_LORA = RW_DECAY_LORA + RW_AAA_LORA + RW_GATE_LORA
RW_LN_EPS = 64e-5

CA_HEADS = 8
CA_HEAD = 64
CA_DIM = CA_HEADS * CA_HEAD
CA_LEFT_CHUNKS = 8
CA_PAD = CA_LEFT_CHUNKS * CHUNK
REL_MIN = -(CHUNK - 1)
REL_MAX = 256

N_BRANCH = 3
BRANCH_DIM = 512
GATE_COLS = N_BRANCH * D_MODEL

MLA_SEC = 640
OFF_GATE = 0
OFF_CA = GATE_COLS
OFF_RW = OFF_CA + 3 * CA_DIM
OFF_LORA = OFF_RW + 3 * RW_DIM
OFF_MLA = OFF_LORA + RW_LORA
IN_PACKED = OFF_MLA + MLA_SEC

VMEM_LIMIT = 56 * 1024 * 1024


def _cparams(*sem):
    return pltpu.CompilerParams(dimension_semantics=sem, vmem_limit_bytes=VMEM_LIMIT)


def _rms(x, g):
    return x * lax.rsqrt(jnp.mean(x * x, axis=-1, keepdims=True) + NORM_EPS) * g


def _dot(a, b):
    return jnp.dot(a, b, preferred_element_type=F32)


def _dot_nt(a, b):
    return lax.dot_general(a, b, (((1,), (1,)), ((), ())), preferred_element_type=F32)


def _dot_tn(a, b):
    return lax.dot_general(a, b, (((0,), (0,)), ((), ())), preferred_element_type=F32)


def _const_spec(shape):
    nd = len(shape)
    return pl.BlockSpec(shape, lambda *_: (0,) * nd, pipeline_mode=pl.Buffered(1))


def _rope_table_kernel(pos_ref, freq_ref, cs_ref, sn_ref):
    ang = pos_ref[...] * freq_ref[...]
    live = lax.broadcasted_iota(jnp.int32, ang.shape, 1) < MLA_ROPE
    cs_ref[...] = jnp.where(live, jnp.cos(ang), 0.0)
    sn_ref[...] = jnp.where(live, jnp.sin(ang), 0.0)


def _rope_table(positions):
    t = positions.size
    tm = min(t, 2048)
    half = MLA_ROPE // 2
    inv_freq = 1.0 / (ROPE_THETA ** (jnp.arange(half, dtype=F32) / half))
    freq = jnp.concatenate([inv_freq, inv_freq, jnp.zeros((128 - MLA_ROPE,), F32)])[None, :]
    pos = positions.astype(F32).reshape(t, 1)
    return pl.pallas_call(
        _rope_table_kernel,
        grid=(t // tm,),
        in_specs=[pl.BlockSpec((tm, 1), lambda i: (i, 0)), _const_spec((1, 128))],
        out_specs=[pl.BlockSpec((tm, 128), lambda i: (i, 0))] * 2,
        out_shape=[jax.ShapeDtypeStruct((t, 128), F32)] * 2,
        compiler_params=_cparams("parallel"),
        name="rope_table",
    )(pos, freq)


def _in_proj_kernel(x_ref, g_ref, w_ref, o_ref, xn_ref):
    @pl.when(pl.program_id(1) == 0)
    def _():
        xn_ref[...] = _rms(x_ref[...], g_ref[...]).astype(BF16)

    o_ref[...] = _dot(xn_ref[...], w_ref[...]).astype(o_ref.dtype)


def _in_proj(h, g, w):
    t, d = h.shape
    n = w.shape[1]
    tm = min(t, 1024)
    tn = n // 5
    return pl.pallas_call(
        _in_proj_kernel,
        grid=(t // tm, n // tn),
        in_specs=[
            pl.BlockSpec((tm, d), lambda i, j: (i, 0)),
            _const_spec((1, d)),
            pl.BlockSpec((d, tn), lambda i, j: (0, j)),
        ],
        out_specs=pl.BlockSpec((tm, tn), lambda i, j: (i, j)),
        out_shape=jax.ShapeDtypeStruct((t, n), BF16),
        scratch_shapes=[pltpu.VMEM((tm, d), BF16)],
        compiler_params=_cparams("parallel", "arbitrary"),
        name="in_proj",
    )(h, g, w)


def _mla_proj_kernel(z_ref, cs_ref, sn_ref, gq_ref, gkv_ref, wq_ref, wqr_ref, wk_ref, wv_ref,
                     q_ref, k_ref, kr_ref, v_ref):
    z = z_ref[...].astype(F32)
    cs = cs_ref[...]
    sn = sn_ref[...]
    qn = _rms(z[:, :MLA_Q_RANK], gq_ref[...]).astype(BF16)
    q = _dot(qn, wq_ref[...])
    qr = _dot(qn, wqr_ref[...])
    scale = MLA_QK ** -0.5
    for h in range(MLA_HEADS):
        o = h * MLA_HEAD_PAD
        q_ref[:, o:o + 128] = (q[:, o:o + 128] * scale).astype(BF16)
        rope = q[:, o + 128:o + 256] * cs + qr[:, h * 128:(h + 1) * 128] * sn
        q_ref[:, o + 128:o + 256] = (rope * scale).astype(BF16)
    kvn = _rms(z[:, MLA_Q_RANK:MLA_Q_RANK + MLA_KV_RANK], gkv_ref[...]).astype(BF16)
    k_ref[...] = _dot(kvn, wk_ref[...]).astype(BF16)
    v_ref[...] = _dot(kvn, wv_ref[...]).astype(BF16)
    kr_ref[...] = (z[:, 384:512] * cs + z[:, 512:640] * sn).astype(BF16)


def _mla_proj(z, cs, sn, gq, gkv, wq, wqr, wk, wv):
    t = z.shape[0]
    tm = min(t, 1024)
    hv = MLA_HEADS * MLA_V
    row = lambda w: pl.BlockSpec((tm, w), lambda i: (i, 0))
    return pl.pallas_call(
        _mla_proj_kernel,
        grid=(t // tm,),
        in_specs=[
            pl.BlockSpec((tm, MLA_SEC), lambda i: (i, OFF_MLA // MLA_SEC)),
            row(128), row(128),
            _const_spec(gq.shape), _const_spec(gkv.shape), _const_spec(wq.shape),
            _const_spec(wqr.shape), _const_spec(wk.shape), _const_spec(wv.shape),
        ],
        out_specs=[row(MLA_HEADS * MLA_HEAD_PAD), row(hv), row(128), row(hv)],
        out_shape=[
            jax.ShapeDtypeStruct((t, MLA_HEADS * MLA_HEAD_PAD), BF16),
            jax.ShapeDtypeStruct((t, hv), BF16),
            jax.ShapeDtypeStruct((t, 128), BF16),
            jax.ShapeDtypeStruct((t, hv), BF16),
        ],
        compiler_params=_cparams("parallel"),
        name="mla_proj",
    )(z, cs, sn, gq, gkv, wq, wqr, wk, wv)


MLA_TQ = 256


def _mla_attn_kernel(q_ref, k_ref, kr_ref, v_ref, o_ref, m_ref, l_ref, acc_ref):
    i = pl.program_id(1)
    tq = MLA_TQ
    m_ref[...] = jnp.full(m_ref.shape, -1e30, F32)
    l_ref[...] = jnp.zeros(l_ref.shape, F32)
    acc_ref[...] = jnp.zeros(acc_ref.shape, F32)

    def step(j, masked):
        start = pl.multiple_of(j * tq, tq)
        kr = kr_ref[0, pl.ds(start, tq), :]
        for h in range(MLA_HEADS):
            qh = q_ref[0, :, h * MLA_HEAD_PAD:(h + 1) * MLA_HEAD_PAD]
            kh = jnp.concatenate([k_ref[0, pl.ds(start, tq), h * 128:(h + 1) * 128], kr], axis=1)
            s = _dot_nt(qh, kh)
            if masked:
                rc = lax.broadcasted_iota(jnp.int32, s.shape, 0) // CHUNK
                cc = lax.broadcasted_iota(jnp.int32, s.shape, 1) // CHUNK
                s = jnp.where(cc <= rc, s, -1e30)
            m_prev = m_ref[h]
            m_new = jnp.maximum(m_prev, jnp.max(s, axis=-1, keepdims=True))
            a = jnp.exp(m_prev - m_new)
            p = jnp.exp(s - m_new[:, :1])
            l_ref[h] = a * l_ref[h] + jnp.sum(p, axis=-1, keepdims=True)
            vh = v_ref[0, pl.ds(start, tq), h * MLA_V:(h + 1) * MLA_V]
            acc_ref[h] = a * acc_ref[h] + _dot(p.astype(BF16), vh)
            m_ref[h] = m_new

    def body(j, c):
        step(j, False)
        return c

    lax.fori_loop(0, i, body, 0)
    step(i, True)
    for h in range(MLA_HEADS):
        o_ref[0, :, h * MLA_V:(h + 1) * MLA_V] = (acc_ref[h] / l_ref[h]).astype(o_ref.dtype)


def _mla_attn(q, k, kr, v):
    b, s, _ = q.shape
    tq = MLA_TQ
    hv = MLA_HEADS * MLA_V
    full = lambda w: pl.BlockSpec((1, s, w), lambda bi, i: (bi, 0, 0))
    return pl.pallas_call(
        _mla_attn_kernel,
        grid=(b, s // tq),
        in_specs=[
            pl.BlockSpec((1, tq, MLA_HEADS * MLA_HEAD_PAD), lambda bi, i: (bi, i, 0)),
            full(hv), full(128), full(hv),
        ],
        out_specs=pl.BlockSpec((1, tq, hv), lambda bi, i: (bi, i, 0)),
        out_shape=jax.ShapeDtypeStruct((b, s, hv), BF16),
        scratch_shapes=[pltpu.VMEM((MLA_HEADS, tq, 128), F32)] * 3,
        compiler_params=_cparams("parallel", "arbitrary"),
        name="mla_attn",
    )(q, k, kr, v)


CA_TQ = 256
CA_BAND_BLK = CA_TQ + CA_PAD


def _ca_attn_kernel(q_ref, k_ref, v_ref, bias_ref, o_ref):
    i = pl.program_id(1)
    start = pl.multiple_of(i * CA_TQ, CA_TQ)
    q = q_ref[0]
    col = lax.broadcasted_iota(jnp.int32, (CA_TQ, CA_BAND_BLK), 1)
    in_seq = col >= CA_PAD - start
    lane = lax.broadcasted_iota(jnp.int32, (CA_TQ, 128), 1)
    lo = lane < CA_HEAD
    zero = jnp.zeros((), BF16)
    for pair in range(CA_HEADS // 2):
        sl = slice(pair * 128, (pair + 1) * 128)
        q2 = q[:, sl]
        k2 = k_ref[0, pl.ds(start, CA_BAND_BLK), sl]
        v2 = v_ref[0, pl.ds(start, CA_BAND_BLK), sl]
        outs = []
        for sub in range(2):
            qh = jnp.where(lo if sub == 0 else ~lo, q2, zero)
            s = _dot_nt(qh, k2) * (CA_HEAD ** -0.5) + bias_ref[2 * pair + sub]
            s = jnp.where(in_seq, s, -1e30)
            m = jnp.max(s, axis=-1, keepdims=True)
            p = jnp.exp(s - m)
            l = jnp.sum(p, axis=-1, keepdims=True)
            outs.append(_dot(p.astype(BF16), v2) / l)
        o_ref[0, :, sl] = jnp.where(lo, outs[0], outs[1]).astype(o_ref.dtype)


def _ca_attn(z3, kp, vp, bias):
    b, s, _ = z3.shape
    sp = kp.shape[1]
    full = pl.BlockSpec((1, sp, CA_DIM), lambda bi, i: (bi, 0, 0))
    return pl.pallas_call(
        _ca_attn_kernel,
        grid=(b, s // CA_TQ),
        in_specs=[
            pl.BlockSpec((1, CA_TQ, CA_DIM), lambda bi, i: (bi, i, OFF_CA // CA_DIM)),
            full, full,
            _const_spec(bias.shape),
        ],
        out_specs=pl.BlockSpec((1, CA_TQ, CA_DIM), lambda bi, i: (bi, i, 0)),
        out_shape=jax.ShapeDtypeStruct((b, s, CA_DIM), BF16),
        compiler_params=_cparams("parallel", "arbitrary"),
        name="ca_attn",
    )(z3, kp, vp, bias)


def _ca_bias_table(rel_bias):
    row = jnp.arange(CA_TQ)[:, None]
    col = jnp.arange(CA_BAND_BLK)[None, :]
    dist = CA_PAD + row - col
    idx = jnp.clip(dist, REL_MIN, REL_MAX) - REL_MIN
    bias = rel_bias.astype(F32)[idx].transpose(2, 0, 1)
    rc = row // CHUNK
    cc = col // CHUNK
    ok = (cc >= rc) & (cc <= rc + CA_LEFT_CHUNKS)
    return jnp.where(ok[None], bias, -1e30)


def _split_bf16(x):
    hi = x.astype(BF16)
    lo = (x - hi.astype(F32)).astype(BF16)
    return hi, lo


def _head_sum(x, ones_bd):
    hi, lo = _split_bf16(x)
    return _dot(hi, ones_bd) + _dot(lo, ones_bd)


def _rw_prep_kernel(seq_len, zr_ref, zk_ref, zv_ref, zl_ref, pr_ref, pk_ref, pv_ref, plr_ref,
                    mu_r, mu_k, mu_v, mu_l, w0, w_up, a0, a_up, g_up, k_k, k_a, ones_bd,
                    r_o, k_o, v_o, al_o, be_o, lw_o, g_o):
    tm = zr_ref.shape[0]
    first = (pl.program_id(0) * tm) % seq_len == 0
    row0 = lax.broadcasted_iota(jnp.int32, (tm, 1), 0) == 0

    def shift(z_ref, p_ref, mu):
        z = z_ref[...].astype(F32)
        last = jnp.where(first, 0.0, p_ref[...].astype(F32)[15:16, :])
        prev = jnp.where(row0, last, pltpu.roll(z, 1, axis=0))
        return z + (prev - z) * mu[...]

    r = shift(zr_ref, pr_ref, mu_r)
    k = shift(zk_ref, pk_ref, mu_k)
    v = shift(zv_ref, pv_ref, mu_v)
    xl = shift(zl_ref, plr_ref, mu_l)
    xw = jnp.tanh(xl[:, :RW_DECAY_LORA]).astype(BF16)
    xa = xl[:, RW_DECAY_LORA:RW_DECAY_LORA + RW_AAA_LORA].astype(BF16)
    xg = jax.nn.sigmoid(xl[:, RW_DECAY_LORA + RW_AAA_LORA:]).astype(BF16)
    lw = -math.exp(-0.5) * jax.nn.sigmoid(w0[...] + _dot(xw, w_up[...]))
    a = jax.nn.sigmoid(a0[...] + _dot(xa, a_up[...]))
    g = _dot(xg, g_up[...])
    kk = k * k_k[...]
    ss = _head_sum(kk * kk, ones_bd[...])
    kk = kk * lax.rsqrt(jnp.maximum(ss, 1e-24))
    k = k * (1.0 + (a - 1.0) * k_a[...])
    r_o[...] = r.astype(BF16)
    k_o[...] = k.astype(BF16)
    v_o[...] = v.astype(BF16)
    al_o[...] = (-kk).astype(BF16)
    be_o[...] = (kk * a).astype(BF16)
    lw_o[...] = lw
    g_o[...] = g.astype(BF16)


def _rw_prep(z, seq_len, mu, w0, w_up, a0, a_up, g_up, k_k, k_a, ones_bd):
    t = z.shape[0]
    tm = min(seq_len, 512)
    c512 = lambda j: pl.BlockSpec((tm, RW_DIM), lambda i: (i, OFF_RW // RW_DIM + j))
    prev = lambda w, jb: pl.BlockSpec((16, w), lambda i: (jnp.maximum(i * (tm // 16) - 1, 0), jb))
    mu_r, mu_k, mu_v, mu_l = (mu[:, :RW_DIM], mu[:, RW_DIM:2 * RW_DIM], mu[:, 2 * RW_DIM:3 * RW_DIM],
                              mu[:, 3 * RW_DIM:])
    consts = [mu_r, mu_k, mu_v, mu_l, w0, w_up, a0, a_up, g_up, k_k, k_a, ones_bd]
    out = lambda dt: jax.ShapeDtypeStruct((t, RW_DIM), dt)
    return pl.pallas_call(
        functools.partial(_rw_prep_kernel, seq_len),
        grid=(t // tm,),
        in_specs=[
            c512(0), c512(1), c512(2),
            pl.BlockSpec((tm, RW_LORA), lambda i: (i, OFF_LORA // RW_LORA)),
            prev(RW_DIM, OFF_RW // RW_DIM), prev(RW_DIM, OFF_RW // RW_DIM + 1),
            prev(RW_DIM, OFF_RW // RW_DIM + 2), prev(RW_LORA, OFF_LORA // RW_LORA),
        ] + [_const_spec(c.shape) for c in consts],
        out_specs=[pl.BlockSpec((tm, RW_DIM), lambda i: (i, 0))] * 7,
        out_shape=[out(BF16), out(BF16), out(BF16), out(BF16), out(BF16), out(F32), out(BF16)],
        compiler_params=_cparams("parallel"),
        name="rw_prep",
    )(z, z, z, z, z, z, z, z, *consts)


def _rw_chunk_kernel(r_ref, k_ref, v_ref, al_ref, be_ref, lw_ref, g_ref, rk_ref, lnw_ref, lnb_ref,
                     bd_ref, o_ref, s_ref):
    L = CHUNK

    @pl.when(pl.program_id(1) == 0)
    def _():
        s_ref[...] = jnp.zeros(s_ref.shape, F32)

    bdm = bd_ref[...]

    def bd(x):
        xb = x.astype(BF16)
        return jnp.concatenate([xb] * RW_HEADS, axis=0) * bdm

    row = lax.broadcasted_iota(jnp.int32, (L, RW_DIM), 0)
    sub = lax.broadcasted_iota(jnp.int32, (L, RW_DIM), 1) % RW_HEAD
    strict = sub < row
    incl = sub <= row
    eye = (sub == row).astype(F32)
    tri = (lax.broadcasted_iota(jnp.int32, (L, L), 1)
           <= lax.broadcasted_iota(jnp.int32, (L, L), 0)).astype(BF16)

    lw = lw_ref[...]
    lw_hi, lw_lo = _split_bf16(lw)
    cum = _dot(tri, lw_hi) + _dot(tri, lw_lo)
    cum_l = cum[L - 1:L, :]
    r = r_ref[...].astype(F32)
    k = k_ref[...].astype(F32)
    v = v_ref[...].astype(F32)
    al = al_ref[...].astype(F32)
    be = be_ref[...].astype(F32)
    e_neg = jnp.exp(-cum)
    e_tail = jnp.exp(cum_l - cum)
    ar = jnp.concatenate([al * jnp.exp(cum - lw), r * jnp.exp(cum)], axis=0).astype(BF16)

    s_bf = s_ref[...].astype(BF16)
    ar_s = _dot_nt(ar, s_bf)
    a_b = _dot_nt(ar, bd(be * e_neg))
    a_k = _dot_nt(ar, bd(k * e_neg))
    n = jnp.where(strict, a_b[:L], 0.0)
    a_ak = jnp.where(strict, a_k[:L], 0.0)
    a_rb = jnp.where(incl, a_b[L:], 0.0)
    a_rk = jnp.where(incl, a_k[L:], 0.0)

    p = eye + n
    nk = _dot(n.astype(BF16), bd(n))
    steps = int(math.log2(L)) - 1
    for it in range(steps):
        m = bd(nk)
        if it + 1 < steps:
            res = _dot(jnp.concatenate([p, nk], axis=0).astype(BF16), m)
            p = p + res[:L]
            nk = res[L:]
        else:
            p = p + _dot(p.astype(BF16), m)

    bd_v = bd(v)
    x = ar_s[:L] + _dot(a_ak.astype(BF16), bd_v)
    u = _dot(p.astype(BF16), bd(x))
    y = ar_s[L:] + _dot(jnp.concatenate([a_rb, a_rk], axis=1).astype(BF16),
                        jnp.concatenate([bd(u), bd_v], axis=0))

    uv = jnp.concatenate([u, v], axis=0).astype(BF16)
    bk = jnp.concatenate([be * e_tail, k * e_tail], axis=0).astype(BF16)
    s_ref[...] = s_ref[...] * jnp.exp(cum_l) + _dot_tn(uv, bk) * bdm.astype(F32)

    inv = 1.0 / RW_HEAD
    mean = _head_sum(y, bdm) * inv
    d = y - mean
    var = _head_sum(d * d, bdm) * inv
    yn = d * lax.rsqrt(var + RW_LN_EPS) * lnw_ref[...] + lnb_ref[...]
    bonus = _head_sum(r * k * rk_ref[...], bdm) * v
    o_ref[...] = ((yn + bonus) * g_ref[...].astype(F32)).astype(o_ref.dtype)


def _rw_chunk(r, k, v, al, be, lw, g, seq_len, r_k, ln_w, ln_b, ones_bd):
    t = r.shape[0]
    nc = seq_len // CHUNK
    blk = pl.BlockSpec((CHUNK, RW_DIM), lambda b, c: (b * nc + c, 0))
    consts = [r_k, ln_w, ln_b, ones_bd]
    return pl.pallas_call(
        _rw_chunk_kernel,
        grid=(t // seq_len, nc),
        in_specs=[blk] * 7 + [_const_spec(c.shape) for c in consts],
        out_specs=blk,
        out_shape=jax.ShapeDtypeStruct((t, RW_DIM), BF16),
        scratch_shapes=[pltpu.VMEM((RW_DIM, RW_DIM), F32)],
        compiler_params=_cparams("parallel", "arbitrary"),
        name="rw_chunk",
    )(r, k, v, al, be, lw, g, *consts)


def _merge_kernel(om_ref, or_ref, oc_ref, zg_ref, h_ref, wb_ref, wo_ref, g_ref, o_ref):
    merged = None
    for n, b_ref in enumerate((om_ref, or_ref, oc_ref)):
        y = _dot(b_ref[...], wb_ref[n * BRANCH_DIM:(n + 1) * BRANCH_DIM, :])
        gate = jax.nn.sigmoid(zg_ref[:, n * D_MODEL:(n + 1) * D_MODEL].astype(F32))
        merged = gate * y if merged is None else merged + gate * y
    out = _dot(merged.astype(BF16), wo_ref[...])
    o_ref[...] = h_ref[...] + _rms(out, g_ref[...])


def _merge(o_mla, o_rw, o_ca, z, h, w_branch, w_out, g):
    t = h.shape[0]
    tm = min(t, 512)
    row = lambda w: pl.BlockSpec((tm, w), lambda i: (i, 0))
    return pl.pallas_call(
        _merge_kernel,
        grid=(t // tm,),
        in_specs=[
            row(BRANCH_DIM), row(BRANCH_DIM), row(BRANCH_DIM),
            pl.BlockSpec((tm, GATE_COLS), lambda i: (i, OFF_GATE // GATE_COLS)),
            row(D_MODEL),
            _const_spec(w_branch.shape), _const_spec(w_out.shape), _const_spec(g.shape),
        ],
        out_specs=row(D_MODEL),
        out_shape=jax.ShapeDtypeStruct((t, D_MODEL), F32),
        compiler_params=_cparams("parallel"),
        name="merge",
    )(o_mla, o_rw, o_ca, z, h, w_branch, w_out, g)


FF_SPLIT = 4


def _ff_ple_kernel(h_ref, p_ref, g1_ref, w1_ref, w2_ref, g2_ref, wg_ref, wp_ref, o_ref):
    h = h_ref[...]
    f = _rms(h, g1_ref[...]).astype(BF16)
    cw = D_FF // FF_SPLIT
    acc = None
    for c in range(FF_SPLIT):
        a = jnp.maximum(_dot(f, w1_ref[:, c * cw:(c + 1) * cw]), 0.0)
        part = _dot((a * a).astype(BF16), w2_ref[c * cw:(c + 1) * cw, :])
        acc = part if acc is None else acc + part
    h = h + _rms(acc, g2_ref[...])
    gate = jax.nn.sigmoid(_dot(h.astype(BF16), wg_ref[...]))
    o_ref[...] = h + gate * _dot(p_ref[...].astype(BF16), wp_ref[...])


def _ff_ple(h, p, g1, w1, w2, g2, wg, wp):
    t = h.shape[0]
    tm = min(t, 512)
    row = lambda w: pl.BlockSpec((tm, w), lambda i: (i, 0))
    consts = [g1, w1, w2, g2, wg, wp]
    return pl.pallas_call(
        _ff_ple_kernel,
        grid=(t // tm,),
        in_specs=[row(D_MODEL), row(D_PLE)] + [_const_spec(c.shape) for c in consts],
        out_specs=row(D_MODEL),
        out_shape=jax.ShapeDtypeStruct((t, D_MODEL), F32),
        compiler_params=_cparams("parallel"),
        name="ff_ple",
    )(h, p, *consts)


def _rot_half_cols(w):
    half = w.shape[1] // 2
    return jnp.concatenate([-w[:, half:], w[:, :half]], axis=1)


def _pack_w_in(w):
    d = w.shape[0]
    mla_cols = MLA_Q_RANK + MLA_KV_RANK + MLA_ROPE
    rw_cols = 3 * RW_DIM + RW_LORA
    w_mla = w[:, :mla_cols]
    w_rw = w[:, mla_cols:mla_cols + rw_cols]
    w_ca = w[:, mla_cols + rw_cols:mla_cols + rw_cols + 3 * CA_DIM]
    w_gate = w[:, mla_cols + rw_cols + 3 * CA_DIM:]
    w_kr = w_mla[:, MLA_Q_RANK + MLA_KV_RANK:]
    z64 = jnp.zeros((d, 64), w.dtype)
    packed = jnp.concatenate(
        [w_gate, w_ca, w_rw, w_mla[:, :MLA_Q_RANK + MLA_KV_RANK], w_kr, z64, _rot_half_cols(w_kr), z64], axis=1)
    return packed.astype(BF16)


def _pack_w_uq(w):
    r = w.shape[0]
    w = w.reshape(r, MLA_HEADS, MLA_QK)
    z64 = jnp.zeros((r, MLA_HEADS, 64), w.dtype)
    rope = w[:, :, MLA_NOPE:]
    rot = jnp.concatenate([-rope[:, :, MLA_ROPE // 2:], rope[:, :, :MLA_ROPE // 2]], axis=2)
    wq = jnp.concatenate([w, z64], axis=2).reshape(r, MLA_HEADS * MLA_HEAD_PAD)
    wqr = jnp.concatenate([rot, z64], axis=2).reshape(r, MLA_HEADS * 128)
    return wq.astype(BF16), wqr.astype(BF16)


def _pack_w_ukv(w):
    r = w.shape[0]
    w = w.reshape(r, MLA_HEADS, MLA_NOPE + MLA_V)
    wk = w[:, :, :MLA_NOPE].reshape(r, MLA_HEADS * MLA_NOPE)
    wv = w[:, :, MLA_NOPE:].reshape(r, MLA_HEADS * MLA_V)
    return wk.astype(BF16), wv.astype(BF16)


def kernel(x, p, positions, pre_mix_g, w_in, mla_q_norm_g, mla_kv_norm_g, mla_w_uq, mla_w_ukv, rw_mu, rw_w0, rw_w_up, rw_a0, rw_a_up, rw_g_up, rw_k_k, rw_k_a, rw_r_k, rw_ln_w, rw_ln_b, ca_rel_bias, w_branch, w_out, post_mix_g, pre_ff_g, w_ff1, w_ff2, post_ff_g, w_ple_gate, w_ple_proj):
    b, s, d = x.shape
    t = b * s
    depth = w_in.shape[0]
    row = lambda a: a.reshape(1, -1).astype(F32)
    head_id = jnp.arange(RW_DIM) // RW_HEAD
    ones_bd = (head_id[:, None] == head_id[None, :]).astype(BF16)
    cs, sn = _rope_table(positions)
    h = x.reshape(t, d)
    for i in range(depth):
        z = _in_proj(h, row(pre_mix_g[i]), _pack_w_in(w_in[i]))
        wq, wqr = _pack_w_uq(mla_w_uq[i])
        wk, wv = _pack_w_ukv(mla_w_ukv[i])
        q, kn, kr, v = _mla_proj(z, cs, sn, row(mla_q_norm_g[i]), row(mla_kv_norm_g[i]), wq, wqr, wk, wv)
        o_mla = _mla_attn(q.reshape(b, s, -1), kn.reshape(b, s, -1), kr.reshape(b, s, -1),
                          v.reshape(b, s, -1)).reshape(t, -1)
        rr, rk, rv, al, be, lw, rg = _rw_prep(
            z, s, row(rw_mu[i]), row(rw_w0[i]), rw_w_up[i].astype(BF16), row(rw_a0[i]),
            rw_a_up[i].astype(BF16), rw_g_up[i].astype(BF16), row(rw_k_k[i]), row(rw_k_a[i]), ones_bd)
        o_rw = _rw_chunk(rr, rk, rv, al, be, lw, rg, s, row(rw_r_k[i]), row(rw_ln_w[i]), row(rw_ln_b[i]),
                         ones_bd)
        z3 = z.reshape(b, s, -1)
        pad = lambda a: jnp.pad(a, ((0, 0), (CA_PAD, 0), (0, 0)))
        kp = pad(z3[:, :, OFF_CA + CA_DIM:OFF_CA + 2 * CA_DIM])
        vp = pad(z3[:, :, OFF_CA + 2 * CA_DIM:OFF_CA + 3 * CA_DIM])
        o_ca = _ca_attn(z3, kp, vp, _ca_bias_table(ca_rel_bias[i])).reshape(t, -1)
        h = _merge(o_mla, o_rw, o_ca, z, h, w_branch[i].astype(BF16), w_out[i].astype(BF16),
                   row(post_mix_g[i]))
        h = _ff_ple(h, p[i].reshape(t, -1), row(pre_ff_g[i]), w_ff1[i].astype(BF16), w_ff2[i].astype(BF16),
                    row(post_ff_g[i]), w_ple_gate[i].astype(BF16), w_ple_proj[i].astype(BF16))
    return h.reshape(b, s, d)
```

```python
import functools
import math

import jax
import jax.numpy as jnp
from jax import lax
from jax.experimental import pallas as pl
from jax.experimental.pallas import tpu as pltpu

F32 = jnp.float32
BF16 = jnp.bfloat16

D_MODEL = 1024
D_PLE = 256
D_FF = 4 * D_MODEL
NORM_EPS = 1e-6
CHUNK = 64

MLA_HEADS = 4
MLA_NOPE = 128
MLA_ROPE = 64
MLA_V = 128
MLA_Q_RANK = 256
MLA_KV_RANK = 128
ROPE_THETA = 10000.0
MLA_QK = MLA_NOPE + MLA_ROPE
MLA_HEAD_PAD = 256

RW_HEADS = 8
RW_HEAD = 64
RW_DIM = RW_HEADS * RW_HEAD
RW_DECAY_LORA = 64
RW_AAA_LORA = 64
RW_GATE_LORA = 128
RW_LORA = RW_DECAY_LORA + RW_AAA_LORA + RW_GATE_LORA
RW_LN_EPS = 64e-5

CA_HEADS = 8
CA_HEAD = 64
CA_DIM = CA_HEADS * CA_HEAD
CA_LEFT_CHUNKS = 8
CA_PAD = CA_LEFT_CHUNKS * CHUNK
REL_MIN = -(CHUNK - 1)
REL_MAX = 256

N_BRANCH = 3
BRANCH_DIM = 512
GATE_COLS = N_BRANCH * D_MODEL

MLA_SEC = 640
OFF_GATE = 0
OFF_CA = GATE_COLS
OFF_RW = OFF_CA + 3 * CA_DIM
OFF_LORA = OFF_RW + 3 * RW_DIM
OFF_MLA = OFF_LORA + RW_LORA
IN_PACKED = OFF_MLA + MLA_SEC

VMEM_LIMIT = 56 * 1024 * 1024


def _cparams(*sem):
    return pltpu.CompilerParams(dimension_semantics=sem, vmem_limit_bytes=VMEM_LIMIT)


def _rms(x, g):
    return x * lax.rsqrt(jnp.mean(x * x, axis=-1, keepdims=True) + NORM_EPS) * g


def _dot(a, b):
    return jnp.dot(a, b, preferred_element_type=F32)


def _dot_nt(a, b):
    return lax.dot_general(a, b, (((1,), (1,)), ((), ())), preferred_element_type=F32)


def _dot_tn(a, b):
    return lax.dot_general(a, b, (((0,), (0,)), ((), ())), preferred_element_type=F32)


def _const_spec(shape):
    nd = len(shape)
    return pl.BlockSpec(shape, lambda *_: (0,) * nd, pipeline_mode=pl.Buffered(1))


def _rope_table_kernel(pos_c_ref, pos_r_ref, freq_r_ref, freq_c_ref, cs_ref, sn_ref, cst_ref, snt_ref):
    ang = pos_c_ref[...] * freq_r_ref[...]
    live = lax.broadcasted_iota(jnp.int32, ang.shape, 1) < MLA_ROPE
    cs_ref[...] = jnp.where(live, jnp.cos(ang), 0.0)
    sn_ref[...] = jnp.where(live, jnp.sin(ang), 0.0)
    ang_t = freq_c_ref[...] * pos_r_ref[...]
    live_t = lax.broadcasted_iota(jnp.int32, ang_t.shape, 0) < MLA_ROPE
    cst_ref[...] = jnp.where(live_t, jnp.cos(ang_t), 0.0)
    snt_ref[...] = jnp.where(live_t, jnp.sin(ang_t), 0.0)


def _rope_table(positions):
    t = positions.size
    tm = min(t, 2048)
    half = MLA_ROPE // 2
    inv_freq = 1.0 / (ROPE_THETA ** (jnp.arange(half, dtype=F32) / half))
    freq = jnp.concatenate([inv_freq, inv_freq, jnp.zeros((128 - MLA_ROPE,), F32)])
    pos = positions.astype(F32)
    return pl.pallas_call(
        _rope_table_kernel,
        grid=(t // tm,),
        in_specs=[pl.BlockSpec((tm, 1), lambda i: (i, 0)), pl.BlockSpec((1, tm), lambda i: (0, i)),
                  _const_spec((1, 128)), _const_spec((128, 1))],
        out_specs=[pl.BlockSpec((tm, 128), lambda i: (i, 0))] * 2 + [pl.BlockSpec((128, tm), lambda i: (0, i))] * 2,
        out_shape=[jax.ShapeDtypeStruct((t, 128), F32)] * 2 + [jax.ShapeDtypeStruct((128, t), F32)] * 2,
        compiler_params=_cparams("parallel"),
        name="rope_table",
    )(pos.reshape(t, 1), pos.reshape(1, t), freq[None, :], freq[:, None])


def _in_proj_kernel(x_ref, g_ref, w_ref, o_ref, xn_ref):
    @pl.when(pl.program_id(1) == 0)
    def _():
        xn_ref[...] = _rms(x_ref[...], g_ref[...]).astype(BF16)

    o_ref[...] = _dot(xn_ref[...], w_ref[...]).astype(o_ref.dtype)


def _in_proj(h, g, w):
    t, d = h.shape
    n = w.shape[1]
    tm = min(t, 1024)
    tn = n // 5
    return pl.pallas_call(
        _in_proj_kernel,
        grid=(t // tm, n // tn),
        in_specs=[
            pl.BlockSpec((tm, d), lambda i, j: (i, 0)),
            _const_spec((1, d)),
            pl.BlockSpec((d, tn), lambda i, j: (0, j)),
        ],
        out_specs=pl.BlockSpec((tm, tn), lambda i, j: (i, j)),
        out_shape=jax.ShapeDtypeStruct((t, n), BF16),
        scratch_shapes=[pltpu.VMEM((tm, d), BF16)],
        compiler_params=_cparams("parallel", "arbitrary"),
        name="in_proj",
    )(h, g, w)


def _mla_proj_kernel(z_ref, cs_ref, sn_ref, cst_ref, snt_ref, gq_ref, gkv_ref, wqt_ref, wqrt_ref, wk_ref, wvt_ref,
                     qt_ref, k_ref, kr_ref, vt_ref):
    z = z_ref[...].astype(F32)
    cst = cst_ref[...]
    snt = snt_ref[...]
    qn = _rms(z[:, :MLA_Q_RANK], gq_ref[...]).astype(BF16)
    qt = _dot_nt(wqt_ref[...], qn)
    qrt = _dot_nt(wqrt_ref[...], qn)
    scale = MLA_QK ** -0.5
    for h in range(MLA_HEADS):
        o = h * MLA_HEAD_PAD
        qt_ref[o:o + 128, :] = (qt[o:o + 128] * scale).astype(BF16)
        rope = qt[o + 128:o + 256] * cst + qrt[h * 128:(h + 1) * 128] * snt
        qt_ref[o + 128:o + 256, :] = (rope * scale).astype(BF16)
    kvn = _rms(z[:, MLA_Q_RANK:MLA_Q_RANK + MLA_KV_RANK], gkv_ref[...]).astype(BF16)
    k_ref[...] = _dot(kvn, wk_ref[...]).astype(BF16)
    vt_ref[...] = _dot_nt(wvt_ref[...], kvn).astype(BF16)
    kr_ref[...] = (z[:, 384:512] * cs_ref[...] + z[:, 512:640] * sn_ref[...]).astype(BF16)


def _mla_proj(z, cs, sn, cst, snt, gq, gkv, wqt, wqrt, wk, wvt):
    t = z.shape[0]
    tm = min(t, 1024)
    hv = MLA_HEADS * MLA_V
    hq = MLA_HEADS * MLA_HEAD_PAD
    row = lambda w: pl.BlockSpec((tm, w), lambda i: (i, 0))
    col = lambda w: pl.BlockSpec((w, tm), lambda i: (0, i))
    consts = [gq, gkv, wqt, wqrt, wk, wvt]
    return pl.pallas_call(
        _mla_proj_kernel,
        grid=(t // tm,),
        in_specs=[pl.BlockSpec((tm, MLA_SEC), lambda i: (i, OFF_MLA // MLA_SEC)),
                  row(128), row(128), col(128), col(128)] + [_const_spec(c.shape) for c in consts],
        out_specs=[col(hq), row(hv), row(128), col(hv)],
        out_shape=[
            jax.ShapeDtypeStruct((hq, t), BF16),
            jax.ShapeDtypeStruct((t, hv), BF16),
            jax.ShapeDtypeStruct((t, 128), BF16),
            jax.ShapeDtypeStruct((hv, t), BF16),
        ],
        compiler_params=_cparams("parallel"),
        name="mla_proj",
    )(z, cs, sn, cst, snt, *consts)


MLA_TQ = 512


def _mla_attn_kernel(qt_ref, k_ref, kr_ref, vt_ref, o_ref, m_ref, l_ref, acc_ref):
    i = pl.program_id(1)
    tq = MLA_TQ
    m_ref[...] = jnp.full(m_ref.shape, -1e30, F32)
    l_ref[...] = jnp.zeros(l_ref.shape, F32)
    acc_ref[...] = jnp.zeros(acc_ref.shape, F32)

    def step(j, masked):
        start = pl.multiple_of(j * tq, tq)
        kr = kr_ref[0, pl.ds(start, tq), :]
        for h in range(MLA_HEADS):
            kh = jnp.concatenate([k_ref[0, pl.ds(start, tq), h * 128:(h + 1) * 128], kr], axis=1)
            s = _dot(kh, qt_ref[h * MLA_HEAD_PAD:(h + 1) * MLA_HEAD_PAD, :])
            if masked:
                kc = lax.broadcasted_iota(jnp.int32, s.shape, 0) // CHUNK
                qc = lax.broadcasted_iota(jnp.int32, s.shape, 1) // CHUNK
                s = jnp.where(kc <= qc, s, -1e30)
            m_prev = m_ref[h]
            m_new = jnp.maximum(m_prev, jnp.max(s, axis=0, keepdims=True))
            a = jnp.exp(m_prev - m_new)
            p = jnp.exp(s - m_new)
            l_ref[h] = a * l_ref[h] + jnp.sum(p, axis=0, keepdims=True)
            vh = vt_ref[h * MLA_V:(h + 1) * MLA_V, pl.ds(start, tq)]
            acc_ref[h] = a * acc_ref[h] + _dot(vh, p.astype(BF16))
            m_ref[h] = m_new

    def body(j, c):
        step(j, False)
        return c

    lax.fori_loop(0, i, body, 0)
    step(i, True)
    for h in range(MLA_HEADS):
        o_ref[0, :, h * MLA_V:(h + 1) * MLA_V] = (acc_ref[h] / l_ref[h]).T.astype(o_ref.dtype)


def _mla_attn(qt, k, kr, vt, b):
    hq, t = qt.shape
    s = t // b
    tq = MLA_TQ
    nq = s // tq
    hv = MLA_HEADS * MLA_V
    full = lambda w: pl.BlockSpec((1, s, w), lambda bi, i: (bi, 0, 0))
    return pl.pallas_call(
        _mla_attn_kernel,
        grid=(b, nq),
        in_specs=[
            pl.BlockSpec((hq, tq), lambda bi, i: (0, bi * nq + i)),
            full(hv), full(128),
            pl.BlockSpec((hv, s), lambda bi, i: (0, bi)),
        ],
        out_specs=pl.BlockSpec((1, tq, hv), lambda bi, i: (bi, i, 0)),
        out_shape=jax.ShapeDtypeStruct((b, s, hv), BF16),
        scratch_shapes=[pltpu.VMEM((MLA_HEADS, 1, tq), F32), pltpu.VMEM((MLA_HEADS, 1, tq), F32),
                        pltpu.VMEM((MLA_HEADS, MLA_V, tq), F32)],
        compiler_params=_cparams("parallel", "arbitrary"),
        name="mla_attn",
    )(qt, k.reshape(b, s, hv), kr.reshape(b, s, 128), vt)


CA_TQ = 256
CA_BAND_BLK = CA_TQ + CA_PAD


def _ca_attn_kernel(q_ref, k_ref, v_ref, bias_ref, o_ref):
    i = pl.program_id(1)
    start = pl.multiple_of(i * CA_TQ, CA_TQ)
    q = q_ref[0]
    col = lax.broadcasted_iota(jnp.int32, (CA_TQ, CA_BAND_BLK), 1)
    in_seq = col >= CA_PAD - start
    lane = lax.broadcasted_iota(jnp.int32, (CA_TQ, 128), 1)
    lo = lane < CA_HEAD
    zero = jnp.zeros((), BF16)
    for pair in range(CA_HEADS // 2):
        sl = slice(pair * 128, (pair + 1) * 128)
        q2 = q[:, sl]
        k2 = k_ref[0, pl.ds(start, CA_BAND_BLK), sl]
        v2 = v_ref[0, pl.ds(start, CA_BAND_BLK), sl]
        outs = []
        for sub in range(2):
            qh = jnp.where(lo if sub == 0 else ~lo, q2, zero)
            s = _dot_nt(qh, k2) * (CA_HEAD ** -0.5) + bias_ref[2 * pair + sub]
            s = jnp.where(in_seq, s, -1e30)
            m = jnp.max(s, axis=-1, keepdims=True)
            p = jnp.exp(s - m)
            l = jnp.sum(p, axis=-1, keepdims=True)
            outs.append(_dot(p.astype(BF16), v2) / l)
        o_ref[0, :, sl] = jnp.where(lo, outs[0], outs[1]).astype(o_ref.dtype)


def _ca_attn(z3, kp, vp, bias):
    b, s, _ = z3.shape
    sp = kp.shape[1]
    full = pl.BlockSpec((1, sp, CA_DIM), lambda bi, i: (bi, 0, 0))
    return pl.pallas_call(
        _ca_attn_kernel,
        grid=(b, s // CA_TQ),
        in_specs=[
            pl.BlockSpec((1, CA_TQ, CA_DIM), lambda bi, i: (bi, i, OFF_CA // CA_DIM)),
            full, full,
            _const_spec(bias.shape),
        ],
        out_specs=pl.BlockSpec((1, CA_TQ, CA_DIM), lambda bi, i: (bi, i, 0)),
        out_shape=jax.ShapeDtypeStruct((b, s, CA_DIM), BF16),
        compiler_params=_cparams("parallel", "arbitrary"),
        name="ca_attn",
    )(z3, kp, vp, bias)


def _ca_bias_table(rel_bias):
    width = CA_TQ + CA_BAND_BLK
    m = jnp.arange(width)
    delta = jnp.where(m < CA_BAND_BLK, m, m - width)
    idx = jnp.clip(CA_PAD - delta, REL_MIN, REL_MAX) - REL_MIN
    t1 = rel_bias.astype(F32)[idx].T
    flat = jnp.tile(t1, (1, CA_TQ))[:, :CA_TQ * (width - 1)]
    bias = flat.reshape(CA_HEADS, CA_TQ, width - 1)[:, :, :CA_BAND_BLK]
    rc = jnp.arange(CA_TQ)[:, None] // CHUNK
    cc = jnp.arange(CA_BAND_BLK)[None, :] // CHUNK
    ok = (cc >= rc) & (cc <= rc + CA_LEFT_CHUNKS)
    return jnp.where(ok[None], bias, -1e30)


def _split_bf16(x):
    hi = x.astype(BF16)
    lo = (x - hi.astype(F32)).astype(BF16)
    return hi, lo


def _head_sum(x, ones_bd):
    hi, lo = _split_bf16(x)
    return _dot(hi, ones_bd) + _dot(lo, ones_bd)


def _rw_prep_kernel(seq_len, zr_ref, zk_ref, zv_ref, zl_ref, pr_ref, pk_ref, pv_ref, plr_ref,
                    mu_r, mu_k, mu_v, mu_l, w0, w_up, a0, a_up, g_up, k_k, k_a, ones_bd,
                    r_o, k_o, v_o, al_o, be_o, lw_o, g_o):
    tm = zr_ref.shape[0]
    first = (pl.program_id(0) * tm) % seq_len == 0
    row0 = lax.broadcasted_iota(jnp.int32, (tm, 1), 0) == 0

    def shift(z_ref, p_ref, mu):
        z = z_ref[...].astype(F32)
        last = jnp.where(first, 0.0, p_ref[...].astype(F32)[15:16, :])
        prev = jnp.where(row0, last, pltpu.roll(z, 1, axis=0))
        return z + (prev - z) * mu[...]

    r = shift(zr_ref, pr_ref, mu_r)
    k = shift(zk_ref, pk_ref, mu_k)
    v = shift(zv_ref, pv_ref, mu_v)
    xl = shift(zl_ref, plr_ref, mu_l)
    xw = jnp.tanh(xl[:, :RW_DECAY_LORA]).astype(BF16)
    xa = xl[:, RW_DECAY_LORA:RW_DECAY_LORA + RW_AAA_LORA].astype(BF16)
    xg = jax.nn.sigmoid(xl[:, RW_DECAY_LORA + RW_AAA_LORA:]).astype(BF16)
    lw = -math.exp(-0.5) * jax.nn.sigmoid(w0[...] + _dot(xw, w_up[...]))
    a = jax.nn.sigmoid(a0[...] + _dot(xa, a_up[...]))
    g = _dot(xg, g_up[...])
    kk = k * k_k[...]
    ss = _head_sum(kk * kk, ones_bd[...])
    kk = kk * lax.rsqrt(jnp.maximum(ss, 1e-24))
    k = k * (1.0 + (a - 1.0) * k_a[...])
    r_o[...] = r.astype(BF16)
    k_o[...] = k.astype(BF16)
    v_o[...] = v.astype(BF16)
    al_o[...] = (-kk).astype(BF16)
    be_o[...] = (kk * a).astype(BF16)
    lw_o[...] = lw
    g_o[...] = g.astype(BF16)


def _rw_prep(z, seq_len, mu, w0, w_up, a0, a_up, g_up, k_k, k_a, ones_bd):
    t = z.shape[0]
    tm = min(seq_len, 512)
    c512 = lambda j: pl.BlockSpec((tm, RW_DIM), lambda i: (i, OFF_RW // RW_DIM + j))
    prev = lambda w, jb: pl.BlockSpec((16, w), lambda i: (jnp.maximum(i * (tm // 16) - 1, 0), jb))
    mu_r, mu_k, mu_v, mu_l = (mu[:, :RW_DIM], mu[:, RW_DIM:2 * RW_DIM], mu[:, 2 * RW_DIM:3 * RW_DIM],
                              mu[:, 3 * RW_DIM:])
    consts = [mu_r, mu_k, mu_v, mu_l, w0, w_up, a0, a_up, g_up, k_k, k_a, ones_bd]
    out = lambda dt: jax.ShapeDtypeStruct((t, RW_DIM), dt)
    return pl.pallas_call(
        functools.partial(_rw_prep_kernel, seq_len),
        grid=(t // tm,),
        in_specs=[
            c512(0), c512(1), c512(2),
            pl.BlockSpec((tm, RW_LORA), lambda i: (i, OFF_LORA // RW_LORA)),
            prev(RW_DIM, OFF_RW // RW_DIM), prev(RW_DIM, OFF_RW // RW_DIM + 1),
            prev(RW_DIM, OFF_RW // RW_DIM + 2), prev(RW_LORA, OFF_LORA // RW_LORA),
        ] + [_const_spec(c.shape) for c in consts],
        out_specs=[pl.BlockSpec((tm, RW_DIM), lambda i: (i, 0))] * 7,
        out_shape=[out(BF16), out(BF16), out(BF16), out(BF16), out(BF16), out(F32), out(BF16)],
        compiler_params=_cparams("parallel"),
        name="rw_prep",
    )(z, z, z, z, z, z, z, z, *consts)


def _rw_chunk_kernel(r_ref, k_ref, v_ref, al_ref, be_ref, lw_ref, g_ref, rk_ref, lnw_ref, lnb_ref,
                     bd_ref, o_ref, s_ref):
    L = CHUNK

    @pl.when(pl.program_id(1) == 0)
    def _():
        s_ref[...] = jnp.zeros(s_ref.shape, F32)

    bdm = bd_ref[...]

    def bd(x):
        xb = x.astype(BF16)
        return jnp.concatenate([xb] * RW_HEADS, axis=0) * bdm

    row = lax.broadcasted_iota(jnp.int32, (L, RW_DIM), 0)
    sub = lax.broadcasted_iota(jnp.int32, (L, RW_DIM), 1) % RW_HEAD
    strict = sub < row
    incl = sub <= row
    eye = (sub == row).astype(F32)
    tri = (lax.broadcasted_iota(jnp.int32, (L, L), 1)
           <= lax.broadcasted_iota(jnp.int32, (L, L), 0)).astype(BF16)

    lw = lw_ref[...]
    lw_hi, lw_lo = _split_bf16(lw)
    cum = _dot(tri, lw_hi) + _dot(tri, lw_lo)
    cum_l = cum[L - 1:L, :]
    r = r_ref[...].astype(F32)
    k = k_ref[...].astype(F32)
    v = v_ref[...].astype(F32)
    al = al_ref[...].astype(F32)
    be = be_ref[...].astype(F32)
    e_neg = jnp.exp(-cum)
    e_tail = jnp.exp(cum_l - cum)
    ar = jnp.concatenate([al * jnp.exp(cum - lw), r * jnp.exp(cum)], axis=0).astype(BF16)

    s_bf = s_ref[...].astype(BF16)
    ar_s = _dot_nt(ar, s_bf)
    a_b = _dot_nt(ar, bd(be * e_neg))
    a_k = _dot_nt(ar, bd(k * e_neg))
    n = jnp.where(strict, a_b[:L], 0.0)
    a_ak = jnp.where(strict, a_k[:L], 0.0)
    a_rb = jnp.where(incl, a_b[L:], 0.0)
    a_rk = jnp.where(incl, a_k[L:], 0.0)

    p = eye + n
    nk = _dot(n.astype(BF16), bd(n))
    steps = int(math.log2(L)) - 1
    for it in range(steps):
        m = bd(nk)
        if it + 1 < steps:
            res = _dot(jnp.concatenate([p, nk], axis=0).astype(BF16), m)
            p = p + res[:L]
            nk = res[L:]
        else:
            p = p + _dot(p.astype(BF16), m)

    bd_v = bd(v)
    x = ar_s[:L] + _dot(a_ak.astype(BF16), bd_v)
    u = _dot(p.astype(BF16), bd(x))
    y = ar_s[L:] + _dot(jnp.concatenate([a_rb, a_rk], axis=1).astype(BF16),
                        jnp.concatenate([bd(u), bd_v], axis=0))

    uv = jnp.concatenate([u, v], axis=0).astype(BF16)
    bk = jnp.concatenate([be * e_tail, k * e_tail], axis=0).astype(BF16)
    s_ref[...] = s_ref[...] * jnp.exp(cum_l) + _dot_tn(uv, bk) * bdm.astype(F32)

    inv = 1.0 / RW_HEAD
    mean = _head_sum(y, bdm) * inv
    d = y - mean
    var = _head_sum(d * d, bdm) * inv
    yn = d * lax.rsqrt(var + RW_LN_EPS) * lnw_ref[...] + lnb_ref[...]
    bonus = _head_sum(r * k * rk_ref[...], bdm) * v
    o_ref[...] = ((yn + bonus) * g_ref[...].astype(F32)).astype(o_ref.dtype)


def _rw_chunk(r, k, v, al, be, lw, g, seq_len, r_k, ln_w, ln_b, ones_bd):
    t = r.shape[0]
    nc = seq_len // CHUNK
    blk = pl.BlockSpec((CHUNK, RW_DIM), lambda b, c: (b * nc + c, 0))
    consts = [r_k, ln_w, ln_b, ones_bd]
    return pl.pallas_call(
        _rw_chunk_kernel,
        grid=(t // seq_len, nc),
        in_specs=[blk] * 7 + [_const_spec(c.shape) for c in consts],
        out_specs=blk,
        out_shape=jax.ShapeDtypeStruct((t, RW_DIM), BF16),
        scratch_shapes=[pltpu.VMEM((RW_DIM, RW_DIM), F32)],
        compiler_params=_cparams("parallel", "arbitrary"),
        name="rw_chunk",
    )(r, k, v, al, be, lw, g, *consts)


def _merge_kernel(om_ref, or_ref, oc_ref, zg_ref, h_ref, wb_ref, wo_ref, g_ref, o_ref):
    merged = None
    for n, b_ref in enumerate((om_ref, or_ref, oc_ref)):
        y = _dot(b_ref[...], wb_ref[n * BRANCH_DIM:(n + 1) * BRANCH_DIM, :])
        gate = jax.nn.sigmoid(zg_ref[:, n * D_MODEL:(n + 1) * D_MODEL].astype(F32))
        merged = gate * y if merged is None else merged + gate * y
    out = _dot(merged.astype(BF16), wo_ref[...])
    o_ref[...] = h_ref[...] + _rms(out, g_ref[...])


def _merge(o_mla, o_rw, o_ca, z, h, w_branch, w_out, g):
    t = h.shape[0]
    tm = min(t, 512)
    row = lambda w: pl.BlockSpec((tm, w), lambda i: (i, 0))
    return pl.pallas_call(
        _merge_kernel,
        grid=(t // tm,),
        in_specs=[
            row(BRANCH_DIM), row(BRANCH_DIM), row(BRANCH_DIM),
            pl.BlockSpec((tm, GATE_COLS), lambda i: (i, OFF_GATE // GATE_COLS)),
            row(D_MODEL),
            _const_spec(w_branch.shape), _const_spec(w_out.shape), _const_spec(g.shape),
        ],
        out_specs=row(D_MODEL),
        out_shape=jax.ShapeDtypeStruct((t, D_MODEL), F32),
        compiler_params=_cparams("parallel"),
        name="merge",
    )(o_mla, o_rw, o_ca, z, h, w_branch, w_out, g)


FF_SPLIT = 4


def _ff_ple_kernel(h_ref, p_ref, g1_ref, w1_ref, w2_ref, g2_ref, wg_ref, wp_ref, o_ref):
    h = h_ref[...]
    f = _rms(h, g1_ref[...]).astype(BF16)
    cw = D_FF // FF_SPLIT
    acc = None
    for c in range(FF_SPLIT):
        a = jnp.maximum(_dot(f, w1_ref[:, c * cw:(c + 1) * cw]), 0.0)
        part = _dot((a * a).astype(BF16), w2_ref[c * cw:(c + 1) * cw, :])
        acc = part if acc is None else acc + part
    h = h + _rms(acc, g2_ref[...])
    gate = jax.nn.sigmoid(_dot(h.astype(BF16), wg_ref[...]))
    o_ref[...] = h + gate * _dot(p_ref[...].astype(BF16), wp_ref[...])


def _ff_ple(h, p, g1, w1, w2, g2, wg, wp):
    t = h.shape[0]
    tm = min(t, 512)
    row = lambda w: pl.BlockSpec((tm, w), lambda i: (i, 0))
    consts = [g1, w1, w2, g2, wg, wp]
    return pl.pallas_call(
        _ff_ple_kernel,
        grid=(t // tm,),
        in_specs=[row(D_MODEL), row(D_PLE)] + [_const_spec(c.shape) for c in consts],
        out_specs=row(D_MODEL),
        out_shape=jax.ShapeDtypeStruct((t, D_MODEL), F32),
        compiler_params=_cparams("parallel"),
        name="ff_ple",
    )(h, p, *consts)


def _rot_half_cols(w):
    half = w.shape[1] // 2
    return jnp.concatenate([-w[:, half:], w[:, :half]], axis=1)


def _pack_w_in(w):
    d = w.shape[0]
    mla_cols = MLA_Q_RANK + MLA_KV_RANK + MLA_ROPE
    rw_cols = 3 * RW_DIM + RW_LORA
    w_mla = w[:, :mla_cols]
    w_rw = w[:, mla_cols:mla_cols + rw_cols]
    w_ca = w[:, mla_cols + rw_cols:mla_cols + rw_cols + 3 * CA_DIM]
    w_gate = w[:, mla_cols + rw_cols + 3 * CA_DIM:]
    w_kr = w_mla[:, MLA_Q_RANK + MLA_KV_RANK:]
    z64 = jnp.zeros((d, 64), w.dtype)
    packed = jnp.concatenate(
        [w_gate, w_ca, w_rw, w_mla[:, :MLA_Q_RANK + MLA_KV_RANK], w_kr, z64, _rot_half_cols(w_kr), z64], axis=1)
    return packed.astype(BF16)


def _pack_w_uq(w):
    r = w.shape[0]
    w = w.reshape(r, MLA_HEADS, MLA_QK)
    z64 = jnp.zeros((r, MLA_HEADS, 64), w.dtype)
    rope = w[:, :, MLA_NOPE:]
    rot = jnp.concatenate([-rope[:, :, MLA_ROPE // 2:], rope[:, :, :MLA_ROPE // 2]], axis=2)
    wq = jnp.concatenate([w, z64], axis=2).reshape(r, MLA_HEADS * MLA_HEAD_PAD)
    wqr = jnp.concatenate([rot, z64], axis=2).reshape(r, MLA_HEADS * 128)
    return wq.T.astype(BF16), wqr.T.astype(BF16)


def _pack_w_ukv(w):
    r = w.shape[0]
    w = w.reshape(r, MLA_HEADS, MLA_NOPE + MLA_V)
    wk = w[:, :, :MLA_NOPE].reshape(r, MLA_HEADS * MLA_NOPE)
    wv = w[:, :, MLA_NOPE:].reshape(r, MLA_HEADS * MLA_V)
    return wk.astype(BF16), wv.T.astype(BF16)


def kernel(x, p, positions, pre_mix_g, w_in, mla_q_norm_g, mla_kv_norm_g, mla_w_uq, mla_w_ukv, rw_mu, rw_w0, rw_w_up, rw_a0, rw_a_up, rw_g_up, rw_k_k, rw_k_a, rw_r_k, rw_ln_w, rw_ln_b, ca_rel_bias, w_branch, w_out, post_mix_g, pre_ff_g, w_ff1, w_ff2, post_ff_g, w_ple_gate, w_ple_proj):
    b, s, d = x.shape
    t = b * s
    depth = w_in.shape[0]
    row = lambda a: a.reshape(1, -1).astype(F32)
    head_id = jnp.arange(RW_DIM) // RW_HEAD
    ones_bd = (head_id[:, None] == head_id[None, :]).astype(BF16)
    cs, sn, cst, snt = _rope_table(positions)
    h = x.reshape(t, d)
    for i in range(depth):
        z = _in_proj(h, row(pre_mix_g[i]), _pack_w_in(w_in[i]))
        wqt, wqrt = _pack_w_uq(mla_w_uq[i])
        wk, wvt = _pack_w_ukv(mla_w_ukv[i])
        qt, kn, kr, vt = _mla_proj(z, cs, sn, cst, snt, row(mla_q_norm_g[i]), row(mla_kv_norm_g[i]),
                                   wqt, wqrt, wk, wvt)
        o_mla = _mla_attn(qt, kn, kr, vt, b).reshape(t, -1)
        rr, rk, rv, al, be, lw, rg = _rw_prep(
            z, s, row(rw_mu[i]), row(rw_w0[i]), rw_w_up[i].astype(BF16), row(rw_a0[i]),
            rw_a_up[i].astype(BF16), rw_g_up[i].astype(BF16), row(rw_k_k[i]), row(rw_k_a[i]), ones_bd)
        o_rw = _rw_chunk(rr, rk, rv, al, be, lw, rg, s, row(rw_r_k[i]), row(rw_ln_w[i]), row(rw_ln_b[i]),
                         ones_bd)
        z3 = z.reshape(b, s, -1)
        pad = lambda a: jnp.pad(a, ((0, 0), (CA_PAD, 0), (0, 0)))
        kp = pad(z3[:, :, OFF_CA + CA_DIM:OFF_CA + 2 * CA_DIM])
        vp = pad(z3[:, :, OFF_CA + 2 * CA_DIM:OFF_CA + 3 * CA_DIM])
        o_ca = _ca_attn(z3, kp, vp, _ca_bias_table(ca_rel_bias[i])).reshape(t, -1)
        h = _merge(o_mla, o_rw, o_ca, z, h, w_branch[i].astype(BF16), w_out[i].astype(BF16),
                   row(post_mix_g[i]))
        h = _ff_ple(h, p[i].reshape(t, -1), row(pre_ff_g[i]), w_ff1[i].astype(BF16), w_ff2[i].astype(BF16),
                    row(post_ff_g[i]), w_ple_gate[i].astype(BF16), w_ple_proj[i].astype(BF16))
    return h.reshape(b, s, d)
```

```python
import functools
import math

import jax
import jax.numpy as jnp
from jax import lax
from jax.experimental import pallas as pl
from jax.experimental.pallas import tpu as pltpu

F32 = jnp.float32
BF16 = jnp.bfloat16

D_MODEL = 1024
D_PLE = 256
D_FF = 4 * D_MODEL
NORM_EPS = 1e-6
CHUNK = 64

MLA_HEADS = 4
MLA_NOPE = 128
MLA_ROPE = 64
MLA_V = 128
MLA_Q_RANK = 256
MLA_KV_RANK = 128
ROPE_THETA = 10000.0
MLA_QK = MLA_NOPE + MLA_ROPE
MLA_HEAD_PAD = 256

RW_HEADS = 8
RW_HEAD = 64
RW_DIM = RW_HEADS * RW_HEAD
RW_DECAY_LORA = 64
RW_AAA_LORA = 64
RW_GATE_LORA = 128
RW_LORA = RW_DECAY_LORA + RW_AAA_LORA + RW_GATE_LORA
RW_LN_EPS = 64e-5

CA_HEADS = 8
CA_HEAD = 64
CA_DIM = CA_HEADS * CA_HEAD
CA_LEFT_CHUNKS = 8
CA_PAD = CA_LEFT_CHUNKS * CHUNK
REL_MIN = -(CHUNK - 1)
REL_MAX = 256

N_BRANCH = 3
BRANCH_DIM = 512
GATE_COLS = N_BRANCH * D_MODEL

MLA_SEC = 640
OFF_GATE = 0
OFF_CA = GATE_COLS
OFF_RW = OFF_CA + 3 * CA_DIM
OFF_LORA = OFF_RW + 3 * RW_DIM
OFF_MLA = OFF_LORA + RW_LORA
IN_PACKED = OFF_MLA + MLA_SEC

VMEM_LIMIT = 56 * 1024 * 1024


def _cparams(*sem):
    return pltpu.CompilerParams(dimension_semantics=sem, vmem_limit_bytes=VMEM_LIMIT)


def _rms(x, g):
    return x * lax.rsqrt(jnp.mean(x * x, axis=-1, keepdims=True) + NORM_EPS) * g


def _dot(a, b):
    return jnp.dot(a, b, preferred_element_type=F32)


def _dot_nt(a, b):
    return lax.dot_general(a, b, (((1,), (1,)), ((), ())), preferred_element_type=F32)


def _dot_tn(a, b):
    return lax.dot_general(a, b, (((0,), (0,)), ((), ())), preferred_element_type=F32)


def _const_spec(shape):
    nd = len(shape)
    return pl.BlockSpec(shape, lambda *_: (0,) * nd, pipeline_mode=pl.Buffered(1))


def _rope_table_kernel(pos_c_ref, pos_r_ref, freq_r_ref, freq_c_ref, cs_ref, sn_ref, cst_ref, snt_ref):
    ang = pos_c_ref[...] * freq_r_ref[...]
    live = lax.broadcasted_iota(jnp.int32, ang.shape, 1) < MLA_ROPE
    cs_ref[...] = jnp.where(live, jnp.cos(ang), 0.0)
    sn_ref[...] = jnp.where(live, jnp.sin(ang), 0.0)
    ang_t = freq_c_ref[...] * pos_r_ref[...]
    live_t = lax.broadcasted_iota(jnp.int32, ang_t.shape, 0) < MLA_ROPE
    cst_ref[...] = jnp.where(live_t, jnp.cos(ang_t), 0.0)
    snt_ref[...] = jnp.where(live_t, jnp.sin(ang_t), 0.0)


def _rope_table(positions):
    t = positions.size
    tm = min(t, 2048)
    half = MLA_ROPE // 2
    inv_freq = 1.0 / (ROPE_THETA ** (jnp.arange(half, dtype=F32) / half))
    freq = jnp.concatenate([inv_freq, inv_freq, jnp.zeros((128 - MLA_ROPE,), F32)])
    pos = positions.astype(F32)
    return pl.pallas_call(
        _rope_table_kernel,
        grid=(t // tm,),
        in_specs=[pl.BlockSpec((tm, 1), lambda i: (i, 0)), pl.BlockSpec((1, tm), lambda i: (0, i)),
                  _const_spec((1, 128)), _const_spec((128, 1))],
        out_specs=[pl.BlockSpec((tm, 128), lambda i: (i, 0))] * 2 + [pl.BlockSpec((128, tm), lambda i: (0, i))] * 2,
        out_shape=[jax.ShapeDtypeStruct((t, 128), F32)] * 2 + [jax.ShapeDtypeStruct((128, t), F32)] * 2,
        compiler_params=_cparams("parallel"),
        name="rope_table",
    )(pos.reshape(t, 1), pos.reshape(1, t), freq[None, :], freq[:, None])


def _in_proj_kernel(x_ref, g_ref, w_ref, o_ref, xn_ref):
    @pl.when(pl.program_id(1) == 0)
    def _():
        xn_ref[...] = _rms(x_ref[...], g_ref[...]).astype(BF16)

    o_ref[...] = _dot(xn_ref[...], w_ref[...]).astype(o_ref.dtype)


def _in_proj(h, g, w):
    t, d = h.shape
    n = w.shape[1]
    tm = min(t, 1024)
    tn = n // 5
    return pl.pallas_call(
        _in_proj_kernel,
        grid=(t // tm, n // tn),
        in_specs=[
            pl.BlockSpec((tm, d), lambda i, j: (i, 0)),
            _const_spec((1, d)),
            pl.BlockSpec((d, tn), lambda i, j: (0, j)),
        ],
        out_specs=pl.BlockSpec((tm, tn), lambda i, j: (i, j)),
        out_shape=jax.ShapeDtypeStruct((t, n), BF16),
        scratch_shapes=[pltpu.VMEM((tm, d), BF16)],
        compiler_params=_cparams("parallel", "arbitrary"),
        name="in_proj",
    )(h, g, w)


def _mla_proj_kernel(z_ref, cs_ref, sn_ref, cst_ref, snt_ref, gq_ref, gkv_ref, wqt_ref, wqrt_ref, wk_ref, wvt_ref,
                     qt_ref, k_ref, kr_ref, vt_ref):
    z = z_ref[...].astype(F32)
    cst = cst_ref[...]
    snt = snt_ref[...]
    qn = _rms(z[:, :MLA_Q_RANK], gq_ref[...]).astype(BF16)
    qt = _dot_nt(wqt_ref[...], qn)
    qrt = _dot_nt(wqrt_ref[...], qn)
    scale = MLA_QK ** -0.5
    for h in range(MLA_HEADS):
        o = h * MLA_HEAD_PAD
        qt_ref[o:o + 128, :] = (qt[o:o + 128] * scale).astype(BF16)
        rope = qt[o + 128:o + 256] * cst + qrt[h * 128:(h + 1) * 128] * snt
        qt_ref[o + 128:o + 256, :] = (rope * scale).astype(BF16)
    kvn = _rms(z[:, MLA_Q_RANK:MLA_Q_RANK + MLA_KV_RANK], gkv_ref[...]).astype(BF16)
    k_ref[...] = _dot(kvn, wk_ref[...]).astype(BF16)
    vt_ref[...] = _dot_nt(wvt_ref[...], kvn).astype(BF16)
    kr_ref[...] = (z[:, 384:512] * cs_ref[...] + z[:, 512:640] * sn_ref[...]).astype(BF16)


def _mla_proj(z, cs, sn, cst, snt, gq, gkv, wqt, wqrt, wk, wvt):
    t = z.shape[0]
    tm = min(t, 1024)
    hv = MLA_HEADS * MLA_V
    hq = MLA_HEADS * MLA_HEAD_PAD
    row = lambda w: pl.BlockSpec((tm, w), lambda i: (i, 0))
    col = lambda w: pl.BlockSpec((w, tm), lambda i: (0, i))
    consts = [gq, gkv, wqt, wqrt, wk, wvt]
    return pl.pallas_call(
        _mla_proj_kernel,
        grid=(t // tm,),
        in_specs=[pl.BlockSpec((tm, MLA_SEC), lambda i: (i, OFF_MLA // MLA_SEC)),
                  row(128), row(128), col(128), col(128)] + [_const_spec(c.shape) for c in consts],
        out_specs=[col(hq), row(hv), row(128), col(hv)],
        out_shape=[
            jax.ShapeDtypeStruct((hq, t), BF16),
            jax.ShapeDtypeStruct((t, hv), BF16),
            jax.ShapeDtypeStruct((t, 128), BF16),
            jax.ShapeDtypeStruct((hv, t), BF16),
        ],
        compiler_params=_cparams("parallel"),
        name="mla_proj",
    )(z, cs, sn, cst, snt, *consts)


MLA_TQ = 512


def _mla_attn_kernel(qt_ref, k_ref, kr_ref, vt_ref, o_ref, m_ref, l_ref, acc_ref):
    i = pl.program_id(1)
    tq = MLA_TQ
    m_ref[...] = jnp.full(m_ref.shape, -1e30, F32)
    l_ref[...] = jnp.zeros(l_ref.shape, F32)
    acc_ref[...] = jnp.zeros(acc_ref.shape, F32)

    def step(j, masked):
        start = pl.multiple_of(j * tq, tq)
        kr = kr_ref[0, pl.ds(start, tq), :]
        for h in range(MLA_HEADS):
            kh = jnp.concatenate([k_ref[0, pl.ds(start, tq), h * 128:(h + 1) * 128], kr], axis=1)
            s = _dot(kh, qt_ref[h * MLA_HEAD_PAD:(h + 1) * MLA_HEAD_PAD, :])
            if masked:
                kc = lax.broadcasted_iota(jnp.int32, s.shape, 0) // CHUNK
                qc = lax.broadcasted_iota(jnp.int32, s.shape, 1) // CHUNK
                s = jnp.where(kc <= qc, s, -1e30)
            m_prev = m_ref[h]
            m_new = jnp.maximum(m_prev, jnp.max(s, axis=0, keepdims=True))
            a = jnp.exp(m_prev - m_new)
            p = jnp.exp(s - m_new)
            l_ref[h] = a * l_ref[h] + jnp.sum(p, axis=0, keepdims=True)
            vh = vt_ref[h * MLA_V:(h + 1) * MLA_V, pl.ds(start, tq)]
            acc_ref[h] = a * acc_ref[h] + _dot(vh, p.astype(BF16))
            m_ref[h] = m_new

    def body(j, c):
        step(j, False)
        return c

    lax.fori_loop(0, i, body, 0)
    step(i, True)
    for h in range(MLA_HEADS):
        o_ref[0, :, h * MLA_V:(h + 1) * MLA_V] = (acc_ref[h] / l_ref[h]).T.astype(o_ref.dtype)


def _mla_attn(qt, k, kr, vt, b):
    hq, t = qt.shape
    s = t // b
    tq = MLA_TQ
    nq = s // tq
    hv = MLA_HEADS * MLA_V
    full = lambda w: pl.BlockSpec((1, s, w), lambda bi, i: (bi, 0, 0))
    return pl.pallas_call(
        _mla_attn_kernel,
        grid=(b, nq),
        in_specs=[
            pl.BlockSpec((hq, tq), lambda bi, i: (0, bi * nq + i)),
            full(hv), full(128),
            pl.BlockSpec((hv, s), lambda bi, i: (0, bi)),
        ],
        out_specs=pl.BlockSpec((1, tq, hv), lambda bi, i: (bi, i, 0)),
        out_shape=jax.ShapeDtypeStruct((b, s, hv), BF16),
        scratch_shapes=[pltpu.VMEM((MLA_HEADS, 1, tq), F32), pltpu.VMEM((MLA_HEADS, 1, tq), F32),
                        pltpu.VMEM((MLA_HEADS, MLA_V, tq), F32)],
        compiler_params=_cparams("parallel", "arbitrary"),
        name="mla_attn",
    )(qt, k.reshape(b, s, hv), kr.reshape(b, s, 128), vt)


CA_TQ = 256
CA_BAND_BLK = CA_TQ + CA_PAD


def _ca_attn_kernel(q_ref, k_ref, v_ref, bias_ref, o_ref):
    i = pl.program_id(1)
    start = pl.multiple_of(i * CA_TQ, CA_TQ)
    q = q_ref[0]
    col = lax.broadcasted_iota(jnp.int32, (CA_TQ, CA_BAND_BLK), 1)
    in_seq = col >= CA_PAD - start
    lane = lax.broadcasted_iota(jnp.int32, (CA_TQ, 128), 1)
    lo = lane < CA_HEAD
    zero = jnp.zeros((), BF16)
    for pair in range(CA_HEADS // 2):
        sl = slice(pair * 128, (pair + 1) * 128)
        q2 = q[:, sl]
        k2 = k_ref[0, pl.ds(start, CA_BAND_BLK), sl]
        v2 = v_ref[0, pl.ds(start, CA_BAND_BLK), sl]
        outs = []
        for sub in range(2):
            qh = jnp.where(lo if sub == 0 else ~lo, q2, zero)
            s = _dot_nt(qh, k2) * (CA_HEAD ** -0.5) + bias_ref[2 * pair + sub]
            s = jnp.where(in_seq, s, -1e30)
            m = jnp.max(s, axis=-1, keepdims=True)
            p = jnp.exp(s - m)
            l = jnp.sum(p, axis=-1, keepdims=True)
            outs.append(_dot(p.astype(BF16), v2) / l)
        o_ref[0, :, sl] = jnp.where(lo, outs[0], outs[1]).astype(o_ref.dtype)


def _ca_attn(z3, kp, vp, bias):
    b, s, _ = z3.shape
    sp = kp.shape[1]
    full = pl.BlockSpec((1, sp, CA_DIM), lambda bi, i: (bi, 0, 0))
    return pl.pallas_call(
        _ca_attn_kernel,
        grid=(b, s // CA_TQ),
        in_specs=[
            pl.BlockSpec((1, CA_TQ, CA_DIM), lambda bi, i: (bi, i, OFF_CA // CA_DIM)),
            full, full,
            _const_spec(bias.shape),
        ],
        out_specs=pl.BlockSpec((1, CA_TQ, CA_DIM), lambda bi, i: (bi, i, 0)),
        out_shape=jax.ShapeDtypeStruct((b, s, CA_DIM), BF16),
        compiler_params=_cparams("parallel", "arbitrary"),
        name="ca_attn",
    )(z3, kp, vp, bias)


def _ca_bias_table(rel_bias):
    width = CA_TQ + CA_BAND_BLK
    m = jnp.arange(width)
    delta = jnp.where(m < CA_BAND_BLK, m, m - width)
    idx = jnp.clip(CA_PAD - delta, REL_MIN, REL_MAX) - REL_MIN
    t1 = rel_bias.astype(F32)[idx].T
    flat = jnp.tile(t1, (1, CA_TQ))[:, :CA_TQ * (width - 1)]
    bias = flat.reshape(CA_HEADS, CA_TQ, width - 1)[:, :, :CA_BAND_BLK]
    rc = jnp.arange(CA_TQ)[:, None] // CHUNK
    cc = jnp.arange(CA_BAND_BLK)[None, :] // CHUNK
    ok = (cc >= rc) & (cc <= rc + CA_LEFT_CHUNKS)
    return jnp.where(ok[None], bias, -1e30)


def _split_bf16(x):
    hi = x.astype(BF16)
    lo = (x - hi.astype(F32)).astype(BF16)
    return hi, lo


def _head_sum(x, ones_bd):
    hi, lo = _split_bf16(x)
    return _dot(hi, ones_bd) + _dot(lo, ones_bd)


def _rw_prep_kernel(seq_len, zr_ref, zk_ref, zv_ref, zl_ref, pr_ref, pk_ref, pv_ref, plr_ref,
                    mu_r, mu_k, mu_v, mu_l, w0, w_up, a0, a_up, g_up, k_k, k_a, ones_bd,
                    r_o, k_o, v_o, al_o, be_o, lw_o, g_o):
    tm = zr_ref.shape[0]
    first = (pl.program_id(0) * tm) % seq_len == 0
    row0 = lax.broadcasted_iota(jnp.int32, (tm, 1), 0) == 0

    def shift(z_ref, p_ref, mu):
        z = z_ref[...].astype(F32)
        last = jnp.where(first, 0.0, p_ref[...].astype(F32)[15:16, :])
        prev = jnp.where(row0, last, pltpu.roll(z, 1, axis=0))
        return z + (prev - z) * mu[...]

    r = shift(zr_ref, pr_ref, mu_r)
    k = shift(zk_ref, pk_ref, mu_k)
    v = shift(zv_ref, pv_ref, mu_v)
    xl = shift(zl_ref, plr_ref, mu_l)
    xw = jnp.tanh(xl[:, :RW_DECAY_LORA]).astype(BF16)
    xa = xl[:, RW_DECAY_LORA:RW_DECAY_LORA + RW_AAA_LORA].astype(BF16)
    xg = jax.nn.sigmoid(xl[:, RW_DECAY_LORA + RW_AAA_LORA:]).astype(BF16)
    lw = -math.exp(-0.5) * jax.nn.sigmoid(w0[...] + _dot(xw, w_up[...]))
    a = jax.nn.sigmoid(a0[...] + _dot(xa, a_up[...]))
    g = _dot(xg, g_up[...])
    kk = k * k_k[...]
    ss = _head_sum(kk * kk, ones_bd[...])
    kk = kk * lax.rsqrt(jnp.maximum(ss, 1e-24))
    k = k * (1.0 + (a - 1.0) * k_a[...])
    r_o[...] = r.astype(BF16)
    k_o[...] = k.astype(BF16)
    v_o[...] = v.astype(BF16)
    al_o[...] = (-kk).astype(BF16)
    be_o[...] = (kk * a).astype(BF16)
    lw_o[...] = lw
    g_o[...] = g.astype(BF16)


def _rw_prep(z, seq_len, mu, w0, w_up, a0, a_up, g_up, k_k, k_a, ones_bd):
    t = z.shape[0]
    tm = min(seq_len, 512)
    c512 = lambda j: pl.BlockSpec((tm, RW_DIM), lambda i: (i, OFF_RW // RW_DIM + j))
    prev = lambda w, jb: pl.BlockSpec((16, w), lambda i: (jnp.maximum(i * (tm // 16) - 1, 0), jb))
    mu_r, mu_k, mu_v, mu_l = (mu[:, :RW_DIM], mu[:, RW_DIM:2 * RW_DIM], mu[:, 2 * RW_DIM:3 * RW_DIM],
                              mu[:, 3 * RW_DIM:])
    consts = [mu_r, mu_k, mu_v, mu_l, w0, w_up, a0, a_up, g_up, k_k, k_a, ones_bd]
    out = lambda dt: jax.ShapeDtypeStruct((t, RW_DIM), dt)
    return pl.pallas_call(
        functools.partial(_rw_prep_kernel, seq_len),
        grid=(t // tm,),
        in_specs=[
            c512(0), c512(1), c512(2),
            pl.BlockSpec((tm, RW_LORA), lambda i: (i, OFF_LORA // RW_LORA)),
            prev(RW_DIM, OFF_RW // RW_DIM), prev(RW_DIM, OFF_RW // RW_DIM + 1),
            prev(RW_DIM, OFF_RW // RW_DIM + 2), prev(RW_LORA, OFF_LORA // RW_LORA),
        ] + [_const_spec(c.shape) for c in consts],
        out_specs=[pl.BlockSpec((tm, RW_DIM), lambda i: (i, 0))] * 7,
        out_shape=[out(BF16), out(BF16), out(BF16), out(BF16), out(BF16), out(F32), out(BF16)],
        compiler_params=_cparams("parallel"),
        name="rw_prep",
    )(z, z, z, z, z, z, z, z, *consts)


RW_GROUP = 256


def _rw_chunk_kernel(rows, r_ref, k_ref, v_ref, al_ref, be_ref, lw_ref, g_ref, rk_ref, lnw_ref, lnb_ref,
                     bd_ref, o_ref, s_ref):
    L = CHUNK
    W = RW_GROUP
    reps = W // RW_HEAD

    @pl.when(pl.program_id(1) == 0)
    def _():
        s_ref[...] = jnp.zeros(s_ref.shape, F32)

    bdm = bd_ref[...]

    def bd(x):
        return jnp.concatenate([x.astype(BF16)] * reps, axis=0) * bdm

    row = lax.broadcasted_iota(jnp.int32, (L, W), 0)
    sub = lax.broadcasted_iota(jnp.int32, (L, W), 1) % RW_HEAD
    strict = sub < row
    incl = sub <= row
    eye = (sub == row).astype(F32)
    tri = (lax.broadcasted_iota(jnp.int32, (L, L), 1)
           <= lax.broadcasted_iota(jnp.int32, (L, L), 0)).astype(BF16)
    inv = 1.0 / RW_HEAD

    def instance(b, gi):
        sl = slice(gi * W, (gi + 1) * W)
        lw = lw_ref[b, :, sl]
        lw_hi, lw_lo = _split_bf16(lw)
        cum = _dot(tri, lw_hi) + _dot(tri, lw_lo)
        yield
        cum_l = cum[L - 1:L, :]
        r = r_ref[b, :, sl].astype(F32)
        k = k_ref[b, :, sl].astype(F32)
        v = v_ref[b, :, sl].astype(F32)
        be = be_ref[b, :, sl].astype(F32)
        e_neg = jnp.exp(-cum)
        e_tail = jnp.exp(cum_l - cum)
        ar = jnp.concatenate([al_ref[b, :, sl].astype(F32) * jnp.exp(cum - lw), r * jnp.exp(cum)],
                             axis=0).astype(BF16)
        s0 = s_ref[b, gi]
        ar_s = _dot_nt(ar, s0.astype(BF16))
        a_b = _dot_nt(ar, bd(be * e_neg))
        a_k = _dot_nt(ar, bd(k * e_neg))
        yield
        n = jnp.where(strict, a_b[:L], 0.0)
        a_ak = jnp.where(strict, a_k[:L], 0.0)
        a_rb = jnp.where(incl, a_b[L:], 0.0)
        a_rk = jnp.where(incl, a_k[L:], 0.0)
        p = eye + n
        nk = _dot(n.astype(BF16), bd(n))
        bd_v = bd(v)
        x0 = _dot(a_ak.astype(BF16), bd_v)
        yield
        steps = int(math.log2(L)) - 1
        for it in range(steps):
            m = bd(nk)
            if it + 1 < steps:
                res = _dot(jnp.concatenate([p, nk], axis=0).astype(BF16), m)
                p = p + res[:L]
                nk = res[L:]
            else:
                p = p + _dot(p.astype(BF16), m)
            yield
        u = _dot(p.astype(BF16), bd(ar_s[:L] + x0))
        yield
        y = ar_s[L:] + _dot(jnp.concatenate([a_rb, a_rk], axis=1).astype(BF16),
                            jnp.concatenate([bd(u), bd_v], axis=0))
        uv = jnp.concatenate([u, v], axis=0).astype(BF16)
        bk = jnp.concatenate([be * e_tail, k * e_tail], axis=0).astype(BF16)
        s_ref[b, gi] = s0 * jnp.exp(cum_l) + _dot_tn(uv, bk) * bdm.astype(F32)
        yield
        hi, lo = _split_bf16(jnp.concatenate([y, r * k * rk_ref[:, sl]], axis=0))
        sums = _dot(jnp.concatenate([hi, lo], axis=0), bdm)
        sums = sums[:2 * L] + sums[2 * L:]
        yield
        d = y - sums[:L] * inv
        hi, lo = _split_bf16(d * d)
        var = _dot(jnp.concatenate([hi, lo], axis=0), bdm)
        var = (var[:L] + var[L:]) * inv
        yield
        yn = d * lax.rsqrt(var + RW_LN_EPS) * lnw_ref[:, sl] + lnb_ref[:, sl]
        out = (yn + sums[L:] * v) * g_ref[b, :, sl].astype(F32)
        o_ref[b, :, sl] = out.astype(o_ref.dtype)

    live = [instance(b, gi) for b in range(rows) for gi in range(RW_DIM // W)]
    while live:
        live = [g for g in live if next(g, True) is None]


def _rw_chunk(r, k, v, al, be, lw, g, b, r_k, ln_w, ln_b):
    t = r.shape[0]
    s = t // b
    rows = math.gcd(b, 4)
    gid = jnp.arange(RW_GROUP) // RW_HEAD
    ones_bd = (gid[:, None] == gid[None, :]).astype(BF16)
    blk = pl.BlockSpec((rows, CHUNK, RW_DIM), lambda bb, c: (bb, c, 0))
    consts = [r_k, ln_w, ln_b, ones_bd]
    seq = [a.reshape(b, s, RW_DIM) for a in (r, k, v, al, be, lw, g)]
    return pl.pallas_call(
        functools.partial(_rw_chunk_kernel, rows),
        grid=(b // rows, s // CHUNK),
        in_specs=[blk] * 7 + [_const_spec(c.shape) for c in consts],
        out_specs=blk,
        out_shape=jax.ShapeDtypeStruct((b, s, RW_DIM), BF16),
        scratch_shapes=[pltpu.VMEM((rows, RW_DIM // RW_GROUP, RW_GROUP, RW_GROUP), F32)],
        compiler_params=_cparams("parallel", "arbitrary"),
        name="rw_chunk",
    )(*seq, *consts).reshape(t, RW_DIM)


def _merge_kernel(om_ref, or_ref, oc_ref, zg_ref, h_ref, wb_ref, wo_ref, g_ref, o_ref):
    merged = None
    for n, b_ref in enumerate((om_ref, or_ref, oc_ref)):
        y = _dot(b_ref[...], wb_ref[n * BRANCH_DIM:(n + 1) * BRANCH_DIM, :])
        gate = jax.nn.sigmoid(zg_ref[:, n * D_MODEL:(n + 1) * D_MODEL].astype(F32))
        merged = gate * y if merged is None else merged + gate * y
    out = _dot(merged.astype(BF16), wo_ref[...])
    o_ref[...] = h_ref[...] + _rms(out, g_ref[...])


def _merge(o_mla, o_rw, o_ca, z, h, w_branch, w_out, g):
    t = h.shape[0]
    tm = min(t, 512)
    row = lambda w: pl.BlockSpec((tm, w), lambda i: (i, 0))
    return pl.pallas_call(
        _merge_kernel,
        grid=(t // tm,),
        in_specs=[
            row(BRANCH_DIM), row(BRANCH_DIM), row(BRANCH_DIM),
            pl.BlockSpec((tm, GATE_COLS), lambda i: (i, OFF_GATE // GATE_COLS)),
            row(D_MODEL),
            _const_spec(w_branch.shape), _const_spec(w_out.shape), _const_spec(g.shape),
        ],
        out_specs=row(D_MODEL),
        out_shape=jax.ShapeDtypeStruct((t, D_MODEL), F32),
        compiler_params=_cparams("parallel"),
        name="merge",
    )(o_mla, o_rw, o_ca, z, h, w_branch, w_out, g)


FF_SPLIT = 4


def _ff_ple_kernel(h_ref, p_ref, g1_ref, w1_ref, w2_ref, g2_ref, wg_ref, wp_ref, o_ref):
    h = h_ref[...]
    f = _rms(h, g1_ref[...]).astype(BF16)
    cw = D_FF // FF_SPLIT
    acc = None
    for c in range(FF_SPLIT):
        a = jnp.maximum(_dot(f, w1_ref[:, c * cw:(c + 1) * cw]), 0.0)
        part = _dot((a * a).astype(BF16), w2_ref[c * cw:(c + 1) * cw, :])
        acc = part if acc is None else acc + part
    h = h + _rms(acc, g2_ref[...])
    gate = jax.nn.sigmoid(_dot(h.astype(BF16), wg_ref[...]))
    o_ref[...] = h + gate * _dot(p_ref[...].astype(BF16), wp_ref[...])


def _ff_ple(h, p, g1, w1, w2, g2, wg, wp):
    t = h.shape[0]
    tm = min(t, 512)
    row = lambda w: pl.BlockSpec((tm, w), lambda i: (i, 0))
    consts = [g1, w1, w2, g2, wg, wp]
    return pl.pallas_call(
        _ff_ple_kernel,
        grid=(t // tm,),
        in_specs=[row(D_MODEL), row(D_PLE)] + [_const_spec(c.shape) for c in consts],
        out_specs=row(D_MODEL),
        out_shape=jax.ShapeDtypeStruct((t, D_MODEL), F32),
        compiler_params=_cparams("parallel"),
        name="ff_ple",
    )(h, p, *consts)


def _rot_half_cols(w):
    half = w.shape[1] // 2
    return jnp.concatenate([-w[:, half:], w[:, :half]], axis=1)


def _pack_w_in(w):
    d = w.shape[0]
    mla_cols = MLA_Q_RANK + MLA_KV_RANK + MLA_ROPE
    rw_cols = 3 * RW_DIM + RW_LORA
    w_mla = w[:, :mla_cols]
    w_rw = w[:, mla_cols:mla_cols + rw_cols]
    w_ca = w[:, mla_cols + rw_cols:mla_cols + rw_cols + 3 * CA_DIM]
    w_gate = w[:, mla_cols + rw_cols + 3 * CA_DIM:]
    w_kr = w_mla[:, MLA_Q_RANK + MLA_KV_RANK:]
    z64 = jnp.zeros((d, 64), w.dtype)
    packed = jnp.concatenate(
        [w_gate, w_ca, w_rw, w_mla[:, :MLA_Q_RANK + MLA_KV_RANK], w_kr, z64, _rot_half_cols(w_kr), z64], axis=1)
    return packed.astype(BF16)


def _pack_w_uq(w):
    r = w.shape[0]
    w = w.reshape(r, MLA_HEADS, MLA_QK)
    z64 = jnp.zeros((r, MLA_HEADS, 64), w.dtype)
    rope = w[:, :, MLA_NOPE:]
    rot = jnp.concatenate([-rope[:, :, MLA_ROPE // 2:], rope[:, :, :MLA_ROPE // 2]], axis=2)
    wq = jnp.concatenate([w, z64], axis=2).reshape(r, MLA_HEADS * MLA_HEAD_PAD)
    wqr = jnp.concatenate([rot, z64], axis=2).reshape(r, MLA_HEADS * 128)
    return wq.T.astype(BF16), wqr.T.astype(BF16)


def _pack_w_ukv(w):
    r = w.shape[0]
    w = w.reshape(r, MLA_HEADS, MLA_NOPE + MLA_V)
    wk = w[:, :, :MLA_NOPE].reshape(r, MLA_HEADS * MLA_NOPE)
    wv = w[:, :, MLA_NOPE:].reshape(r, MLA_HEADS * MLA_V)
    return wk.astype(BF16), wv.T.astype(BF16)


def kernel(x, p, positions, pre_mix_g, w_in, mla_q_norm_g, mla_kv_norm_g, mla_w_uq, mla_w_ukv, rw_mu, rw_w0, rw_w_up, rw_a0, rw_a_up, rw_g_up, rw_k_k, rw_k_a, rw_r_k, rw_ln_w, rw_ln_b, ca_rel_bias, w_branch, w_out, post_mix_g, pre_ff_g, w_ff1, w_ff2, post_ff_g, w_ple_gate, w_ple_proj):
    b, s, d = x.shape
    t = b * s
    depth = w_in.shape[0]
    row = lambda a: a.reshape(1, -1).astype(F32)
    head_id = jnp.arange(RW_DIM) // RW_HEAD
    ones_bd = (head_id[:, None] == head_id[None, :]).astype(BF16)
    cs, sn, cst, snt = _rope_table(positions)
    h = x.reshape(t, d)
    for i in range(depth):
        z = _in_proj(h, row(pre_mix_g[i]), _pack_w_in(w_in[i]))
        wqt, wqrt = _pack_w_uq(mla_w_uq[i])
        wk, wvt = _pack_w_ukv(mla_w_ukv[i])
        qt, kn, kr, vt = _mla_proj(z, cs, sn, cst, snt, row(mla_q_norm_g[i]), row(mla_kv_norm_g[i]),
                                   wqt, wqrt, wk, wvt)
        o_mla = _mla_attn(qt, kn, kr, vt, b).reshape(t, -1)
        rr, rk, rv, al, be, lw, rg = _rw_prep(
            z, s, row(rw_mu[i]), row(rw_w0[i]), rw_w_up[i].astype(BF16), row(rw_a0[i]),
            rw_a_up[i].astype(BF16), rw_g_up[i].astype(BF16), row(rw_k_k[i]), row(rw_k_a[i]), ones_bd)
        o_rw = _rw_chunk(rr, rk, rv, al, be, lw, rg, b, row(rw_r_k[i]), row(rw_ln_w[i]), row(rw_ln_b[i]))
        z3 = z.reshape(b, s, -1)
        pad = lambda a: jnp.pad(a, ((0, 0), (CA_PAD, 0), (0, 0)))
        kp = pad(z3[:, :, OFF_CA + CA_DIM:OFF_CA + 2 * CA_DIM])
        vp = pad(z3[:, :, OFF_CA + 2 * CA_DIM:OFF_CA + 3 * CA_DIM])
        o_ca = _ca_attn(z3, kp, vp, _ca_bias_table(ca_rel_bias[i])).reshape(t, -1)
        h = _merge(o_mla, o_rw, o_ca, z, h, w_branch[i].astype(BF16), w_out[i].astype(BF16),
                   row(post_mix_g[i]))
        h = _ff_ple(h, p[i].reshape(t, -1), row(pre_ff_g[i]), w_ff1[i].astype(BF16), w_ff2[i].astype(BF16),
                    row(post_ff_g[i]), w_ple_gate[i].astype(BF16), w_ple_proj[i].astype(BF16))
    return h.reshape(b, s, d)
```

```python
import functools
import math

import jax
import jax.numpy as jnp
from jax import lax
from jax.experimental import pallas as pl
from jax.experimental.pallas import tpu as pltpu

F32 = jnp.float32
BF16 = jnp.bfloat16

D_MODEL = 1024
D_PLE = 256
D_FF = 4 * D_MODEL
NORM_EPS = 1e-6
CHUNK = 64

MLA_HEADS = 4
MLA_NOPE = 128
MLA_ROPE = 64
MLA_V = 128
MLA_Q_RANK = 256
MLA_KV_RANK = 128
ROPE_THETA = 10000.0
MLA_QK = MLA_NOPE + MLA_ROPE
MLA_HEAD_PAD = 256

RW_HEADS = 8
RW_HEAD = 64
RW_DIM = RW_HEADS * RW_HEAD
RW_DECAY_LORA = 64
RW_AAA_LORA = 64
RW_GATE_LORA = 128
RW_LORA = RW_DECAY_LORA + RW_AAA_LORA + RW_GATE_LORA
RW_LN_EPS = 64e-5

CA_HEADS = 8
CA_HEAD = 64
CA_DIM = CA_HEADS * CA_HEAD
CA_LEFT_CHUNKS = 8
CA_PAD = CA_LEFT_CHUNKS * CHUNK
REL_MIN = -(CHUNK - 1)
REL_MAX = 256

N_BRANCH = 3
BRANCH_DIM = 512
GATE_COLS = N_BRANCH * D_MODEL

OFF_GATE = 0
OFF_CAK = GATE_COLS
OFF_RW = OFF_CAK + CA_DIM
OFF_LORA = OFF_RW + 3 * RW_DIM
OFF_ZQ = OFF_LORA + RW_LORA
OFF_ZKV = OFF_ZQ + MLA_Q_RANK
OFF_KR = OFF_ZKV + MLA_KV_RANK
OFF_KRR = OFF_KR + 128
IN_PACKED = OFF_KRR + 128
LOG2E = math.log2(math.e)

VMEM_LIMIT = 56 * 1024 * 1024


def _cparams(*sem):
    return pltpu.CompilerParams(dimension_semantics=sem, vmem_limit_bytes=VMEM_LIMIT)


def _rms(x, g):
    return x * lax.rsqrt(jnp.mean(x * x, axis=-1, keepdims=True) + NORM_EPS) * g


def _dot(a, b):
    return jnp.dot(a, b, preferred_element_type=F32)


def _dot_nt(a, b):
    return lax.dot_general(a, b, (((1,), (1,)), ((), ())), preferred_element_type=F32)


def _dot_tn(a, b):
    return lax.dot_general(a, b, (((0,), (0,)), ((), ())), preferred_element_type=F32)


def _const_spec(shape):
    nd = len(shape)
    return pl.BlockSpec(shape, lambda *_: (0,) * nd, pipeline_mode=pl.Buffered(1))


def _rope_table_kernel(pos_c_ref, pos_r_ref, freq_r_ref, freq_c_ref, cs_ref, sn_ref, cst_ref, snt_ref):
    ang = pos_c_ref[...] * freq_r_ref[...]
    live = lax.broadcasted_iota(jnp.int32, ang.shape, 1) < MLA_ROPE
    cs_ref[...] = jnp.where(live, jnp.cos(ang), 0.0)
    sn_ref[...] = jnp.where(live, jnp.sin(ang), 0.0)
    ang_t = freq_c_ref[...] * pos_r_ref[...]
    live_t = lax.broadcasted_iota(jnp.int32, ang_t.shape, 0) < MLA_ROPE
    cst_ref[...] = jnp.where(live_t, jnp.cos(ang_t), 0.0)
    snt_ref[...] = jnp.where(live_t, jnp.sin(ang_t), 0.0)


def _rope_table(positions):
    t = positions.size
    tm = min(t, 2048)
    half = MLA_ROPE // 2
    inv_freq = 1.0 / (ROPE_THETA ** (jnp.arange(half, dtype=F32) / half))
    freq = jnp.concatenate([inv_freq, inv_freq, jnp.zeros((128 - MLA_ROPE,), F32)])
    pos = positions.astype(F32)
    return pl.pallas_call(
        _rope_table_kernel,
        grid=(t // tm,),
        in_specs=[pl.BlockSpec((tm, 1), lambda i: (i, 0)), pl.BlockSpec((1, tm), lambda i: (0, i)),
                  _const_spec((1, 128)), _const_spec((128, 1))],
        out_specs=[pl.BlockSpec((tm, 128), lambda i: (i, 0))] * 2 + [pl.BlockSpec((128, tm), lambda i: (0, i))] * 2,
        out_shape=[jax.ShapeDtypeStruct((t, 128), F32)] * 2 + [jax.ShapeDtypeStruct((128, t), F32)] * 2,
        compiler_params=_cparams("parallel"),
        name="rope_table",
    )(pos.reshape(t, 1), pos.reshape(1, t), freq[None, :], freq[:, None])


IN_CHUNK = 1536


def _in_proj_kernel(x_ref, g_ref, w_ref, wt_ref, o_ref, ot_ref):
    xn = _rms(x_ref[...], g_ref[...]).astype(BF16)
    n = w_ref.shape[1]
    for c in range(0, n, IN_CHUNK):
        e = min(c + IN_CHUNK, n)
        o_ref[:, c:e] = _dot(xn, w_ref[:, c:e]).astype(o_ref.dtype)
    ot_ref[...] = _dot_nt(wt_ref[...], xn).astype(ot_ref.dtype)


def _in_proj(h, g, w, wt):
    t, d = h.shape
    n = w.shape[1]
    nt = wt.shape[0]
    tm = min(t, 512)
    return pl.pallas_call(
        _in_proj_kernel,
        grid=(t // tm,),
        in_specs=[pl.BlockSpec((tm, d), lambda i: (i, 0)), _const_spec((1, d)), _const_spec(w.shape),
                  _const_spec(wt.shape)],
        out_specs=[pl.BlockSpec((tm, n), lambda i: (i, 0)), pl.BlockSpec((nt, tm), lambda i: (0, i))],
        out_shape=[jax.ShapeDtypeStruct((t, n), BF16), jax.ShapeDtypeStruct((nt, t), BF16)],
        compiler_params=_cparams("parallel"),
        name="in_proj",
    )(h, g, w, wt)


def _mla_proj_kernel(zq_ref, zkv_ref, zkr_ref, zkrr_ref, cs_ref, sn_ref, cst_ref, snt_ref, gq_ref, gkv_ref,
                     wqt_ref, wqrt_ref, wk_ref, wvt_ref, qt_ref, k_ref, kr_ref, vt_ref):
    cst = cst_ref[...]
    snt = snt_ref[...]
    qn = _rms(zq_ref[...].astype(F32), gq_ref[...]).astype(BF16)
    qt = _dot_nt(wqt_ref[...], qn)
    qrt = _dot_nt(wqrt_ref[...], qn)
    scale = MLA_QK ** -0.5 * LOG2E
    for h in range(MLA_HEADS):
        o = h * MLA_HEAD_PAD
        qt_ref[o:o + 128, :] = (qt[o:o + 128] * scale).astype(BF16)
        rope = qt[o + 128:o + 256] * cst + qrt[h * 128:(h + 1) * 128] * snt
        qt_ref[o + 128:o + 256, :] = (rope * scale).astype(BF16)
    kvn = _rms(zkv_ref[...].astype(F32), gkv_ref[...]).astype(BF16)
    k_ref[...] = _dot(kvn, wk_ref[...]).astype(BF16)
    vt_ref[...] = _dot_nt(wvt_ref[...], kvn).astype(BF16)
    kr_ref[...] = (zkr_ref[...].astype(F32) * cs_ref[...] + zkrr_ref[...].astype(F32) * sn_ref[...]).astype(BF16)


def _mla_proj(z, cs, sn, cst, snt, gq, gkv, wqt, wqrt, wk, wvt):
    t = z.shape[0]
    tm = min(t, 1024)
    hv = MLA_HEADS * MLA_V
    hq = MLA_HEADS * MLA_HEAD_PAD
    row = lambda w: pl.BlockSpec((tm, w), lambda i: (i, 0))
    col = lambda w: pl.BlockSpec((w, tm), lambda i: (0, i))
    consts = [gq, gkv, wqt, wqrt, wk, wvt]
    return pl.pallas_call(
        _mla_proj_kernel,
        grid=(t // tm,),
        in_specs=[pl.BlockSpec((tm, MLA_Q_RANK), lambda i: (i, OFF_ZQ // MLA_Q_RANK)),
                  pl.BlockSpec((tm, MLA_KV_RANK), lambda i: (i, OFF_ZKV // MLA_KV_RANK)),
                  pl.BlockSpec((tm, 128), lambda i: (i, OFF_KR // 128)),
                  pl.BlockSpec((tm, 128), lambda i: (i, OFF_KRR // 128)),
                  row(128), row(128), col(128), col(128)] + [_const_spec(c.shape) for c in consts],
        out_specs=[col(hq), row(hv), row(128), col(hv)],
        out_shape=[
            jax.ShapeDtypeStruct((hq, t), BF16),
            jax.ShapeDtypeStruct((t, hv), BF16),
            jax.ShapeDtypeStruct((t, 128), BF16),
            jax.ShapeDtypeStruct((hv, t), BF16),
        ],
        compiler_params=_cparams("parallel"),
        name="mla_proj",
    )(z, z, z, z, cs, sn, cst, snt, *consts)


MLA_TQ = 512


def _mla_attn_kernel(qt_ref, k_ref, kr_ref, vt_ref, o_ref, m_ref, l_ref, acc_ref):
    i = pl.program_id(1)
    tq = MLA_TQ
    m_ref[...] = jnp.full(m_ref.shape, -1e30, F32)
    l_ref[...] = jnp.zeros(l_ref.shape, F32)
    acc_ref[...] = jnp.zeros(acc_ref.shape, F32)

    def step(j, masked):
        start = pl.multiple_of(j * tq, tq)
        kr = kr_ref[0, pl.ds(start, tq), :]

        def scores(h):
            kh = jnp.concatenate([k_ref[0, pl.ds(start, tq), h * 128:(h + 1) * 128], kr], axis=1)
            s = _dot(kh, qt_ref[h * MLA_HEAD_PAD:(h + 1) * MLA_HEAD_PAD, :])
            if masked:
                kc = lax.broadcasted_iota(jnp.int32, s.shape, 0) // CHUNK
                qc = lax.broadcasted_iota(jnp.int32, s.shape, 1) // CHUNK
                s = jnp.where(kc <= qc, s, -1e30)
            return s

        def update(h, s):
            m_prev = m_ref[h]
            m_new = jnp.maximum(m_prev, jnp.max(s, axis=0, keepdims=True))
            a = jnp.exp2(m_prev - m_new)
            p = jnp.exp2(s - m_new)
            l_ref[h] = a * l_ref[h] + jnp.sum(p, axis=0, keepdims=True)
            vh = vt_ref[h * MLA_V:(h + 1) * MLA_V, pl.ds(start, tq)]
            acc_ref[h] = a * acc_ref[h] + _dot(vh, p.astype(BF16))
            m_ref[h] = m_new

        s_cur = scores(0)
        for h in range(1, MLA_HEADS):
            s_next = scores(h)
            update(h - 1, s_cur)
            s_cur = s_next
        update(MLA_HEADS - 1, s_cur)

    def body(j, c):
        step(j, False)
        return c

    lax.fori_loop(0, i, body, 0)
    step(i, True)
    for h in range(MLA_HEADS):
        o_ref[0, :, h * MLA_V:(h + 1) * MLA_V] = (acc_ref[h] / l_ref[h]).T.astype(o_ref.dtype)


def _mla_attn(qt, k, kr, vt, b):
    hq, t = qt.shape
    s = t // b
    tq = MLA_TQ
    nq = s // tq
    hv = MLA_HEADS * MLA_V
    full = lambda w: pl.BlockSpec((1, s, w), lambda bi, i: (bi, 0, 0))
    return pl.pallas_call(
        _mla_attn_kernel,
        grid=(b, nq),
        in_specs=[
            pl.BlockSpec((hq, tq), lambda bi, i: (0, bi * nq + i)),
            full(hv), full(128),
            pl.BlockSpec((hv, s), lambda bi, i: (0, bi)),
        ],
        out_specs=pl.BlockSpec((1, tq, hv), lambda bi, i: (bi, i, 0)),
        out_shape=jax.ShapeDtypeStruct((b, s, hv), BF16),
        scratch_shapes=[pltpu.VMEM((MLA_HEADS, 1, tq), F32), pltpu.VMEM((MLA_HEADS, 1, tq), F32),
                        pltpu.VMEM((MLA_HEADS, MLA_V, tq), F32)],
        compiler_params=_cparams("parallel", "arbitrary"),
        name="mla_attn",
    )(qt, k.reshape(b, s, hv), kr.reshape(b, s, 128), vt)


CA_TQ = 256
CA_BAND_BLK = CA_TQ + CA_PAD


def _ca_attn_kernel(qt_ref, k_ref, vt_ref, bias_ref, o_ref):
    i = pl.program_id(1)
    start = pl.multiple_of(i * CA_TQ, CA_TQ)
    key = lax.broadcasted_iota(jnp.int32, (CA_BAND_BLK, CA_TQ), 0)
    in_seq = key >= CA_PAD - start
    lo = lax.broadcasted_iota(jnp.int32, (128, CA_TQ), 0) < CA_HEAD
    zero = jnp.zeros((), BF16)

    def scores(h):
        sl = slice((h // 2) * 128, (h // 2 + 1) * 128)
        k2 = k_ref[0, pl.ds(start, CA_BAND_BLK), sl]
        qh = jnp.where(lo if h % 2 == 0 else ~lo, qt_ref[sl, :], zero)
        s = _dot(k2, qh) + bias_ref[h]
        return jnp.where(in_seq, s, -1e30)

    def attend(h, s):
        p = jnp.exp2(s - jnp.max(s, axis=0, keepdims=True))
        l = jnp.sum(p, axis=0, keepdims=True)
        vh = vt_ref[h * CA_HEAD:(h + 1) * CA_HEAD, pl.ds(start, CA_BAND_BLK)]
        return _dot(vh, p.astype(BF16)) / l

    outs = []
    s_cur = scores(0)
    for h in range(1, CA_HEADS):
        s_next = scores(h)
        outs.append(attend(h - 1, s_cur))
        s_cur = s_next
    outs.append(attend(CA_HEADS - 1, s_cur))
    for pair in range(CA_HEADS // 2):
        both = jnp.concatenate(outs[2 * pair:2 * pair + 2], axis=0)
        o_ref[0, :, pair * 128:(pair + 1) * 128] = both.T.astype(o_ref.dtype)


def _ca_attn(zt, kp, vtp, bias, b):
    t = zt.shape[1]
    s = t // b
    sp = s + CA_PAD
    nq = s // CA_TQ
    return pl.pallas_call(
        _ca_attn_kernel,
        grid=(b, nq),
        in_specs=[
            pl.BlockSpec((CA_DIM, CA_TQ), lambda bi, i: (0, bi * nq + i)),
            pl.BlockSpec((1, sp, CA_DIM), lambda bi, i: (bi, 0, 0)),
            pl.BlockSpec((CA_DIM, sp), lambda bi, i: (0, bi)),
            _const_spec(bias.shape),
        ],
        out_specs=pl.BlockSpec((1, CA_TQ, CA_DIM), lambda bi, i: (bi, i, 0)),
        out_shape=jax.ShapeDtypeStruct((b, s, CA_DIM), BF16),
        compiler_params=_cparams("parallel", "arbitrary"),
        name="ca_attn",
    )(zt, kp, vtp, bias)


def _ca_bias_table(rel_bias):
    width = CA_TQ + CA_BAND_BLK
    m = jnp.arange(width)
    delta = jnp.where(m < CA_BAND_BLK, m, m - width)
    idx = jnp.clip(CA_PAD - delta, REL_MIN, REL_MAX) - REL_MIN
    t1 = rel_bias.astype(F32)[idx].T
    flat = jnp.tile(t1, (1, CA_TQ))[:, :CA_TQ * (width - 1)]
    bias = flat.reshape(CA_HEADS, CA_TQ, width - 1)[:, :, :CA_BAND_BLK]
    rc = jnp.arange(CA_TQ)[:, None] // CHUNK
    cc = jnp.arange(CA_BAND_BLK)[None, :] // CHUNK
    ok = (cc >= rc) & (cc <= rc + CA_LEFT_CHUNKS)
    return jnp.where(ok[None], bias * LOG2E, -1e30).transpose(0, 2, 1)


def _split_bf16(x):
    hi = x.astype(BF16)
    lo = (x - hi.astype(F32)).astype(BF16)
    return hi, lo


def _head_sum(x, ones_bd):
    hi, lo = _split_bf16(x)
    return _dot(hi, ones_bd) + _dot(lo, ones_bd)


def _rw_prep_kernel(seq_len, zr_ref, zk_ref, zv_ref, zl_ref, pr_ref, pk_ref, pv_ref, plr_ref,
                    mu_r, mu_k, mu_v, mu_l, w0, w_up, a0, a_up, g_up, k_k, k_a, ones_bd,
                    r_o, k_o, v_o, al_o, be_o, lw_o, g_o):
    tm = zr_ref.shape[0]
    first = (pl.program_id(0) * tm) % seq_len == 0
    row0 = lax.broadcasted_iota(jnp.int32, (tm, 1), 0) == 0

    def shift(z_ref, p_ref, mu):
        z = z_ref[...].astype(F32)
        last = jnp.where(first, 0.0, p_ref[...].astype(F32)[15:16, :])
        prev = jnp.where(row0, last, pltpu.roll(z, 1, axis=0))
        return z + (prev - z) * mu[...]

    r = shift(zr_ref, pr_ref, mu_r)
    k = shift(zk_ref, pk_ref, mu_k)
    v = shift(zv_ref, pv_ref, mu_v)
    xl = shift(zl_ref, plr_ref, mu_l)
    xw = jnp.tanh(xl[:, :RW_DECAY_LORA]).astype(BF16)
    xa = xl[:, RW_DECAY_LORA:RW_DECAY_LORA + RW_AAA_LORA].astype(BF16)
    xg = jax.nn.sigmoid(xl[:, RW_DECAY_LORA + RW_AAA_LORA:]).astype(BF16)
    lw = -math.exp(-0.5) * jax.nn.sigmoid(w0[...] + _dot(xw, w_up[...]))
    a = jax.nn.sigmoid(a0[...] + _dot(xa, a_up[...]))
    g = _dot(xg, g_up[...])
    kk = k * k_k[...]
    ss = _head_sum(kk * kk, ones_bd[...])
    kk = kk * lax.rsqrt(jnp.maximum(ss, 1e-24))
    k = k * (1.0 + (a - 1.0) * k_a[...])
    r_o[...] = r.astype(BF16)
    k_o[...] = k.astype(BF16)
    v_o[...] = v.astype(BF16)
    al_o[...] = (-kk).astype(BF16)
    be_o[...] = (kk * a).astype(BF16)
    lw_o[...] = lw
    g_o[...] = g.astype(BF16)


def _rw_prep(z, seq_len, mu, w0, w_up, a0, a_up, g_up, k_k, k_a, ones_bd):
    t = z.shape[0]
    tm = min(seq_len, 512)
    c512 = lambda j: pl.BlockSpec((tm, RW_DIM), lambda i: (i, OFF_RW // RW_DIM + j))
    prev = lambda w, jb: pl.BlockSpec((16, w), lambda i: (jnp.maximum(i * (tm // 16) - 1, 0), jb))
    mu_r, mu_k, mu_v, mu_l = (mu[:, :RW_DIM], mu[:, RW_DIM:2 * RW_DIM], mu[:, 2 * RW_DIM:3 * RW_DIM],
                              mu[:, 3 * RW_DIM:])
    consts = [mu_r, mu_k, mu_v, mu_l, w0, w_up, a0, a_up, g_up, k_k, k_a, ones_bd]
    out = lambda dt: jax.ShapeDtypeStruct((t, RW_DIM), dt)
    return pl.pallas_call(
        functools.partial(_rw_prep_kernel, seq_len),
        grid=(t // tm,),
        in_specs=[
            c512(0), c512(1), c512(2),
            pl.BlockSpec((tm, RW_LORA), lambda i: (i, OFF_LORA // RW_LORA)),
            prev(RW_DIM, OFF_RW // RW_DIM), prev(RW_DIM, OFF_RW // RW_DIM + 1),
            prev(RW_DIM, OFF_RW // RW_DIM + 2), prev(RW_LORA, OFF_LORA // RW_LORA),
        ] + [_const_spec(c.shape) for c in consts],
        out_specs=[pl.BlockSpec((tm, RW_DIM), lambda i: (i, 0))] * 7,
        out_shape=[out(BF16), out(BF16), out(BF16), out(BF16), out(BF16), out(F32), out(BF16)],
        compiler_params=_cparams("parallel"),
        name="rw_prep",
    )(z, z, z, z, z, z, z, z, *consts)


RW_GROUP = 256


def _rw_chunk_kernel(rows, r_ref, k_ref, v_ref, al_ref, be_ref, lw_ref, g_ref, rk_ref, lnw_ref, lnb_ref,
                     bd_ref, o_ref, s_ref):
    L = CHUNK
    W = RW_GROUP
    reps = W // RW_HEAD

    @pl.when(pl.program_id(1) == 0)
    def _():
        s_ref[...] = jnp.zeros(s_ref.shape, F32)

    bdm = bd_ref[...]

    def bd(x):
        return jnp.concatenate([x.astype(BF16)] * reps, axis=0) * bdm

    row = lax.broadcasted_iota(jnp.int32, (L, W), 0)
    sub = lax.broadcasted_iota(jnp.int32, (L, W), 1) % RW_HEAD
    strict = sub < row
    incl = sub <= row
    eye = (sub == row).astype(F32)
    tri = (lax.broadcasted_iota(jnp.int32, (L, L), 1)
           <= lax.broadcasted_iota(jnp.int32, (L, L), 0)).astype(BF16)
    inv = 1.0 / RW_HEAD

    def instance(b, gi):
        sl = slice(gi * W, (gi + 1) * W)
        lw = lw_ref[b, :, sl]
        lw_hi, lw_lo = _split_bf16(lw)
        cum = _dot(tri, lw_hi) + _dot(tri, lw_lo)
        yield
        cum_l = cum[L - 1:L, :]
        r = r_ref[b, :, sl].astype(F32)
        k = k_ref[b, :, sl].astype(F32)
        v = v_ref[b, :, sl].astype(F32)
        be = be_ref[b, :, sl].astype(F32)
        e_neg = jnp.exp(-cum)
        e_tail = jnp.exp(cum_l - cum)
        ar = jnp.concatenate([al_ref[b, :, sl].astype(F32) * jnp.exp(cum - lw), r * jnp.exp(cum)],
                             axis=0).astype(BF16)
        s0 = s_ref[b, gi]
        ar_s = _dot_nt(ar, s0.astype(BF16))
        a_b = _dot_nt(ar, bd(be * e_neg))
        a_k = _dot_nt(ar, bd(k * e_neg))
        yield
        n = jnp.where(strict, a_b[:L], 0.0)
        a_ak = jnp.where(strict, a_k[:L], 0.0)
        a_rb = jnp.where(incl, a_b[L:], 0.0)
        a_rk = jnp.where(incl, a_k[L:], 0.0)
        p = eye + n
        nk = _dot(n.astype(BF16), bd(n))
        bd_v = bd(v)
        x0 = _dot(a_ak.astype(BF16), bd_v)
        yield
        steps = int(math.log2(L)) - 1
        for it in range(steps):
            m = bd(nk)
            if it + 1 < steps:
                res = _dot(jnp.concatenate([p, nk], axis=0).astype(BF16), m)
                p = p + res[:L]
                nk = res[L:]
            else:
                p = p + _dot(p.astype(BF16), m)
            yield
        u = _dot(p.astype(BF16), bd(ar_s[:L] + x0))
        yield
        y = ar_s[L:] + _dot(jnp.concatenate([a_rb, a_rk], axis=1).astype(BF16),
                            jnp.concatenate([bd(u), bd_v], axis=0))
        uv = jnp.concatenate([u, v], axis=0).astype(BF16)
        bk = jnp.concatenate([be * e_tail, k * e_tail], axis=0).astype(BF16)
        s_ref[b, gi] = s0 * jnp.exp(cum_l) + _dot_tn(uv, bk) * bdm.astype(F32)
        yield
        hi, lo = _split_bf16(jnp.concatenate([y, r * k * rk_ref[:, sl]], axis=0))
        sums = _dot(jnp.concatenate([hi, lo], axis=0), bdm)
        sums = sums[:2 * L] + sums[2 * L:]
        yield
        d = y - sums[:L] * inv
        hi, lo = _split_bf16(d * d)
        var = _dot(jnp.concatenate([hi, lo], axis=0), bdm)
        var = (var[:L] + var[L:]) * inv
        yield
        yn = d * lax.rsqrt(var + RW_LN_EPS) * lnw_ref[:, sl] + lnb_ref[:, sl]
        out = (yn + sums[L:] * v) * g_ref[b, :, sl].astype(F32)
        o_ref[b, :, sl] = out.astype(o_ref.dtype)

    live = [instance(b, gi) for b in range(rows) for gi in range(RW_DIM // W)]
    while live:
        live = [g for g in live if next(g, True) is None]


def _rw_chunk(r, k, v, al, be, lw, g, b, r_k, ln_w, ln_b):
    t = r.shape[0]
    s = t // b
    rows = math.gcd(b, 4)
    gid = jnp.arange(RW_GROUP) // RW_HEAD
    ones_bd = (gid[:, None] == gid[None, :]).astype(BF16)
    blk = pl.BlockSpec((rows, CHUNK, RW_DIM), lambda bb, c: (bb, c, 0))
    consts = [r_k, ln_w, ln_b, ones_bd]
    seq = [a.reshape(b, s, RW_DIM) for a in (r, k, v, al, be, lw, g)]
    return pl.pallas_call(
        functools.partial(_rw_chunk_kernel, rows),
        grid=(b // rows, s // CHUNK),
        in_specs=[blk] * 7 + [_const_spec(c.shape) for c in consts],
        out_specs=blk,
        out_shape=jax.ShapeDtypeStruct((b, s, RW_DIM), BF16),
        scratch_shapes=[pltpu.VMEM((rows, RW_DIM // RW_GROUP, RW_GROUP, RW_GROUP), F32)],
        compiler_params=_cparams("parallel", "arbitrary"),
        name="rw_chunk",
    )(*seq, *consts).reshape(t, RW_DIM)


def _merge_kernel(om_ref, or_ref, oc_ref, g0_ref, g1_ref, g2_ref, h_ref, wb_ref, wo_ref, g_ref, o_ref):
    merged = None
    for n, (b_ref, zg_ref) in enumerate(((om_ref, g0_ref), (or_ref, g1_ref), (oc_ref, g2_ref))):
        y = _dot(b_ref[...], wb_ref[n * BRANCH_DIM:(n + 1) * BRANCH_DIM, :])
        gate = jax.nn.sigmoid(zg_ref[...].astype(F32))
        merged = gate * y if merged is None else merged + gate * y
    out = _dot(merged.astype(BF16), wo_ref[...])
    o_ref[...] = h_ref[...] + _rms(out, g_ref[...])


def _merge(o_mla, o_rw, o_ca, z, h, w_branch, w_out, g):
    t = h.shape[0]
    tm = min(t, 512)
    row = lambda w: pl.BlockSpec((tm, w), lambda i: (i, 0))
    return pl.pallas_call(
        _merge_kernel,
        grid=(t // tm,),
        in_specs=[
            row(BRANCH_DIM), row(BRANCH_DIM), row(BRANCH_DIM),
            pl.BlockSpec((tm, D_MODEL), lambda i: (i, 0)),
            pl.BlockSpec((tm, D_MODEL), lambda i: (i, 1)),
            pl.BlockSpec((tm, D_MODEL), lambda i: (i, 2)),
            row(D_MODEL),
            _const_spec(w_branch.shape), _const_spec(w_out.shape), _const_spec(g.shape),
        ],
        out_specs=row(D_MODEL),
        out_shape=jax.ShapeDtypeStruct((t, D_MODEL), F32),
        compiler_params=_cparams("parallel"),
        name="merge",
    )(o_mla, o_rw, o_ca, z, z, z, h, w_branch, w_out, g)


FF_SPLIT = 4


def _ff_ple_kernel(h_ref, p_ref, g1_ref, w1_ref, w2_ref, g2_ref, wg_ref, wp_ref, o_ref):
    h = h_ref[...]
    f = _rms(h, g1_ref[...]).astype(BF16)
    cw = D_FF // FF_SPLIT
    acc = None
    for c in range(FF_SPLIT):
        a = jnp.maximum(_dot(f, w1_ref[:, c * cw:(c + 1) * cw]), 0.0)
        part = _dot((a * a).astype(BF16), w2_ref[c * cw:(c + 1) * cw, :])
        acc = part if acc is None else acc + part
    h = h + _rms(acc, g2_ref[...])
    gate = jax.nn.sigmoid(_dot(h.astype(BF16), wg_ref[...]))
    o_ref[...] = h + gate * _dot(p_ref[...].astype(BF16), wp_ref[...])


def _ff_ple(h, p, g1, w1, w2, g2, wg, wp):
    t = h.shape[0]
    tm = min(t, 512)
    row = lambda w: pl.BlockSpec((tm, w), lambda i: (i, 0))
    consts = [g1, w1, w2, g2, wg, wp]
    return pl.pallas_call(
        _ff_ple_kernel,
        grid=(t // tm,),
        in_specs=[row(D_MODEL), row(D_PLE)] + [_const_spec(c.shape) for c in consts],
        out_specs=row(D_MODEL),
        out_shape=jax.ShapeDtypeStruct((t, D_MODEL), F32),
        compiler_params=_cparams("parallel"),
        name="ff_ple",
    )(h, p, *consts)


def _rot_half_cols(w):
    half = w.shape[1] // 2
    return jnp.concatenate([-w[:, half:], w[:, :half]], axis=1)


def _pack_w_in(w):
    d = w.shape[0]
    mla_cols = MLA_Q_RANK + MLA_KV_RANK + MLA_ROPE
    rw_cols = 3 * RW_DIM + RW_LORA
    w_mla = w[:, :mla_cols]
    w_rw = w[:, mla_cols:mla_cols + rw_cols]
    w_ca = w[:, mla_cols + rw_cols:mla_cols + rw_cols + 3 * CA_DIM]
    w_gate = w[:, mla_cols + rw_cols + 3 * CA_DIM:]
    w_kr = w_mla[:, MLA_Q_RANK + MLA_KV_RANK:]
    z64 = jnp.zeros((d, 64), w.dtype)
    packed = jnp.concatenate(
        [w_gate, w_ca[:, CA_DIM:2 * CA_DIM], w_rw, w_mla[:, :MLA_Q_RANK + MLA_KV_RANK], w_kr, z64,
         _rot_half_cols(w_kr), z64], axis=1)
    wt = jnp.concatenate([w_ca[:, :CA_DIM] * (CA_HEAD ** -0.5 * LOG2E), w_ca[:, 2 * CA_DIM:]], axis=1).T
    return packed.astype(BF16), wt.astype(BF16)


def _pack_w_uq(w):
    r = w.shape[0]
    w = w.reshape(r, MLA_HEADS, MLA_QK)
    z64 = jnp.zeros((r, MLA_HEADS, 64), w.dtype)
    rope = w[:, :, MLA_NOPE:]
    rot = jnp.concatenate([-rope[:, :, MLA_ROPE // 2:], rope[:, :, :MLA_ROPE // 2]], axis=2)
    wq = jnp.concatenate([w, z64], axis=2).reshape(r, MLA_HEADS * MLA_HEAD_PAD)
    wqr = jnp.concatenate([rot, z64], axis=2).reshape(r, MLA_HEADS * 128)
    return wq.T.astype(BF16), wqr.T.astype(BF16)


def _pack_w_ukv(w):
    r = w.shape[0]
    w = w.reshape(r, MLA_HEADS, MLA_NOPE + MLA_V)
    wk = w[:, :, :MLA_NOPE].reshape(r, MLA_HEADS * MLA_NOPE)
    wv = w[:, :, MLA_NOPE:].reshape(r, MLA_HEADS * MLA_V)
    return wk.astype(BF16), wv.T.astype(BF16)


def kernel(x, p, positions, pre_mix_g, w_in, mla_q_norm_g, mla_kv_norm_g, mla_w_uq, mla_w_ukv, rw_mu, rw_w0, rw_w_up, rw_a0, rw_a_up, rw_g_up, rw_k_k, rw_k_a, rw_r_k, rw_ln_w, rw_ln_b, ca_rel_bias, w_branch, w_out, post_mix_g, pre_ff_g, w_ff1, w_ff2, post_ff_g, w_ple_gate, w_ple_proj):
    b, s, d = x.shape
    t = b * s
    depth = w_in.shape[0]
    row = lambda a: a.reshape(1, -1).astype(F32)
    head_id = jnp.arange(RW_DIM) // RW_HEAD
    ones_bd = (head_id[:, None] == head_id[None, :]).astype(BF16)
    cs, sn, cst, snt = _rope_table(positions)
    h = x.reshape(t, d)
    for i in range(depth):
        z, zt = _in_proj(h, row(pre_mix_g[i]), *_pack_w_in(w_in[i]))
        wqt, wqrt = _pack_w_uq(mla_w_uq[i])
        wk, wvt = _pack_w_ukv(mla_w_ukv[i])
        qt, kn, kr, vt = _mla_proj(z, cs, sn, cst, snt, row(mla_q_norm_g[i]), row(mla_kv_norm_g[i]),
                                   wqt, wqrt, wk, wvt)
        o_mla = _mla_attn(qt, kn, kr, vt, b).reshape(t, -1)
        rr, rk, rv, al, be, lw, rg = _rw_prep(
            z, s, row(rw_mu[i]), row(rw_w0[i]), rw_w_up[i].astype(BF16), row(rw_a0[i]),
            rw_a_up[i].astype(BF16), rw_g_up[i].astype(BF16), row(rw_k_k[i]), row(rw_k_a[i]), ones_bd)
        o_rw = _rw_chunk(rr, rk, rv, al, be, lw, rg, b, row(rw_r_k[i]), row(rw_ln_w[i]), row(rw_ln_b[i]))
        kp = jnp.pad(z.reshape(b, s, -1)[:, :, OFF_CAK:OFF_CAK + CA_DIM], ((0, 0), (CA_PAD, 0), (0, 0)))
        vtp = jnp.pad(zt[CA_DIM:].reshape(CA_DIM, b, s), ((0, 0), (0, 0), (CA_PAD, 0))).reshape(CA_DIM, -1)
        o_ca = _ca_attn(zt, kp, vtp, _ca_bias_table(ca_rel_bias[i]), b).reshape(t, -1)
        h = _merge(o_mla, o_rw, o_ca, z, h, w_branch[i].astype(BF16), w_out[i].astype(BF16),
                   row(post_mix_g[i]))
        h = _ff_ple(h, p[i].reshape(t, -1), row(pre_ff_g[i]), w_ff1[i].astype(BF16), w_ff2[i].astype(BF16),
                    row(post_ff_g[i]), w_ple_gate[i].astype(BF16), w_ple_proj[i].astype(BF16))
    return h.reshape(b, s, d)
```

```python
import functools
import math

import jax
import jax.numpy as jnp
from jax import lax
from jax.experimental import pallas as pl
from jax.experimental.pallas import tpu as pltpu

F32 = jnp.float32
BF16 = jnp.bfloat16

D_MODEL = 1024
D_PLE = 256
D_FF = 4 * D_MODEL
NORM_EPS = 1e-6
CHUNK = 64

MLA_HEADS = 4
MLA_NOPE = 128
MLA_ROPE = 64
MLA_V = 128
MLA_Q_RANK = 256
MLA_KV_RANK = 128
ROPE_THETA = 10000.0
MLA_QK = MLA_NOPE + MLA_ROPE
MLA_HEAD_PAD = 256

RW_HEADS = 8
RW_HEAD = 64
RW_DIM = RW_HEADS * RW_HEAD
RW_DECAY_LORA = 64
RW_AAA_LORA = 64
RW_GATE_LORA = 128
RW_LORA = RW_DECAY_LORA + RW_AAA_LORA + RW_GATE_LORA
RW_LN_EPS = 64e-5

CA_HEADS = 8
CA_HEAD = 64
CA_DIM = CA_HEADS * CA_HEAD
CA_LEFT_CHUNKS = 8
CA_PAD = CA_LEFT_CHUNKS * CHUNK
REL_MIN = -(CHUNK - 1)
REL_MAX = 256

N_BRANCH = 3
BRANCH_DIM = 512
GATE_COLS = N_BRANCH * D_MODEL

OFF_GATE = 0
OFF_CAK = GATE_COLS
OFF_RW = OFF_CAK + CA_DIM
OFF_LORA = OFF_RW + 3 * RW_DIM
OFF_ZQ = OFF_LORA + RW_LORA
OFF_ZKV = OFF_ZQ + MLA_Q_RANK
OFF_KR = OFF_ZKV + MLA_KV_RANK
OFF_KRR = OFF_KR + 128
IN_PACKED = OFF_KRR + 128
LOG2E = math.log2(math.e)

VMEM_LIMIT = 56 * 1024 * 1024


def _cparams(*sem):
    return pltpu.CompilerParams(dimension_semantics=sem, vmem_limit_bytes=VMEM_LIMIT)


def _rms(x, g):
    return x * lax.rsqrt(jnp.mean(x * x, axis=-1, keepdims=True) + NORM_EPS) * g


def _dot(a, b):
    return jnp.dot(a, b, preferred_element_type=F32)


def _dot_nt(a, b):
    return lax.dot_general(a, b, (((1,), (1,)), ((), ())), preferred_element_type=F32)


def _dot_tn(a, b):
    return lax.dot_general(a, b, (((0,), (0,)), ((), ())), preferred_element_type=F32)


def _const_spec(shape):
    nd = len(shape)
    return pl.BlockSpec(shape, lambda *_: (0,) * nd, pipeline_mode=pl.Buffered(1))


def _pshape(param):
    return param[0].shape[1:]


def _pspec(param):
    stack, layer = param
    nd = stack.ndim - 1
    return pl.BlockSpec((None,) + stack.shape[1:], lambda *_: (layer,) + (0,) * nd, pipeline_mode=pl.Buffered(1))


def _pargs(params):
    return [stack for stack, _ in params]


def _rope_table_kernel(pos_c_ref, pos_r_ref, freq_r_ref, freq_c_ref, cs_ref, sn_ref, cst_ref, snt_ref):
    ang = pos_c_ref[...] * freq_r_ref[...]
    live = lax.broadcasted_iota(jnp.int32, ang.shape, 1) < MLA_ROPE
    cs_ref[...] = jnp.where(live, jnp.cos(ang), 0.0)
    sn_ref[...] = jnp.where(live, jnp.sin(ang), 0.0)
    ang_t = freq_c_ref[...] * pos_r_ref[...]
    live_t = lax.broadcasted_iota(jnp.int32, ang_t.shape, 0) < MLA_ROPE
    cst_ref[...] = jnp.where(live_t, jnp.cos(ang_t), 0.0)
    snt_ref[...] = jnp.where(live_t, jnp.sin(ang_t), 0.0)


def _rope_table(positions):
    t = positions.size
    tm = min(t, 2048)
    half = MLA_ROPE // 2
    inv_freq = 1.0 / (ROPE_THETA ** (jnp.arange(half, dtype=F32) / half))
    freq = jnp.concatenate([inv_freq, inv_freq, jnp.zeros((128 - MLA_ROPE,), F32)])
    pos = positions.astype(F32)
    return pl.pallas_call(
        _rope_table_kernel,
        grid=(t // tm,),
        in_specs=[pl.BlockSpec((tm, 1), lambda i: (i, 0)), pl.BlockSpec((1, tm), lambda i: (0, i)),
                  _const_spec((1, 128)), _const_spec((128, 1))],
        out_specs=[pl.BlockSpec((tm, 128), lambda i: (i, 0))] * 2 + [pl.BlockSpec((128, tm), lambda i: (0, i))] * 2,
        out_shape=[jax.ShapeDtypeStruct((t, 128), F32)] * 2 + [jax.ShapeDtypeStruct((128, t), F32)] * 2,
        compiler_params=_cparams("parallel"),
        name="rope_table",
    )(pos.reshape(t, 1), pos.reshape(1, t), freq[None, :], freq[:, None])


IN_CHUNK = 1536


def _in_proj_kernel(x_ref, g_ref, w_ref, wt_ref, o_ref, ot_ref):
    xn = _rms(x_ref[...], g_ref[...]).astype(BF16)
    n = w_ref.shape[1]
    for c in range(0, n, IN_CHUNK):
        e = min(c + IN_CHUNK, n)
        o_ref[:, c:e] = _dot(xn, w_ref[:, c:e]).astype(o_ref.dtype)
    ot_ref[...] = _dot_nt(wt_ref[...], xn).astype(ot_ref.dtype)


def _in_proj(h, g, w, wt):
    t, d = h.shape
    n = _pshape(w)[1]
    nt = _pshape(wt)[0]
    tm = min(t, 512)
    return pl.pallas_call(
        _in_proj_kernel,
        grid=(t // tm,),
        in_specs=[pl.BlockSpec((tm, d), lambda i: (i, 0)), _pspec(g), _pspec(w), _pspec(wt)],
        out_specs=[pl.BlockSpec((tm, n), lambda i: (i, 0)), pl.BlockSpec((nt, tm), lambda i: (0, i))],
        out_shape=[jax.ShapeDtypeStruct((t, n), BF16), jax.ShapeDtypeStruct((nt, t), BF16)],
        compiler_params=_cparams("parallel"),
        name="in_proj",
    )(h, *_pargs([g, w, wt]))


def _mla_proj_kernel(zq_ref, zkv_ref, zkr_ref, zkrr_ref, cs_ref, sn_ref, cst_ref, snt_ref, gq_ref, gkv_ref,
                     wqt_ref, wqrt_ref, wk_ref, wvt_ref, qt_ref, k_ref, kr_ref, vt_ref):
    cst = cst_ref[...]
    snt = snt_ref[...]
    qn = _rms(zq_ref[...].astype(F32), gq_ref[...]).astype(BF16)
    qt = _dot_nt(wqt_ref[...], qn)
    qrt = _dot_nt(wqrt_ref[...], qn)
    scale = MLA_QK ** -0.5 * LOG2E
    for h in range(MLA_HEADS):
        o = h * MLA_HEAD_PAD
        qt_ref[o:o + 128, :] = (qt[o:o + 128] * scale).astype(BF16)
        rope = qt[o + 128:o + 256] * cst + qrt[h * 128:(h + 1) * 128] * snt
        qt_ref[o + 128:o + 256, :] = (rope * scale).astype(BF16)
    kvn = _rms(zkv_ref[...].astype(F32), gkv_ref[...]).astype(BF16)
    k_ref[...] = _dot(kvn, wk_ref[...]).astype(BF16)
    vt_ref[...] = _dot_nt(wvt_ref[...], kvn).astype(BF16)
    kr_ref[...] = (zkr_ref[...].astype(F32) * cs_ref[...] + zkrr_ref[...].astype(F32) * sn_ref[...]).astype(BF16)


def _mla_proj(z, cs, sn, cst, snt, gq, gkv, wqt, wqrt, wk, wvt):
    t = z.shape[0]
    tm = min(t, 1024)
    hv = MLA_HEADS * MLA_V
    hq = MLA_HEADS * MLA_HEAD_PAD
    row = lambda w: pl.BlockSpec((tm, w), lambda i: (i, 0))
    col = lambda w: pl.BlockSpec((w, tm), lambda i: (0, i))
    consts = [gq, gkv, wqt, wqrt, wk, wvt]
    return pl.pallas_call(
        _mla_proj_kernel,
        grid=(t // tm,),
        in_specs=[pl.BlockSpec((tm, MLA_Q_RANK), lambda i: (i, OFF_ZQ // MLA_Q_RANK)),
                  pl.BlockSpec((tm, MLA_KV_RANK), lambda i: (i, OFF_ZKV // MLA_KV_RANK)),
                  pl.BlockSpec((tm, 128), lambda i: (i, OFF_KR // 128)),
                  pl.BlockSpec((tm, 128), lambda i: (i, OFF_KRR // 128)),
                  row(128), row(128), col(128), col(128)] + [_pspec(c) for c in consts],
        out_specs=[col(hq), row(hv), row(128), col(hv)],
        out_shape=[
            jax.ShapeDtypeStruct((hq, t), BF16),
            jax.ShapeDtypeStruct((t, hv), BF16),
            jax.ShapeDtypeStruct((t, 128), BF16),
            jax.ShapeDtypeStruct((hv, t), BF16),
        ],
        compiler_params=_cparams("parallel"),
        name="mla_proj",
    )(z, z, z, z, cs, sn, cst, snt, *_pargs(consts))


MLA_TQ = 512


def _mla_attn_kernel(qt_ref, k_ref, kr_ref, vt_ref, o_ref, m_ref, l_ref, acc_ref):
    i = pl.program_id(1)
    tq = MLA_TQ
    m_ref[...] = jnp.full(m_ref.shape, -1e30, F32)
    l_ref[...] = jnp.zeros(l_ref.shape, F32)
    acc_ref[...] = jnp.zeros(acc_ref.shape, F32)

    def step(j, masked):
        start = pl.multiple_of(j * tq, tq)
        kr = kr_ref[0, pl.ds(start, tq), :]

        def scores(h):
            kh = jnp.concatenate([k_ref[0, pl.ds(start, tq), h * 128:(h + 1) * 128], kr], axis=1)
            s = _dot(kh, qt_ref[h * MLA_HEAD_PAD:(h + 1) * MLA_HEAD_PAD, :])
            if masked:
                kc = lax.broadcasted_iota(jnp.int32, s.shape, 0) // CHUNK
                qc = lax.broadcasted_iota(jnp.int32, s.shape, 1) // CHUNK
                s = jnp.where(kc <= qc, s, -1e30)
            return s

        def update(h, s):
            m_prev = m_ref[h]
            m_new = jnp.maximum(m_prev, jnp.max(s, axis=0, keepdims=True))
            a = jnp.exp2(m_prev - m_new)
            p = jnp.exp2(s - m_new)
            l_ref[h] = a * l_ref[h] + jnp.sum(p, axis=0, keepdims=True)
            vh = vt_ref[h * MLA_V:(h + 1) * MLA_V, pl.ds(start, tq)]
            acc_ref[h] = a * acc_ref[h] + _dot(vh, p.astype(BF16))
            m_ref[h] = m_new

        s_cur = scores(0)
        for h in range(1, MLA_HEADS):
            s_next = scores(h)
            update(h - 1, s_cur)
            s_cur = s_next
        update(MLA_HEADS - 1, s_cur)

    def body(j, c):
        step(j, False)
        return c

    lax.fori_loop(0, i, body, 0)
    step(i, True)
    for h in range(MLA_HEADS):
        o_ref[0, :, h * MLA_V:(h + 1) * MLA_V] = (acc_ref[h] / l_ref[h]).T.astype(o_ref.dtype)


def _mla_attn(qt, k, kr, vt, b):
    hq, t = qt.shape
    s = t // b
    tq = MLA_TQ
    nq = s // tq
    hv = MLA_HEADS * MLA_V
    full = lambda w: pl.BlockSpec((1, s, w), lambda bi, i: (bi, 0, 0))
    return pl.pallas_call(
        _mla_attn_kernel,
        grid=(b, nq),
        in_specs=[
            pl.BlockSpec((hq, tq), lambda bi, i: (0, bi * nq + i)),
            full(hv), full(128),
            pl.BlockSpec((hv, s), lambda bi, i: (0, bi)),
        ],
        out_specs=pl.BlockSpec((1, tq, hv), lambda bi, i: (bi, i, 0)),
        out_shape=jax.ShapeDtypeStruct((b, s, hv), BF16),
        scratch_shapes=[pltpu.VMEM((MLA_HEADS, 1, tq), F32), pltpu.VMEM((MLA_HEADS, 1, tq), F32),
                        pltpu.VMEM((MLA_HEADS, MLA_V, tq), F32)],
        compiler_params=_cparams("parallel", "arbitrary"),
        name="mla_attn",
    )(qt, k.reshape(b, s, hv), kr.reshape(b, s, 128), vt)


CA_TQ = 256
CA_BAND_BLK = CA_TQ + CA_PAD


def _ca_attn_kernel(qt_ref, k_ref, vt_ref, bias_ref, o_ref):
    i = pl.program_id(1)
    lo = lax.broadcasted_iota(jnp.int32, (128, CA_TQ), 0) < CA_HEAD
    zero = jnp.zeros((), BF16)

    def block(k_start, n_keys, bias_off):
        def scores(h):
            sl = slice((h // 2) * 128, (h // 2 + 1) * 128)
            k2 = k_ref[0, pl.ds(k_start, n_keys), sl]
            qh = jnp.where(lo if h % 2 == 0 else ~lo, qt_ref[sl, :], zero)
            return _dot(k2, qh) + bias_ref[h, bias_off:bias_off + n_keys, :]

        def attend(h, s):
            p = jnp.exp2(s - jnp.max(s, axis=0, keepdims=True))
            l = jnp.sum(p, axis=0, keepdims=True)
            vh = vt_ref[h * CA_HEAD:(h + 1) * CA_HEAD, pl.ds(k_start, n_keys)]
            return _dot(vh, p.astype(BF16)) / l

        outs = []
        s_cur = scores(0)
        for h in range(1, CA_HEADS):
            s_next = scores(h)
            outs.append(attend(h - 1, s_cur))
            s_cur = s_next
        outs.append(attend(CA_HEADS - 1, s_cur))
        for pair in range(CA_HEADS // 2):
            both = jnp.concatenate(outs[2 * pair:2 * pair + 2], axis=0)
            o_ref[0, :, pair * 128:(pair + 1) * 128] = both.T.astype(o_ref.dtype)

    lead = CA_PAD // CA_TQ
    for j in range(lead):
        pl.when(i == j)(functools.partial(block, 0, (j + 1) * CA_TQ, CA_PAD - j * CA_TQ))
    pl.when(i >= lead)(lambda: block(pl.multiple_of(i * CA_TQ - CA_PAD, CA_TQ), CA_BAND_BLK, 0))


def _ca_attn(zt, z, bias, b):
    t = zt.shape[1]
    s = t // b
    nq = s // CA_TQ
    return pl.pallas_call(
        _ca_attn_kernel,
        grid=(b, nq),
        in_specs=[
            pl.BlockSpec((CA_DIM, CA_TQ), lambda bi, i: (0, bi * nq + i)),
            pl.BlockSpec((1, s, CA_DIM), lambda bi, i: (bi, 0, OFF_CAK // CA_DIM)),
            pl.BlockSpec((CA_DIM, s), lambda bi, i: (1, bi)),
            _const_spec(bias.shape),
        ],
        out_specs=pl.BlockSpec((1, CA_TQ, CA_DIM), lambda bi, i: (bi, i, 0)),
        out_shape=jax.ShapeDtypeStruct((b, s, CA_DIM), BF16),
        compiler_params=_cparams("parallel", "arbitrary"),
        name="ca_attn",
    )(zt, z.reshape(b, s, -1), zt, bias)


def _ca_bias_table(rel_bias):
    width = CA_TQ + CA_BAND_BLK
    m = jnp.arange(width)
    delta = jnp.where(m < CA_TQ, m, m - width)
    idx = jnp.clip(CA_PAD + delta, REL_MIN, REL_MAX) - REL_MIN
    t1 = rel_bias.astype(F32)[idx].T * LOG2E
    flat = jnp.tile(t1, (1, CA_BAND_BLK))[:, :CA_BAND_BLK * (width - 1)]
    bias = flat.reshape(CA_HEADS, CA_BAND_BLK, width - 1)[:, :, :CA_TQ]
    cc = jnp.arange(CA_BAND_BLK)[:, None] // CHUNK
    rc = jnp.arange(CA_TQ)[None, :] // CHUNK
    ok = (cc >= rc) & (cc <= rc + CA_LEFT_CHUNKS)
    return jnp.where(ok[None], bias, -1e30)


def _split_bf16(x):
    hi = x.astype(BF16)
    lo = (x - hi.astype(F32)).astype(BF16)
    return hi, lo


def _head_sum(x, ones_bd):
    hi, lo = _split_bf16(x)
    return _dot(hi, ones_bd) + _dot(lo, ones_bd)


def _rw_prep_kernel(seq_len, zr_ref, zk_ref, zv_ref, zl_ref, pr_ref, pk_ref, pv_ref, plr_ref,
                    mu_r, mu_k, mu_v, mu_l, w0, w_up, a0, a_up, g_up, k_k, k_a, ones_bd,
                    r_o, k_o, v_o, al_o, be_o, lw_o, g_o):
    tm = zr_ref.shape[0]
    first = (pl.program_id(0) * tm) % seq_len == 0
    row0 = lax.broadcasted_iota(jnp.int32, (tm, 1), 0) == 0

    def shift(z_ref, p_ref, mu):
        z = z_ref[...].astype(F32)
        last = jnp.where(first, 0.0, p_ref[...].astype(F32)[15:16, :])
        prev = jnp.where(row0, last, pltpu.roll(z, 1, axis=0))
        return z + (prev - z) * mu[...]

    r = shift(zr_ref, pr_ref, mu_r)
    k = shift(zk_ref, pk_ref, mu_k)
    v = shift(zv_ref, pv_ref, mu_v)
    xl = shift(zl_ref, plr_ref, mu_l)
    xw = jnp.tanh(xl[:, :RW_DECAY_LORA]).astype(BF16)
    xa = xl[:, RW_DECAY_LORA:RW_DECAY_LORA + RW_AAA_LORA].astype(BF16)
    xg = jax.nn.sigmoid(xl[:, RW_DECAY_LORA + RW_AAA_LORA:]).astype(BF16)
    lw = -math.exp(-0.5) * jax.nn.sigmoid(w0[...] + _dot(xw, w_up[...]))
    a = jax.nn.sigmoid(a0[...] + _dot(xa, a_up[...]))
    g = _dot(xg, g_up[...])
    kk = k * k_k[...]
    ss = _head_sum(kk * kk, ones_bd[...])
    kk = kk * lax.rsqrt(jnp.maximum(ss, 1e-24))
    k = k * (1.0 + (a - 1.0) * k_a[...])
    r_o[...] = r.astype(BF16)
    k_o[...] = k.astype(BF16)
    v_o[...] = v.astype(BF16)
    al_o[...] = (-kk).astype(BF16)
    be_o[...] = (kk * a).astype(BF16)
    lw_o[...] = lw
    g_o[...] = g.astype(BF16)


def _rw_prep(z, seq_len, consts, ones_bd):
    t = z.shape[0]
    tm = min(seq_len, 512)
    c512 = lambda j: pl.BlockSpec((tm, RW_DIM), lambda i: (i, OFF_RW // RW_DIM + j))
    prev = lambda w, jb: pl.BlockSpec((16, w), lambda i: (jnp.maximum(i * (tm // 16) - 1, 0), jb))
    out = lambda dt: jax.ShapeDtypeStruct((t, RW_DIM), dt)
    return pl.pallas_call(
        functools.partial(_rw_prep_kernel, seq_len),
        grid=(t // tm,),
        in_specs=[
            c512(0), c512(1), c512(2),
            pl.BlockSpec((tm, RW_LORA), lambda i: (i, OFF_LORA // RW_LORA)),
            prev(RW_DIM, OFF_RW // RW_DIM), prev(RW_DIM, OFF_RW // RW_DIM + 1),
            prev(RW_DIM, OFF_RW // RW_DIM + 2), prev(RW_LORA, OFF_LORA // RW_LORA),
        ] + [_pspec(c) for c in consts] + [_const_spec(ones_bd.shape)],
        out_specs=[pl.BlockSpec((tm, RW_DIM), lambda i: (i, 0))] * 7,
        out_shape=[out(BF16), out(BF16), out(BF16), out(BF16), out(BF16), out(F32), out(BF16)],
        compiler_params=_cparams("parallel"),
        name="rw_prep",
    )(z, z, z, z, z, z, z, z, *_pargs(consts), ones_bd)


RW_GROUP = 256


def _rw_chunk_kernel(rows, r_ref, k_ref, v_ref, al_ref, be_ref, lw_ref, g_ref, rk_ref, lnw_ref, lnb_ref,
                     bd_ref, o_ref, s_ref):
    L = CHUNK
    W = RW_GROUP
    reps = W // RW_HEAD

    @pl.when(pl.program_id(1) == 0)
    def _():
        s_ref[...] = jnp.zeros(s_ref.shape, F32)

    bdm = bd_ref[...]

    def bd(x):
        return jnp.concatenate([x.astype(BF16)] * reps, axis=0) * bdm

    row = lax.broadcasted_iota(jnp.int32, (L, W), 0)
    sub = lax.broadcasted_iota(jnp.int32, (L, W), 1) % RW_HEAD
    strict = sub < row
    incl = sub <= row
    eye = (sub == row).astype(F32)
    tri = (lax.broadcasted_iota(jnp.int32, (L, L), 1)
           <= lax.broadcasted_iota(jnp.int32, (L, L), 0)).astype(BF16)
    inv = 1.0 / RW_HEAD

    def instance(b, gi):
        sl = slice(gi * W, (gi + 1) * W)
        lw = lw_ref[b, :, sl]
        lw_hi, lw_lo = _split_bf16(lw)
        cum = _dot(tri, lw_hi) + _dot(tri, lw_lo)
        yield
        cum_l = cum[L - 1:L, :]
        r = r_ref[b, :, sl].astype(F32)
        k = k_ref[b, :, sl].astype(F32)
        v = v_ref[b, :, sl].astype(F32)
        be = be_ref[b, :, sl].astype(F32)
        e_neg = jnp.exp(-cum)
        e_tail = jnp.exp(cum_l - cum)
        ar = jnp.concatenate([al_ref[b, :, sl].astype(F32) * jnp.exp(cum - lw), r * jnp.exp(cum)],
                             axis=0).astype(BF16)
        s0 = s_ref[b, gi]
        ar_s = _dot_nt(ar, s0.astype(BF16))
        a_b = _dot_nt(ar, bd(be * e_neg))
        a_k = _dot_nt(ar, bd(k * e_neg))
        yield
        n = jnp.where(strict, a_b[:L], 0.0)
        a_ak = jnp.where(strict, a_k[:L], 0.0)
        a_rb = jnp.where(incl, a_b[L:], 0.0)
        a_rk = jnp.where(incl, a_k[L:], 0.0)
        p = eye + n
        nk = _dot(n.astype(BF16), bd(n))
        bd_v = bd(v)
        x0 = _dot(a_ak.astype(BF16), bd_v)
        yield
        steps = int(math.log2(L)) - 1
        for it in range(steps):
            m = bd(nk)
            if it + 1 < steps:
                res = _dot(jnp.concatenate([p, nk], axis=0).astype(BF16), m)
                p = p + res[:L]
                nk = res[L:]
            else:
                p = p + _dot(p.astype(BF16), m)
            yield
        u = _dot(p.astype(BF16), bd(ar_s[:L] + x0))
        yield
        y = ar_s[L:] + _dot(jnp.concatenate([a_rb, a_rk], axis=1).astype(BF16),
                            jnp.concatenate([bd(u), bd_v], axis=0))
        uv = jnp.concatenate([u, v], axis=0).astype(BF16)
        bk = jnp.concatenate([be * e_tail, k * e_tail], axis=0).astype(BF16)
        s_ref[b, gi] = s0 * jnp.exp(cum_l) + _dot_tn(uv, bk) * bdm.astype(F32)
        yield
        hi, lo = _split_bf16(jnp.concatenate([y, r * k * rk_ref[:, sl]], axis=0))
        sums = _dot(jnp.concatenate([hi, lo], axis=0), bdm)
        sums = sums[:2 * L] + sums[2 * L:]
        yield
        d = y - sums[:L] * inv
        hi, lo = _split_bf16(d * d)
        var = _dot(jnp.concatenate([hi, lo], axis=0), bdm)
        var = (var[:L] + var[L:]) * inv
        yield
        yn = d * lax.rsqrt(var + RW_LN_EPS) * lnw_ref[:, sl] + lnb_ref[:, sl]
        out = (yn + sums[L:] * v) * g_ref[b, :, sl].astype(F32)
        o_ref[b, :, sl] = out.astype(o_ref.dtype)

    live = [instance(b, gi) for b in range(rows) for gi in range(RW_DIM // W)]
    while live:
        live = [g for g in live if next(g, True) is None]


def _rw_chunk(r, k, v, al, be, lw, g, b, r_k, ln_w, ln_b):
    t = r.shape[0]
    s = t // b
    rows = math.gcd(b, 4)
    gid = jnp.arange(RW_GROUP) // RW_HEAD
    ones_bd = (gid[:, None] == gid[None, :]).astype(BF16)
    blk = pl.BlockSpec((rows, CHUNK, RW_DIM), lambda bb, c: (bb, c, 0))
    consts = [r_k, ln_w, ln_b]
    seq = [a.reshape(b, s, RW_DIM) for a in (r, k, v, al, be, lw, g)]
    return pl.pallas_call(
        functools.partial(_rw_chunk_kernel, rows),
        grid=(b // rows, s // CHUNK),
        in_specs=[blk] * 7 + [_pspec(c) for c in consts] + [_const_spec(ones_bd.shape)],
        out_specs=blk,
        out_shape=jax.ShapeDtypeStruct((b, s, RW_DIM), BF16),
        scratch_shapes=[pltpu.VMEM((rows, RW_DIM // RW_GROUP, RW_GROUP, RW_GROUP), F32)],
        compiler_params=_cparams("parallel", "arbitrary"),
        name="rw_chunk",
    )(*seq, *_pargs(consts), ones_bd).reshape(t, RW_DIM)


def _merge_kernel(om_ref, or_ref, oc_ref, g0_ref, g1_ref, g2_ref, h_ref, wb_ref, wo_ref, g_ref, o_ref):
    merged = None
    for n, (b_ref, zg_ref) in enumerate(((om_ref, g0_ref), (or_ref, g1_ref), (oc_ref, g2_ref))):
        y = _dot(b_ref[...], wb_ref[n * BRANCH_DIM:(n + 1) * BRANCH_DIM, :])
        gate = jax.nn.sigmoid(zg_ref[...].astype(F32))
        merged = gate * y if merged is None else merged + gate * y
    out = _dot(merged.astype(BF16), wo_ref[...])
    o_ref[...] = h_ref[...] + _rms(out, g_ref[...])


def _merge(o_mla, o_rw, o_ca, z, h, w_branch, w_out, g):
    t = h.shape[0]
    tm = min(t, 512)
    row = lambda w: pl.BlockSpec((tm, w), lambda i: (i, 0))
    return pl.pallas_call(
        _merge_kernel,
        grid=(t // tm,),
        in_specs=[
            row(BRANCH_DIM), row(BRANCH_DIM), row(BRANCH_DIM),
            pl.BlockSpec((tm, D_MODEL), lambda i: (i, 0)),
            pl.BlockSpec((tm, D_MODEL), lambda i: (i, 1)),
            pl.BlockSpec((tm, D_MODEL), lambda i: (i, 2)),
            row(D_MODEL),
            _pspec(w_branch), _pspec(w_out), _pspec(g),
        ],
        out_specs=row(D_MODEL),
        out_shape=jax.ShapeDtypeStruct((t, D_MODEL), F32),
        compiler_params=_cparams("parallel"),
        name="merge",
    )(o_mla, o_rw, o_ca, z, z, z, h, *_pargs([w_branch, w_out, g]))


FF_SPLIT = 4


def _ff_ple_kernel(h_ref, p_ref, g1_ref, w1_ref, w2_ref, g2_ref, wg_ref, wp_ref, o_ref):
    h = h_ref[...]
    f = _rms(h, g1_ref[...]).astype(BF16)
    cw = D_FF // FF_SPLIT
    acc = None
    for c in range(FF_SPLIT):
        a = jnp.maximum(_dot(f, w1_ref[:, c * cw:(c + 1) * cw]), 0.0)
        part = _dot((a * a).astype(BF16), w2_ref[c * cw:(c + 1) * cw, :])
        acc = part if acc is None else acc + part
    h = h + _rms(acc, g2_ref[...])
    gate = jax.nn.sigmoid(_dot(h.astype(BF16), wg_ref[...]))
    o_ref[...] = h + gate * _dot(p_ref[...].astype(BF16), wp_ref[...])


def _ff_ple(h, p, consts):
    t = h.shape[0]
    tm = min(t, 512)
    row = lambda w: pl.BlockSpec((tm, w), lambda i: (i, 0))
    p_stack, layer = p
    return pl.pallas_call(
        _ff_ple_kernel,
        grid=(t // tm,),
        in_specs=[row(D_MODEL), pl.BlockSpec((None, tm, D_PLE), lambda i: (layer, i, 0))]
        + [_pspec(c) for c in consts],
        out_specs=row(D_MODEL),
        out_shape=jax.ShapeDtypeStruct((t, D_MODEL), F32),
        compiler_params=_cparams("parallel"),
        name="ff_ple",
    )(h, p_stack, *_pargs(consts))


def _rot_half_cols(w):
    half = w.shape[-1] // 2
    return jnp.concatenate([-w[..., half:], w[..., :half]], axis=-1)


def _pack_w_in(w):
    mla_cols = MLA_Q_RANK + MLA_KV_RANK + MLA_ROPE
    rw_cols = 3 * RW_DIM + RW_LORA
    w_mla = w[..., :mla_cols]
    w_rw = w[..., mla_cols:mla_cols + rw_cols]
    w_ca = w[..., mla_cols + rw_cols:mla_cols + rw_cols + 3 * CA_DIM]
    w_gate = w[..., mla_cols + rw_cols + 3 * CA_DIM:]
    w_kr = w_mla[..., MLA_Q_RANK + MLA_KV_RANK:]
    z64 = jnp.zeros(w.shape[:-1] + (64,), w.dtype)
    packed = jnp.concatenate(
        [w_gate, w_ca[..., CA_DIM:2 * CA_DIM], w_rw, w_mla[..., :MLA_Q_RANK + MLA_KV_RANK], w_kr, z64,
         _rot_half_cols(w_kr), z64], axis=-1)
    wt = jnp.concatenate([w_ca[..., :CA_DIM] * (CA_HEAD ** -0.5 * LOG2E), w_ca[..., 2 * CA_DIM:]], axis=-1)
    return packed.astype(BF16), jnp.swapaxes(wt, 1, 2).astype(BF16)


def _pack_w_uq(w):
    nl, r, _ = w.shape
    w = w.reshape(nl, r, MLA_HEADS, MLA_QK)
    z64 = jnp.zeros((nl, r, MLA_HEADS, 64), w.dtype)
    rot = _rot_half_cols(w[..., MLA_NOPE:])
    wq = jnp.concatenate([w, z64], axis=-1).reshape(nl, r, MLA_HEADS * MLA_HEAD_PAD)
    wqr = jnp.concatenate([rot, z64], axis=-1).reshape(nl, r, MLA_HEADS * 128)
    return jnp.swapaxes(wq, 1, 2).astype(BF16), jnp.swapaxes(wqr, 1, 2).astype(BF16)


def _pack_w_ukv(w):
    nl, r, _ = w.shape
    w = w.reshape(nl, r, MLA_HEADS, MLA_NOPE + MLA_V)
    wk = w[..., :MLA_NOPE].reshape(nl, r, MLA_HEADS * MLA_NOPE)
    wv = w[..., MLA_NOPE:].reshape(nl, r, MLA_HEADS * MLA_V)
    return wk.astype(BF16), jnp.swapaxes(wv, 1, 2).astype(BF16)


def kernel(x, p, positions, pre_mix_g, w_in, mla_q_norm_g, mla_kv_norm_g, mla_w_uq, mla_w_ukv, rw_mu, rw_w0, rw_w_up, rw_a0, rw_a_up, rw_g_up, rw_k_k, rw_k_a, rw_r_k, rw_ln_w, rw_ln_b, ca_rel_bias, w_branch, w_out, post_mix_g, pre_ff_g, w_ff1, w_ff2, post_ff_g, w_ple_gate, w_ple_proj):
    b, s, d = x.shape
    t = b * s
    depth = w_in.shape[0]
    row = lambda a: a.reshape(depth, 1, -1).astype(F32)
    bf = lambda a: a.astype(BF16)
    w_in_p, w_in_t = _pack_w_in(w_in)
    wqt, wqrt = _pack_w_uq(mla_w_uq)
    wk, wvt = _pack_w_ukv(mla_w_ukv)
    mu = row(rw_mu)
    stacks = dict(
        in_proj=[row(pre_mix_g), w_in_p, w_in_t],
        mla_proj=[row(mla_q_norm_g), row(mla_kv_norm_g), wqt, wqrt, wk, wvt],
        rw_prep=[mu[..., :RW_DIM], mu[..., RW_DIM:2 * RW_DIM], mu[..., 2 * RW_DIM:3 * RW_DIM], mu[..., 3 * RW_DIM:],
                 row(rw_w0), bf(rw_w_up), row(rw_a0), bf(rw_a_up), bf(rw_g_up), row(rw_k_k), row(rw_k_a)],
        rw_chunk=[row(rw_r_k), row(rw_ln_w), row(rw_ln_b)],
        merge=[bf(w_branch), bf(w_out), row(post_mix_g)],
        ff_ple=[row(pre_ff_g), bf(w_ff1), bf(w_ff2), row(post_ff_g), bf(w_ple_gate), bf(w_ple_proj)],
    )
    p3 = p.reshape(depth, t, -1)
    head_id = jnp.arange(RW_DIM) // RW_HEAD
    ones_bd = (head_id[:, None] == head_id[None, :]).astype(BF16)
    cs, sn, cst, snt = _rope_table(positions)
    h = x.reshape(t, d)
    for i in range(depth):
        prm = {name: [(a, i) for a in arrs] for name, arrs in stacks.items()}
        z, zt = _in_proj(h, *prm["in_proj"])
        qt, kn, kr, vt = _mla_proj(z, cs, sn, cst, snt, *prm["mla_proj"])
        o_mla = _mla_attn(qt, kn, kr, vt, b).reshape(t, -1)
        rr, rk, rv, al, be, lw, rg = _rw_prep(z, s, prm["rw_prep"], ones_bd)
        o_rw = _rw_chunk(rr, rk, rv, al, be, lw, rg, b, *prm["rw_chunk"])
        o_ca = _ca_attn(zt, z, _ca_bias_table(ca_rel_bias[i]), b).reshape(t, -1)
        h = _merge(o_mla, o_rw, o_ca, z, h, *prm["merge"])
        h = _ff_ple(h, (p3, i), prm["ff_ple"])
    return h.reshape(b, s, d)
```

```python
import functools
import math

import jax
import jax.numpy as jnp
from jax import lax
from jax.experimental import pallas as pl
from jax.experimental.pallas import tpu as pltpu

F32 = jnp.float32
BF16 = jnp.bfloat16

D_MODEL = 1024
D_PLE = 256
D_FF = 4 * D_MODEL
NORM_EPS = 1e-6
CHUNK = 64

MLA_HEADS = 4
MLA_NOPE = 128
MLA_ROPE = 64
MLA_V = 128
MLA_Q_RANK = 256
MLA_KV_RANK = 128
ROPE_THETA = 10000.0
MLA_QK = MLA_NOPE + MLA_ROPE
MLA_HEAD_PAD = 256

RW_HEADS = 8
RW_HEAD = 64
RW_DIM = RW_HEADS * RW_HEAD
RW_DECAY_LORA = 64
RW_AAA_LORA = 64
RW_GATE_LORA = 128
RW_LORA = RW_DECAY_LORA + RW_AAA_LORA + RW_GATE_LORA
RW_LN_EPS = 64e-5

CA_HEADS = 8
CA_HEAD = 64
CA_DIM = CA_HEADS * CA_HEAD
CA_LEFT_CHUNKS = 8
CA_PAD = CA_LEFT_CHUNKS * CHUNK
REL_MIN = -(CHUNK - 1)
REL_MAX = 256

N_BRANCH = 3
BRANCH_DIM = 512
GATE_COLS = N_BRANCH * D_MODEL

OFF_GATE = 0
OFF_CAK = GATE_COLS
OFF_RW = OFF_CAK + CA_DIM
OFF_LORA = OFF_RW + 3 * RW_DIM
OFF_ZQ = OFF_LORA + RW_LORA
OFF_ZKV = OFF_ZQ + MLA_Q_RANK
OFF_KR = OFF_ZKV + MLA_KV_RANK
OFF_KRR = OFF_KR + 128
IN_PACKED = OFF_KRR + 128
LOG2E = math.log2(math.e)

VMEM_LIMIT = 56 * 1024 * 1024


def _cparams(*sem):
    return pltpu.CompilerParams(dimension_semantics=sem, vmem_limit_bytes=VMEM_LIMIT)


def _rms(x, g):
    return x * lax.rsqrt(jnp.mean(x * x, axis=-1, keepdims=True) + NORM_EPS) * g


def _dot(a, b):
    return jnp.dot(a, b, preferred_element_type=F32)


def _dot_nt(a, b):
    return lax.dot_general(a, b, (((1,), (1,)), ((), ())), preferred_element_type=F32)


def _dot_tn(a, b):
    return lax.dot_general(a, b, (((0,), (0,)), ((), ())), preferred_element_type=F32)


def _const_spec(shape):
    nd = len(shape)
    return pl.BlockSpec(shape, lambda *_: (0,) * nd, pipeline_mode=pl.Buffered(1))


def _pshape(param):
    return param[0].shape[1:]


def _pspec(param):
    stack, layer = param
    nd = stack.ndim - 1
    return pl.BlockSpec((None,) + stack.shape[1:], lambda *_: (layer,) + (0,) * nd, pipeline_mode=pl.Buffered(1))


def _pargs(params):
    return [stack for stack, _ in params]


def _rope_table_kernel(pos_c_ref, pos_r_ref, freq_r_ref, freq_c_ref, cs_ref, sn_ref, cst_ref, snt_ref):
    ang = pos_c_ref[...] * freq_r_ref[...]
    live = lax.broadcasted_iota(jnp.int32, ang.shape, 1) < MLA_ROPE
    cs_ref[...] = jnp.where(live, jnp.cos(ang), 0.0)
    sn_ref[...] = jnp.where(live, jnp.sin(ang), 0.0)
    ang_t = freq_c_ref[...] * pos_r_ref[...]
    live_t = lax.broadcasted_iota(jnp.int32, ang_t.shape, 0) < MLA_ROPE
    cst_ref[...] = jnp.where(live_t, jnp.cos(ang_t), 0.0)
    snt_ref[...] = jnp.where(live_t, jnp.sin(ang_t), 0.0)


def _rope_table(positions):
    t = positions.size
    tm = min(t, 2048)
    half = MLA_ROPE // 2
    inv_freq = 1.0 / (ROPE_THETA ** (jnp.arange(half, dtype=F32) / half))
    freq = jnp.concatenate([inv_freq, inv_freq, jnp.zeros((128 - MLA_ROPE,), F32)])
    pos = positions.astype(F32)
    return pl.pallas_call(
        _rope_table_kernel,
        grid=(t // tm,),
        in_specs=[pl.BlockSpec((tm, 1), lambda i: (i, 0)), pl.BlockSpec((1, tm), lambda i: (0, i)),
                  _const_spec((1, 128)), _const_spec((128, 1))],
        out_specs=[pl.BlockSpec((tm, 128), lambda i: (i, 0))] * 2 + [pl.BlockSpec((128, tm), lambda i: (0, i))] * 2,
        out_shape=[jax.ShapeDtypeStruct((t, 128), F32)] * 2 + [jax.ShapeDtypeStruct((128, t), F32)] * 2,
        compiler_params=_cparams("parallel"),
        name="rope_table",
    )(pos.reshape(t, 1), pos.reshape(1, t), freq[None, :], freq[:, None])


IN_CHUNK = 1536


def _in_proj_kernel(x_ref, g_ref, w_ref, wt_ref, o_ref, ot_ref):
    xn = _rms(x_ref[...], g_ref[...]).astype(BF16)
    n = w_ref.shape[1]
    for c in range(0, n, IN_CHUNK):
        e = min(c + IN_CHUNK, n)
        o_ref[:, c:e] = _dot(xn, w_ref[:, c:e]).astype(o_ref.dtype)
    ot_ref[...] = _dot_nt(wt_ref[...], xn).astype(ot_ref.dtype)


def _in_proj(h, g, w, wt):
    t, d = h.shape
    n = _pshape(w)[1]
    nt = _pshape(wt)[0]
    tm = min(t, 512)
    return pl.pallas_call(
        _in_proj_kernel,
        grid=(t // tm,),
        in_specs=[pl.BlockSpec((tm, d), lambda i: (i, 0)), _pspec(g), _pspec(w), _pspec(wt)],
        out_specs=[pl.BlockSpec((tm, n), lambda i: (i, 0)), pl.BlockSpec((nt, tm), lambda i: (0, i))],
        out_shape=[jax.ShapeDtypeStruct((t, n), BF16), jax.ShapeDtypeStruct((nt, t), BF16)],
        compiler_params=_cparams("parallel"),
        name="in_proj",
    )(h, *_pargs([g, w, wt]))


def _mla_proj_kernel(zq_ref, zkv_ref, zkr_ref, zkrr_ref, cs_ref, sn_ref, cst_ref, snt_ref, gq_ref, gkv_ref,
                     wqt_ref, wqrt_ref, wk_ref, wvt_ref, qt_ref, k_ref, kr_ref, vt_ref):
    cst = cst_ref[...]
    snt = snt_ref[...]
    qn = _rms(zq_ref[...].astype(F32), gq_ref[...]).astype(BF16)
    qt = _dot_nt(wqt_ref[...], qn)
    qrt = _dot_nt(wqrt_ref[...], qn)
    scale = MLA_QK ** -0.5 * LOG2E
    for h in range(MLA_HEADS):
        o = h * MLA_HEAD_PAD
        qt_ref[o:o + 128, :] = (qt[o:o + 128] * scale).astype(BF16)
        rope = qt[o + 128:o + 256] * cst + qrt[h * 128:(h + 1) * 128] * snt
        qt_ref[o + 128:o + 256, :] = (rope * scale).astype(BF16)
    kvn = _rms(zkv_ref[...].astype(F32), gkv_ref[...]).astype(BF16)
    k_ref[...] = _dot(kvn, wk_ref[...]).astype(BF16)
    vt_ref[...] = _dot_nt(wvt_ref[...], kvn).astype(BF16)
    kr_ref[...] = (zkr_ref[...].astype(F32) * cs_ref[...] + zkrr_ref[...].astype(F32) * sn_ref[...]).astype(BF16)


def _mla_proj(z, cs, sn, cst, snt, gq, gkv, wqt, wqrt, wk, wvt):
    t = z.shape[0]
    tm = min(t, 1024)
    hv = MLA_HEADS * MLA_V
    hq = MLA_HEADS * MLA_HEAD_PAD
    row = lambda w: pl.BlockSpec((tm, w), lambda i: (i, 0))
    col = lambda w: pl.BlockSpec((w, tm), lambda i: (0, i))
    consts = [gq, gkv, wqt, wqrt, wk, wvt]
    return pl.pallas_call(
        _mla_proj_kernel,
        grid=(t // tm,),
        in_specs=[pl.BlockSpec((tm, MLA_Q_RANK), lambda i: (i, OFF_ZQ // MLA_Q_RANK)),
                  pl.BlockSpec((tm, MLA_KV_RANK), lambda i: (i, OFF_ZKV // MLA_KV_RANK)),
                  pl.BlockSpec((tm, 128), lambda i: (i, OFF_KR // 128)),
                  pl.BlockSpec((tm, 128), lambda i: (i, OFF_KRR // 128)),
                  row(128), row(128), col(128), col(128)] + [_pspec(c) for c in consts],
        out_specs=[col(hq), row(hv), row(128), col(hv)],
        out_shape=[
            jax.ShapeDtypeStruct((hq, t), BF16),
            jax.ShapeDtypeStruct((t, hv), BF16),
            jax.ShapeDtypeStruct((t, 128), BF16),
            jax.ShapeDtypeStruct((hv, t), BF16),
        ],
        compiler_params=_cparams("parallel"),
        name="mla_proj",
    )(z, z, z, z, cs, sn, cst, snt, *_pargs(consts))


MLA_TQ = 512


def _mla_attn_kernel(qt_ref, k_ref, kr_ref, vt_ref, o_ref, m_ref, l_ref, acc_ref):
    i = pl.program_id(1)
    tq = MLA_TQ
    m_ref[...] = jnp.full(m_ref.shape, -1e30, F32)
    l_ref[...] = jnp.zeros(l_ref.shape, F32)
    acc_ref[...] = jnp.zeros(acc_ref.shape, F32)

    def run(blocks):
        def scores(j, masked, h):
            start = pl.multiple_of(j * tq, tq)
            kh = jnp.concatenate([k_ref[0, pl.ds(start, tq), h * 128:(h + 1) * 128],
                                  kr_ref[0, pl.ds(start, tq), :]], axis=1)
            s = _dot(kh, qt_ref[h * MLA_HEAD_PAD:(h + 1) * MLA_HEAD_PAD, :])
            if masked:
                kc = lax.broadcasted_iota(jnp.int32, s.shape, 0) // CHUNK
                qc = lax.broadcasted_iota(jnp.int32, s.shape, 1) // CHUNK
                s = jnp.where(kc <= qc, s, -1e30)
            return s

        def update(j, h, s):
            start = pl.multiple_of(j * tq, tq)
            m_prev = m_ref[h]
            m_new = jnp.maximum(m_prev, jnp.max(s, axis=0, keepdims=True))
            a = jnp.exp2(m_prev - m_new)
            p = jnp.exp2(s - m_new)
            l_ref[h] = a * l_ref[h] + jnp.sum(p, axis=0, keepdims=True)
            vh = vt_ref[h * MLA_V:(h + 1) * MLA_V, pl.ds(start, tq)]
            acc_ref[h] = a * acc_ref[h] + _dot(vh, p.astype(BF16))
            m_ref[h] = m_new

        items = [(j, masked, h) for j, masked in blocks for h in range(MLA_HEADS)]
        s_cur = scores(*items[0])
        for prev, nxt in zip(items[:-1], items[1:]):
            s_next = scores(*nxt)
            update(prev[0], prev[2], s_cur)
            s_cur = s_next
        update(items[-1][0], items[-1][2], s_cur)

    def body(jj, c):
        run([(2 * jj, False), (2 * jj + 1, False)])
        return c

    lax.fori_loop(0, i // 2, body, 0)
    pl.when(i % 2 == 1)(lambda: run([(i - 1, False), (i, True)]))
    pl.when(i % 2 == 0)(lambda: run([(i, True)]))
    for h in range(MLA_HEADS):
        o_ref[0, :, h * MLA_V:(h + 1) * MLA_V] = (acc_ref[h] / l_ref[h]).T.astype(o_ref.dtype)


def _mla_attn(qt, k, kr, vt, b):
    hq, t = qt.shape
    s = t // b
    tq = MLA_TQ
    nq = s // tq
    hv = MLA_HEADS * MLA_V
    full = lambda w: pl.BlockSpec((1, s, w), lambda bi, i: (bi, 0, 0))
    return pl.pallas_call(
        _mla_attn_kernel,
        grid=(b, nq),
        in_specs=[
            pl.BlockSpec((hq, tq), lambda bi, i: (0, bi * nq + i)),
            full(hv), full(128),
            pl.BlockSpec((hv, s), lambda bi, i: (0, bi)),
        ],
        out_specs=pl.BlockSpec((1, tq, hv), lambda bi, i: (bi, i, 0)),
        out_shape=jax.ShapeDtypeStruct((b, s, hv), BF16),
        scratch_shapes=[pltpu.VMEM((MLA_HEADS, 1, tq), F32), pltpu.VMEM((MLA_HEADS, 1, tq), F32),
                        pltpu.VMEM((MLA_HEADS, MLA_V, tq), F32)],
        compiler_params=_cparams("parallel", "arbitrary"),
        name="mla_attn",
    )(qt, k.reshape(b, s, hv), kr.reshape(b, s, 128), vt)


CA_TQ = 256
CA_BAND_BLK = CA_TQ + CA_PAD


def _ca_attn_kernel(qt_ref, k_ref, vt_ref, bias_ref, o_ref):
    i = pl.program_id(1)
    lo = lax.broadcasted_iota(jnp.int32, (128, CA_TQ), 0) < CA_HEAD
    zero = jnp.zeros((), BF16)

    def block(k_start, n_keys, bias_off):
        def scores(h):
            sl = slice((h // 2) * 128, (h // 2 + 1) * 128)
            k2 = k_ref[0, pl.ds(k_start, n_keys), sl]
            qh = jnp.where(lo if h % 2 == 0 else ~lo, qt_ref[sl, :], zero)
            return _dot(k2, qh) + bias_ref[h, bias_off:bias_off + n_keys, :]

        def attend(h, s):
            p = jnp.exp2(s - jnp.max(s, axis=0, keepdims=True))
            l = jnp.sum(p, axis=0, keepdims=True)
            vh = vt_ref[h * CA_HEAD:(h + 1) * CA_HEAD, pl.ds(k_start, n_keys)]
            return _dot(vh, p.astype(BF16)) / l

        outs = []
        s_cur = scores(0)
        for h in range(1, CA_HEADS):
            s_next = scores(h)
            outs.append(attend(h - 1, s_cur))
            s_cur = s_next
        outs.append(attend(CA_HEADS - 1, s_cur))
        for pair in range(CA_HEADS // 2):
            both = jnp.concatenate(outs[2 * pair:2 * pair + 2], axis=0)
            o_ref[0, :, pair * 128:(pair + 1) * 128] = both.T.astype(o_ref.dtype)

    lead = CA_PAD // CA_TQ
    for j in range(lead):
        pl.when(i == j)(functools.partial(block, 0, (j + 1) * CA_TQ, CA_PAD - j * CA_TQ))
    pl.when(i >= lead)(lambda: block(pl.multiple_of(i * CA_TQ - CA_PAD, CA_TQ), CA_BAND_BLK, 0))


def _ca_attn(zt, z, bias, b):
    t = zt.shape[1]
    s = t // b
    nq = s // CA_TQ
    return pl.pallas_call(
        _ca_attn_kernel,
        grid=(b, nq),
        in_specs=[
            pl.BlockSpec((CA_DIM, CA_TQ), lambda bi, i: (0, bi * nq + i)),
            pl.BlockSpec((1, s, CA_DIM), lambda bi, i: (bi, 0, OFF_CAK // CA_DIM)),
            pl.BlockSpec((CA_DIM, s), lambda bi, i: (1, bi)),
            _const_spec(bias.shape),
        ],
        out_specs=pl.BlockSpec((1, CA_TQ, CA_DIM), lambda bi, i: (bi, i, 0)),
        out_shape=jax.ShapeDtypeStruct((b, s, CA_DIM), BF16),
        compiler_params=_cparams("parallel", "arbitrary"),
        name="ca_attn",
    )(zt, z.reshape(b, s, -1), zt, bias)


def _ca_bias_table(rel_bias):
    width = CA_TQ + CA_BAND_BLK
    m = jnp.arange(width)
    delta = jnp.where(m < CA_BAND_BLK, m, m - width)
    idx = jnp.clip(CA_PAD - delta, REL_MIN, REL_MAX) - REL_MIN
    t1 = rel_bias.astype(F32)[idx].T * LOG2E
    flat = jnp.tile(t1, (1, CA_TQ))[:, :CA_TQ * (width - 1)]
    bias = flat.reshape(CA_HEADS, CA_TQ, width - 1)[:, :, :CA_BAND_BLK]
    rc = jnp.arange(CA_TQ)[:, None] // CHUNK
    cc = jnp.arange(CA_BAND_BLK)[None, :] // CHUNK
    ok = (cc >= rc) & (cc <= rc + CA_LEFT_CHUNKS)
    return jnp.where(ok[None], bias, -1e30).transpose(0, 2, 1)


def _split_bf16(x):
    hi = x.astype(BF16)
    lo = (x - hi.astype(F32)).astype(BF16)
    return hi, lo


def _head_sum(x, ones_bd):
    hi, lo = _split_bf16(x)
    return _dot(hi, ones_bd) + _dot(lo, ones_bd)


def _rw_prep_kernel(seq_len, zr_ref, zk_ref, zv_ref, zl_ref, pr_ref, pk_ref, pv_ref, plr_ref,
                    mu_r, mu_k, mu_v, mu_l, w0, w_up, a0, a_up, g_up, k_k, k_a, ones_bd,
                    r_o, k_o, v_o, al_o, be_o, lw_o, g_o):
    tm = zr_ref.shape[0]
    first = (pl.program_id(0) * tm) % seq_len == 0
    row0 = lax.broadcasted_iota(jnp.int32, (tm, 1), 0) == 0

    def shift(z_ref, p_ref, mu):
        z = z_ref[...].astype(F32)
        last = jnp.where(first, 0.0, p_ref[...].astype(F32)[15:16, :])
        prev = jnp.where(row0, last, pltpu.roll(z, 1, axis=0))
        return z + (prev - z) * mu[...]

    r = shift(zr_ref, pr_ref, mu_r)
    k = shift(zk_ref, pk_ref, mu_k)
    v = shift(zv_ref, pv_ref, mu_v)
    xl = shift(zl_ref, plr_ref, mu_l)
    xw = jnp.tanh(xl[:, :RW_DECAY_LORA]).astype(BF16)
    xa = xl[:, RW_DECAY_LORA:RW_DECAY_LORA + RW_AAA_LORA].astype(BF16)
    xg = jax.nn.sigmoid(xl[:, RW_DECAY_LORA + RW_AAA_LORA:]).astype(BF16)
    lw = -math.exp(-0.5) * jax.nn.sigmoid(w0[...] + _dot(xw, w_up[...]))
    a = jax.nn.sigmoid(a0[...] + _dot(xa, a_up[...]))
    g = _dot(xg, g_up[...])
    kk = k * k_k[...]
    ss = _head_sum(kk * kk, ones_bd[...])
    kk = kk * lax.rsqrt(jnp.maximum(ss, 1e-24))
    k = k * (1.0 + (a - 1.0) * k_a[...])
    r_o[...] = r.astype(BF16)
    k_o[...] = k.astype(BF16)
    v_o[...] = v.astype(BF16)
    al_o[...] = (-kk).astype(BF16)
    be_o[...] = (kk * a).astype(BF16)
    lw_o[...] = lw
    g_o[...] = g.astype(BF16)


def _rw_prep(z, seq_len, consts, ones_bd):
    t = z.shape[0]
    tm = min(seq_len, 512)
    c512 = lambda j: pl.BlockSpec((tm, RW_DIM), lambda i: (i, OFF_RW // RW_DIM + j))
    prev = lambda w, jb: pl.BlockSpec((16, w), lambda i: (jnp.maximum(i * (tm // 16) - 1, 0), jb))
    out = lambda dt: jax.ShapeDtypeStruct((t, RW_DIM), dt)
    return pl.pallas_call(
        functools.partial(_rw_prep_kernel, seq_len),
        grid=(t // tm,),
        in_specs=[
            c512(0), c512(1), c512(2),
            pl.BlockSpec((tm, RW_LORA), lambda i: (i, OFF_LORA // RW_LORA)),
            prev(RW_DIM, OFF_RW // RW_DIM), prev(RW_DIM, OFF_RW // RW_DIM + 1),
            prev(RW_DIM, OFF_RW // RW_DIM + 2), prev(RW_LORA, OFF_LORA // RW_LORA),
        ] + [_pspec(c) for c in consts] + [_const_spec(ones_bd.shape)],
        out_specs=[pl.BlockSpec((tm, RW_DIM), lambda i: (i, 0))] * 7,
        out_shape=[out(BF16), out(BF16), out(BF16), out(BF16), out(BF16), out(F32), out(BF16)],
        compiler_params=_cparams("parallel"),
        name="rw_prep",
    )(z, z, z, z, z, z, z, z, *_pargs(consts), ones_bd)


RW_GROUP = 256


def _rw_chunk_kernel(rows, r_ref, k_ref, v_ref, al_ref, be_ref, lw_ref, g_ref, rk_ref, lnw_ref, lnb_ref,
                     bd_ref, o_ref, s_ref):
    L = CHUNK
    W = RW_GROUP
    reps = W // RW_HEAD

    @pl.when(pl.program_id(1) == 0)
    def _():
        s_ref[...] = jnp.zeros(s_ref.shape, F32)

    bdm = bd_ref[...]

    def bd(x):
        return jnp.concatenate([x.astype(BF16)] * reps, axis=0) * bdm

    row = lax.broadcasted_iota(jnp.int32, (L, W), 0)
    sub = lax.broadcasted_iota(jnp.int32, (L, W), 1) % RW_HEAD
    strict = sub < row
    incl = sub <= row
    eye = (sub == row).astype(F32)
    tri = (lax.broadcasted_iota(jnp.int32, (L, L), 1)
           <= lax.broadcasted_iota(jnp.int32, (L, L), 0)).astype(BF16)
    inv = 1.0 / RW_HEAD

    def instance(b, gi):
        sl = slice(gi * W, (gi + 1) * W)
        lw = lw_ref[b, :, sl]
        lw_hi, lw_lo = _split_bf16(lw)
        cum = _dot(tri, lw_hi) + _dot(tri, lw_lo)
        yield
        cum_l = cum[L - 1:L, :]
        r = r_ref[b, :, sl].astype(F32)
        k = k_ref[b, :, sl].astype(F32)
        v = v_ref[b, :, sl].astype(F32)
        be = be_ref[b, :, sl].astype(F32)
        e_neg = jnp.exp(-cum)
        e_tail = jnp.exp(cum_l - cum)
        ar = jnp.concatenate([al_ref[b, :, sl].astype(F32) * jnp.exp(cum - lw), r * jnp.exp(cum)],
                             axis=0).astype(BF16)
        s0 = s_ref[b, gi]
        ar_s = _dot_nt(ar, s0.astype(BF16))
        a_b = _dot_nt(ar, bd(be * e_neg))
        a_k = _dot_nt(ar, bd(k * e_neg))
        yield
        n = jnp.where(strict, a_b[:L], 0.0)
        a_ak = jnp.where(strict, a_k[:L], 0.0)
        a_rb = jnp.where(incl, a_b[L:], 0.0)
        a_rk = jnp.where(incl, a_k[L:], 0.0)
        p = eye + n
        nk = _dot(n.astype(BF16), bd(n))
        bd_v = bd(v)
        x0 = _dot(a_ak.astype(BF16), bd_v)
        yield
        steps = int(math.log2(L)) - 1
        for it in range(steps):
            m = bd(nk)
            if it + 1 < steps:
                res = _dot(jnp.concatenate([p, nk], axis=0).astype(BF16), m)
                p = p + res[:L]
                nk = res[L:]
            else:
                p = p + _dot(p.astype(BF16), m)
            yield
        u = _dot(p.astype(BF16), bd(ar_s[:L] + x0))
        yield
        y = ar_s[L:] + _dot(jnp.concatenate([a_rb, a_rk], axis=1).astype(BF16),
                            jnp.concatenate([bd(u), bd_v], axis=0))
        uv = jnp.concatenate([u, v], axis=0).astype(BF16)
        bk = jnp.concatenate([be * e_tail, k * e_tail], axis=0).astype(BF16)
        s_ref[b, gi] = s0 * jnp.exp(cum_l) + _dot_tn(uv, bk) * bdm.astype(F32)
        yield
        hi, lo = _split_bf16(jnp.concatenate([y, r * k * rk_ref[:, sl]], axis=0))
        sums = _dot(jnp.concatenate([hi, lo], axis=0), bdm)
        sums = sums[:2 * L] + sums[2 * L:]
        yield
        d = y - sums[:L] * inv
        hi, lo = _split_bf16(d * d)
        var = _dot(jnp.concatenate([hi, lo], axis=0), bdm)
        var = (var[:L] + var[L:]) * inv
        yield
        yn = d * lax.rsqrt(var + RW_LN_EPS) * lnw_ref[:, sl] + lnb_ref[:, sl]
        out = (yn + sums[L:] * v) * g_ref[b, :, sl].astype(F32)
        o_ref[b, :, sl] = out.astype(o_ref.dtype)

    live = [instance(b, gi) for b in range(rows) for gi in range(RW_DIM // W)]
    while live:
        live = [g for g in live if next(g, True) is None]


def _rw_chunk(r, k, v, al, be, lw, g, b, r_k, ln_w, ln_b):
    t = r.shape[0]
    s = t // b
    rows = math.gcd(b, 8)
    gid = jnp.arange(RW_GROUP) // RW_HEAD
    ones_bd = (gid[:, None] == gid[None, :]).astype(BF16)
    blk = pl.BlockSpec((rows, CHUNK, RW_DIM), lambda bb, c: (bb, c, 0))
    consts = [r_k, ln_w, ln_b]
    seq = [a.reshape(b, s, RW_DIM) for a in (r, k, v, al, be, lw, g)]
    return pl.pallas_call(
        functools.partial(_rw_chunk_kernel, rows),
        grid=(b // rows, s // CHUNK),
        in_specs=[blk] * 7 + [_pspec(c) for c in consts] + [_const_spec(ones_bd.shape)],
        out_specs=blk,
        out_shape=jax.ShapeDtypeStruct((b, s, RW_DIM), BF16),
        scratch_shapes=[pltpu.VMEM((rows, RW_DIM // RW_GROUP, RW_GROUP, RW_GROUP), F32)],
        compiler_params=_cparams("parallel", "arbitrary"),
        name="rw_chunk",
    )(*seq, *_pargs(consts), ones_bd).reshape(t, RW_DIM)


FF_SPLIT = 4


def _post_kernel(om_ref, or_ref, oc_ref, g0_ref, g1_ref, g2_ref, h_ref, p_ref, wb_ref, wo_ref, gm_ref,
                 g1n_ref, w1_ref, w2_ref, g2n_ref, wg_ref, wp_ref, o_ref):
    merged = None
    for n, (b_ref, zg_ref) in enumerate(((om_ref, g0_ref), (or_ref, g1_ref), (oc_ref, g2_ref))):
        y = _dot(b_ref[...], wb_ref[n * BRANCH_DIM:(n + 1) * BRANCH_DIM, :])
        gate = jax.nn.sigmoid(zg_ref[...].astype(F32))
        merged = gate * y if merged is None else merged + gate * y
    h = h_ref[...] + _rms(_dot(merged.astype(BF16), wo_ref[...]), gm_ref[...])
    f = _rms(h, g1n_ref[...]).astype(BF16)
    cw = D_FF // FF_SPLIT
    acc = None
    for c in range(FF_SPLIT):
        a = jnp.maximum(_dot(f, w1_ref[:, c * cw:(c + 1) * cw]), 0.0)
        part = _dot((a * a).astype(BF16), w2_ref[c * cw:(c + 1) * cw, :])
        acc = part if acc is None else acc + part
    h = h + _rms(acc, g2n_ref[...])
    gate = jax.nn.sigmoid(_dot(h.astype(BF16), wg_ref[...]))
    o_ref[...] = h + gate * _dot(p_ref[...].astype(BF16), wp_ref[...])


def _post(o_mla, o_rw, o_ca, z, h, p, consts):
    t = h.shape[0]
    tm = min(t, 512)
    row = lambda w: pl.BlockSpec((tm, w), lambda i: (i, 0))
    gate = lambda n: pl.BlockSpec((tm, D_MODEL), lambda i: (i, OFF_GATE // D_MODEL + n))
    p_stack, layer = p
    return pl.pallas_call(
        _post_kernel,
        grid=(t // tm,),
        in_specs=[row(BRANCH_DIM), row(BRANCH_DIM), row(BRANCH_DIM), gate(0), gate(1), gate(2), row(D_MODEL),
                  pl.BlockSpec((None, tm, D_PLE), lambda i: (layer, i, 0))] + [_pspec(c) for c in consts],
        out_specs=row(D_MODEL),
        out_shape=jax.ShapeDtypeStruct((t, D_MODEL), F32),
        compiler_params=_cparams("parallel"),
        name="post",
    )(o_mla, o_rw, o_ca, z, z, z, h, p_stack, *_pargs(consts))


def _rot_half_cols(w):
    half = w.shape[-1] // 2
    return jnp.concatenate([-w[..., half:], w[..., :half]], axis=-1)


def _pack_w_in(w):
    mla_cols = MLA_Q_RANK + MLA_KV_RANK + MLA_ROPE
    rw_cols = 3 * RW_DIM + RW_LORA
    w_mla = w[..., :mla_cols]
    w_rw = w[..., mla_cols:mla_cols + rw_cols]
    w_ca = w[..., mla_cols + rw_cols:mla_cols + rw_cols + 3 * CA_DIM]
    w_gate = w[..., mla_cols + rw_cols + 3 * CA_DIM:]
    w_kr = w_mla[..., MLA_Q_RANK + MLA_KV_RANK:]
    z64 = jnp.zeros(w.shape[:-1] + (64,), w.dtype)
    packed = jnp.concatenate(
        [w_gate, w_ca[..., CA_DIM:2 * CA_DIM], w_rw, w_mla[..., :MLA_Q_RANK + MLA_KV_RANK], w_kr, z64,
         _rot_half_cols(w_kr), z64], axis=-1)
    wt = jnp.concatenate([w_ca[..., :CA_DIM] * (CA_HEAD ** -0.5 * LOG2E), w_ca[..., 2 * CA_DIM:]], axis=-1)
    return packed.astype(BF16), jnp.swapaxes(wt, 1, 2).astype(BF16)


def _pack_w_uq(w):
    nl, r, _ = w.shape
    w = w.reshape(nl, r, MLA_HEADS, MLA_QK)
    z64 = jnp.zeros((nl, r, MLA_HEADS, 64), w.dtype)
    rot = _rot_half_cols(w[..., MLA_NOPE:])
    wq = jnp.concatenate([w, z64], axis=-1).reshape(nl, r, MLA_HEADS * MLA_HEAD_PAD)
    wqr = jnp.concatenate([rot, z64], axis=-1).reshape(nl, r, MLA_HEADS * 128)
    return jnp.swapaxes(wq, 1, 2).astype(BF16), jnp.swapaxes(wqr, 1, 2).astype(BF16)


def _pack_w_ukv(w):
    nl, r, _ = w.shape
    w = w.reshape(nl, r, MLA_HEADS, MLA_NOPE + MLA_V)
    wk = w[..., :MLA_NOPE].reshape(nl, r, MLA_HEADS * MLA_NOPE)
    wv = w[..., MLA_NOPE:].reshape(nl, r, MLA_HEADS * MLA_V)
    return wk.astype(BF16), jnp.swapaxes(wv, 1, 2).astype(BF16)


def kernel(x, p, positions, pre_mix_g, w_in, mla_q_norm_g, mla_kv_norm_g, mla_w_uq, mla_w_ukv, rw_mu, rw_w0, rw_w_up, rw_a0, rw_a_up, rw_g_up, rw_k_k, rw_k_a, rw_r_k, rw_ln_w, rw_ln_b, ca_rel_bias, w_branch, w_out, post_mix_g, pre_ff_g, w_ff1, w_ff2, post_ff_g, w_ple_gate, w_ple_proj):
    b, s, d = x.shape
    t = b * s
    depth = w_in.shape[0]
    row = lambda a: a.reshape(depth, 1, -1).astype(F32)
    bf = lambda a: a.astype(BF16)
    w_in_p, w_in_t = _pack_w_in(w_in)
    wqt, wqrt = _pack_w_uq(mla_w_uq)
    wk, wvt = _pack_w_ukv(mla_w_ukv)
    mu = row(rw_mu)
    stacks = dict(
        in_proj=[row(pre_mix_g), w_in_p, w_in_t],
        mla_proj=[row(mla_q_norm_g), row(mla_kv_norm_g), wqt, wqrt, wk, wvt],
        rw_prep=[mu[..., :RW_DIM], mu[..., RW_DIM:2 * RW_DIM], mu[..., 2 * RW_DIM:3 * RW_DIM], mu[..., 3 * RW_DIM:],
                 row(rw_w0), bf(rw_w_up), row(rw_a0), bf(rw_a_up), bf(rw_g_up), row(rw_k_k), row(rw_k_a)],
        rw_chunk=[row(rw_r_k), row(rw_ln_w), row(rw_ln_b)],
        post=[bf(w_branch), bf(w_out), row(post_mix_g),
              row(pre_ff_g), bf(w_ff1), bf(w_ff2), row(post_ff_g), bf(w_ple_gate), bf(w_ple_proj)],
    )
    p3 = p.reshape(depth, t, -1)
    head_id = jnp.arange(RW_DIM) // RW_HEAD
    ones_bd = (head_id[:, None] == head_id[None, :]).astype(BF16)
    cs, sn, cst, snt = _rope_table(positions)
    h = x.reshape(t, d)
    for i in range(depth):
        prm = {name: [(a, i) for a in arrs] for name, arrs in stacks.items()}
        z, zt = _in_proj(h, *prm["in_proj"])
        qt, kn, kr, vt = _mla_proj(z, cs, sn, cst, snt, *prm["mla_proj"])
        o_mla = _mla_attn(qt, kn, kr, vt, b).reshape(t, -1)
        rr, rk, rv, al, be, lw, rg = _rw_prep(z, s, prm["rw_prep"], ones_bd)
        o_rw = _rw_chunk(rr, rk, rv, al, be, lw, rg, b, *prm["rw_chunk"])
        o_ca = _ca_attn(zt, z, _ca_bias_table(ca_rel_bias[i]), b).reshape(t, -1)
        h = _post(o_mla, o_rw, o_ca, z, h, (p3, i), prm["post"])
    return h.reshape(b, s, d)
```

```python
import functools
import math

import jax
import jax.numpy as jnp
from jax import lax
from jax.experimental import pallas as pl
from jax.experimental.pallas import tpu as pltpu

F32 = jnp.float32
BF16 = jnp.bfloat16

D_MODEL = 1024
D_PLE = 256
D_FF = 4 * D_MODEL
NORM_EPS = 1e-6
CHUNK = 64

MLA_HEADS = 4
MLA_NOPE = 128
MLA_ROPE = 64
MLA_V = 128
MLA_Q_RANK = 256
MLA_KV_RANK = 128
ROPE_THETA = 10000.0
MLA_QK = MLA_NOPE + MLA_ROPE
MLA_HEAD_PAD = 256

RW_HEADS = 8
RW_HEAD = 64
RW_DIM = RW_HEADS * RW_HEAD
RW_DECAY_LORA = 64
RW_AAA_LORA = 64
RW_GATE_LORA = 128
RW_LORA = RW_DECAY_LORA + RW_AAA_LORA + RW_GATE_LORA
RW_LN_EPS = 64e-5

CA_HEADS = 8
CA_HEAD = 64
CA_DIM = CA_HEADS * CA_HEAD
CA_LEFT_CHUNKS = 8
CA_PAD = CA_LEFT_CHUNKS * CHUNK
REL_MIN = -(CHUNK - 1)
REL_MAX = 256

N_BRANCH = 3
BRANCH_DIM = 512
GATE_COLS = N_BRANCH * D_MODEL

OFF_GATE = 0
OFF_CAK = GATE_COLS
OFF_RW = OFF_CAK + CA_DIM
OFF_LORA = OFF_RW + 3 * RW_DIM
OFF_ZQ = OFF_LORA + RW_LORA
OFF_ZKV = OFF_ZQ + MLA_Q_RANK
OFF_KR = OFF_ZKV + MLA_KV_RANK
OFF_KRR = OFF_KR + 128
IN_PACKED = OFF_KRR + 128
LOG2E = math.log2(math.e)

VMEM_LIMIT = 56 * 1024 * 1024


def _cparams(*sem):
    return pltpu.CompilerParams(dimension_semantics=sem, vmem_limit_bytes=VMEM_LIMIT)


def _rms(x, g):
    return x * lax.rsqrt(jnp.mean(x * x, axis=-1, keepdims=True) + NORM_EPS) * g


def _dot(a, b):
    return jnp.dot(a, b, preferred_element_type=F32)


def _dot_nt(a, b):
    return lax.dot_general(a, b, (((1,), (1,)), ((), ())), preferred_element_type=F32)


def _dot_tn(a, b):
    return lax.dot_general(a, b, (((0,), (0,)), ((), ())), preferred_element_type=F32)


def _const_spec(shape):
    nd = len(shape)
    return pl.BlockSpec(shape, lambda *_: (0,) * nd, pipeline_mode=pl.Buffered(1))


def _pshape(param):
    return param[0].shape[1:]


def _pspec(param):
    stack, layer = param
    nd = stack.ndim - 1
    return pl.BlockSpec((None,) + stack.shape[1:], lambda *_: (layer,) + (0,) * nd, pipeline_mode=pl.Buffered(1))


def _pargs(params):
    return [stack for stack, _ in params]


PIPE_DEPTH = 2


def _pipelined(items, produce, consume):
    pending, outs = [], []
    for it in items:
        pending.append((it, produce(it)))
        if len(pending) > PIPE_DEPTH:
            outs.append(consume(*pending.pop(0)))
    outs.extend(consume(*pc) for pc in pending)
    return outs


def _rope_table_kernel(pos_c_ref, pos_r_ref, freq_r_ref, freq_c_ref, cs_ref, sn_ref, cst_ref, snt_ref):
    ang = pos_c_ref[...] * freq_r_ref[...]
    live = lax.broadcasted_iota(jnp.int32, ang.shape, 1) < MLA_ROPE
    cs_ref[...] = jnp.where(live, jnp.cos(ang), 0.0)
    sn_ref[...] = jnp.where(live, jnp.sin(ang), 0.0)
    ang_t = freq_c_ref[...] * pos_r_ref[...]
    live_t = lax.broadcasted_iota(jnp.int32, ang_t.shape, 0) < MLA_ROPE
    cst_ref[...] = jnp.where(live_t, jnp.cos(ang_t), 0.0)
    snt_ref[...] = jnp.where(live_t, jnp.sin(ang_t), 0.0)


def _rope_table(positions):
    t = positions.size
    tm = min(t, 2048)
    half = MLA_ROPE // 2
    inv_freq = 1.0 / (ROPE_THETA ** (jnp.arange(half, dtype=F32) / half))
    freq = jnp.concatenate([inv_freq, inv_freq, jnp.zeros((128 - MLA_ROPE,), F32)])
    pos = positions.astype(F32)
    return pl.pallas_call(
        _rope_table_kernel,
        grid=(t // tm,),
        in_specs=[pl.BlockSpec((tm, 1), lambda i: (i, 0)), pl.BlockSpec((1, tm), lambda i: (0, i)),
                  _const_spec((1, 128)), _const_spec((128, 1))],
        out_specs=[pl.BlockSpec((tm, 128), lambda i: (i, 0))] * 2 + [pl.BlockSpec((128, tm), lambda i: (0, i))] * 2,
        out_shape=[jax.ShapeDtypeStruct((t, 128), F32)] * 2 + [jax.ShapeDtypeStruct((128, t), F32)] * 2,
        compiler_params=_cparams("parallel"),
        name="rope_table",
    )(pos.reshape(t, 1), pos.reshape(1, t), freq[None, :], freq[:, None])


IN_CHUNK = 1536


def _in_proj_kernel(x_ref, g_ref, w_ref, wt_ref, o_ref, ot_ref):
    xn = _rms(x_ref[...], g_ref[...]).astype(BF16)
    n = w_ref.shape[1]
    for c in range(0, n, IN_CHUNK):
        e = min(c + IN_CHUNK, n)
        o_ref[:, c:e] = _dot(xn, w_ref[:, c:e]).astype(o_ref.dtype)
    ot_ref[...] = _dot_nt(wt_ref[...], xn).astype(ot_ref.dtype)


def _in_proj(h, g, w, wt):
    t, d = h.shape
    n = _pshape(w)[1]
    nt = _pshape(wt)[0]
    tm = min(t, 512)
    return pl.pallas_call(
        _in_proj_kernel,
        grid=(t // tm,),
        in_specs=[pl.BlockSpec((tm, d), lambda i: (i, 0)), _pspec(g), _pspec(w), _pspec(wt)],
        out_specs=[pl.BlockSpec((tm, n), lambda i: (i, 0)), pl.BlockSpec((nt, tm), lambda i: (0, i))],
        out_shape=[jax.ShapeDtypeStruct((t, n), BF16), jax.ShapeDtypeStruct((nt, t), BF16)],
        compiler_params=_cparams("parallel"),
        name="in_proj",
    )(h, *_pargs([g, w, wt]))


def _mla_proj_kernel(zq_ref, zkv_ref, zkr_ref, zkrr_ref, cs_ref, sn_ref, cst_ref, snt_ref, gq_ref, gkv_ref,
                     wqt_ref, wqrt_ref, wk_ref, wvt_ref, qt_ref, k_ref, kr_ref, vt_ref):
    cst = cst_ref[...]
    snt = snt_ref[...]
    qn = _rms(zq_ref[...].astype(F32), gq_ref[...]).astype(BF16)
    qt = _dot_nt(wqt_ref[...], qn)
    qrt = _dot_nt(wqrt_ref[...], qn)
    scale = MLA_QK ** -0.5 * LOG2E
    for h in range(MLA_HEADS):
        o = h * MLA_HEAD_PAD
        qt_ref[o:o + 128, :] = (qt[o:o + 128] * scale).astype(BF16)
        rope = qt[o + 128:o + 256] * cst + qrt[h * 128:(h + 1) * 128] * snt
        qt_ref[o + 128:o + 256, :] = (rope * scale).astype(BF16)
    kvn = _rms(zkv_ref[...].astype(F32), gkv_ref[...]).astype(BF16)
    k_ref[...] = _dot(kvn, wk_ref[...]).astype(BF16)
    vt_ref[...] = _dot_nt(wvt_ref[...], kvn).astype(BF16)
    kr_ref[...] = (zkr_ref[...].astype(F32) * cs_ref[...] + zkrr_ref[...].astype(F32) * sn_ref[...]).astype(BF16)


def _mla_proj(z, cs, sn, cst, snt, gq, gkv, wqt, wqrt, wk, wvt):
    t = z.shape[0]
    tm = min(t, 1024)
    hv = MLA_HEADS * MLA_V
    hq = MLA_HEADS * MLA_HEAD_PAD
    row = lambda w: pl.BlockSpec((tm, w), lambda i: (i, 0))
    col = lambda w: pl.BlockSpec((w, tm), lambda i: (0, i))
    consts = [gq, gkv, wqt, wqrt, wk, wvt]
    return pl.pallas_call(
        _mla_proj_kernel,
        grid=(t // tm,),
        in_specs=[pl.BlockSpec((tm, MLA_Q_RANK), lambda i: (i, OFF_ZQ // MLA_Q_RANK)),
                  pl.BlockSpec((tm, MLA_KV_RANK), lambda i: (i, OFF_ZKV // MLA_KV_RANK)),
                  pl.BlockSpec((tm, 128), lambda i: (i, OFF_KR // 128)),
                  pl.BlockSpec((tm, 128), lambda i: (i, OFF_KRR // 128)),
                  row(128), row(128), col(128), col(128)] + [_pspec(c) for c in consts],
        out_specs=[col(hq), row(hv), row(128), col(hv)],
        out_shape=[
            jax.ShapeDtypeStruct((hq, t), BF16),
            jax.ShapeDtypeStruct((t, hv), BF16),
            jax.ShapeDtypeStruct((t, 128), BF16),
            jax.ShapeDtypeStruct((hv, t), BF16),
        ],
        compiler_params=_cparams("parallel"),
        name="mla_proj",
    )(z, z, z, z, cs, sn, cst, snt, *_pargs(consts))


MLA_TQ = 512


def _mla_attn_kernel(qt_ref, k_ref, kr_ref, vt_ref, o_ref, m_ref, l_ref, acc_ref):
    i = pl.program_id(1)
    tq = MLA_TQ
    m_ref[...] = jnp.full(m_ref.shape, -1e30, F32)
    l_ref[...] = jnp.zeros(l_ref.shape, F32)
    acc_ref[...] = jnp.zeros(acc_ref.shape, F32)

    def run(blocks):
        def scores(j, masked, h):
            start = pl.multiple_of(j * tq, tq)
            kh = jnp.concatenate([k_ref[0, pl.ds(start, tq), h * 128:(h + 1) * 128],
                                  kr_ref[0, pl.ds(start, tq), :]], axis=1)
            s = _dot(kh, qt_ref[h * MLA_HEAD_PAD:(h + 1) * MLA_HEAD_PAD, :])
            if masked:
                kc = lax.broadcasted_iota(jnp.int32, s.shape, 0) // CHUNK
                qc = lax.broadcasted_iota(jnp.int32, s.shape, 1) // CHUNK
                s = jnp.where(kc <= qc, s, -1e30)
            return s

        def update(j, h, s):
            start = pl.multiple_of(j * tq, tq)
            m_prev = m_ref[h]
            m_new = jnp.maximum(m_prev, jnp.max(s, axis=0, keepdims=True))
            a = jnp.exp2(m_prev - m_new)
            p = jnp.exp2(s - m_new)
            l_ref[h] = a * l_ref[h] + jnp.sum(p, axis=0, keepdims=True)
            vh = vt_ref[h * MLA_V:(h + 1) * MLA_V, pl.ds(start, tq)]
            acc_ref[h] = a * acc_ref[h] + _dot(vh, p.astype(BF16))
            m_ref[h] = m_new

        items = [(j, masked, h) for j, masked in blocks for h in range(MLA_HEADS)]
        _pipelined(items, lambda it: scores(*it), lambda it, s: update(it[0], it[2], s))

    def body(jj, c):
        run([(2 * jj, False), (2 * jj + 1, False)])
        return c

    lax.fori_loop(0, i // 2, body, 0)
    pl.when(i % 2 == 1)(lambda: run([(i - 1, False), (i, True)]))
    pl.when(i % 2 == 0)(lambda: run([(i, True)]))
    for h in range(MLA_HEADS):
        o_ref[0, :, h * MLA_V:(h + 1) * MLA_V] = (acc_ref[h] / l_ref[h]).T.astype(o_ref.dtype)


def _mla_attn(qt, k, kr, vt, b):
    hq, t = qt.shape
    s = t // b
    tq = MLA_TQ
    nq = s // tq
    hv = MLA_HEADS * MLA_V
    full = lambda w: pl.BlockSpec((1, s, w), lambda bi, i: (bi, 0, 0))
    return pl.pallas_call(
        _mla_attn_kernel,
        grid=(b, nq),
        in_specs=[
            pl.BlockSpec((hq, tq), lambda bi, i: (0, bi * nq + i)),
            full(hv), full(128),
            pl.BlockSpec((hv, s), lambda bi, i: (0, bi)),
        ],
        out_specs=pl.BlockSpec((1, tq, hv), lambda bi, i: (bi, i, 0)),
        out_shape=jax.ShapeDtypeStruct((b, s, hv), BF16),
        scratch_shapes=[pltpu.VMEM((MLA_HEADS, 1, tq), F32), pltpu.VMEM((MLA_HEADS, 1, tq), F32),
                        pltpu.VMEM((MLA_HEADS, MLA_V, tq), F32)],
        compiler_params=_cparams("parallel", "arbitrary"),
        name="mla_attn",
    )(qt, k.reshape(b, s, hv), kr.reshape(b, s, 128), vt)


CA_TQ = 256
CA_BAND_BLK = CA_TQ + CA_PAD


def _ca_attn_kernel(qt_ref, k_ref, vt_ref, bias_ref, o_ref):
    i = pl.program_id(1)
    lo = lax.broadcasted_iota(jnp.int32, (128, CA_TQ), 0) < CA_HEAD
    zero = jnp.zeros((), BF16)

    def block(k_start, n_keys, bias_off):
        def scores(h):
            sl = slice((h // 2) * 128, (h // 2 + 1) * 128)
            k2 = k_ref[0, pl.ds(k_start, n_keys), sl]
            qh = jnp.where(lo if h % 2 == 0 else ~lo, qt_ref[sl, :], zero)
            return _dot(k2, qh) + bias_ref[h, bias_off:bias_off + n_keys, :]

        def attend(h, s):
            p = jnp.exp2(s - jnp.max(s, axis=0, keepdims=True))
            l = jnp.sum(p, axis=0, keepdims=True)
            vh = vt_ref[h * CA_HEAD:(h + 1) * CA_HEAD, pl.ds(k_start, n_keys)]
            return _dot(vh, p.astype(BF16)) / l

        outs = _pipelined(list(range(CA_HEADS)), scores, attend)
        for pair in range(CA_HEADS // 2):
            both = jnp.concatenate(outs[2 * pair:2 * pair + 2], axis=0)
            o_ref[0, :, pair * 128:(pair + 1) * 128] = both.T.astype(o_ref.dtype)

    lead = CA_PAD // CA_TQ
    for j in range(lead):
        pl.when(i == j)(functools.partial(block, 0, (j + 1) * CA_TQ, CA_PAD - j * CA_TQ))
    pl.when(i >= lead)(lambda: block(pl.multiple_of(i * CA_TQ - CA_PAD, CA_TQ), CA_BAND_BLK, 0))


def _ca_attn(zt, z, bias, b):
    t = zt.shape[1]
    s = t // b
    nq = s // CA_TQ
    return pl.pallas_call(
        _ca_attn_kernel,
        grid=(b, nq),
        in_specs=[
            pl.BlockSpec((CA_DIM, CA_TQ), lambda bi, i: (0, bi * nq + i)),
            pl.BlockSpec((1, s, CA_DIM), lambda bi, i: (bi, 0, OFF_CAK // CA_DIM)),
            pl.BlockSpec((CA_DIM, s), lambda bi, i: (1, bi)),
            _const_spec(bias.shape),
        ],
        out_specs=pl.BlockSpec((1, CA_TQ, CA_DIM), lambda bi, i: (bi, i, 0)),
        out_shape=jax.ShapeDtypeStruct((b, s, CA_DIM), BF16),
        compiler_params=_cparams("parallel", "arbitrary"),
        name="ca_attn",
    )(zt, z.reshape(b, s, -1), zt, bias)


def _ca_bias_table(rel_bias):
    width = CA_TQ + CA_BAND_BLK
    m = jnp.arange(width)
    delta = jnp.where(m < CA_BAND_BLK, m, m - width)
    idx = jnp.clip(CA_PAD - delta, REL_MIN, REL_MAX) - REL_MIN
    t1 = rel_bias.astype(F32)[idx].T * LOG2E
    flat = jnp.tile(t1, (1, CA_TQ))[:, :CA_TQ * (width - 1)]
    bias = flat.reshape(CA_HEADS, CA_TQ, width - 1)[:, :, :CA_BAND_BLK]
    rc = jnp.arange(CA_TQ)[:, None] // CHUNK
    cc = jnp.arange(CA_BAND_BLK)[None, :] // CHUNK
    ok = (cc >= rc) & (cc <= rc + CA_LEFT_CHUNKS)
    return jnp.where(ok[None], bias, -1e30).transpose(0, 2, 1)


def _split_bf16(x):
    hi = x.astype(BF16)
    lo = (x - hi.astype(F32)).astype(BF16)
    return hi, lo


RW_GROUP = 256


def _rw_kernel(rows, zr_ref, zk_ref, zv_ref, zl_ref, mur_ref, muk_ref, muv_ref, mul_ref, w0_ref, wup_ref,
               a0_ref, aup_ref, gup_ref, kk_ref, ka_ref, rk_ref, lnw_ref, lnb_ref, bd_ref,
               o_ref, s_ref, cr_ref, ck_ref, cv_ref, cl_ref):
    L = CHUNK
    W = RW_GROUP
    reps = W // RW_HEAD

    @pl.when(pl.program_id(1) == 0)
    def _():
        for ref in (s_ref, cr_ref, ck_ref, cv_ref, cl_ref):
            ref[...] = jnp.zeros(ref.shape, F32)

    bdm = bd_ref[...]

    def bd(x):
        return jnp.concatenate([x.astype(BF16)] * reps, axis=0) * bdm

    def head_sum(x):
        hi, lo = _split_bf16(x)
        both = _dot(jnp.concatenate([hi, lo], axis=0), bdm)
        return both[:x.shape[0]] + both[x.shape[0]:]

    row0 = lax.broadcasted_iota(jnp.int32, (L, 1), 0) == 0
    row = lax.broadcasted_iota(jnp.int32, (L, W), 0)
    sub = lax.broadcasted_iota(jnp.int32, (L, W), 1) % RW_HEAD
    strict = sub < row
    incl = sub <= row
    eye = (sub == row).astype(F32)
    tri = (lax.broadcasted_iota(jnp.int32, (L, L), 1)
           <= lax.broadcasted_iota(jnp.int32, (L, L), 0)).astype(BF16)
    inv = 1.0 / RW_HEAD

    def shift(z, carry_ref, idx, mu):
        prev = jnp.where(row0, carry_ref[idx], pltpu.roll(z, 1, axis=0))
        carry_ref[idx] = z[L - 1:L, :]
        return z + (prev - z) * mu

    def lora_inputs(b):
        xl = shift(zl_ref[b].astype(F32), cl_ref, (b,), mul_ref[...])
        return (jnp.tanh(xl[:, :RW_DECAY_LORA]).astype(BF16),
                xl[:, RW_DECAY_LORA:RW_DECAY_LORA + RW_AAA_LORA].astype(BF16),
                jax.nn.sigmoid(xl[:, RW_DECAY_LORA + RW_AAA_LORA:]).astype(BF16))

    def instance(b, gi, xw, xa, xg):
        sl = slice(gi * W, (gi + 1) * W)
        r = shift(zr_ref[b, :, sl].astype(F32), cr_ref, (b, slice(None), sl), mur_ref[:, sl])
        k = shift(zk_ref[b, :, sl].astype(F32), ck_ref, (b, slice(None), sl), muk_ref[:, sl])
        v = shift(zv_ref[b, :, sl].astype(F32), cv_ref, (b, slice(None), sl), muv_ref[:, sl])
        lw = -math.exp(-0.5) * jax.nn.sigmoid(w0_ref[:, sl] + _dot(xw, wup_ref[:, sl]))
        a = jax.nn.sigmoid(a0_ref[:, sl] + _dot(xa, aup_ref[:, sl]))
        g = _dot(xg, gup_ref[:, sl])
        kk = k * kk_ref[:, sl]
        ss = head_sum(kk * kk)
        yield
        kk = kk * lax.rsqrt(jnp.maximum(ss, 1e-24))
        k = k * (1.0 + (a - 1.0) * ka_ref[:, sl])
        be = kk * a
        lw_hi, lw_lo = _split_bf16(lw)
        cum = _dot(tri, lw_hi) + _dot(tri, lw_lo)
        yield
        cum_l = cum[L - 1:L, :]
        e_neg = jnp.exp(-cum)
        e_tail = jnp.exp(cum_l - cum)
        ar = jnp.concatenate([-kk * jnp.exp(cum - lw), r * jnp.exp(cum)], axis=0).astype(BF16)
        s0 = s_ref[b, gi]
        ar_s = _dot_nt(ar, s0.astype(BF16))
        a_b = _dot_nt(ar, bd(be * e_neg))
        a_k = _dot_nt(ar, bd(k * e_neg))
        yield
        n = jnp.where(strict, a_b[:L], 0.0)
        a_ak = jnp.where(strict, a_k[:L], 0.0)
        a_rb = jnp.where(incl, a_b[L:], 0.0)
        a_rk = jnp.where(incl, a_k[L:], 0.0)
        p = eye + n
        nk = _dot(n.astype(BF16), bd(n))
        bd_v = bd(v)
        x0 = _dot(a_ak.astype(BF16), bd_v)
        yield
        steps = int(math.log2(L)) - 1
        for it in range(steps):
            m = bd(nk)
            if it + 1 < steps:
                res = _dot(jnp.concatenate([p, nk], axis=0).astype(BF16), m)
                p = p + res[:L]
                nk = res[L:]
            else:
                p = p + _dot(p.astype(BF16), m)
            yield
        u = _dot(p.astype(BF16), bd(ar_s[:L] + x0))
        yield
        y = ar_s[L:] + _dot(jnp.concatenate([a_rb, a_rk], axis=1).astype(BF16),
                            jnp.concatenate([bd(u), bd_v], axis=0))
        uv = jnp.concatenate([u, v], axis=0).astype(BF16)
        bk = jnp.concatenate([be * e_tail, k * e_tail], axis=0).astype(BF16)
        s_ref[b, gi] = s0 * jnp.exp(cum_l) + _dot_tn(uv, bk) * bdm.astype(F32)
        yield
        sums = head_sum(jnp.concatenate([y, r * k * rk_ref[:, sl]], axis=0))
        yield
        d = y - sums[:L] * inv
        var = head_sum(d * d) * inv
        yield
        yn = d * lax.rsqrt(var + RW_LN_EPS) * lnw_ref[:, sl] + lnb_ref[:, sl]
        o_ref[b, :, sl] = ((yn + sums[L:] * v) * g).astype(o_ref.dtype)

    live = []
    for b in range(rows):
        lora = lora_inputs(b)
        live += [instance(b, gi, *lora) for gi in range(RW_DIM // W)]
    while live:
        live = [g for g in live if next(g, True) is None]


def _rw_mixer(z, b, consts):
    t = z.shape[0]
    s = t // b
    rows = math.gcd(b, 8)
    gid = jnp.arange(RW_GROUP) // RW_HEAD
    ones_bd = (gid[:, None] == gid[None, :]).astype(BF16)
    sec = lambda w, j: pl.BlockSpec((rows, CHUNK, w), lambda bb, c: (bb, c, j))
    z3 = z.reshape(b, s, -1)
    return pl.pallas_call(
        functools.partial(_rw_kernel, rows),
        grid=(b // rows, s // CHUNK),
        in_specs=[sec(RW_DIM, OFF_RW // RW_DIM), sec(RW_DIM, OFF_RW // RW_DIM + 1),
                  sec(RW_DIM, OFF_RW // RW_DIM + 2), sec(RW_LORA, OFF_LORA // RW_LORA)]
        + [_pspec(c) for c in consts] + [_const_spec(ones_bd.shape)],
        out_specs=sec(RW_DIM, 0),
        out_shape=jax.ShapeDtypeStruct((b, s, RW_DIM), BF16),
        scratch_shapes=[pltpu.VMEM((rows, RW_DIM // RW_GROUP, RW_GROUP, RW_GROUP), F32)]
        + [pltpu.VMEM((rows, 1, RW_DIM), F32)] * 3 + [pltpu.VMEM((rows, 1, RW_LORA), F32)],
        compiler_params=_cparams("parallel", "arbitrary"),
        name="rw_mixer",
    )(z3, z3, z3, z3, *_pargs(consts), ones_bd).reshape(t, RW_DIM)


FF_SPLIT = 4


def _post_kernel(om_ref, or_ref, oc_ref, g0_ref, g1_ref, g2_ref, h_ref, p_ref, wb_ref, wo_ref, gm_ref,
                 g1n_ref, w1_ref, w2_ref, g2n_ref, wg_ref, wp_ref, o_ref):
    merged = None
    for n, (b_ref, zg_ref) in enumerate(((om_ref, g0_ref), (or_ref, g1_ref), (oc_ref, g2_ref))):
        y = _dot(b_ref[...], wb_ref[n * BRANCH_DIM:(n + 1) * BRANCH_DIM, :])
        gate = jax.nn.sigmoid(zg_ref[...].astype(F32))
        merged = gate * y if merged is None else merged + gate * y
    h = h_ref[...] + _rms(_dot(merged.astype(BF16), wo_ref[...]), gm_ref[...])
    f = _rms(h, g1n_ref[...]).astype(BF16)
    cw = D_FF // FF_SPLIT
    acc = None
    for c in range(FF_SPLIT):
        a = jnp.maximum(_dot(f, w1_ref[:, c * cw:(c + 1) * cw]), 0.0)
        part = _dot((a * a).astype(BF16), w2_ref[c * cw:(c + 1) * cw, :])
        acc = part if acc is None else acc + part
    h = h + _rms(acc, g2n_ref[...])
    gate = jax.nn.sigmoid(_dot(h.astype(BF16), wg_ref[...]))
    o_ref[...] = h + gate * _dot(p_ref[...].astype(BF16), wp_ref[...])


def _post(o_mla, o_rw, o_ca, z, h, p, consts):
    t = h.shape[0]
    tm = min(t, 512)
    row = lambda w: pl.BlockSpec((tm, w), lambda i: (i, 0))
    gate = lambda n: pl.BlockSpec((tm, D_MODEL), lambda i: (i, OFF_GATE // D_MODEL + n))
    p_stack, layer = p
    return pl.pallas_call(
        _post_kernel,
        grid=(t // tm,),
        in_specs=[row(BRANCH_DIM), row(BRANCH_DIM), row(BRANCH_DIM), gate(0), gate(1), gate(2), row(D_MODEL),
                  pl.BlockSpec((None, tm, D_PLE), lambda i: (layer, i, 0))] + [_pspec(c) for c in consts],
        out_specs=row(D_MODEL),
        out_shape=jax.ShapeDtypeStruct((t, D_MODEL), F32),
        compiler_params=_cparams("parallel"),
        name="post",
    )(o_mla, o_rw, o_ca, z, z, z, h, p_stack, *_pargs(consts))


def _rot_half_cols(w):
    half = w.shape[-1] // 2
    return jnp.concatenate([-w[..., half:], w[..., :half]], axis=-1)


def _pack_w_in(w):
    mla_cols = MLA_Q_RANK + MLA_KV_RANK + MLA_ROPE
    rw_cols = 3 * RW_DIM + RW_LORA
    w_mla = w[..., :mla_cols]
    w_rw = w[..., mla_cols:mla_cols + rw_cols]
    w_ca = w[..., mla_cols + rw_cols:mla_cols + rw_cols + 3 * CA_DIM]
    w_gate = w[..., mla_cols + rw_cols + 3 * CA_DIM:]
    w_kr = w_mla[..., MLA_Q_RANK + MLA_KV_RANK:]
    z64 = jnp.zeros(w.shape[:-1] + (64,), w.dtype)
    packed = jnp.concatenate(
        [w_gate, w_ca[..., CA_DIM:2 * CA_DIM], w_rw, w_mla[..., :MLA_Q_RANK + MLA_KV_RANK], w_kr, z64,
         _rot_half_cols(w_kr), z64], axis=-1)
    wt = jnp.concatenate([w_ca[..., :CA_DIM] * (CA_HEAD ** -0.5 * LOG2E), w_ca[..., 2 * CA_DIM:]], axis=-1)
    return packed.astype(BF16), jnp.swapaxes(wt, 1, 2).astype(BF16)


def _pack_w_uq(w):
    nl, r, _ = w.shape
    w = w.reshape(nl, r, MLA_HEADS, MLA_QK)
    z64 = jnp.zeros((nl, r, MLA_HEADS, 64), w.dtype)
    rot = _rot_half_cols(w[..., MLA_NOPE:])
    wq = jnp.concatenate([w, z64], axis=-1).reshape(nl, r, MLA_HEADS * MLA_HEAD_PAD)
    wqr = jnp.concatenate([rot, z64], axis=-1).reshape(nl, r, MLA_HEADS * 128)
    return jnp.swapaxes(wq, 1, 2).astype(BF16), jnp.swapaxes(wqr, 1, 2).astype(BF16)


def _pack_w_ukv(w):
    nl, r, _ = w.shape
    w = w.reshape(nl, r, MLA_HEADS, MLA_NOPE + MLA_V)
    wk = w[..., :MLA_NOPE].reshape(nl, r, MLA_HEADS * MLA_NOPE)
    wv = w[..., MLA_NOPE:].reshape(nl, r, MLA_HEADS * MLA_V)
    return wk.astype(BF16), jnp.swapaxes(wv, 1, 2).astype(BF16)


def kernel(x, p, positions, pre_mix_g, w_in, mla_q_norm_g, mla_kv_norm_g, mla_w_uq, mla_w_ukv, rw_mu, rw_w0, rw_w_up, rw_a0, rw_a_up, rw_g_up, rw_k_k, rw_k_a, rw_r_k, rw_ln_w, rw_ln_b, ca_rel_bias, w_branch, w_out, post_mix_g, pre_ff_g, w_ff1, w_ff2, post_ff_g, w_ple_gate, w_ple_proj):
    b, s, d = x.shape
    t = b * s
    depth = w_in.shape[0]
    row = lambda a: a.reshape(depth, 1, -1).astype(F32)
    bf = lambda a: a.astype(BF16)
    w_in_p, w_in_t = _pack_w_in(w_in)
    wqt, wqrt = _pack_w_uq(mla_w_uq)
    wk, wvt = _pack_w_ukv(mla_w_ukv)
    mu = row(rw_mu)
    stacks = dict(
        in_proj=[row(pre_mix_g), w_in_p, w_in_t],
        mla_proj=[row(mla_q_norm_g), row(mla_kv_norm_g), wqt, wqrt, wk, wvt],
        rw=[mu[..., :RW_DIM], mu[..., RW_DIM:2 * RW_DIM], mu[..., 2 * RW_DIM:3 * RW_DIM], mu[..., 3 * RW_DIM:],
            row(rw_w0), bf(rw_w_up), row(rw_a0), bf(rw_a_up), bf(rw_g_up), row(rw_k_k), row(rw_k_a),
            row(rw_r_k), row(rw_ln_w), row(rw_ln_b)],
        post=[bf(w_branch), bf(w_out), row(post_mix_g),
              row(pre_ff_g), bf(w_ff1), bf(w_ff2), row(post_ff_g), bf(w_ple_gate), bf(w_ple_proj)],
    )
    p3 = p.reshape(depth, t, -1)
    cs, sn, cst, snt = _rope_table(positions)
    h = x.reshape(t, d)
    for i in range(depth):
        prm = {name: [(a, i) for a in arrs] for name, arrs in stacks.items()}
        z, zt = _in_proj(h, *prm["in_proj"])
        qt, kn, kr, vt = _mla_proj(z, cs, sn, cst, snt, *prm["mla_proj"])
        o_mla = _mla_attn(qt, kn, kr, vt, b).reshape(t, -1)
        o_rw = _rw_mixer(z, b, prm["rw"])
        o_ca = _ca_attn(zt, z, _ca_bias_table(ca_rel_bias[i]), b).reshape(t, -1)
        h = _post(o_mla, o_rw, o_ca, z, h, (p3, i), prm["post"])
    return h.reshape(b, s, d)
```

```python
import functools
import math

import jax
import jax.numpy as jnp
from jax import lax
from jax.experimental import pallas as pl
from jax.experimental.pallas import tpu as pltpu

F32 = jnp.float32
BF16 = jnp.bfloat16

D_MODEL = 1024
D_PLE = 256
D_FF = 4 * D_MODEL
NORM_EPS = 1e-6
CHUNK = 64

MLA_HEADS = 4
MLA_NOPE = 128
MLA_ROPE = 64
MLA_V = 128
MLA_Q_RANK = 256
MLA_KV_RANK = 128
ROPE_THETA = 10000.0
MLA_QK = MLA_NOPE + MLA_ROPE
MLA_HEAD_PAD = 256

RW_HEADS = 8
RW_HEAD = 64
RW_DIM = RW_HEADS * RW_HEAD
RW_DECAY_LORA = 64
RW_AAA_LORA = 64
RW_GATE_LORA = 128
RW_LORA = RW_DECAY_LORA + RW_AAA_LORA + RW_GATE_LORA
RW_LN_EPS = 64e-5

CA_HEADS = 8
CA_HEAD = 64
CA_DIM = CA_HEADS * CA_HEAD
CA_LEFT_CHUNKS = 8
CA_PAD = CA_LEFT_CHUNKS * CHUNK
REL_MIN = -(CHUNK - 1)
REL_MAX = 256

N_BRANCH = 3
BRANCH_DIM = 512
GATE_COLS = N_BRANCH * D_MODEL

OFF_GATE = 0
OFF_CAK = GATE_COLS
OFF_RW = OFF_CAK + CA_DIM
OFF_LORA = OFF_RW + 3 * RW_DIM
OFF_ZQ = OFF_LORA + RW_LORA
OFF_ZKV = OFF_ZQ + MLA_Q_RANK
OFF_KR = OFF_ZKV + MLA_KV_RANK
OFF_KRR = OFF_KR + 128
IN_PACKED = OFF_KRR + 128
LOG2E = math.log2(math.e)

VMEM_LIMIT = 56 * 1024 * 1024


def _cparams(*sem):
    return pltpu.CompilerParams(dimension_semantics=sem, vmem_limit_bytes=VMEM_LIMIT)


def _rms(x, g):
    return x * lax.rsqrt(jnp.mean(x * x, axis=-1, keepdims=True) + NORM_EPS) * g


def _dot(a, b):
    return jnp.dot(a, b, preferred_element_type=F32)


def _dot_nt(a, b):
    return lax.dot_general(a, b, (((1,), (1,)), ((), ())), preferred_element_type=F32)


def _dot_tn(a, b):
    return lax.dot_general(a, b, (((0,), (0,)), ((), ())), preferred_element_type=F32)


def _const_spec(shape):
    nd = len(shape)
    return pl.BlockSpec(shape, lambda *_: (0,) * nd, pipeline_mode=pl.Buffered(1))


def _pshape(param):
    return param[0].shape[1:]


def _pspec(param):
    stack, layer = param
    nd = stack.ndim - 1
    return pl.BlockSpec((None,) + stack.shape[1:], lambda *_: (layer,) + (0,) * nd, pipeline_mode=pl.Buffered(1))


def _pargs(params):
    return [stack for stack, _ in params]


PIPE_DEPTH = 2
ONES_ROWS = 16


def _pipelined(items, produce, consume):
    pending, outs = [], []
    for it in items:
        pending.append((it, produce(it)))
        if len(pending) > PIPE_DEPTH:
            outs.append(consume(*pending.pop(0)))
    outs.extend(consume(*pc) for pc in pending)
    return outs


def _rope_table_kernel(pos_ref, freq_ref, cs_ref, sn_ref, cst_ref, snt_ref):
    ang = freq_ref[...] * pos_ref[...]
    dead = jnp.zeros((128 - MLA_ROPE, ang.shape[1]), F32)
    cst = jnp.concatenate([jnp.cos(ang), dead], axis=0)
    snt = jnp.concatenate([jnp.sin(ang), dead], axis=0)
    cst_ref[...] = cst
    snt_ref[...] = snt
    cs_ref[...] = cst.T
    sn_ref[...] = snt.T


def _rope_table(positions):
    t = positions.size
    tm = min(t, 2048)
    half = MLA_ROPE // 2
    inv_freq = 1.0 / (ROPE_THETA ** (jnp.arange(half, dtype=F32) / half))
    freq = jnp.concatenate([inv_freq, inv_freq])[:, None]
    return pl.pallas_call(
        _rope_table_kernel,
        grid=(t // tm,),
        in_specs=[pl.BlockSpec((1, tm), lambda i: (0, i)), _const_spec(freq.shape)],
        out_specs=[pl.BlockSpec((tm, 128), lambda i: (i, 0))] * 2 + [pl.BlockSpec((128, tm), lambda i: (0, i))] * 2,
        out_shape=[jax.ShapeDtypeStruct((t, 128), F32)] * 2 + [jax.ShapeDtypeStruct((128, t), F32)] * 2,
        compiler_params=_cparams("parallel"),
        name="rope_table",
    )(positions.astype(F32).reshape(1, t), freq)


IN_CHUNK = 1536


def _in_proj_kernel(x_ref, g_ref, w_ref, wt_ref, o_ref, ot_ref):
    xn = _rms(x_ref[...], g_ref[...]).astype(BF16)
    n = w_ref.shape[1]
    for c in range(0, n, IN_CHUNK):
        e = min(c + IN_CHUNK, n)
        o_ref[:, c:e] = _dot(xn, w_ref[:, c:e]).astype(o_ref.dtype)
    ot_ref[...] = _dot_nt(wt_ref[...], xn).astype(ot_ref.dtype)


def _in_proj(h, g, w, wt):
    t, d = h.shape
    n = _pshape(w)[1]
    nt = _pshape(wt)[0]
    tm = min(t, 512)
    return pl.pallas_call(
        _in_proj_kernel,
        grid=(t // tm,),
        in_specs=[pl.BlockSpec((tm, d), lambda i: (i, 0)), _pspec(g), _pspec(w), _pspec(wt)],
        out_specs=[pl.BlockSpec((tm, n), lambda i: (i, 0)), pl.BlockSpec((nt, tm), lambda i: (0, i))],
        out_shape=[jax.ShapeDtypeStruct((t, n), BF16), jax.ShapeDtypeStruct((nt, t), BF16)],
        compiler_params=_cparams("parallel"),
        name="in_proj",
    )(h, *_pargs([g, w, wt]))


def _mla_proj_kernel(zq_ref, zkv_ref, zkr_ref, zkrr_ref, cs_ref, sn_ref, cst_ref, snt_ref, gq_ref, gkv_ref,
                     wqt_ref, wqrt_ref, wk_ref, wvt_ref, qt_ref, k_ref, kr_ref, vt_ref):
    cst = cst_ref[...]
    snt = snt_ref[...]
    qn = _rms(zq_ref[...].astype(F32), gq_ref[...]).astype(BF16)
    qt = _dot_nt(wqt_ref[...], qn)
    qrt = _dot_nt(wqrt_ref[...], qn)
    scale = MLA_QK ** -0.5 * LOG2E
    for h in range(MLA_HEADS):
        o = h * MLA_HEAD_PAD
        qt_ref[o:o + 128, :] = (qt[o:o + 128] * scale).astype(BF16)
        rope = qt[o + 128:o + 256] * cst + qrt[h * 128:(h + 1) * 128] * snt
        qt_ref[o + 128:o + 256, :] = (rope * scale).astype(BF16)
    kvn = _rms(zkv_ref[...].astype(F32), gkv_ref[...]).astype(BF16)
    k_ref[...] = _dot(kvn, wk_ref[...]).astype(BF16)
    vt_ref[...] = _dot_nt(wvt_ref[...], kvn).astype(BF16)
    kr_ref[...] = (zkr_ref[...].astype(F32) * cs_ref[...] + zkrr_ref[...].astype(F32) * sn_ref[...]).astype(BF16)


def _mla_proj(z, cs, sn, cst, snt, gq, gkv, wqt, wqrt, wk, wvt):
    t = z.shape[0]
    tm = min(t, 1024)
    hv = MLA_HEADS * MLA_V
    hq = MLA_HEADS * MLA_HEAD_PAD
    row = lambda w: pl.BlockSpec((tm, w), lambda i: (i, 0))
    col = lambda w: pl.BlockSpec((w, tm), lambda i: (0, i))
    consts = [gq, gkv, wqt, wqrt, wk, wvt]
    return pl.pallas_call(
        _mla_proj_kernel,
        grid=(t // tm,),
        in_specs=[pl.BlockSpec((tm, MLA_Q_RANK), lambda i: (i, OFF_ZQ // MLA_Q_RANK)),
                  pl.BlockSpec((tm, MLA_KV_RANK), lambda i: (i, OFF_ZKV // MLA_KV_RANK)),
                  pl.BlockSpec((tm, 128), lambda i: (i, OFF_KR // 128)),
                  pl.BlockSpec((tm, 128), lambda i: (i, OFF_KRR // 128)),
                  row(128), row(128), col(128), col(128)] + [_pspec(c) for c in consts],
        out_specs=[col(hq), row(hv), row(128), col(hv)],
        out_shape=[
            jax.ShapeDtypeStruct((hq, t), BF16),
            jax.ShapeDtypeStruct((t, hv), BF16),
            jax.ShapeDtypeStruct((t, 128), BF16),
            jax.ShapeDtypeStruct((hv, t), BF16),
        ],
        compiler_params=_cparams("parallel"),
        name="mla_proj",
    )(z, z, z, z, cs, sn, cst, snt, *_pargs(consts))


MLA_TQ = 512


def _mla_attn_kernel(qt_ref, k_ref, kr_ref, vt_ref, o_ref, m_ref, acc_ref):
    i = pl.program_id(1)
    tq = MLA_TQ
    m_ref[...] = jnp.full(m_ref.shape, -1e30, F32)
    acc_ref[...] = jnp.zeros(acc_ref.shape, F32)
    ones = jnp.ones((ONES_ROWS, tq), BF16)

    def run(blocks):
        def scores(j, masked, h):
            start = pl.multiple_of(j * tq, tq)
            kh = jnp.concatenate([k_ref[0, pl.ds(start, tq), h * 128:(h + 1) * 128],
                                  kr_ref[0, pl.ds(start, tq), :]], axis=1)
            s = _dot(kh, qt_ref[h * MLA_HEAD_PAD:(h + 1) * MLA_HEAD_PAD, :])
            if masked:
                kc = lax.broadcasted_iota(jnp.int32, s.shape, 0) // CHUNK
                qc = lax.broadcasted_iota(jnp.int32, s.shape, 1) // CHUNK
                s = jnp.where(kc <= qc, s, -1e30)
            return s

        def update(j, h, s):
            start = pl.multiple_of(j * tq, tq)
            m_prev = m_ref[h]
            m_new = jnp.maximum(m_prev, jnp.max(s, axis=0, keepdims=True))
            a = jnp.exp2(m_prev - m_new)
            p = jnp.exp2(s - m_new).astype(BF16)
            vh = jnp.concatenate([vt_ref[h * MLA_V:(h + 1) * MLA_V, pl.ds(start, tq)], ones], axis=0)
            acc_ref[h] = a * acc_ref[h] + _dot(vh, p)
            m_ref[h] = m_new

        items = [(j, masked, h) for j, masked in blocks for h in range(MLA_HEADS)]
        _pipelined(items, lambda it: scores(*it), lambda it, s: update(it[0], it[2], s))

    def body(jj, c):
        run([(2 * jj, False), (2 * jj + 1, False)])
        return c

    lax.fori_loop(0, i // 2, body, 0)
    pl.when(i % 2 == 1)(lambda: run([(i - 1, False), (i, True)]))
    pl.when(i % 2 == 0)(lambda: run([(i, True)]))
    for h in range(MLA_HEADS):
        acc = acc_ref[h]
        o_ref[0, :, h * MLA_V:(h + 1) * MLA_V] = (acc[:MLA_V] / acc[MLA_V:MLA_V + 1]).T.astype(o_ref.dtype)


def _mla_attn(qt, k, kr, vt, b):
    hq, t = qt.shape
    s = t // b
    tq = MLA_TQ
    nq = s // tq
    hv = MLA_HEADS * MLA_V
    full = lambda w: pl.BlockSpec((1, s, w), lambda bi, i: (bi, 0, 0))
    return pl.pallas_call(
        _mla_attn_kernel,
        grid=(b, nq),
        in_specs=[
            pl.BlockSpec((hq, tq), lambda bi, i: (0, bi * nq + i)),
            full(hv), full(128),
            pl.BlockSpec((hv, s), lambda bi, i: (0, bi)),
        ],
        out_specs=pl.BlockSpec((1, tq, hv), lambda bi, i: (bi, i, 0)),
        out_shape=jax.ShapeDtypeStruct((b, s, hv), BF16),
        scratch_shapes=[pltpu.VMEM((MLA_HEADS, 1, tq), F32),
                        pltpu.VMEM((MLA_HEADS, MLA_V + ONES_ROWS, tq), F32)],
        compiler_params=_cparams("parallel", "arbitrary"),
        name="mla_attn",
    )(qt, k.reshape(b, s, hv), kr.reshape(b, s, 128), vt)


CA_TQ = 256
CA_BAND_BLK = CA_TQ + CA_PAD


def _ca_attn_kernel(qt_ref, k_ref, vt_ref, bias_ref, o_ref):
    i = pl.program_id(1)
    lo = lax.broadcasted_iota(jnp.int32, (128, CA_TQ), 0) < CA_HEAD
    zero = jnp.zeros((), BF16)

    def block(k_start, n_keys, bias_off):
        def scores(h):
            sl = slice((h // 2) * 128, (h // 2 + 1) * 128)
            k2 = k_ref[0, pl.ds(k_start, n_keys), sl]
            qh = jnp.where(lo if h % 2 == 0 else ~lo, qt_ref[sl, :], zero)
            return _dot(k2, qh) + bias_ref[h, bias_off:bias_off + n_keys, :]

        def attend(h, s):
            p = jnp.exp2(s - jnp.max(s, axis=0, keepdims=True)).astype(BF16)
            vh = jnp.concatenate([vt_ref[h * CA_HEAD:(h + 1) * CA_HEAD, pl.ds(k_start, n_keys)],
                                  jnp.ones((ONES_ROWS, n_keys), BF16)], axis=0)
            out = _dot(vh, p)
            return out[:CA_HEAD] / out[CA_HEAD:CA_HEAD + 1]

        outs = _pipelined(list(range(CA_HEADS)), scores, attend)
        for pair in range(CA_HEADS // 2):
            both = jnp.concatenate(outs[2 * pair:2 * pair + 2], axis=0)
            o_ref[0, :, pair * 128:(pair + 1) * 128] = both.T.astype(o_ref.dtype)

    lead = CA_PAD // CA_TQ
    for j in range(lead):
        pl.when(i == j)(functools.partial(block, 0, (j + 1) * CA_TQ, CA_PAD - j * CA_TQ))
    pl.when(i >= lead)(lambda: block(pl.multiple_of(i * CA_TQ - CA_PAD, CA_TQ), CA_BAND_BLK, 0))


def _ca_attn(zt, z, bias, b):
    t = zt.shape[1]
    s = t // b
    nq = s // CA_TQ
    return pl.pallas_call(
        _ca_attn_kernel,
        grid=(b, nq),
        in_specs=[
            pl.BlockSpec((CA_DIM, CA_TQ), lambda bi, i: (0, bi * nq + i)),
            pl.BlockSpec((1, s, CA_DIM), lambda bi, i: (bi, 0, OFF_CAK // CA_DIM)),
            pl.BlockSpec((CA_DIM, s), lambda bi, i: (1, bi)),
            _const_spec(bias.shape),
        ],
        out_specs=pl.BlockSpec((1, CA_TQ, CA_DIM), lambda bi, i: (bi, i, 0)),
        out_shape=jax.ShapeDtypeStruct((b, s, CA_DIM), BF16),
        compiler_params=_cparams("parallel", "arbitrary"),
        name="ca_attn",
    )(zt, z.reshape(b, s, -1), zt, bias)


def _ca_bias_table(rel_bias):
    width = CA_TQ + CA_BAND_BLK
    m = jnp.arange(width)
    delta = jnp.where(m < CA_BAND_BLK, m, m - width)
    idx = jnp.clip(CA_PAD - delta, REL_MIN, REL_MAX) - REL_MIN
    t1 = rel_bias.astype(F32)[idx].T * LOG2E
    flat = jnp.tile(t1, (1, CA_TQ))[:, :CA_TQ * (width - 1)]
    bias = flat.reshape(CA_HEADS, CA_TQ, width - 1)[:, :, :CA_BAND_BLK]
    rc = jnp.arange(CA_TQ)[:, None] // CHUNK
    cc = jnp.arange(CA_BAND_BLK)[None, :] // CHUNK
    ok = (cc >= rc) & (cc <= rc + CA_LEFT_CHUNKS)
    return jnp.where(ok[None], bias, -1e30).transpose(0, 2, 1)


def _split_bf16(x):
    hi = x.astype(BF16)
    lo = (x - hi.astype(F32)).astype(BF16)
    return hi, lo


RW_GROUP = 256


def _rw_kernel(rows, zr_ref, zk_ref, zv_ref, zl_ref, mur_ref, muk_ref, muv_ref, mul_ref, w0_ref, wup_ref,
               a0_ref, aup_ref, gup_ref, kk_ref, ka_ref, rk_ref, lnw_ref, lnb_ref, bd_ref,
               o_ref, s_ref, cr_ref, ck_ref, cv_ref, cl_ref):
    L = CHUNK
    W = RW_GROUP
    reps = W // RW_HEAD

    @pl.when(pl.program_id(1) == 0)
    def _():
        for ref in (s_ref, cr_ref, ck_ref, cv_ref, cl_ref):
            ref[...] = jnp.zeros(ref.shape, F32)

    bdm = bd_ref[...]

    def bd(x):
        return jnp.concatenate([x.astype(BF16)] * reps, axis=0) * bdm

    def head_sum(x):
        return _dot(x.astype(BF16), bdm)

    row0 = lax.broadcasted_iota(jnp.int32, (L, 1), 0) == 0
    row = lax.broadcasted_iota(jnp.int32, (L, W), 0)
    sub = lax.broadcasted_iota(jnp.int32, (L, W), 1) % RW_HEAD
    strict = sub < row
    incl = sub <= row
    eye = (sub == row).astype(F32)
    tri = (lax.broadcasted_iota(jnp.int32, (L, L), 1)
           <= lax.broadcasted_iota(jnp.int32, (L, L), 0)).astype(BF16)
    inv = 1.0 / RW_HEAD

    def shift(z, carry_ref, idx, mu):
        prev = jnp.where(row0, carry_ref[idx], pltpu.roll(z, 1, axis=0))
        carry_ref[idx] = z[L - 1:L, :]
        return z + (prev - z) * mu

    def lora_inputs(b):
        xl = shift(zl_ref[b].astype(F32), cl_ref, (b,), mul_ref[...])
        return (jnp.tanh(xl[:, :RW_DECAY_LORA]).astype(BF16),
                xl[:, RW_DECAY_LORA:RW_DECAY_LORA + RW_AAA_LORA].astype(BF16),
                jax.nn.sigmoid(xl[:, RW_DECAY_LORA + RW_AAA_LORA:]).astype(BF16))

    def instance(b, gi, xw, xa, xg):
        sl = slice(gi * W, (gi + 1) * W)
        r = shift(zr_ref[b, :, sl].astype(F32), cr_ref, (b, slice(None), sl), mur_ref[:, sl])
        k = shift(zk_ref[b, :, sl].astype(F32), ck_ref, (b, slice(None), sl), muk_ref[:, sl])
        v = shift(zv_ref[b, :, sl].astype(F32), cv_ref, (b, slice(None), sl), muv_ref[:, sl])
        lw = -math.exp(-0.5) * jax.nn.sigmoid(w0_ref[:, sl] + _dot(xw, wup_ref[:, sl]))
        a = jax.nn.sigmoid(a0_ref[:, sl] + _dot(xa, aup_ref[:, sl]))
        g = _dot(xg, gup_ref[:, sl])
        kk = k * kk_ref[:, sl]
        ss = head_sum(kk * kk)
        yield
        kk = kk * lax.rsqrt(jnp.maximum(ss, 1e-24))
        k = k * (1.0 + (a - 1.0) * ka_ref[:, sl])
        be = kk * a
        lw_hi, lw_lo = _split_bf16(lw)
        cum = _dot(tri, lw_hi) + _dot(tri, lw_lo)
        yield
        cum_l = cum[L - 1:L, :]
        e_neg = jnp.exp(-cum)
        e_tail = jnp.exp(cum_l - cum)
        ar = jnp.concatenate([-kk * jnp.exp(cum - lw), r * jnp.exp(cum)], axis=0).astype(BF16)
        s0 = s_ref[b, gi]
        ar_s = _dot_nt(ar, s0.astype(BF16))
        a_b = _dot_nt(ar, bd(be * e_neg))
        a_k = _dot_nt(ar, bd(k * e_neg))
        yield
        n = jnp.where(strict, a_b[:L], 0.0)
        a_ak = jnp.where(strict, a_k[:L], 0.0)
        a_rb = jnp.where(incl, a_b[L:], 0.0)
        a_rk = jnp.where(incl, a_k[L:], 0.0)
        p = eye + n
        nk = _dot(n.astype(BF16), bd(n))
        bd_v = bd(v)
        x0 = _dot(a_ak.astype(BF16), bd_v)
        yield
        steps = int(math.log2(L)) - 1
        for it in range(steps):
            m = bd(nk)
            if it + 1 < steps:
                res = _dot(jnp.concatenate([p, nk], axis=0).astype(BF16), m)
                p = p + res[:L]
                nk = res[L:]
            else:
                p = p + _dot(p.astype(BF16), m)
            yield
        u = _dot(p.astype(BF16), bd(ar_s[:L] + x0))
        yield
        y = ar_s[L:] + _dot(jnp.concatenate([a_rb, a_rk], axis=1).astype(BF16),
                            jnp.concatenate([bd(u), bd_v], axis=0))
        uv = jnp.concatenate([u, v], axis=0).astype(BF16)
        bk = jnp.concatenate([be * e_tail, k * e_tail], axis=0).astype(BF16)
        s_ref[b, gi] = s0 * jnp.exp(cum_l) + _dot_tn(uv, bk) * bdm.astype(F32)
        yield
        sums = head_sum(jnp.concatenate([y, r * k * rk_ref[:, sl]], axis=0))
        yield
        d = y - sums[:L] * inv
        var = head_sum(d * d) * inv
        yield
        yn = d * lax.rsqrt(var + RW_LN_EPS) * lnw_ref[:, sl] + lnb_ref[:, sl]
        o_ref[b, :, sl] = ((yn + sums[L:] * v) * g).astype(o_ref.dtype)

    live = []
    for b in range(rows):
        lora = lora_inputs(b)
        live += [instance(b, gi, *lora) for gi in range(RW_DIM // W)]
    while live:
        live = [g for g in live if next(g, True) is None]


def _rw_mixer(z, b, consts):
    t = z.shape[0]
    s = t // b
    rows = math.gcd(b, 8)
    gid = jnp.arange(RW_GROUP) // RW_HEAD
    ones_bd = (gid[:, None] == gid[None, :]).astype(BF16)
    sec = lambda w, j: pl.BlockSpec((rows, CHUNK, w), lambda bb, c: (bb, c, j))
    z3 = z.reshape(b, s, -1)
    return pl.pallas_call(
        functools.partial(_rw_kernel, rows),
        grid=(b // rows, s // CHUNK),
        in_specs=[sec(RW_DIM, OFF_RW // RW_DIM), sec(RW_DIM, OFF_RW // RW_DIM + 1),
                  sec(RW_DIM, OFF_RW // RW_DIM + 2), sec(RW_LORA, OFF_LORA // RW_LORA)]
        + [_pspec(c) for c in consts] + [_const_spec(ones_bd.shape)],
        out_specs=sec(RW_DIM, 0),
        out_shape=jax.ShapeDtypeStruct((b, s, RW_DIM), BF16),
        scratch_shapes=[pltpu.VMEM((rows, RW_DIM // RW_GROUP, RW_GROUP, RW_GROUP), F32)]
        + [pltpu.VMEM((rows, 1, RW_DIM), F32)] * 3 + [pltpu.VMEM((rows, 1, RW_LORA), F32)],
        compiler_params=_cparams("parallel", "arbitrary"),
        name="rw_mixer",
    )(z3, z3, z3, z3, *_pargs(consts), ones_bd).reshape(t, RW_DIM)


FF_SPLIT = 4


def _post_kernel(om_ref, or_ref, oc_ref, g0_ref, g1_ref, g2_ref, h_ref, p_ref, wb_ref, wo_ref, gm_ref,
                 g1n_ref, w1_ref, w2_ref, g2n_ref, wg_ref, wp_ref, o_ref):
    merged = None
    for n, (b_ref, zg_ref) in enumerate(((om_ref, g0_ref), (or_ref, g1_ref), (oc_ref, g2_ref))):
        y = _dot(b_ref[...], wb_ref[n * BRANCH_DIM:(n + 1) * BRANCH_DIM, :])
        gate = jax.nn.sigmoid(zg_ref[...].astype(F32))
        merged = gate * y if merged is None else merged + gate * y
    h = h_ref[...] + _rms(_dot(merged.astype(BF16), wo_ref[...]), gm_ref[...])
    f = _rms(h, g1n_ref[...]).astype(BF16)
    cw = D_FF // FF_SPLIT
    acc = None
    for c in range(FF_SPLIT):
        a = jnp.maximum(_dot(f, w1_ref[:, c * cw:(c + 1) * cw]), 0.0)
        part = _dot((a * a).astype(BF16), w2_ref[c * cw:(c + 1) * cw, :])
        acc = part if acc is None else acc + part
    h = h + _rms(acc, g2n_ref[...])
    gate = jax.nn.sigmoid(_dot(h.astype(BF16), wg_ref[...]))
    o_ref[...] = h + gate * _dot(p_ref[...].astype(BF16), wp_ref[...])


def _post(o_mla, o_rw, o_ca, z, h, p, consts):
    t = h.shape[0]
    tm = min(t, 512)
    row = lambda w: pl.BlockSpec((tm, w), lambda i: (i, 0))
    gate = lambda n: pl.BlockSpec((tm, D_MODEL), lambda i: (i, OFF_GATE // D_MODEL + n))
    p_stack, layer = p
    return pl.pallas_call(
        _post_kernel,
        grid=(t // tm,),
        in_specs=[row(BRANCH_DIM), row(BRANCH_DIM), row(BRANCH_DIM), gate(0), gate(1), gate(2), row(D_MODEL),
                  pl.BlockSpec((None, tm, D_PLE), lambda i: (layer, i, 0))] + [_pspec(c) for c in consts],
        out_specs=row(D_MODEL),
        out_shape=jax.ShapeDtypeStruct((t, D_MODEL), F32),
        compiler_params=_cparams("parallel"),
        name="post",
    )(o_mla, o_rw, o_ca, z, z, z, h, p_stack, *_pargs(consts))


def _rot_half_cols(w):
    half = w.shape[-1] // 2
    return jnp.concatenate([-w[..., half:], w[..., :half]], axis=-1)


def _pack_w_in(w):
    mla_cols = MLA_Q_RANK + MLA_KV_RANK + MLA_ROPE
    rw_cols = 3 * RW_DIM + RW_LORA
    w_mla = w[..., :mla_cols]
    w_rw = w[..., mla_cols:mla_cols + rw_cols]
    w_ca = w[..., mla_cols + rw_cols:mla_cols + rw_cols + 3 * CA_DIM]
    w_gate = w[..., mla_cols + rw_cols + 3 * CA_DIM:]
    w_kr = w_mla[..., MLA_Q_RANK + MLA_KV_RANK:]
    z64 = jnp.zeros(w.shape[:-1] + (64,), w.dtype)
    packed = jnp.concatenate(
        [w_gate, w_ca[..., CA_DIM:2 * CA_DIM], w_rw, w_mla[..., :MLA_Q_RANK + MLA_KV_RANK], w_kr, z64,
         _rot_half_cols(w_kr), z64], axis=-1)
    wt = jnp.concatenate([w_ca[..., :CA_DIM] * (CA_HEAD ** -0.5 * LOG2E), w_ca[..., 2 * CA_DIM:]], axis=-1)
    return packed.astype(BF16), jnp.swapaxes(wt, 1, 2).astype(BF16)


def _pack_w_uq(w):
    nl, r, _ = w.shape
    w = w.reshape(nl, r, MLA_HEADS, MLA_QK)
    z64 = jnp.zeros((nl, r, MLA_HEADS, 64), w.dtype)
    rot = _rot_half_cols(w[..., MLA_NOPE:])
    wq = jnp.concatenate([w, z64], axis=-1).reshape(nl, r, MLA_HEADS * MLA_HEAD_PAD)
    wqr = jnp.concatenate([rot, z64], axis=-1).reshape(nl, r, MLA_HEADS * 128)
    return jnp.swapaxes(wq, 1, 2).astype(BF16), jnp.swapaxes(wqr, 1, 2).astype(BF16)


def _pack_w_ukv(w):
    nl, r, _ = w.shape
    w = w.reshape(nl, r, MLA_HEADS, MLA_NOPE + MLA_V)
    wk = w[..., :MLA_NOPE].reshape(nl, r, MLA_HEADS * MLA_NOPE)
    wv = w[..., MLA_NOPE:].reshape(nl, r, MLA_HEADS * MLA_V)
    return wk.astype(BF16), jnp.swapaxes(wv, 1, 2).astype(BF16)


def kernel(x, p, positions, pre_mix_g, w_in, mla_q_norm_g, mla_kv_norm_g, mla_w_uq, mla_w_ukv, rw_mu, rw_w0, rw_w_up, rw_a0, rw_a_up, rw_g_up, rw_k_k, rw_k_a, rw_r_k, rw_ln_w, rw_ln_b, ca_rel_bias, w_branch, w_out, post_mix_g, pre_ff_g, w_ff1, w_ff2, post_ff_g, w_ple_gate, w_ple_proj):
    b, s, d = x.shape
    t = b * s
    depth = w_in.shape[0]
    row = lambda a: a.reshape(depth, 1, -1).astype(F32)
    bf = lambda a: a.astype(BF16)
    w_in_p, w_in_t = _pack_w_in(w_in)
    wqt, wqrt = _pack_w_uq(mla_w_uq)
    wk, wvt = _pack_w_ukv(mla_w_ukv)
    mu = row(rw_mu)
    stacks = dict(
        in_proj=[row(pre_mix_g), w_in_p, w_in_t],
        mla_proj=[row(mla_q_norm_g), row(mla_kv_norm_g), wqt, wqrt, wk, wvt],
        rw=[mu[..., :RW_DIM], mu[..., RW_DIM:2 * RW_DIM], mu[..., 2 * RW_DIM:3 * RW_DIM], mu[..., 3 * RW_DIM:],
            row(rw_w0), bf(rw_w_up), row(rw_a0), bf(rw_a_up), bf(rw_g_up), row(rw_k_k), row(rw_k_a),
            row(rw_r_k), row(rw_ln_w), row(rw_ln_b)],
        post=[bf(w_branch), bf(w_out), row(post_mix_g),
              row(pre_ff_g), bf(w_ff1), bf(w_ff2), row(post_ff_g), bf(w_ple_gate), bf(w_ple_proj)],
    )
    p3 = p.reshape(depth, t, -1)
    cs, sn, cst, snt = _rope_table(positions)
    h = x.reshape(t, d)
    for i in range(depth):
        prm = {name: [(a, i) for a in arrs] for name, arrs in stacks.items()}
        z, zt = _in_proj(h, *prm["in_proj"])
        qt, kn, kr, vt = _mla_proj(z, cs, sn, cst, snt, *prm["mla_proj"])
        o_mla = _mla_attn(qt, kn, kr, vt, b).reshape(t, -1)
        o_rw = _rw_mixer(z, b, prm["rw"])
        o_ca = _ca_attn(zt, z, _ca_bias_table(ca_rel_bias[i]), b).reshape(t, -1)
        h = _post(o_mla, o_rw, o_ca, z, h, (p3, i), prm["post"])
    return h.reshape(b, s, d)
```

```python
import functools
import math

import jax
import jax.numpy as jnp
from jax import lax
from jax.experimental import pallas as pl
from jax.experimental.pallas import tpu as pltpu

F32 = jnp.float32
BF16 = jnp.bfloat16

D_MODEL = 1024
D_PLE = 256
D_FF = 4 * D_MODEL
NORM_EPS = 1e-6
CHUNK = 64

MLA_HEADS = 4
MLA_NOPE = 128
MLA_ROPE = 64
MLA_V = 128
MLA_Q_RANK = 256
MLA_KV_RANK = 128
ROPE_THETA = 10000.0
MLA_QK = MLA_NOPE + MLA_ROPE
MLA_HEAD_PAD = 256

RW_HEADS = 8
RW_HEAD = 64
RW_DIM = RW_HEADS * RW_HEAD
RW_DECAY_LORA = 64
RW_AAA_LORA = 64
RW_GATE_LORA = 128
RW_LORA = RW_DECAY_LORA + RW_AAA_LORA + RW_GATE_LORA
RW_LN_EPS = 64e-5

CA_HEADS = 8
CA_HEAD = 64
CA_DIM = CA_HEADS * CA_HEAD
CA_LEFT_CHUNKS = 8
CA_PAD = CA_LEFT_CHUNKS * CHUNK
REL_MIN = -(CHUNK - 1)
REL_MAX = 256

N_BRANCH = 3
BRANCH_DIM = 512
GATE_COLS = N_BRANCH * D_MODEL

OFF_GATE = 0
OFF_CAK = GATE_COLS
OFF_RW = OFF_CAK + CA_DIM
OFF_LORA = OFF_RW + 3 * RW_DIM
OFF_ZQ = OFF_LORA + RW_LORA
OFF_ZKV = OFF_ZQ + MLA_Q_RANK
OFF_KR = OFF_ZKV + MLA_KV_RANK
OFF_KRR = OFF_KR + 128
IN_PACKED = OFF_KRR + 128
LOG2E = math.log2(math.e)

VMEM_LIMIT = 56 * 1024 * 1024


def _cparams(*sem):
    return pltpu.CompilerParams(dimension_semantics=sem, vmem_limit_bytes=VMEM_LIMIT)


def _rms(x, g):
    return x * lax.rsqrt(jnp.mean(x * x, axis=-1, keepdims=True) + NORM_EPS) * g


def _dot(a, b):
    return jnp.dot(a, b, preferred_element_type=F32)


def _dot_nt(a, b):
    return lax.dot_general(a, b, (((1,), (1,)), ((), ())), preferred_element_type=F32)


def _dot_tn(a, b):
    return lax.dot_general(a, b, (((0,), (0,)), ((), ())), preferred_element_type=F32)


def _const_spec(shape):
    nd = len(shape)
    return pl.BlockSpec(shape, lambda *_: (0,) * nd, pipeline_mode=pl.Buffered(1))


def _pshape(param):
    return param[0].shape[1:]


def _pspec(param):
    stack, layer = param
    nd = stack.ndim - 1
    return pl.BlockSpec((None,) + stack.shape[1:], lambda *_: (layer,) + (0,) * nd, pipeline_mode=pl.Buffered(1))


def _pargs(params):
    return [stack for stack, _ in params]


PIPE_DEPTH = 2
ONES_ROWS = 16


def _pipelined(items, produce, consume):
    pending, outs = [], []
    for it in items:
        pending.append((it, produce(it)))
        if len(pending) > PIPE_DEPTH:
            outs.append(consume(*pending.pop(0)))
    outs.extend(consume(*pc) for pc in pending)
    return outs


def _rope_table_kernel(pos_ref, freq_ref, cs_ref, sn_ref, cst_ref, snt_ref):
    ang = freq_ref[...] * pos_ref[...]
    dead = jnp.zeros((128 - MLA_ROPE, ang.shape[1]), F32)
    cst = jnp.concatenate([jnp.cos(ang), dead], axis=0)
    snt = jnp.concatenate([jnp.sin(ang), dead], axis=0)
    cst_ref[...] = cst
    snt_ref[...] = snt
    cs_ref[...] = cst.T
    sn_ref[...] = snt.T


def _rope_table(positions):
    t = positions.size
    tm = min(t, 2048)
    half = MLA_ROPE // 2
    inv_freq = 1.0 / (ROPE_THETA ** (jnp.arange(half, dtype=F32) / half))
    freq = jnp.concatenate([inv_freq, inv_freq])[:, None]
    return pl.pallas_call(
        _rope_table_kernel,
        grid=(t // tm,),
        in_specs=[pl.BlockSpec((1, tm), lambda i: (0, i)), _const_spec(freq.shape)],
        out_specs=[pl.BlockSpec((tm, 128), lambda i: (i, 0))] * 2 + [pl.BlockSpec((128, tm), lambda i: (0, i))] * 2,
        out_shape=[jax.ShapeDtypeStruct((t, 128), F32)] * 2 + [jax.ShapeDtypeStruct((128, t), F32)] * 2,
        compiler_params=_cparams("parallel"),
        name="rope_table",
    )(positions.astype(F32).reshape(1, t), freq)


IN_RANGES = ((0, 1536), (1536, OFF_CAK), (OFF_CAK, OFF_RW), (OFF_RW, OFF_RW + 896), (OFF_RW + 896, OFF_ZQ),
             (OFF_ZQ, IN_PACKED))


def _mla_project(lat, cs, sn, cst, snt, gq_ref, gkv_ref, wqt_ref, wqrt_ref, wk_ref, wvt_ref,
                 qt_ref, k_ref, kr_ref, vt_ref):
    qn = _rms(lat[:, :MLA_Q_RANK], gq_ref[...]).astype(BF16)
    qt = _dot_nt(wqt_ref[...], qn)
    qrt = _dot_nt(wqrt_ref[...], qn)
    scale = MLA_QK ** -0.5 * LOG2E
    for h in range(MLA_HEADS):
        o = h * MLA_HEAD_PAD
        qt_ref[o:o + 128, :] = (qt[o:o + 128] * scale).astype(BF16)
        rope = qt[o + 128:o + 256] * cst + qrt[h * 128:(h + 1) * 128] * snt
        qt_ref[o + 128:o + 256, :] = (rope * scale).astype(BF16)
    kvn = _rms(lat[:, MLA_Q_RANK:MLA_Q_RANK + MLA_KV_RANK], gkv_ref[...]).astype(BF16)
    k_ref[...] = _dot(kvn, wk_ref[...]).astype(BF16)
    vt_ref[...] = _dot_nt(wvt_ref[...], kvn).astype(BF16)
    kr_ref[...] = (lat[:, OFF_KR - OFF_ZQ:OFF_KRR - OFF_ZQ] * cs + lat[:, OFF_KRR - OFF_ZQ:] * sn).astype(BF16)


def _in_proj_kernel(seq_len, x_ref, cs_ref, sn_ref, cst_ref, snt_ref, g_ref, w_ref, wt_ref, mu_ref, gq_ref, gkv_ref,
                    wqt_ref, wqrt_ref, wk_ref, wvt_ref, o_ref, ot_ref, qt_ref, k_ref, kr_ref, vt_ref, carry_ref):
    tm = x_ref.shape[0]
    xn = _rms(x_ref[...], g_ref[...]).astype(BF16)
    first = (pl.program_id(0) * tm) % seq_len == 0
    row0 = lax.broadcasted_iota(jnp.int32, (tm, 1), 0) == 0
    for c, e in IN_RANGES:
        res = _dot(xn, w_ref[:, c:e])
        if c >= OFF_ZQ:
            _mla_project(res, cs_ref[...], sn_ref[...], cst_ref[...], snt_ref[...], gq_ref, gkv_ref, wqt_ref,
                         wqrt_ref, wk_ref, wvt_ref, qt_ref, k_ref, kr_ref, vt_ref)
            continue
        if c >= OFF_RW:
            cs = slice(c - OFF_RW, e - OFF_RW)
            last = jnp.where(first, 0.0, carry_ref[:, cs])
            prev = jnp.where(row0, last, pltpu.roll(res, 1, axis=0))
            carry_ref[:, cs] = res[tm - 1:tm, :]
            res = res + (prev - res) * mu_ref[:, cs]
        o_ref[:, c:e] = res.astype(o_ref.dtype)
    ot_ref[...] = _dot_nt(wt_ref[...], xn).astype(ot_ref.dtype)


def _in_proj(h, seq_len, tables, consts):
    t, d = h.shape
    nt = _pshape(consts[2])[0]
    tm = min(t, 512)
    hv = MLA_HEADS * MLA_V
    hq = MLA_HEADS * MLA_HEAD_PAD
    row = lambda w: pl.BlockSpec((tm, w), lambda i: (i, 0))
    col = lambda w: pl.BlockSpec((w, tm), lambda i: (0, i))
    return pl.pallas_call(
        functools.partial(_in_proj_kernel, seq_len),
        grid=(t // tm,),
        in_specs=[row(d), row(128), row(128), col(128), col(128)] + [_pspec(c) for c in consts],
        out_specs=[row(OFF_ZQ), col(nt), col(hq), row(hv), row(128), col(hv)],
        out_shape=[
            jax.ShapeDtypeStruct((t, OFF_ZQ), BF16),
            jax.ShapeDtypeStruct((nt, t), BF16),
            jax.ShapeDtypeStruct((hq, t), BF16),
            jax.ShapeDtypeStruct((t, hv), BF16),
            jax.ShapeDtypeStruct((t, 128), BF16),
            jax.ShapeDtypeStruct((hv, t), BF16),
        ],
        scratch_shapes=[pltpu.VMEM((1, OFF_ZQ - OFF_RW), F32)],
        compiler_params=_cparams("arbitrary"),
        name="in_proj",
    )(h, *tables, *_pargs(consts))


MLA_TQ = 512


def _mla_attn_kernel(qt_ref, k_ref, kr_ref, vt_ref, o_ref, m_ref, acc_ref):
    i = pl.program_id(1)
    tq = MLA_TQ
    m_ref[...] = jnp.full(m_ref.shape, -1e30, F32)
    acc_ref[...] = jnp.zeros(acc_ref.shape, F32)
    ones = jnp.ones((ONES_ROWS, tq), BF16)

    def run(blocks):
        def scores(j, masked, h):
            start = pl.multiple_of(j * tq, tq)
            kh = jnp.concatenate([k_ref[0, pl.ds(start, tq), h * 128:(h + 1) * 128],
                                  kr_ref[0, pl.ds(start, tq), :]], axis=1)
            s = _dot(kh, qt_ref[h * MLA_HEAD_PAD:(h + 1) * MLA_HEAD_PAD, :])
            if masked:
                kc = lax.broadcasted_iota(jnp.int32, s.shape, 0) // CHUNK
                qc = lax.broadcasted_iota(jnp.int32, s.shape, 1) // CHUNK
                s = jnp.where(kc <= qc, s, -1e30)
            return s

        def update(j, h, s):
            start = pl.multiple_of(j * tq, tq)
            m_prev = m_ref[h]
            m_new = jnp.maximum(m_prev, jnp.max(s, axis=0, keepdims=True))
            a = jnp.exp2(m_prev - m_new)
            p = jnp.exp2(s - m_new).astype(BF16)
            vh = jnp.concatenate([vt_ref[h * MLA_V:(h + 1) * MLA_V, pl.ds(start, tq)], ones], axis=0)
            acc_ref[h] = a * acc_ref[h] + _dot(vh, p)
            m_ref[h] = m_new

        items = [(j, masked, h) for j, masked in blocks for h in range(MLA_HEADS)]
        _pipelined(items, lambda it: scores(*it), lambda it, s: update(it[0], it[2], s))

    def body(jj, c):
        run([(2 * jj, False), (2 * jj + 1, False)])
        return c

    lax.fori_loop(0, i // 2, body, 0)
    pl.when(i % 2 == 1)(lambda: run([(i - 1, False), (i, True)]))
    pl.when(i % 2 == 0)(lambda: run([(i, True)]))
    for h in range(MLA_HEADS):
        acc = acc_ref[h]
        o_ref[0, :, h * MLA_V:(h + 1) * MLA_V] = (acc[:MLA_V] / acc[MLA_V:MLA_V + 1]).T.astype(o_ref.dtype)


def _mla_attn(qt, k, kr, vt, b):
    hq, t = qt.shape
    s = t // b
    tq = MLA_TQ
    nq = s // tq
    hv = MLA_HEADS * MLA_V
    full = lambda w: pl.BlockSpec((1, s, w), lambda bi, i: (bi, 0, 0))
    return pl.pallas_call(
        _mla_attn_kernel,
        grid=(b, nq),
        in_specs=[
            pl.BlockSpec((hq, tq), lambda bi, i: (0, bi * nq + i)),
            full(hv), full(128),
            pl.BlockSpec((hv, s), lambda bi, i: (0, bi)),
        ],
        out_specs=pl.BlockSpec((1, tq, hv), lambda bi, i: (bi, i, 0)),
        out_shape=jax.ShapeDtypeStruct((b, s, hv), BF16),
        scratch_shapes=[pltpu.VMEM((MLA_HEADS, 1, tq), F32),
                        pltpu.VMEM((MLA_HEADS, MLA_V + ONES_ROWS, tq), F32)],
        compiler_params=_cparams("parallel", "arbitrary"),
        name="mla_attn",
    )(qt, k.reshape(b, s, hv), kr.reshape(b, s, 128), vt)


CA_TQ = 256
CA_BAND_BLK = CA_TQ + CA_PAD


def _ca_attn_kernel(qt_ref, k_ref, vt_ref, bias_ref, o_ref):
    i = pl.program_id(1)
    lo = lax.broadcasted_iota(jnp.int32, (128, CA_TQ), 0) < CA_HEAD
    zero = jnp.zeros((), BF16)

    def block(k_start, n_keys, bias_off):
        def scores(h):
            sl = slice((h // 2) * 128, (h // 2 + 1) * 128)
            k2 = k_ref[0, pl.ds(k_start, n_keys), sl]
            qh = jnp.where(lo if h % 2 == 0 else ~lo, qt_ref[sl, :], zero)
            return _dot(k2, qh) + bias_ref[h, bias_off:bias_off + n_keys, :]

        def attend(h, s):
            p = jnp.exp2(s - jnp.max(s, axis=0, keepdims=True)).astype(BF16)
            vh = jnp.concatenate([vt_ref[h * CA_HEAD:(h + 1) * CA_HEAD, pl.ds(k_start, n_keys)],
                                  jnp.ones((ONES_ROWS, n_keys), BF16)], axis=0)
            out = _dot(vh, p)
            return out[:CA_HEAD] / out[CA_HEAD:CA_HEAD + 1]

        outs = _pipelined(list(range(CA_HEADS)), scores, attend)
        for pair in range(CA_HEADS // 2):
            both = jnp.concatenate(outs[2 * pair:2 * pair + 2], axis=0)
            o_ref[0, :, pair * 128:(pair + 1) * 128] = both.T.astype(o_ref.dtype)

    lead = CA_PAD // CA_TQ
    for j in range(lead):
        pl.when(i == j)(functools.partial(block, 0, (j + 1) * CA_TQ, CA_PAD - j * CA_TQ))
    pl.when(i >= lead)(lambda: block(pl.multiple_of(i * CA_TQ - CA_PAD, CA_TQ), CA_BAND_BLK, 0))


def _ca_attn(zt, z, bias, b):
    t = zt.shape[1]
    s = t // b
    nq = s // CA_TQ
    return pl.pallas_call(
        _ca_attn_kernel,
        grid=(b, nq),
        in_specs=[
            pl.BlockSpec((CA_DIM, CA_TQ), lambda bi, i: (0, bi * nq + i)),
            pl.BlockSpec((1, s, CA_DIM), lambda bi, i: (bi, 0, OFF_CAK // CA_DIM)),
            pl.BlockSpec((CA_DIM, s), lambda bi, i: (1, bi)),
            _const_spec(bias.shape),
        ],
        out_specs=pl.BlockSpec((1, CA_TQ, CA_DIM), lambda bi, i: (bi, i, 0)),
        out_shape=jax.ShapeDtypeStruct((b, s, CA_DIM), BF16),
        compiler_params=_cparams("parallel", "arbitrary"),
        name="ca_attn",
    )(zt, z.reshape(b, s, -1), zt, bias)


def _ca_bias_table(rel_bias):
    width = CA_TQ + CA_BAND_BLK
    m = jnp.arange(width)
    delta = jnp.where(m < CA_BAND_BLK, m, m - width)
    idx = jnp.clip(CA_PAD - delta, REL_MIN, REL_MAX) - REL_MIN
    t1 = rel_bias.astype(F32)[idx].T * LOG2E
    flat = jnp.tile(t1, (1, CA_TQ))[:, :CA_TQ * (width - 1)]
    bias = flat.reshape(CA_HEADS, CA_TQ, width - 1)[:, :, :CA_BAND_BLK]
    rc = jnp.arange(CA_TQ)[:, None] // CHUNK
    cc = jnp.arange(CA_BAND_BLK)[None, :] // CHUNK
    ok = (cc >= rc) & (cc <= rc + CA_LEFT_CHUNKS)
    return jnp.where(ok[None], bias, -1e30).transpose(0, 2, 1)


def _split_bf16(x):
    hi = x.astype(BF16)
    lo = (x - hi.astype(F32)).astype(BF16)
    return hi, lo


RW_GROUP = 256


def _rw_kernel(rows, xr_ref, xk_ref, xv_ref, xl_ref, w0_ref, wup_ref, a0_ref, aup_ref, gup_ref, kk_ref, ka_ref,
               rk_ref, lnw_ref, lnb_ref, bd_ref, bdf_ref, o_ref, s_ref):
    L = CHUNK
    W = RW_GROUP
    reps = W // RW_HEAD

    @pl.when(pl.program_id(1) == 0)
    def _():
        s_ref[...] = jnp.zeros(s_ref.shape, F32)

    bdm = bd_ref[...]

    def bd(x):
        return jnp.concatenate([x.astype(BF16)] * reps, axis=0) * bdm

    def head_sum(x):
        return _dot(x.astype(BF16), bdm)

    row = lax.broadcasted_iota(jnp.int32, (L, W), 0)
    sub = lax.broadcasted_iota(jnp.int32, (L, W), 1) % RW_HEAD
    strict = sub < row
    incl = sub <= row
    eye = (sub == row).astype(F32)
    tri = (lax.broadcasted_iota(jnp.int32, (L, L), 1)
           <= lax.broadcasted_iota(jnp.int32, (L, L), 0)).astype(BF16)
    inv = 1.0 / RW_HEAD

    def lora_inputs(b):
        xl = xl_ref[b].astype(F32)
        return (jnp.tanh(xl[:, :RW_DECAY_LORA]).astype(BF16),
                xl[:, RW_DECAY_LORA:RW_DECAY_LORA + RW_AAA_LORA].astype(BF16),
                jax.nn.sigmoid(xl[:, RW_DECAY_LORA + RW_AAA_LORA:]).astype(BF16))

    def instance(b, gi, xw, xa, xg):
        sl = slice(gi * W, (gi + 1) * W)
        r = xr_ref[b, :, sl].astype(F32)
        k = xk_ref[b, :, sl].astype(F32)
        v = xv_ref[b, :, sl].astype(F32)
        lw = -math.exp(-0.5) * jax.nn.sigmoid(w0_ref[:, sl] + _dot(xw, wup_ref[:, sl]))
        a = jax.nn.sigmoid(a0_ref[:, sl] + _dot(xa, aup_ref[:, sl]))
        g = _dot(xg, gup_ref[:, sl])
        kk = k * kk_ref[:, sl]
        ss = head_sum(kk * kk)
        yield
        kk = kk * lax.rsqrt(jnp.maximum(ss, 1e-24))
        k = k * (1.0 + (a - 1.0) * ka_ref[:, sl])
        be = kk * a
        lw_hi, lw_lo = _split_bf16(lw)
        cum = _dot(tri, lw_hi) + _dot(tri, lw_lo)
        yield
        cum_l = cum[L - 1:L, :]
        e_neg = jnp.exp(-cum)
        e_tail = jnp.exp(cum_l - cum)
        ar = jnp.concatenate([-kk * jnp.exp(cum - lw), r * jnp.exp(cum)], axis=0).astype(BF16)
        s0 = s_ref[b, gi]
        ar_s = _dot_nt(ar, s0.astype(BF16))
        a_b = _dot_nt(ar, bd(be * e_neg))
        a_k = _dot_nt(ar, bd(k * e_neg))
        yield
        n = jnp.where(strict, a_b[:L], 0.0)
        a_ak = jnp.where(strict, a_k[:L], 0.0)
        a_rb = jnp.where(incl, a_b[L:], 0.0)
        a_rk = jnp.where(incl, a_k[L:], 0.0)
        p = eye + n
        nk = _dot(n.astype(BF16), bd(n))
        bd_v = bd(v)
        x0 = _dot(a_ak.astype(BF16), bd_v)
        yield
        steps = int(math.log2(L)) - 1
        for it in range(steps):
            m = bd(nk)
            if it + 1 < steps:
                res = _dot(jnp.concatenate([p, nk], axis=0).astype(BF16), m)
                p = p + res[:L]
                nk = res[L:]
            else:
                p = p + _dot(p.astype(BF16), m)
            yield
        u = _dot(p.astype(BF16), bd(ar_s[:L] + x0))
        yield
        y = ar_s[L:] + _dot(jnp.concatenate([a_rb, a_rk], axis=1).astype(BF16),
                            jnp.concatenate([bd(u), bd_v], axis=0))
        uv = jnp.concatenate([u, v], axis=0).astype(BF16)
        bk = jnp.concatenate([be * e_tail, k * e_tail], axis=0).astype(BF16)
        s_ref[b, gi] = s0 * jnp.exp(cum_l) + _dot_tn(uv, bk) * bdf_ref[...]
        yield
        sums = head_sum(jnp.concatenate([y, r * k * rk_ref[:, sl]], axis=0))
        yield
        d = y - sums[:L] * inv
        var = head_sum(d * d) * inv
        yield
        yn = d * lax.rsqrt(var + RW_LN_EPS) * lnw_ref[:, sl] + lnb_ref[:, sl]
        o_ref[b, :, sl] = ((yn + sums[L:] * v) * g).astype(o_ref.dtype)

    live = []
    for b in range(rows):
        lora = lora_inputs(b)
        live += [instance(b, gi, *lora) for gi in range(RW_DIM // W)]
    while live:
        live = [g for g in live if next(g, True) is None]


def _rw_mixer(z, b, consts):
    t = z.shape[0]
    s = t // b
    rows = math.gcd(b, 8)
    gid = jnp.arange(RW_GROUP) // RW_HEAD
    ones_f32 = (gid[:, None] == gid[None, :]).astype(F32)
    ones_bd = ones_f32.astype(BF16)
    sec = lambda w, j: pl.BlockSpec((rows, CHUNK, w), lambda bb, c: (bb, c, j))
    z3 = z.reshape(b, s, -1)
    return pl.pallas_call(
        functools.partial(_rw_kernel, rows),
        grid=(b // rows, s // CHUNK),
        in_specs=[sec(RW_DIM, OFF_RW // RW_DIM), sec(RW_DIM, OFF_RW // RW_DIM + 1),
                  sec(RW_DIM, OFF_RW // RW_DIM + 2), sec(RW_LORA, OFF_LORA // RW_LORA)]
        + [_pspec(c) for c in consts] + [_const_spec(ones_bd.shape)] * 2,
        out_specs=sec(RW_DIM, 0),
        out_shape=jax.ShapeDtypeStruct((b, s, RW_DIM), BF16),
        scratch_shapes=[pltpu.VMEM((rows, RW_DIM // RW_GROUP, RW_GROUP, RW_GROUP), F32)],
        compiler_params=_cparams("parallel", "arbitrary"),
        name="rw_mixer",
    )(z3, z3, z3, z3, *_pargs(consts), ones_bd, ones_f32).reshape(t, RW_DIM)


FF_SPLIT = 4


def _post_kernel(om_ref, or_ref, oc_ref, g0_ref, g1_ref, g2_ref, h_ref, p_ref, wb_ref, wo_ref, gm_ref,
                 g1n_ref, w1_ref, w2_ref, g2n_ref, wg_ref, wp_ref, o_ref):
    merged = None
    for n, (b_ref, zg_ref) in enumerate(((om_ref, g0_ref), (or_ref, g1_ref), (oc_ref, g2_ref))):
        y = _dot(b_ref[...], wb_ref[n * BRANCH_DIM:(n + 1) * BRANCH_DIM, :])
        gate = jax.nn.sigmoid(zg_ref[...].astype(F32))
        merged = gate * y if merged is None else merged + gate * y
    h = h_ref[...] + _rms(_dot(merged.astype(BF16), wo_ref[...]), gm_ref[...])
    f = _rms(h, g1n_ref[...]).astype(BF16)
    cw = D_FF // FF_SPLIT
    acc = None
    for c in range(FF_SPLIT):
        a = jnp.maximum(_dot(f, w1_ref[:, c * cw:(c + 1) * cw]), 0.0)
        part = _dot((a * a).astype(BF16), w2_ref[c * cw:(c + 1) * cw, :])
        acc = part if acc is None else acc + part
    h = h + _rms(acc, g2n_ref[...])
    gate = jax.nn.sigmoid(_dot(h.astype(BF16), wg_ref[...]))
    o_ref[...] = h + gate * _dot(p_ref[...].astype(BF16), wp_ref[...])


def _post(o_mla, o_rw, o_ca, z, h, p, consts):
    t = h.shape[0]
    tm = min(t, 512)
    row = lambda w: pl.BlockSpec((tm, w), lambda i: (i, 0))
    gate = lambda n: pl.BlockSpec((tm, D_MODEL), lambda i: (i, OFF_GATE // D_MODEL + n))
    p_stack, layer = p
    return pl.pallas_call(
        _post_kernel,
        grid=(t // tm,),
        in_specs=[row(BRANCH_DIM), row(BRANCH_DIM), row(BRANCH_DIM), gate(0), gate(1), gate(2), row(D_MODEL),
                  pl.BlockSpec((None, tm, D_PLE), lambda i: (layer, i, 0))] + [_pspec(c) for c in consts],
        out_specs=row(D_MODEL),
        out_shape=jax.ShapeDtypeStruct((t, D_MODEL), F32),
        compiler_params=_cparams("parallel"),
        name="post",
    )(o_mla, o_rw, o_ca, z, z, z, h, p_stack, *_pargs(consts))


def _rot_half_cols(w):
    half = w.shape[-1] // 2
    return jnp.concatenate([-w[..., half:], w[..., :half]], axis=-1)


def _pack_w_in(w):
    mla_cols = MLA_Q_RANK + MLA_KV_RANK + MLA_ROPE
    rw_cols = 3 * RW_DIM + RW_LORA
    w_mla = w[..., :mla_cols]
    w_rw = w[..., mla_cols:mla_cols + rw_cols]
    w_ca = w[..., mla_cols + rw_cols:mla_cols + rw_cols + 3 * CA_DIM]
    w_gate = w[..., mla_cols + rw_cols + 3 * CA_DIM:]
    w_kr = w_mla[..., MLA_Q_RANK + MLA_KV_RANK:]
    z64 = jnp.zeros(w.shape[:-1] + (64,), w.dtype)
    packed = jnp.concatenate(
        [w_gate, w_ca[..., CA_DIM:2 * CA_DIM], w_rw, w_mla[..., :MLA_Q_RANK + MLA_KV_RANK], w_kr, z64,
         _rot_half_cols(w_kr), z64], axis=-1)
    wt = jnp.concatenate([w_ca[..., :CA_DIM] * (CA_HEAD ** -0.5 * LOG2E), w_ca[..., 2 * CA_DIM:]], axis=-1)
    return packed.astype(BF16), jnp.swapaxes(wt, 1, 2).astype(BF16)


def _pack_w_uq(w):
    nl, r, _ = w.shape
    w = w.reshape(nl, r, MLA_HEADS, MLA_QK)
    z64 = jnp.zeros((nl, r, MLA_HEADS, 64), w.dtype)
    rot = _rot_half_cols(w[..., MLA_NOPE:])
    wq = jnp.concatenate([w, z64], axis=-1).reshape(nl, r, MLA_HEADS * MLA_HEAD_PAD)
    wqr = jnp.concatenate([rot, z64], axis=-1).reshape(nl, r, MLA_HEADS * 128)
    return jnp.swapaxes(wq, 1, 2).astype(BF16), jnp.swapaxes(wqr, 1, 2).astype(BF16)


def _pack_w_ukv(w):
    nl, r, _ = w.shape
    w = w.reshape(nl, r, MLA_HEADS, MLA_NOPE + MLA_V)
    wk = w[..., :MLA_NOPE].reshape(nl, r, MLA_HEADS * MLA_NOPE)
    wv = w[..., MLA_NOPE:].reshape(nl, r, MLA_HEADS * MLA_V)
    return wk.astype(BF16), jnp.swapaxes(wv, 1, 2).astype(BF16)


def kernel(x, p, positions, pre_mix_g, w_in, mla_q_norm_g, mla_kv_norm_g, mla_w_uq, mla_w_ukv, rw_mu, rw_w0, rw_w_up, rw_a0, rw_a_up, rw_g_up, rw_k_k, rw_k_a, rw_r_k, rw_ln_w, rw_ln_b, ca_rel_bias, w_branch, w_out, post_mix_g, pre_ff_g, w_ff1, w_ff2, post_ff_g, w_ple_gate, w_ple_proj):
    b, s, d = x.shape
    t = b * s
    depth = w_in.shape[0]
    row = lambda a: a.reshape(depth, 1, -1).astype(F32)
    bf = lambda a: a.astype(BF16)
    w_in_p, w_in_t = _pack_w_in(w_in)
    wqt, wqrt = _pack_w_uq(mla_w_uq)
    wk, wvt = _pack_w_ukv(mla_w_ukv)
    stacks = dict(
        in_proj=[row(pre_mix_g), w_in_p, w_in_t, row(rw_mu),
                 row(mla_q_norm_g), row(mla_kv_norm_g), wqt, wqrt, wk, wvt],
        rw=[row(rw_w0), bf(rw_w_up), row(rw_a0), bf(rw_a_up), bf(rw_g_up), row(rw_k_k), row(rw_k_a),
            row(rw_r_k), row(rw_ln_w), row(rw_ln_b)],
        post=[bf(w_branch), bf(w_out), row(post_mix_g),
              row(pre_ff_g), bf(w_ff1), bf(w_ff2), row(post_ff_g), bf(w_ple_gate), bf(w_ple_proj)],
    )
    p3 = p.reshape(depth, t, -1)
    rope = _rope_table(positions)
    h = x.reshape(t, d)
    for i in range(depth):
        prm = {name: [(a, i) for a in arrs] for name, arrs in stacks.items()}
        z, zt, qt, kn, kr, vt = _in_proj(h, s, rope, prm["in_proj"])
        o_mla = _mla_attn(qt, kn, kr, vt, b).reshape(t, -1)
        o_rw = _rw_mixer(z, b, prm["rw"])
        o_ca = _ca_attn(zt, z, _ca_bias_table(ca_rel_bias[i]), b).reshape(t, -1)
        h = _post(o_mla, o_rw, o_ca, z, h, (p3, i), prm["post"])
    return h.reshape(b, s, d)
```

```python
import functools
import math

import jax
import jax.numpy as jnp
from jax import lax
from jax.experimental import pallas as pl
from jax.experimental.pallas import tpu as pltpu

F32 = jnp.float32
BF16 = jnp.bfloat16

D_MODEL = 1024
D_PLE = 256
D_FF = 4 * D_MODEL
NORM_EPS = 1e-6
CHUNK = 64

MLA_HEADS = 4
MLA_NOPE = 128
MLA_ROPE = 64
MLA_V = 128
MLA_Q_RANK = 256
MLA_KV_RANK = 128
ROPE_THETA = 10000.0
MLA_QK = MLA_NOPE + MLA_ROPE
MLA_HEAD_PAD = 256

RW_HEADS = 8
RW_HEAD = 64
RW_DIM = RW_HEADS * RW_HEAD
RW_DECAY_LORA = 64
RW_AAA_LORA = 64
RW_GATE_LORA = 128
RW_LORA = RW_DECAY_LORA + RW_AAA_LORA + RW_GATE_LORA
RW_LN_EPS = 64e-5

CA_HEADS = 8
CA_HEAD = 64
CA_DIM = CA_HEADS * CA_HEAD
CA_LEFT_CHUNKS = 8
CA_PAD = CA_LEFT_CHUNKS * CHUNK
REL_MIN = -(CHUNK - 1)
REL_MAX = 256

N_BRANCH = 3
BRANCH_DIM = 512
GATE_COLS = N_BRANCH * D_MODEL

OFF_GATE = 0
OFF_CAK = GATE_COLS
OFF_RW = OFF_CAK + CA_DIM
OFF_LORA = OFF_RW + 3 * RW_DIM
OFF_ZQ = OFF_LORA + RW_LORA
OFF_ZKV = OFF_ZQ + MLA_Q_RANK
OFF_KR = OFF_ZKV + MLA_KV_RANK
OFF_KRR = OFF_KR + 128
IN_PACKED = OFF_KRR + 128
LOG2E = math.log2(math.e)

VMEM_LIMIT = 56 * 1024 * 1024


def _cparams(*sem):
    return pltpu.CompilerParams(dimension_semantics=sem, vmem_limit_bytes=VMEM_LIMIT)


def _rms(x, g):
    return x * lax.rsqrt(jnp.mean(x * x, axis=-1, keepdims=True) + NORM_EPS) * g


def _dot(a, b):
    return jnp.dot(a, b, preferred_element_type=F32)


def _dot_nt(a, b):
    return lax.dot_general(a, b, (((1,), (1,)), ((), ())), preferred_element_type=F32)


def _dot_tn(a, b):
    return lax.dot_general(a, b, (((0,), (0,)), ((), ())), preferred_element_type=F32)


def _const_spec(shape):
    nd = len(shape)
    return pl.BlockSpec(shape, lambda *_: (0,) * nd, pipeline_mode=pl.Buffered(1))


def _pshape(param):
    return param[0].shape[1:]


def _pspec(param):
    stack, layer = param
    nd = stack.ndim - 1
    return pl.BlockSpec((None,) + stack.shape[1:], lambda *_: (layer,) + (0,) * nd, pipeline_mode=pl.Buffered(1))


def _pargs(params):
    return [stack for stack, _ in params]


def _advance(gens, answers):
    live, asked = [], []
    for g, ans in zip(gens, answers):
        try:
            asked.append(g.send(ans))
            live.append(g)
        except StopIteration:
            pass
    return live, asked


PIPE_DEPTH = 2
ONES_ROWS = 16


def _pipelined(items, produce, consume):
    pending, outs = [], []
    for it in items:
        pending.append((it, produce(it)))
        if len(pending) > PIPE_DEPTH:
            outs.append(consume(*pending.pop(0)))
    outs.extend(consume(*pc) for pc in pending)
    return outs


def _rope_table_kernel(pos_ref, freq_ref, cs_ref, sn_ref, cst_ref, snt_ref):
    ang = freq_ref[...] * pos_ref[...]
    dead = jnp.zeros((128 - MLA_ROPE, ang.shape[1]), F32)
    cst = jnp.concatenate([jnp.cos(ang), dead], axis=0)
    snt = jnp.concatenate([jnp.sin(ang), dead], axis=0)
    cst_ref[...] = cst
    snt_ref[...] = snt
    cs_ref[...] = cst.T
    sn_ref[...] = snt.T


def _rope_table(positions):
    t = positions.size
    tm = min(t, 2048)
    half = MLA_ROPE // 2
    inv_freq = 1.0 / (ROPE_THETA ** (jnp.arange(half, dtype=F32) / half))
    freq = jnp.concatenate([inv_freq, inv_freq])[:, None]
    return pl.pallas_call(
        _rope_table_kernel,
        grid=(t // tm,),
        in_specs=[pl.BlockSpec((1, tm), lambda i: (0, i)), _const_spec(freq.shape)],
        out_specs=[pl.BlockSpec((tm, 128), lambda i: (i, 0))] * 2 + [pl.BlockSpec((128, tm), lambda i: (0, i))] * 2,
        out_shape=[jax.ShapeDtypeStruct((t, 128), F32)] * 2 + [jax.ShapeDtypeStruct((128, t), F32)] * 2,
        compiler_params=_cparams("parallel"),
        name="rope_table",
    )(positions.astype(F32).reshape(1, t), freq)


IN_RANGES = ((0, 1536), (1536, OFF_CAK), (OFF_CAK, OFF_RW), (OFF_RW, OFF_RW + 896), (OFF_RW + 896, OFF_ZQ),
             (OFF_ZQ, IN_PACKED))


def _mla_project(lat, cs, sn, cst, snt, gq_ref, gkv_ref, wqt_ref, wqrt_ref, wk_ref, wvt_ref,
                 qt_ref, k_ref, kr_ref, vt_ref):
    qn = _rms(lat[:, :MLA_Q_RANK], gq_ref[...]).astype(BF16)
    qt = _dot_nt(wqt_ref[...], qn)
    qrt = _dot_nt(wqrt_ref[...], qn)
    scale = MLA_QK ** -0.5 * LOG2E
    for h in range(MLA_HEADS):
        o = h * MLA_HEAD_PAD
        qt_ref[o:o + 128, :] = (qt[o:o + 128] * scale).astype(BF16)
        rope = qt[o + 128:o + 256] * cst + qrt[h * 128:(h + 1) * 128] * snt
        qt_ref[o + 128:o + 256, :] = (rope * scale).astype(BF16)
    kvn = _rms(lat[:, MLA_Q_RANK:MLA_Q_RANK + MLA_KV_RANK], gkv_ref[...]).astype(BF16)
    k_ref[...] = _dot(kvn, wk_ref[...]).astype(BF16)
    vt_ref[...] = _dot_nt(wvt_ref[...], kvn).astype(BF16)
    kr_ref[...] = (lat[:, OFF_KR - OFF_ZQ:OFF_KRR - OFF_ZQ] * cs + lat[:, OFF_KRR - OFF_ZQ:] * sn).astype(BF16)


def _in_proj_kernel(seq_len, x_ref, cs_ref, sn_ref, cst_ref, snt_ref, g_ref, w_ref, wt_ref, mu_ref, gq_ref, gkv_ref,
                    wqt_ref, wqrt_ref, wk_ref, wvt_ref, o_ref, ot_ref, qt_ref, k_ref, kr_ref, vt_ref, carry_ref):
    tm = x_ref.shape[0]
    xn = _rms(x_ref[...], g_ref[...]).astype(BF16)
    first = (pl.program_id(0) * tm) % seq_len == 0
    row0 = lax.broadcasted_iota(jnp.int32, (tm, 1), 0) == 0
    for c, e in IN_RANGES:
        res = _dot(xn, w_ref[:, c:e])
        if c >= OFF_ZQ:
            _mla_project(res, cs_ref[...], sn_ref[...], cst_ref[...], snt_ref[...], gq_ref, gkv_ref, wqt_ref,
                         wqrt_ref, wk_ref, wvt_ref, qt_ref, k_ref, kr_ref, vt_ref)
            continue
        if c >= OFF_RW:
            cs = slice(c - OFF_RW, e - OFF_RW)
            last = jnp.where(first, 0.0, carry_ref[:, cs])
            prev = jnp.where(row0, last, pltpu.roll(res, 1, axis=0))
            carry_ref[:, cs] = res[tm - 1:tm, :]
            res = res + (prev - res) * mu_ref[:, cs]
        o_ref[:, c:e] = res.astype(o_ref.dtype)
    ot_ref[...] = _dot_nt(wt_ref[...], xn).astype(ot_ref.dtype)


def _in_proj(h, seq_len, tables, consts):
    t, d = h.shape
    nt = _pshape(consts[2])[0]
    tm = min(t, 512)
    hv = MLA_HEADS * MLA_V
    hq = MLA_HEADS * MLA_HEAD_PAD
    row = lambda w: pl.BlockSpec((tm, w), lambda i: (i, 0))
    col = lambda w: pl.BlockSpec((w, tm), lambda i: (0, i))
    return pl.pallas_call(
        functools.partial(_in_proj_kernel, seq_len),
        grid=(t // tm,),
        in_specs=[row(d), row(128), row(128), col(128), col(128)] + [_pspec(c) for c in consts],
        out_specs=[row(OFF_ZQ), col(nt), col(hq), row(hv), row(128), col(hv)],
        out_shape=[
            jax.ShapeDtypeStruct((t, OFF_ZQ), BF16),
            jax.ShapeDtypeStruct((nt, t), BF16),
            jax.ShapeDtypeStruct((hq, t), BF16),
            jax.ShapeDtypeStruct((t, hv), BF16),
            jax.ShapeDtypeStruct((t, 128), BF16),
            jax.ShapeDtypeStruct((hv, t), BF16),
        ],
        scratch_shapes=[pltpu.VMEM((1, OFF_ZQ - OFF_RW), F32)],
        compiler_params=_cparams("arbitrary"),
        name="in_proj",
    )(h, *tables, *_pargs(consts))


MLA_TQ = 512


def _mla_attn_kernel(qt_ref, k_ref, kr_ref, vt_ref, o_ref, m_ref, acc_ref):
    i = pl.program_id(1)
    tq = MLA_TQ
    m_ref[...] = jnp.full(m_ref.shape, -1e30, F32)
    acc_ref[...] = jnp.zeros(acc_ref.shape, F32)
    ones = jnp.ones((ONES_ROWS, tq), BF16)

    def run(blocks):
        def scores(j, masked, h):
            start = pl.multiple_of(j * tq, tq)
            kh = jnp.concatenate([k_ref[pl.ds(start, tq), h * 128:(h + 1) * 128],
                                  kr_ref[pl.ds(start, tq), :]], axis=1)
            s = _dot(kh, qt_ref[h * MLA_HEAD_PAD:(h + 1) * MLA_HEAD_PAD, :])
            if masked:
                kc = lax.broadcasted_iota(jnp.int32, s.shape, 0) // CHUNK
                qc = lax.broadcasted_iota(jnp.int32, s.shape, 1) // CHUNK
                s = jnp.where(kc <= qc, s, -1e30)
            return s

        def update(j, h, s):
            start = pl.multiple_of(j * tq, tq)
            m_prev = m_ref[h]
            m_new = jnp.maximum(m_prev, jnp.max(s, axis=0, keepdims=True))
            a = jnp.exp2(m_prev - m_new)
            p = jnp.exp2(s - m_new).astype(BF16)
            vh = jnp.concatenate([vt_ref[h * MLA_V:(h + 1) * MLA_V, pl.ds(start, tq)], ones], axis=0)
            acc_ref[h] = a * acc_ref[h] + _dot(vh, p)
            m_ref[h] = m_new

        items = [(j, masked, h) for j, masked in blocks for h in range(MLA_HEADS)]
        _pipelined(items, lambda it: scores(*it), lambda it, s: update(it[0], it[2], s))

    def body(jj, c):
        run([(2 * jj, False), (2 * jj + 1, False)])
        return c

    lax.fori_loop(0, i // 2, body, 0)
    pl.when(i % 2 == 1)(lambda: run([(i - 1, False), (i, True)]))
    pl.when(i % 2 == 0)(lambda: run([(i, True)]))
    for h in range(MLA_HEADS):
        acc = acc_ref[h]
        o_ref[:, h * MLA_V:(h + 1) * MLA_V] = (acc[:MLA_V] / acc[MLA_V:MLA_V + 1]).T.astype(o_ref.dtype)


def _mla_attn(qt, k, kr, vt, b):
    hq, t = qt.shape
    s = t // b
    tq = MLA_TQ
    nq = s // tq
    hv = MLA_HEADS * MLA_V
    full = lambda w: pl.BlockSpec((s, w), lambda bi, i: (bi, 0))
    return pl.pallas_call(
        _mla_attn_kernel,
        grid=(b, nq),
        in_specs=[
            pl.BlockSpec((hq, tq), lambda bi, i: (0, bi * nq + i)),
            full(hv), full(128),
            pl.BlockSpec((hv, s), lambda bi, i: (0, bi)),
        ],
        out_specs=pl.BlockSpec((tq, hv), lambda bi, i: (bi * nq + i, 0)),
        out_shape=jax.ShapeDtypeStruct((t, hv), BF16),
        scratch_shapes=[pltpu.VMEM((MLA_HEADS, 1, tq), F32),
                        pltpu.VMEM((MLA_HEADS, MLA_V + ONES_ROWS, tq), F32)],
        compiler_params=_cparams("parallel", "arbitrary"),
        name="mla_attn",
    )(qt, k, kr, vt)


CA_TQ = 256
CA_BAND_BLK = CA_TQ + CA_PAD


def _ca_attn_kernel(qt_ref, k_ref, vt_ref, bias_ref, o_ref):
    i = pl.program_id(1)
    lo = lax.broadcasted_iota(jnp.int32, (128, CA_TQ), 0) < CA_HEAD
    zero = jnp.zeros((), BF16)

    def block(k_start, n_keys, bias_off):
        def scores(h):
            sl = slice((h // 2) * 128, (h // 2 + 1) * 128)
            k2 = k_ref[pl.ds(k_start, n_keys), sl]
            qh = jnp.where(lo if h % 2 == 0 else ~lo, qt_ref[sl, :], zero)
            return _dot(k2, qh) + bias_ref[h, bias_off:bias_off + n_keys, :]

        def attend(h, s):
            p = jnp.exp2(s - jnp.max(s, axis=0, keepdims=True)).astype(BF16)
            vh = jnp.concatenate([vt_ref[h * CA_HEAD:(h + 1) * CA_HEAD, pl.ds(k_start, n_keys)],
                                  jnp.ones((ONES_ROWS, n_keys), BF16)], axis=0)
            out = _dot(vh, p)
            return out[:CA_HEAD] / out[CA_HEAD:CA_HEAD + 1]

        outs = _pipelined(list(range(CA_HEADS)), scores, attend)
        for pair in range(CA_HEADS // 2):
            both = jnp.concatenate(outs[2 * pair:2 * pair + 2], axis=0)
            o_ref[:, pair * 128:(pair + 1) * 128] = both.T.astype(o_ref.dtype)

    lead = CA_PAD // CA_TQ
    for j in range(lead):
        pl.when(i == j)(functools.partial(block, 0, (j + 1) * CA_TQ, CA_PAD - j * CA_TQ))
    pl.when(i >= lead)(lambda: block(pl.multiple_of(i * CA_TQ - CA_PAD, CA_TQ), CA_BAND_BLK, 0))


def _ca_attn(zt, z, bias, b):
    t = zt.shape[1]
    s = t // b
    nq = s // CA_TQ
    return pl.pallas_call(
        _ca_attn_kernel,
        grid=(b, nq),
        in_specs=[
            pl.BlockSpec((CA_DIM, CA_TQ), lambda bi, i: (0, bi * nq + i)),
            pl.BlockSpec((s, CA_DIM), lambda bi, i: (bi, OFF_CAK // CA_DIM)),
            pl.BlockSpec((CA_DIM, s), lambda bi, i: (1, bi)),
            _const_spec(bias.shape),
        ],
        out_specs=pl.BlockSpec((CA_TQ, CA_DIM), lambda bi, i: (bi * nq + i, 0)),
        out_shape=jax.ShapeDtypeStruct((t, CA_DIM), BF16),
        compiler_params=_cparams("parallel", "arbitrary"),
        name="ca_attn",
    )(zt, z, zt, bias)


def _ca_bias_table(rel_bias):
    width = CA_TQ + CA_BAND_BLK
    m = jnp.arange(width)
    delta = jnp.where(m < CA_BAND_BLK, m, m - width)
    idx = jnp.clip(CA_PAD - delta, REL_MIN, REL_MAX) - REL_MIN
    t1 = rel_bias.astype(F32)[idx].T * LOG2E
    flat = jnp.tile(t1, (1, CA_TQ))[:, :CA_TQ * (width - 1)]
    bias = flat.reshape(CA_HEADS, CA_TQ, width - 1)[:, :, :CA_BAND_BLK]
    rc = jnp.arange(CA_TQ)[:, None] // CHUNK
    cc = jnp.arange(CA_BAND_BLK)[None, :] // CHUNK
    ok = (cc >= rc) & (cc <= rc + CA_LEFT_CHUNKS)
    return jnp.where(ok[None], bias, -1e30).transpose(0, 2, 1)


def _split_bf16(x):
    hi = x.astype(BF16)
    lo = (x - hi.astype(F32)).astype(BF16)
    return hi, lo


RW_GROUP = 256


def _rw_kernel(rows, xr_ref, xk_ref, xv_ref, xl_ref, w0_ref, wup_ref, a0_ref, aup_ref, gup_ref, kk_ref, ka_ref,
               rk_ref, lnw_ref, lnb_ref, bd_ref, bdf_ref, o_ref, s_ref):
    L = CHUNK
    W = RW_GROUP
    reps = W // RW_HEAD

    @pl.when(pl.program_id(1) == 0)
    def _():
        s_ref[...] = jnp.zeros(s_ref.shape, F32)

    bdm = bd_ref[...]

    def bd(x):
        return jnp.concatenate([x.astype(BF16)] * reps, axis=0) * bdm

    def head_sums(xs):
        out = _dot(jnp.concatenate([x.astype(BF16) for x in xs], axis=0), bdm)
        offs = [0]
        for x in xs:
            offs.append(offs[-1] + x.shape[0])
        return [out[lo:hi] for lo, hi in zip(offs[:-1], offs[1:])]

    row = lax.broadcasted_iota(jnp.int32, (L, W), 0)
    sub = lax.broadcasted_iota(jnp.int32, (L, W), 1) % RW_HEAD
    strict = sub < row
    incl = sub <= row
    eye = (sub == row).astype(F32)
    tri = (lax.broadcasted_iota(jnp.int32, (L, L), 1)
           <= lax.broadcasted_iota(jnp.int32, (L, L), 0)).astype(BF16)
    inv = 1.0 / RW_HEAD

    def lora_inputs(b):
        xl = xl_ref[b].astype(F32)
        return (jnp.tanh(xl[:, :RW_DECAY_LORA]).astype(BF16),
                xl[:, RW_DECAY_LORA:RW_DECAY_LORA + RW_AAA_LORA].astype(BF16),
                jax.nn.sigmoid(xl[:, RW_DECAY_LORA + RW_AAA_LORA:]).astype(BF16))

    def instance(b, gi, xw, xa, xg):
        sl = slice(gi * W, (gi + 1) * W)
        r = xr_ref[b, :, sl].astype(F32)
        k = xk_ref[b, :, sl].astype(F32)
        v = xv_ref[b, :, sl].astype(F32)
        lw = -math.exp(-0.5) * jax.nn.sigmoid(w0_ref[:, sl] + _dot(xw, wup_ref[:, sl]))
        a = jax.nn.sigmoid(a0_ref[:, sl] + _dot(xa, aup_ref[:, sl]))
        g = _dot(xg, gup_ref[:, sl])
        kk = k * kk_ref[:, sl]
        ss = yield kk * kk
        kk = kk * lax.rsqrt(jnp.maximum(ss, 1e-24))
        k = k * (1.0 + (a - 1.0) * ka_ref[:, sl])
        be = kk * a
        lw_hi, lw_lo = _split_bf16(lw)
        cum = _dot(tri, lw_hi) + _dot(tri, lw_lo)
        yield
        cum_l = cum[L - 1:L, :]
        e_neg = jnp.exp(-cum)
        e_tail = jnp.exp(cum_l - cum)
        ar = jnp.concatenate([-kk * jnp.exp(cum - lw), r * jnp.exp(cum)], axis=0).astype(BF16)
        s0 = s_ref[b, gi]
        ar_s = _dot_nt(ar, s0.astype(BF16))
        a_b = _dot_nt(ar, bd(be * e_neg))
        a_k = _dot_nt(ar, bd(k * e_neg))
        yield
        n = jnp.where(strict, a_b[:L], 0.0)
        a_ak = jnp.where(strict, a_k[:L], 0.0)
        a_rb = jnp.where(incl, a_b[L:], 0.0)
        a_rk = jnp.where(incl, a_k[L:], 0.0)
        p = eye + n
        nk = _dot(n.astype(BF16), bd(n))
        bd_v = bd(v)
        x0 = _dot(a_ak.astype(BF16), bd_v)
        yield
        steps = int(math.log2(L)) - 1
        for it in range(steps):
            m = bd(nk)
            if it + 1 < steps:
                res = _dot(jnp.concatenate([p, nk], axis=0).astype(BF16), m)
                p = p + res[:L]
                nk = res[L:]
            else:
                p = p + _dot(p.astype(BF16), m)
            yield
        u = _dot(p.astype(BF16), bd(ar_s[:L] + x0))
        yield
        y = ar_s[L:] + _dot(jnp.concatenate([a_rb, a_rk], axis=1).astype(BF16),
                            jnp.concatenate([bd(u), bd_v], axis=0))
        uv = jnp.concatenate([u, v], axis=0).astype(BF16)
        bk = jnp.concatenate([be * e_tail, k * e_tail], axis=0).astype(BF16)
        s_ref[b, gi] = s0 * jnp.exp(cum_l) + _dot_tn(uv, bk) * bdf_ref[...]
        yield
        sums = yield jnp.concatenate([y, r * k * rk_ref[:, sl]], axis=0)
        d = y - sums[:L] * inv
        var = (yield d * d) * inv
        yn = d * lax.rsqrt(var + RW_LN_EPS) * lnw_ref[:, sl] + lnb_ref[:, sl]
        o_ref[b, :, sl] = ((yn + sums[L:] * v) * g).astype(o_ref.dtype)

    live = []
    for b in range(rows):
        lora = lora_inputs(b)
        live += [instance(b, gi, *lora) for gi in range(RW_DIM // W)]
    asked = [next(g) for g in live]
    while live:
        answers = head_sums(asked) if asked[0] is not None else asked
        live, asked = _advance(live, answers)


def _rw_mixer(z, b, consts):
    t = z.shape[0]
    s = t // b
    rows = math.gcd(b, 8)
    gid = jnp.arange(RW_GROUP) // RW_HEAD
    ones_f32 = (gid[:, None] == gid[None, :]).astype(F32)
    ones_bd = ones_f32.astype(BF16)
    sec = lambda w, j: pl.BlockSpec((rows, CHUNK, w), lambda bb, c: (bb, c, j))
    z3 = z.reshape(b, s, -1)
    return pl.pallas_call(
        functools.partial(_rw_kernel, rows),
        grid=(b // rows, s // CHUNK),
        in_specs=[sec(RW_DIM, OFF_RW // RW_DIM), sec(RW_DIM, OFF_RW // RW_DIM + 1),
                  sec(RW_DIM, OFF_RW // RW_DIM + 2), sec(RW_LORA, OFF_LORA // RW_LORA)]
        + [_pspec(c) for c in consts] + [_const_spec(ones_bd.shape)] * 2,
        out_specs=sec(RW_DIM, 0),
        out_shape=jax.ShapeDtypeStruct((b, s, RW_DIM), BF16),
        scratch_shapes=[pltpu.VMEM((rows, RW_DIM // RW_GROUP, RW_GROUP, RW_GROUP), F32)],
        compiler_params=_cparams("parallel", "arbitrary"),
        name="rw_mixer",
    )(z3, z3, z3, z3, *_pargs(consts), ones_bd, ones_f32)


FF_SPLIT = 4


def _post_kernel(om_ref, or_ref, oc_ref, g0_ref, g1_ref, g2_ref, h_ref, p_ref, wb_ref, wo_ref, gm_ref,
                 g1n_ref, w1_ref, w2_ref, g2n_ref, wg_ref, wp_ref, o_ref):
    merged = None
    for n, (b_ref, zg_ref) in enumerate(((om_ref, g0_ref), (or_ref, g1_ref), (oc_ref, g2_ref))):
        y = _dot(b_ref[...], wb_ref[n * BRANCH_DIM:(n + 1) * BRANCH_DIM, :])
        gate = jax.nn.sigmoid(zg_ref[...].astype(F32))
        merged = gate * y if merged is None else merged + gate * y
    h = h_ref[...] + _rms(_dot(merged.astype(BF16), wo_ref[...]), gm_ref[...])
    f = _rms(h, g1n_ref[...]).astype(BF16)
    cw = D_FF // FF_SPLIT
    acc = None
    for c in range(FF_SPLIT):
        a = jnp.maximum(_dot(f, w1_ref[:, c * cw:(c + 1) * cw]), 0.0)
        part = _dot((a * a).astype(BF16), w2_ref[c * cw:(c + 1) * cw, :])
        acc = part if acc is None else acc + part
    h = h + _rms(acc, g2n_ref[...])
    gate = jax.nn.sigmoid(_dot(h.astype(BF16), wg_ref[...]))
    o_ref[...] = h + gate * _dot(p_ref[...].astype(BF16), wp_ref[...])


def _post(o_mla, o_rw, o_ca, z, h, p, consts):
    t = h.shape[0]
    tm = min(o_rw.shape[1], 512)
    per_seq = o_rw.shape[1] // tm
    row = lambda w: pl.BlockSpec((tm, w), lambda i: (i, 0))
    gate = lambda n: pl.BlockSpec((tm, D_MODEL), lambda i: (i, OFF_GATE // D_MODEL + n))
    p_stack, layer = p
    return pl.pallas_call(
        _post_kernel,
        grid=(t // tm,),
        in_specs=[row(BRANCH_DIM), pl.BlockSpec((None, tm, BRANCH_DIM), lambda i: (i // per_seq, i % per_seq, 0)),
                  row(BRANCH_DIM), gate(0), gate(1), gate(2), row(D_MODEL),
                  pl.BlockSpec((None, tm, D_PLE), lambda i: (layer, i, 0))] + [_pspec(c) for c in consts],
        out_specs=row(D_MODEL),
        out_shape=jax.ShapeDtypeStruct((t, D_MODEL), F32),
        compiler_params=_cparams("parallel"),
        name="post",
    )(o_mla, o_rw, o_ca, z, z, z, h, p_stack, *_pargs(consts))


def _rot_half_cols(w):
    half = w.shape[-1] // 2
    return jnp.concatenate([-w[..., half:], w[..., :half]], axis=-1)


def _pack_w_in(w):
    mla_cols = MLA_Q_RANK + MLA_KV_RANK + MLA_ROPE
    rw_cols = 3 * RW_DIM + RW_LORA
    ca_q = (w[..., mla_cols + rw_cols:mla_cols + rw_cols + CA_DIM] * (CA_HEAD ** -0.5 * LOG2E)).astype(BF16)
    w = w.astype(BF16)
    w_mla = w[..., :mla_cols]
    w_rw = w[..., mla_cols:mla_cols + rw_cols]
    w_ca = w[..., mla_cols + rw_cols:mla_cols + rw_cols + 3 * CA_DIM]
    w_gate = w[..., mla_cols + rw_cols + 3 * CA_DIM:]
    w_kr = w_mla[..., MLA_Q_RANK + MLA_KV_RANK:]
    z64 = jnp.zeros(w.shape[:-1] + (64,), w.dtype)
    packed = jnp.concatenate(
        [w_gate, w_ca[..., CA_DIM:2 * CA_DIM], w_rw, w_mla[..., :MLA_Q_RANK + MLA_KV_RANK], w_kr, z64,
         _rot_half_cols(w_kr), z64], axis=-1)
    wt = jnp.concatenate([ca_q, w_ca[..., 2 * CA_DIM:]], axis=-1)
    return packed, jnp.swapaxes(wt, 1, 2)


def _pack_w_uq(w):
    nl, r, _ = w.shape
    w = w.reshape(nl, r, MLA_HEADS, MLA_QK)
    z64 = jnp.zeros((nl, r, MLA_HEADS, 64), w.dtype)
    rot = _rot_half_cols(w[..., MLA_NOPE:])
    wq = jnp.concatenate([w, z64], axis=-1).reshape(nl, r, MLA_HEADS * MLA_HEAD_PAD)
    wqr = jnp.concatenate([rot, z64], axis=-1).reshape(nl, r, MLA_HEADS * 128)
    return jnp.swapaxes(wq, 1, 2).astype(BF16), jnp.swapaxes(wqr, 1, 2).astype(BF16)


def _pack_w_ukv(w):
    nl, r, _ = w.shape
    w = w.reshape(nl, r, MLA_HEADS, MLA_NOPE + MLA_V)
    wk = w[..., :MLA_NOPE].reshape(nl, r, MLA_HEADS * MLA_NOPE)
    wv = w[..., MLA_NOPE:].reshape(nl, r, MLA_HEADS * MLA_V)
    return wk.astype(BF16), jnp.swapaxes(wv, 1, 2).astype(BF16)


def kernel(x, p, positions, pre_mix_g, w_in, mla_q_norm_g, mla_kv_norm_g, mla_w_uq, mla_w_ukv, rw_mu, rw_w0, rw_w_up, rw_a0, rw_a_up, rw_g_up, rw_k_k, rw_k_a, rw_r_k, rw_ln_w, rw_ln_b, ca_rel_bias, w_branch, w_out, post_mix_g, pre_ff_g, w_ff1, w_ff2, post_ff_g, w_ple_gate, w_ple_proj):
    b, s, d = x.shape
    t = b * s
    depth = w_in.shape[0]
    row = lambda a: a.reshape(depth, 1, -1).astype(F32)
    bf = lambda a: a.astype(BF16)
    w_in_p, w_in_t = _pack_w_in(w_in)
    wqt, wqrt = _pack_w_uq(mla_w_uq)
    wk, wvt = _pack_w_ukv(mla_w_ukv)
    stacks = dict(
        in_proj=[row(pre_mix_g), w_in_p, w_in_t, row(rw_mu),
                 row(mla_q_norm_g), row(mla_kv_norm_g), wqt, wqrt, wk, wvt],
        rw=[row(rw_w0), bf(rw_w_up), row(rw_a0), bf(rw_a_up), bf(rw_g_up), row(rw_k_k), row(rw_k_a),
            row(rw_r_k), row(rw_ln_w), row(rw_ln_b)],
        post=[bf(w_branch), bf(w_out), row(post_mix_g),
              row(pre_ff_g), bf(w_ff1), bf(w_ff2), row(post_ff_g), bf(w_ple_gate), bf(w_ple_proj)],
    )
    p3 = p.reshape(depth, t, -1)
    rope = _rope_table(positions)
    h = x.reshape(t, d)
    for i in range(depth):
        prm = {name: [(a, i) for a in arrs] for name, arrs in stacks.items()}
        z, zt, qt, kn, kr, vt = _in_proj(h, s, rope, prm["in_proj"])
        o_mla = _mla_attn(qt, kn, kr, vt, b)
        o_rw = _rw_mixer(z, b, prm["rw"])
        o_ca = _ca_attn(zt, z, _ca_bias_table(ca_rel_bias[i]), b)
        h = _post(o_mla, o_rw, o_ca, z, h, (p3, i), prm["post"])
    return h.reshape(b, s, d)
```

```python
import functools
import math

import jax
import jax.numpy as jnp
from jax import lax
from jax.experimental import pallas as pl
from jax.experimental.pallas import tpu as pltpu

F32 = jnp.float32
BF16 = jnp.bfloat16

D_MODEL = 1024
D_PLE = 256
D_FF = 4 * D_MODEL
NORM_EPS = 1e-6
CHUNK = 64

MLA_HEADS = 4
MLA_NOPE = 128
MLA_ROPE = 64
MLA_V = 128
MLA_Q_RANK = 256
MLA_KV_RANK = 128
ROPE_THETA = 10000.0
MLA_QK = MLA_NOPE + MLA_ROPE
MLA_HEAD_PAD = 256

RW_HEADS = 8
RW_HEAD = 64
RW_DIM = RW_HEADS * RW_HEAD
RW_DECAY_LORA = 64
RW_AAA_LORA = 64
RW_GATE_LORA = 128
RW_LORA = RW_DECAY_LORA + RW_AAA_LORA + RW_GATE_LORA
RW_LN_EPS = 64e-5

CA_HEADS = 8
CA_HEAD = 64
CA_DIM = CA_HEADS * CA_HEAD
CA_LEFT_CHUNKS = 8
CA_PAD = CA_LEFT_CHUNKS * CHUNK
REL_MIN = -(CHUNK - 1)
REL_MAX = 256

N_BRANCH = 3
BRANCH_DIM = 512
GATE_COLS = N_BRANCH * D_MODEL

OFF_GATE = 0
OFF_CAK = GATE_COLS
OFF_RW = OFF_CAK + CA_DIM
OFF_LORA = OFF_RW + 3 * RW_DIM
OFF_ZQ = OFF_LORA + RW_LORA
OFF_ZKV = OFF_ZQ + MLA_Q_RANK
OFF_KR = OFF_ZKV + MLA_KV_RANK
OFF_KRR = OFF_KR + 128
IN_PACKED = OFF_KRR + 128
LOG2E = math.log2(math.e)

VMEM_LIMIT = 56 * 1024 * 1024


def _cparams(*sem):
    return pltpu.CompilerParams(dimension_semantics=sem, vmem_limit_bytes=VMEM_LIMIT)


def _rms(x, g):
    return x * lax.rsqrt(jnp.mean(x * x, axis=-1, keepdims=True) + NORM_EPS) * g


def _dot(a, b):
    return jnp.dot(a, b, preferred_element_type=F32)


def _dot_nt(a, b):
    return lax.dot_general(a, b, (((1,), (1,)), ((), ())), preferred_element_type=F32)


def _dot_tn(a, b):
    return lax.dot_general(a, b, (((0,), (0,)), ((), ())), preferred_element_type=F32)


def _const_spec(shape):
    nd = len(shape)
    return pl.BlockSpec(shape, lambda *_: (0,) * nd, pipeline_mode=pl.Buffered(1))


def _pshape(param):
    return param[0].shape[1:]


def _pspec(param):
    stack, layer = param
    nd = stack.ndim - 1
    return pl.BlockSpec((None,) + stack.shape[1:], lambda *_: (layer,) + (0,) * nd, pipeline_mode=pl.Buffered(1))


def _pargs(params):
    return [stack for stack, _ in params]


def _advance(gens, answers):
    live, asked = [], []
    for g, ans in zip(gens, answers):
        try:
            asked.append(g.send(ans))
            live.append(g)
        except StopIteration:
            pass
    return live, asked


PIPE_DEPTH = 2
ONES_ROWS = 16


def _pipelined(items, produce, consume):
    pending, outs = [], []
    for it in items:
        pending.append((it, produce(it)))
        if len(pending) > PIPE_DEPTH:
            outs.append(consume(*pending.pop(0)))
    outs.extend(consume(*pc) for pc in pending)
    return outs


def _rope_table_kernel(pos_ref, freq_ref, cs_ref, sn_ref, cst_ref, snt_ref):
    ang = freq_ref[...] * pos_ref[...]
    dead = jnp.zeros((128 - MLA_ROPE, ang.shape[1]), F32)
    cst = jnp.concatenate([jnp.cos(ang), dead], axis=0)
    snt = jnp.concatenate([jnp.sin(ang), dead], axis=0)
    cst_ref[...] = cst
    snt_ref[...] = snt
    cs_ref[...] = cst.T
    sn_ref[...] = snt.T


def _rope_table(positions):
    t = positions.size
    tm = min(t, 2048)
    half = MLA_ROPE // 2
    inv_freq = 1.0 / (ROPE_THETA ** (jnp.arange(half, dtype=F32) / half))
    freq = jnp.concatenate([inv_freq, inv_freq])[:, None]
    return pl.pallas_call(
        _rope_table_kernel,
        grid=(t // tm,),
        in_specs=[pl.BlockSpec((1, tm), lambda i: (0, i)), _const_spec(freq.shape)],
        out_specs=[pl.BlockSpec((tm, 128), lambda i: (i, 0))] * 2 + [pl.BlockSpec((128, tm), lambda i: (0, i))] * 2,
        out_shape=[jax.ShapeDtypeStruct((t, 128), F32)] * 2 + [jax.ShapeDtypeStruct((128, t), F32)] * 2,
        compiler_params=_cparams("parallel"),
        name="rope_table",
    )(positions.astype(F32).reshape(1, t), freq)


IN_RANGES = ((0, 1536), (1536, OFF_CAK), (OFF_CAK, OFF_RW), (OFF_RW, OFF_RW + 896), (OFF_RW + 896, OFF_ZQ),
             (OFF_ZQ, IN_PACKED))


def _mla_project(lat, cs, sn, cst, snt, gq_ref, gkv_ref, wqt_ref, wqrt_ref, wk_ref, wvt_ref):
    qn = _rms(lat[:, :MLA_Q_RANK], gq_ref[...]).astype(BF16)
    qt = _dot_nt(wqt_ref[...], qn)
    qrt = _dot_nt(wqrt_ref[...], qn)
    scale = MLA_QK ** -0.5 * LOG2E
    parts = []
    for h in range(MLA_HEADS):
        o = h * MLA_HEAD_PAD
        rope = qt[o + 128:o + 256] * cst + qrt[h * 128:(h + 1) * 128] * snt
        parts += [(qt[o:o + 128] * scale).astype(BF16), (rope * scale).astype(BF16)]
    kvn = _rms(lat[:, MLA_Q_RANK:MLA_Q_RANK + MLA_KV_RANK], gkv_ref[...]).astype(BF16)
    k = _dot(kvn, wk_ref[...]).astype(BF16)
    vt = _dot_nt(wvt_ref[...], kvn).astype(BF16)
    kr = (lat[:, OFF_KR - OFF_ZQ:OFF_KRR - OFF_ZQ] * cs + lat[:, OFF_KRR - OFF_ZQ:] * sn).astype(BF16)
    return jnp.concatenate(parts, axis=0), k, kr, vt


def _in_proj_kernel(seq_len, x_ref, cs_ref, sn_ref, cst_ref, snt_ref, g_ref, w_ref, wt_ref, mu_ref, gq_ref, gkv_ref,
                    wqt_ref, wqrt_ref, wk_ref, wvt_ref, o_ref, ot_ref, qt_ref, k_ref, kr_ref, vt_ref, carry_ref):
    tm = x_ref.shape[0]
    half = tm // 2
    first = (pl.program_id(0) * tm) % seq_len == 0
    row0 = lax.broadcasted_iota(jnp.int32, (half, 1), 0) == 0
    edge = {}

    def rows(idx):
        rs = slice(idx * half, (idx + 1) * half)
        xn = _rms(x_ref[rs, :], g_ref[...]).astype(BF16)
        for c, e in IN_RANGES:
            res = _dot(xn, w_ref[:, c:e])
            yield
            if c >= OFF_ZQ:
                qt, k, kr, vt = _mla_project(res, cs_ref[rs, :], sn_ref[rs, :], cst_ref[:, rs], snt_ref[:, rs],
                                             gq_ref, gkv_ref, wqt_ref, wqrt_ref, wk_ref, wvt_ref)
                qt_ref[:, rs] = qt
                k_ref[rs, :] = k
                kr_ref[rs, :] = kr
                vt_ref[:, rs] = vt
                continue
            if c >= OFF_RW:
                cs = slice(c - OFF_RW, e - OFF_RW)
                if idx == 0:
                    last = jnp.where(first, 0.0, carry_ref[:, cs])
                    edge[c] = res[half - 1:half, :]
                else:
                    last = edge[c]
                    carry_ref[:, cs] = res[half - 1:half, :]
                prev = jnp.where(row0, last, pltpu.roll(res, 1, axis=0))
                res = res + (prev - res) * mu_ref[:, cs]
            o_ref[rs, c:e] = res.astype(o_ref.dtype)
        ot_ref[:, rs] = _dot_nt(wt_ref[...], xn).astype(ot_ref.dtype)

    live = [rows(0), rows(1)]
    while live:
        live = [g for g in live if next(g, True) is None]


def _in_proj(h, seq_len, tables, consts):
    t, d = h.shape
    nt = _pshape(consts[2])[0]
    tm = min(t, 512)
    hv = MLA_HEADS * MLA_V
    hq = MLA_HEADS * MLA_HEAD_PAD
    row = lambda w: pl.BlockSpec((tm, w), lambda i: (i, 0))
    col = lambda w: pl.BlockSpec((w, tm), lambda i: (0, i))
    return pl.pallas_call(
        functools.partial(_in_proj_kernel, seq_len),
        grid=(t // tm,),
        in_specs=[row(d), row(128), row(128), col(128), col(128)] + [_pspec(c) for c in consts],
        out_specs=[row(OFF_ZQ), col(nt), col(hq), row(hv), row(128), col(hv)],
        out_shape=[
            jax.ShapeDtypeStruct((t, OFF_ZQ), BF16),
            jax.ShapeDtypeStruct((nt, t), BF16),
            jax.ShapeDtypeStruct((hq, t), BF16),
            jax.ShapeDtypeStruct((t, hv), BF16),
            jax.ShapeDtypeStruct((t, 128), BF16),
            jax.ShapeDtypeStruct((hv, t), BF16),
        ],
        scratch_shapes=[pltpu.VMEM((1, OFF_ZQ - OFF_RW), F32)],
        compiler_params=_cparams("arbitrary"),
        name="in_proj",
    )(h, *tables, *_pargs(consts))


MLA_TQ = 512


def _mla_attn_kernel(qt_ref, k_ref, kr_ref, vt_ref, o_ref, m_ref, acc_ref):
    i = pl.program_id(1)
    tq = MLA_TQ
    m_ref[...] = jnp.full(m_ref.shape, -1e30, F32)
    acc_ref[...] = jnp.zeros(acc_ref.shape, F32)
    ones = jnp.ones((ONES_ROWS, tq), BF16)

    def run(blocks):
        def scores(j, masked, h):
            start = pl.multiple_of(j * tq, tq)
            kh = jnp.concatenate([k_ref[pl.ds(start, tq), h * 128:(h + 1) * 128],
                                  kr_ref[pl.ds(start, tq), :]], axis=1)
            s = _dot(kh, qt_ref[h * MLA_HEAD_PAD:(h + 1) * MLA_HEAD_PAD, :])
            if masked:
                kc = lax.broadcasted_iota(jnp.int32, s.shape, 0) // CHUNK
                qc = lax.broadcasted_iota(jnp.int32, s.shape, 1) // CHUNK
                s = jnp.where(kc <= qc, s, -1e30)
            return s

        def update(j, h, s):
            start = pl.multiple_of(j * tq, tq)
            m_prev = m_ref[h]
            m_new = jnp.maximum(m_prev, jnp.max(s, axis=0, keepdims=True))
            a = jnp.exp2(m_prev - m_new)
            p = jnp.exp2(s - m_new).astype(BF16)
            vh = jnp.concatenate([vt_ref[h * MLA_V:(h + 1) * MLA_V, pl.ds(start, tq)], ones], axis=0)
            acc_ref[h] = a * acc_ref[h] + _dot(vh, p)
            m_ref[h] = m_new

        items = [(j, masked, h) for j, masked in blocks for h in range(MLA_HEADS)]
        _pipelined(items, lambda it: scores(*it), lambda it, s: update(it[0], it[2], s))

    def body(jj, c):
        run([(2 * jj, False), (2 * jj + 1, False)])
        return c

    lax.fori_loop(0, i // 2, body, 0)
    pl.when(i % 2 == 1)(lambda: run([(i - 1, False), (i, True)]))
    pl.when(i % 2 == 0)(lambda: run([(i, True)]))
    for h in range(MLA_HEADS):
        acc = acc_ref[h]
        o_ref[:, h * MLA_V:(h + 1) * MLA_V] = (acc[:MLA_V] / acc[MLA_V:MLA_V + 1]).T.astype(o_ref.dtype)


def _mla_attn(qt, k, kr, vt, b):
    hq, t = qt.shape
    s = t // b
    tq = MLA_TQ
    nq = s // tq
    hv = MLA_HEADS * MLA_V
    full = lambda w: pl.BlockSpec((s, w), lambda bi, i: (bi, 0))
    return pl.pallas_call(
        _mla_attn_kernel,
        grid=(b, nq),
        in_specs=[
            pl.BlockSpec((hq, tq), lambda bi, i: (0, bi * nq + i)),
            full(hv), full(128),
            pl.BlockSpec((hv, s), lambda bi, i: (0, bi)),
        ],
        out_specs=pl.BlockSpec((tq, hv), lambda bi, i: (bi * nq + i, 0)),
        out_shape=jax.ShapeDtypeStruct((t, hv), BF16),
        scratch_shapes=[pltpu.VMEM((MLA_HEADS, 1, tq), F32),
                        pltpu.VMEM((MLA_HEADS, MLA_V + ONES_ROWS, tq), F32)],
        compiler_params=_cparams("parallel", "arbitrary"),
        name="mla_attn",
    )(qt, k, kr, vt)


CA_TQ = 256
CA_BAND_BLK = CA_TQ + CA_PAD


def _ca_attn_kernel(qt_ref, k_ref, vt_ref, bias_ref, o_ref):
    i = pl.program_id(1)
    lo = lax.broadcasted_iota(jnp.int32, (128, CA_TQ), 0) < CA_HEAD
    zero = jnp.zeros((), BF16)

    def block(k_start, n_keys, bias_off):
        def scores(h):
            sl = slice((h // 2) * 128, (h // 2 + 1) * 128)
            k2 = k_ref[pl.ds(k_start, n_keys), sl]
            qh = jnp.where(lo if h % 2 == 0 else ~lo, qt_ref[sl, :], zero)
            return _dot(k2, qh) + bias_ref[h, bias_off:bias_off + n_keys, :]

        def attend(h, s):
            p = jnp.exp2(s - jnp.max(s, axis=0, keepdims=True)).astype(BF16)
            vh = jnp.concatenate([vt_ref[h * CA_HEAD:(h + 1) * CA_HEAD, pl.ds(k_start, n_keys)],
                                  jnp.ones((ONES_ROWS, n_keys), BF16)], axis=0)
            out = _dot(vh, p)
            return out[:CA_HEAD] / out[CA_HEAD:CA_HEAD + 1]

        outs = _pipelined(list(range(CA_HEADS)), scores, attend)
        for pair in range(CA_HEADS // 2):
            both = jnp.concatenate(outs[2 * pair:2 * pair + 2], axis=0)
            o_ref[:, pair * 128:(pair + 1) * 128] = both.T.astype(o_ref.dtype)

    lead = CA_PAD // CA_TQ
    for j in range(lead):
        pl.when(i == j)(functools.partial(block, 0, (j + 1) * CA_TQ, CA_PAD - j * CA_TQ))
    pl.when(i >= lead)(lambda: block(pl.multiple_of(i * CA_TQ - CA_PAD, CA_TQ), CA_BAND_BLK, 0))


def _ca_attn(zt, z, bias, b):
    t = zt.shape[1]
    s = t // b
    nq = s // CA_TQ
    return pl.pallas_call(
        _ca_attn_kernel,
        grid=(b, nq),
        in_specs=[
            pl.BlockSpec((CA_DIM, CA_TQ), lambda bi, i: (0, bi * nq + i)),
            pl.BlockSpec((s, CA_DIM), lambda bi, i: (bi, OFF_CAK // CA_DIM)),
            pl.BlockSpec((CA_DIM, s), lambda bi, i: (1, bi)),
            _const_spec(bias.shape),
        ],
        out_specs=pl.BlockSpec((CA_TQ, CA_DIM), lambda bi, i: (bi * nq + i, 0)),
        out_shape=jax.ShapeDtypeStruct((t, CA_DIM), BF16),
        compiler_params=_cparams("parallel", "arbitrary"),
        name="ca_attn",
    )(zt, z, zt, bias)


def _ca_bias_table(rel_bias):
    width = CA_TQ + CA_BAND_BLK
    m = jnp.arange(width)
    delta = jnp.where(m < CA_BAND_BLK, m, m - width)
    idx = jnp.clip(CA_PAD - delta, REL_MIN, REL_MAX) - REL_MIN
    t1 = rel_bias.astype(F32)[idx].T * LOG2E
    flat = jnp.tile(t1, (1, CA_TQ))[:, :CA_TQ * (width - 1)]
    bias = flat.reshape(CA_HEADS, CA_TQ, width - 1)[:, :, :CA_BAND_BLK]
    rc = jnp.arange(CA_TQ)[:, None] // CHUNK
    cc = jnp.arange(CA_BAND_BLK)[None, :] // CHUNK
    ok = (cc >= rc) & (cc <= rc + CA_LEFT_CHUNKS)
    return jnp.where(ok[None], bias, -1e30).transpose(0, 2, 1)


def _split_bf16(x):
    hi = x.astype(BF16)
    lo = (x - hi.astype(F32)).astype(BF16)
    return hi, lo


RW_GROUP = 256


def _rw_kernel(rows, xr_ref, xk_ref, xv_ref, xl_ref, w0_ref, wup_ref, a0_ref, aup_ref, gup_ref, kk_ref, ka_ref,
               rk_ref, lnw_ref, lnb_ref, bd_ref, bdf_ref, o_ref, s_ref):
    L = CHUNK
    W = RW_GROUP
    reps = W // RW_HEAD

    @pl.when(pl.program_id(1) == 0)
    def _():
        s_ref[...] = jnp.zeros(s_ref.shape, F32)

    bdm = bd_ref[...]

    def bd(x):
        return jnp.concatenate([x.astype(BF16)] * reps, axis=0) * bdm

    def head_sums(xs):
        out = _dot(jnp.concatenate([x.astype(BF16) for x in xs], axis=0), bdm)
        offs = [0]
        for x in xs:
            offs.append(offs[-1] + x.shape[0])
        return [out[lo:hi] for lo, hi in zip(offs[:-1], offs[1:])]

    row = lax.broadcasted_iota(jnp.int32, (L, W), 0)
    sub = lax.broadcasted_iota(jnp.int32, (L, W), 1) % RW_HEAD
    strict = sub < row
    incl = sub <= row
    eye = (sub == row).astype(F32)
    tri = (lax.broadcasted_iota(jnp.int32, (L, L), 1)
           <= lax.broadcasted_iota(jnp.int32, (L, L), 0)).astype(BF16)
    inv = 1.0 / RW_HEAD

    def lora_inputs(b):
        xl = xl_ref[b].astype(F32)
        return (jnp.tanh(xl[:, :RW_DECAY_LORA]).astype(BF16),
                xl[:, RW_DECAY_LORA:RW_DECAY_LORA + RW_AAA_LORA].astype(BF16),
                jax.nn.sigmoid(xl[:, RW_DECAY_LORA + RW_AAA_LORA:]).astype(BF16))

    def instance(b, gi, xw, xa, xg):
        sl = slice(gi * W, (gi + 1) * W)
        r = xr_ref[b, :, sl].astype(F32)
        k = xk_ref[b, :, sl].astype(F32)
        v = xv_ref[b, :, sl].astype(F32)
        lw = -math.exp(-0.5) * jax.nn.sigmoid(w0_ref[:, sl] + _dot(xw, wup_ref[:, sl]))
        a = jax.nn.sigmoid(a0_ref[:, sl] + _dot(xa, aup_ref[:, sl]))
        g = _dot(xg, gup_ref[:, sl])
        kk = k * kk_ref[:, sl]
        ss = yield kk * kk
        kk = kk * lax.rsqrt(jnp.maximum(ss, 1e-24))
        k = k * (1.0 + (a - 1.0) * ka_ref[:, sl])
        be = kk * a
        lw_hi, lw_lo = _split_bf16(lw)
        cum = _dot(tri, lw_hi) + _dot(tri, lw_lo)
        yield
        cum_l = cum[L - 1:L, :]
        e_neg = jnp.exp(-cum)
        e_tail = jnp.exp(cum_l - cum)
        ar = jnp.concatenate([-kk * jnp.exp(cum - lw), r * jnp.exp(cum)], axis=0).astype(BF16)
        s0 = s_ref[b, gi]
        ar_s = _dot_nt(ar, s0.astype(BF16))
        a_b = _dot_nt(ar, bd(be * e_neg))
        a_k = _dot_nt(ar, bd(k * e_neg))
        yield
        n = jnp.where(strict, a_b[:L], 0.0)
        a_ak = jnp.where(strict, a_k[:L], 0.0)
        a_rb = jnp.where(incl, a_b[L:], 0.0)
        a_rk = jnp.where(incl, a_k[L:], 0.0)
        p = eye + n
        nk = _dot(n.astype(BF16), bd(n))
        bd_v = bd(v)
        x0 = _dot(a_ak.astype(BF16), bd_v)
        yield
        steps = int(math.log2(L)) - 1
        for it in range(steps):
            m = bd(nk)
            if it + 1 < steps:
                res = _dot(jnp.concatenate([p, nk], axis=0).astype(BF16), m)
                p = p + res[:L]
                nk = res[L:]
            else:
                p = p + _dot(p.astype(BF16), m)
            yield
        u = _dot(p.astype(BF16), bd(ar_s[:L] + x0))
        yield
        y = ar_s[L:] + _dot(jnp.concatenate([a_rb, a_rk], axis=1).astype(BF16),
                            jnp.concatenate([bd(u), bd_v], axis=0))
        uv = jnp.concatenate([u, v], axis=0).astype(BF16)
        bk = jnp.concatenate([be * e_tail, k * e_tail], axis=0).astype(BF16)
        s_ref[b, gi] = s0 * jnp.exp(cum_l) + _dot_tn(uv, bk) * bdf_ref[...]
        yield
        sums = yield jnp.concatenate([y, r * k * rk_ref[:, sl]], axis=0)
        d = y - sums[:L] * inv
        var = (yield d * d) * inv
        yn = d * lax.rsqrt(var + RW_LN_EPS) * lnw_ref[:, sl] + lnb_ref[:, sl]
        o_ref[b, :, sl] = ((yn + sums[L:] * v) * g).astype(o_ref.dtype)

    live = []
    for b in range(rows):
        lora = lora_inputs(b)
        live += [instance(b, gi, *lora) for gi in range(RW_DIM // W)]
    asked = [next(g) for g in live]
    while live:
        answers = head_sums(asked) if asked[0] is not None else asked
        live, asked = _advance(live, answers)


def _rw_mixer(z, b, consts):
    t = z.shape[0]
    s = t // b
    rows = math.gcd(b, 8)
    gid = jnp.arange(RW_GROUP) // RW_HEAD
    ones_f32 = (gid[:, None] == gid[None, :]).astype(F32)
    ones_bd = ones_f32.astype(BF16)
    sec = lambda w, j: pl.BlockSpec((rows, CHUNK, w), lambda bb, c: (bb, c, j))
    z3 = z.reshape(b, s, -1)
    return pl.pallas_call(
        functools.partial(_rw_kernel, rows),
        grid=(b // rows, s // CHUNK),
        in_specs=[sec(RW_DIM, OFF_RW // RW_DIM), sec(RW_DIM, OFF_RW // RW_DIM + 1),
                  sec(RW_DIM, OFF_RW // RW_DIM + 2), sec(RW_LORA, OFF_LORA // RW_LORA)]
        + [_pspec(c) for c in consts] + [_const_spec(ones_bd.shape)] * 2,
        out_specs=sec(RW_DIM, 0),
        out_shape=jax.ShapeDtypeStruct((b, s, RW_DIM), BF16),
        scratch_shapes=[pltpu.VMEM((rows, RW_DIM // RW_GROUP, RW_GROUP, RW_GROUP), F32)],
        compiler_params=_cparams("parallel", "arbitrary"),
        name="rw_mixer",
    )(z3, z3, z3, z3, *_pargs(consts), ones_bd, ones_f32)


FF_SPLIT = 4


def _post_kernel(om_ref, or_ref, oc_ref, g0_ref, g1_ref, g2_ref, h_ref, p_ref, wb_ref, wo_ref, gm_ref,
                 g1n_ref, w1_ref, w2_ref, g2n_ref, wg_ref, wp_ref, o_ref):
    def rows(rs):
        merged = None
        for n, (b_ref, zg_ref) in enumerate(((om_ref, g0_ref), (or_ref, g1_ref), (oc_ref, g2_ref))):
            y = _dot(b_ref[rs, :], wb_ref[n * BRANCH_DIM:(n + 1) * BRANCH_DIM, :])
            yield
            gate = jax.nn.sigmoid(zg_ref[rs, :].astype(F32))
            merged = gate * y if merged is None else merged + gate * y
        out = _dot(merged.astype(BF16), wo_ref[...])
        yield
        h = h_ref[rs, :] + _rms(out, gm_ref[...])
        f = _rms(h, g1n_ref[...]).astype(BF16)
        cw = D_FF // FF_SPLIT
        acc = None
        for c in range(FF_SPLIT):
            a = _dot(f, w1_ref[:, c * cw:(c + 1) * cw])
            yield
            a = jnp.maximum(a, 0.0)
            part = _dot((a * a).astype(BF16), w2_ref[c * cw:(c + 1) * cw, :])
            yield
            acc = part if acc is None else acc + part
        h = h + _rms(acc, g2n_ref[...])
        gate = _dot(h.astype(BF16), wg_ref[...])
        proj = _dot(p_ref[rs, :].astype(BF16), wp_ref[...])
        yield
        o_ref[rs, :] = h + jax.nn.sigmoid(gate) * proj

    half = h_ref.shape[0] // 2
    live = [rows(slice(0, half)), rows(slice(half, 2 * half))]
    while live:
        live = [g for g in live if next(g, True) is None]


def _post(o_mla, o_rw, o_ca, z, h, p, consts):
    t = h.shape[0]
    tm = min(o_rw.shape[1], 512)
    per_seq = o_rw.shape[1] // tm
    row = lambda w: pl.BlockSpec((tm, w), lambda i: (i, 0))
    gate = lambda n: pl.BlockSpec((tm, D_MODEL), lambda i: (i, OFF_GATE // D_MODEL + n))
    p_stack, layer = p
    return pl.pallas_call(
        _post_kernel,
        grid=(t // tm,),
        in_specs=[row(BRANCH_DIM), pl.BlockSpec((None, tm, BRANCH_DIM), lambda i: (i // per_seq, i % per_seq, 0)),
                  row(BRANCH_DIM), gate(0), gate(1), gate(2), row(D_MODEL),
                  pl.BlockSpec((None, tm, D_PLE), lambda i: (layer, i, 0))] + [_pspec(c) for c in consts],
        out_specs=row(D_MODEL),
        out_shape=jax.ShapeDtypeStruct((t, D_MODEL), F32),
        compiler_params=_cparams("parallel"),
        name="post",
    )(o_mla, o_rw, o_ca, z, z, z, h, p_stack, *_pargs(consts))


def _rot_half_cols(w):
    half = w.shape[-1] // 2
    return jnp.concatenate([-w[..., half:], w[..., :half]], axis=-1)


def _pack_w_in(w):
    mla_cols = MLA_Q_RANK + MLA_KV_RANK + MLA_ROPE
    rw_cols = 3 * RW_DIM + RW_LORA
    ca_q = (w[..., mla_cols + rw_cols:mla_cols + rw_cols + CA_DIM] * (CA_HEAD ** -0.5 * LOG2E)).astype(BF16)
    w = w.astype(BF16)
    w_mla = w[..., :mla_cols]
    w_rw = w[..., mla_cols:mla_cols + rw_cols]
    w_ca = w[..., mla_cols + rw_cols:mla_cols + rw_cols + 3 * CA_DIM]
    w_gate = w[..., mla_cols + rw_cols + 3 * CA_DIM:]
    w_kr = w_mla[..., MLA_Q_RANK + MLA_KV_RANK:]
    z64 = jnp.zeros(w.shape[:-1] + (64,), w.dtype)
    packed = jnp.concatenate(
        [w_gate, w_ca[..., CA_DIM:2 * CA_DIM], w_rw, w_mla[..., :MLA_Q_RANK + MLA_KV_RANK], w_kr, z64,
         _rot_half_cols(w_kr), z64], axis=-1)
    wt = jnp.concatenate([ca_q, w_ca[..., 2 * CA_DIM:]], axis=-1)
    return packed, jnp.swapaxes(wt, 1, 2)


def _pack_w_uq(w):
    nl, r, _ = w.shape
    w = w.reshape(nl, r, MLA_HEADS, MLA_QK)
    z64 = jnp.zeros((nl, r, MLA_HEADS, 64), w.dtype)
    rot = _rot_half_cols(w[..., MLA_NOPE:])
    wq = jnp.concatenate([w, z64], axis=-1).reshape(nl, r, MLA_HEADS * MLA_HEAD_PAD)
    wqr = jnp.concatenate([rot, z64], axis=-1).reshape(nl, r, MLA_HEADS * 128)
    return jnp.swapaxes(wq, 1, 2).astype(BF16), jnp.swapaxes(wqr, 1, 2).astype(BF16)


def _pack_w_ukv(w):
    nl, r, _ = w.shape
    w = w.reshape(nl, r, MLA_HEADS, MLA_NOPE + MLA_V)
    wk = w[..., :MLA_NOPE].reshape(nl, r, MLA_HEADS * MLA_NOPE)
    wv = w[..., MLA_NOPE:].reshape(nl, r, MLA_HEADS * MLA_V)
    return wk.astype(BF16), jnp.swapaxes(wv, 1, 2).astype(BF16)


def kernel(x, p, positions, pre_mix_g, w_in, mla_q_norm_g, mla_kv_norm_g, mla_w_uq, mla_w_ukv, rw_mu, rw_w0, rw_w_up, rw_a0, rw_a_up, rw_g_up, rw_k_k, rw_k_a, rw_r_k, rw_ln_w, rw_ln_b, ca_rel_bias, w_branch, w_out, post_mix_g, pre_ff_g, w_ff1, w_ff2, post_ff_g, w_ple_gate, w_ple_proj):
    b, s, d = x.shape
    t = b * s
    depth = w_in.shape[0]
    row = lambda a: a.reshape(depth, 1, -1).astype(F32)
    bf = lambda a: a.astype(BF16)
    w_in_p, w_in_t = _pack_w_in(w_in)
    wqt, wqrt = _pack_w_uq(mla_w_uq)
    wk, wvt = _pack_w_ukv(mla_w_ukv)
    stacks = dict(
        in_proj=[row(pre_mix_g), w_in_p, w_in_t, row(rw_mu),
                 row(mla_q_norm_g), row(mla_kv_norm_g), wqt, wqrt, wk, wvt],
        rw=[row(rw_w0), bf(rw_w_up), row(rw_a0), bf(rw_a_up), bf(rw_g_up), row(rw_k_k), row(rw_k_a),
            row(rw_r_k), row(rw_ln_w), row(rw_ln_b)],
        post=[bf(w_branch), bf(w_out), row(post_mix_g),
              row(pre_ff_g), bf(w_ff1), bf(w_ff2), row(post_ff_g), bf(w_ple_gate), bf(w_ple_proj)],
    )
    p3 = p.reshape(depth, t, -1)
    rope = _rope_table(positions)
    h = x.reshape(t, d)
    for i in range(depth):
        prm = {name: [(a, i) for a in arrs] for name, arrs in stacks.items()}
        z, zt, qt, kn, kr, vt = _in_proj(h, s, rope, prm["in_proj"])
        o_mla = _mla_attn(qt, kn, kr, vt, b)
        o_rw = _rw_mixer(z, b, prm["rw"])
        o_ca = _ca_attn(zt, z, _ca_bias_table(ca_rel_bias[i]), b)
        h = _post(o_mla, o_rw, o_ca, z, h, (p3, i), prm["post"])
    return h.reshape(b, s, d)
```

```python
import functools
import math

import jax
import jax.numpy as jnp
from jax import lax
from jax.experimental import pallas as pl
from jax.experimental.pallas import tpu as pltpu

F32 = jnp.float32
BF16 = jnp.bfloat16

D_MODEL = 1024
D_PLE = 256
D_FF = 4 * D_MODEL
NORM_EPS = 1e-6
CHUNK = 64

MLA_HEADS = 4
MLA_NOPE = 128
MLA_ROPE = 64
MLA_V = 128
MLA_Q_RANK = 256
MLA_KV_RANK = 128
ROPE_THETA = 10000.0
MLA_QK = MLA_NOPE + MLA_ROPE
MLA_HEAD_PAD = 256

RW_HEADS = 8
RW_HEAD = 64
RW_DIM = RW_HEADS * RW_HEAD
RW_DECAY_LORA = 64
RW_AAA_LORA = 64
RW_GATE_LORA = 128
RW_LORA = RW_DECAY_LORA + RW_AAA_LORA + RW_GATE_LORA
RW_LN_EPS = 64e-5

CA_HEADS = 8
CA_HEAD = 64
CA_DIM = CA_HEADS * CA_HEAD
CA_LEFT_CHUNKS = 8
CA_PAD = CA_LEFT_CHUNKS * CHUNK
REL_MIN = -(CHUNK - 1)
REL_MAX = 256

N_BRANCH = 3
BRANCH_DIM = 512
GATE_COLS = N_BRANCH * D_MODEL

OFF_GATE = 0
OFF_CAK = GATE_COLS
OFF_RW = OFF_CAK + CA_DIM
OFF_LORA = OFF_RW + 3 * RW_DIM
OFF_ZQ = OFF_LORA + RW_LORA
OFF_ZKV = OFF_ZQ + MLA_Q_RANK
OFF_KR = OFF_ZKV + MLA_KV_RANK
OFF_KRR = OFF_KR + 128
IN_PACKED = OFF_KRR + 128
LOG2E = math.log2(math.e)

VMEM_LIMIT = 56 * 1024 * 1024


def _cparams(*sem):
    return pltpu.CompilerParams(dimension_semantics=sem, vmem_limit_bytes=VMEM_LIMIT)


def _rms(x, g):
    return x * lax.rsqrt(jnp.mean(x * x, axis=-1, keepdims=True) + NORM_EPS) * g


def _dot(a, b):
    return jnp.dot(a, b, preferred_element_type=F32)


def _dot_nt(a, b):
    return lax.dot_general(a, b, (((1,), (1,)), ((), ())), preferred_element_type=F32)


def _dot_tn(a, b):
    return lax.dot_general(a, b, (((0,), (0,)), ((), ())), preferred_element_type=F32)


def _const_spec(shape):
    nd = len(shape)
    return pl.BlockSpec(shape, lambda *_: (0,) * nd, pipeline_mode=pl.Buffered(1))


def _pshape(param):
    return param[0].shape[1:]


def _pspec(param):
    stack, layer = param
    nd = stack.ndim - 1
    return pl.BlockSpec((None,) + stack.shape[1:], lambda *_: (layer,) + (0,) * nd, pipeline_mode=pl.Buffered(1))


def _pargs(params):
    return [stack for stack, _ in params]


def _advance(gens, answers):
    live, asked = [], []
    for g, ans in zip(gens, answers):
        try:
            asked.append(g.send(ans))
            live.append(g)
        except StopIteration:
            pass
    return live, asked


PIPE_DEPTH = 2
ONES_ROWS = 16


def _pipelined(items, produce, consume):
    pending, outs = [], []
    for it in items:
        pending.append((it, produce(it)))
        if len(pending) > PIPE_DEPTH:
            outs.append(consume(*pending.pop(0)))
    outs.extend(consume(*pc) for pc in pending)
    return outs


def _rope_table_kernel(pos_ref, freq_ref, cs_ref, sn_ref, cst_ref, snt_ref):
    ang = freq_ref[...] * pos_ref[...]
    dead = jnp.zeros((128 - MLA_ROPE, ang.shape[1]), F32)
    cst = jnp.concatenate([jnp.cos(ang), dead], axis=0)
    snt = jnp.concatenate([jnp.sin(ang), dead], axis=0)
    cst_ref[...] = cst
    snt_ref[...] = snt
    cs_ref[...] = cst.T
    sn_ref[...] = snt.T


def _rope_table(positions):
    t = positions.size
    tm = min(t, 2048)
    half = MLA_ROPE // 2
    inv_freq = 1.0 / (ROPE_THETA ** (jnp.arange(half, dtype=F32) / half))
    freq = jnp.concatenate([inv_freq, inv_freq])[:, None]
    return pl.pallas_call(
        _rope_table_kernel,
        grid=(t // tm,),
        in_specs=[pl.BlockSpec((1, tm), lambda i: (0, i)), _const_spec(freq.shape)],
        out_specs=[pl.BlockSpec((tm, 128), lambda i: (i, 0))] * 2 + [pl.BlockSpec((128, tm), lambda i: (0, i))] * 2,
        out_shape=[jax.ShapeDtypeStruct((t, 128), F32)] * 2 + [jax.ShapeDtypeStruct((128, t), F32)] * 2,
        compiler_params=_cparams("parallel"),
        name="rope_table",
    )(positions.astype(F32).reshape(1, t), freq)


IN_RANGES = ((0, 1536), (1536, OFF_CAK), (OFF_CAK, OFF_RW), (OFF_RW, OFF_RW + 1024), (OFF_RW + 1024, OFF_ZQ),
             (OFF_ZQ, IN_PACKED))


def _mla_project(lat, cs, sn, cst, snt, gq_ref, gkv_ref, wqt_ref, wqrt_ref, wk_ref, wvt_ref):
    qn = _rms(lat[:, :MLA_Q_RANK], gq_ref[...]).astype(BF16)
    qt = _dot_nt(wqt_ref[...], qn)
    qrt = _dot_nt(wqrt_ref[...], qn)
    scale = MLA_QK ** -0.5 * LOG2E
    parts = []
    for h in range(MLA_HEADS):
        o = h * MLA_HEAD_PAD
        rope = qt[o + 128:o + 256] * cst + qrt[h * 128:(h + 1) * 128] * snt
        parts += [(qt[o:o + 128] * scale).astype(BF16), (rope * scale).astype(BF16)]
    kvn = _rms(lat[:, MLA_Q_RANK:MLA_Q_RANK + MLA_KV_RANK], gkv_ref[...]).astype(BF16)
    k = _dot(kvn, wk_ref[...]).astype(BF16)
    vt = _dot_nt(wvt_ref[...], kvn).astype(BF16)
    kr = (lat[:, OFF_KR - OFF_ZQ:OFF_KRR - OFF_ZQ] * cs + lat[:, OFF_KRR - OFF_ZQ:] * sn).astype(BF16)
    return jnp.concatenate(parts, axis=0), k, kr, vt


def _in_proj_kernel(seq_len, x_ref, cs_ref, sn_ref, cst_ref, snt_ref, g_ref, w_ref, wt_ref, mu_ref, gq_ref, gkv_ref,
                    wqt_ref, wqrt_ref, wk_ref, wvt_ref, o_ref, ot_ref, qt_ref, k_ref, kr_ref, vt_ref, carry_ref):
    tm = x_ref.shape[0]
    half = tm // 2
    first = (pl.program_id(0) * tm) % seq_len == 0
    row0 = lax.broadcasted_iota(jnp.int32, (half, 1), 0) == 0
    edge = {}

    def rows(idx):
        rs = slice(idx * half, (idx + 1) * half)
        xn = _rms(x_ref[rs, :], g_ref[...]).astype(BF16)
        for c, e in IN_RANGES:
            res = _dot(xn, w_ref[:, c:e])
            yield
            if c >= OFF_ZQ:
                qt, k, kr, vt = _mla_project(res, cs_ref[rs, :], sn_ref[rs, :], cst_ref[:, rs], snt_ref[:, rs],
                                             gq_ref, gkv_ref, wqt_ref, wqrt_ref, wk_ref, wvt_ref)
                qt_ref[:, rs] = qt
                k_ref[rs, :] = k
                kr_ref[rs, :] = kr
                vt_ref[:, rs] = vt
                continue
            if c >= OFF_RW:
                cs = slice(c - OFF_RW, e - OFF_RW)
                if idx == 0:
                    last = jnp.where(first, 0.0, carry_ref[:, cs])
                    edge[c] = res[half - 1:half, :]
                else:
                    last = edge[c]
                    carry_ref[:, cs] = res[half - 1:half, :]
                prev = jnp.where(row0, last, pltpu.roll(res, 1, axis=0))
                res = res + (prev - res) * mu_ref[:, cs]
            o_ref[rs, c:e] = res.astype(o_ref.dtype)
        ot_ref[:, rs] = _dot_nt(wt_ref[...], xn).astype(ot_ref.dtype)

    live = [rows(0), rows(1)]
    while live:
        live = [g for g in live if next(g, True) is None]


def _in_proj(h, seq_len, tables, consts):
    t, d = h.shape
    nt = _pshape(consts[2])[0]
    tm = min(t, 512)
    hv = MLA_HEADS * MLA_V
    hq = MLA_HEADS * MLA_HEAD_PAD
    row = lambda w: pl.BlockSpec((tm, w), lambda i: (i, 0))
    col = lambda w: pl.BlockSpec((w, tm), lambda i: (0, i))
    return pl.pallas_call(
        functools.partial(_in_proj_kernel, seq_len),
        grid=(t // tm,),
        in_specs=[row(d), row(128), row(128), col(128), col(128)] + [_pspec(c) for c in consts],
        out_specs=[row(OFF_ZQ), col(nt), col(hq), row(hv), row(128), col(hv)],
        out_shape=[
            jax.ShapeDtypeStruct((t, OFF_ZQ), BF16),
            jax.ShapeDtypeStruct((nt, t), BF16),
            jax.ShapeDtypeStruct((hq, t), BF16),
            jax.ShapeDtypeStruct((t, hv), BF16),
            jax.ShapeDtypeStruct((t, 128), BF16),
            jax.ShapeDtypeStruct((hv, t), BF16),
        ],
        scratch_shapes=[pltpu.VMEM((1, OFF_ZQ - OFF_RW), F32)],
        compiler_params=_cparams("arbitrary"),
        name="in_proj",
    )(h, *tables, *_pargs(consts))


MLA_TQ = 512


def _mla_attn_kernel(qt_ref, k_ref, kr_ref, vt_ref, o_ref, m_ref, acc_ref):
    i = pl.program_id(1)
    tq = MLA_TQ
    m_ref[...] = jnp.full(m_ref.shape, -1e30, F32)
    acc_ref[...] = jnp.zeros(acc_ref.shape, F32)
    ones = jnp.ones((ONES_ROWS, tq), BF16)

    def run(blocks):
        def scores(j, masked, h):
            start = pl.multiple_of(j * tq, tq)
            kh = jnp.concatenate([k_ref[pl.ds(start, tq), h * 128:(h + 1) * 128],
                                  kr_ref[pl.ds(start, tq), :]], axis=1)
            s = _dot(kh, qt_ref[h * MLA_HEAD_PAD:(h + 1) * MLA_HEAD_PAD, :])
            if masked:
                kc = lax.broadcasted_iota(jnp.int32, s.shape, 0) // CHUNK
                qc = lax.broadcasted_iota(jnp.int32, s.shape, 1) // CHUNK
                s = jnp.where(kc <= qc, s, -1e30)
            return s

        def update(j, h, s):
            start = pl.multiple_of(j * tq, tq)
            m_prev = m_ref[h]
            m_new = jnp.maximum(m_prev, jnp.max(s, axis=0, keepdims=True))
            a = jnp.exp2(m_prev - m_new)
            p = jnp.exp2(s - m_new).astype(BF16)
            vh = jnp.concatenate([vt_ref[h * MLA_V:(h + 1) * MLA_V, pl.ds(start, tq)], ones], axis=0)
            acc_ref[h] = a * acc_ref[h] + _dot(vh, p)
            m_ref[h] = m_new

        items = [(j, masked, h) for j, masked in blocks for h in range(MLA_HEADS)]
        _pipelined(items, lambda it: scores(*it), lambda it, s: update(it[0], it[2], s))

    def body(jj, c):
        run([(2 * jj, False), (2 * jj + 1, False)])
        return c

    lax.fori_loop(0, i // 2, body, 0)
    pl.when(i % 2 == 1)(lambda: run([(i - 1, False), (i, True)]))
    pl.when(i % 2 == 0)(lambda: run([(i, True)]))
    for h in range(MLA_HEADS):
        acc = acc_ref[h]
        o_ref[:, h * MLA_V:(h + 1) * MLA_V] = (acc[:MLA_V] / acc[MLA_V:MLA_V + 1]).T.astype(o_ref.dtype)


def _mla_attn(qt, k, kr, vt, b):
    hq, t = qt.shape
    s = t // b
    tq = MLA_TQ
    nq = s // tq
    hv = MLA_HEADS * MLA_V
    full = lambda w: pl.BlockSpec((s, w), lambda bi, i: (bi, 0))
    return pl.pallas_call(
        _mla_attn_kernel,
        grid=(b, nq),
        in_specs=[
            pl.BlockSpec((hq, tq), lambda bi, i: (0, bi * nq + i)),
            full(hv), full(128),
            pl.BlockSpec((hv, s), lambda bi, i: (0, bi)),
        ],
        out_specs=pl.BlockSpec((tq, hv), lambda bi, i: (bi * nq + i, 0)),
        out_shape=jax.ShapeDtypeStruct((t, hv), BF16),
        scratch_shapes=[pltpu.VMEM((MLA_HEADS, 1, tq), F32),
                        pltpu.VMEM((MLA_HEADS, MLA_V + ONES_ROWS, tq), F32)],
        compiler_params=_cparams("parallel", "arbitrary"),
        name="mla_attn",
    )(qt, k, kr, vt)


CA_TQ = 256
CA_BAND_BLK = CA_TQ + CA_PAD


def _ca_attn_kernel(qt_ref, k_ref, vt_ref, bias_ref, o_ref):
    i = pl.program_id(1)
    lo = lax.broadcasted_iota(jnp.int32, (128, CA_TQ), 0) < CA_HEAD
    zero = jnp.zeros((), BF16)

    def block(k_start, n_keys, bias_off):
        def scores(h):
            sl = slice((h // 2) * 128, (h // 2 + 1) * 128)
            k2 = k_ref[pl.ds(k_start, n_keys), sl]
            qh = jnp.where(lo if h % 2 == 0 else ~lo, qt_ref[sl, :], zero)
            return _dot(k2, qh) + bias_ref[h, bias_off:bias_off + n_keys, :]

        def attend(h, s):
            p = jnp.exp2(s - jnp.max(s, axis=0, keepdims=True)).astype(BF16)
            vh = jnp.concatenate([vt_ref[h * CA_HEAD:(h + 1) * CA_HEAD, pl.ds(k_start, n_keys)],
                                  jnp.ones((ONES_ROWS, n_keys), BF16)], axis=0)
            out = _dot(vh, p)
            return out[:CA_HEAD] / out[CA_HEAD:CA_HEAD + 1]

        outs = _pipelined(list(range(CA_HEADS)), scores, attend)
        for pair in range(CA_HEADS // 2):
            both = jnp.concatenate(outs[2 * pair:2 * pair + 2], axis=0)
            o_ref[:, pair * 128:(pair + 1) * 128] = both.T.astype(o_ref.dtype)

    lead = CA_PAD // CA_TQ
    for j in range(lead):
        pl.when(i == j)(functools.partial(block, 0, (j + 1) * CA_TQ, CA_PAD - j * CA_TQ))
    pl.when(i >= lead)(lambda: block(pl.multiple_of(i * CA_TQ - CA_PAD, CA_TQ), CA_BAND_BLK, 0))


def _ca_attn(zt, z, bias, b):
    t = zt.shape[1]
    s = t // b
    nq = s // CA_TQ
    return pl.pallas_call(
        _ca_attn_kernel,
        grid=(b, nq),
        in_specs=[
            pl.BlockSpec((CA_DIM, CA_TQ), lambda bi, i: (0, bi * nq + i)),
            pl.BlockSpec((s, CA_DIM), lambda bi, i: (bi, OFF_CAK // CA_DIM)),
            pl.BlockSpec((CA_DIM, s), lambda bi, i: (1, bi)),
            _const_spec(bias.shape),
        ],
        out_specs=pl.BlockSpec((CA_TQ, CA_DIM), lambda bi, i: (bi * nq + i, 0)),
        out_shape=jax.ShapeDtypeStruct((t, CA_DIM), BF16),
        compiler_params=_cparams("parallel", "arbitrary"),
        name="ca_attn",
    )(zt, z, zt, bias)


def _ca_bias_kernel(t_ref, o_ref):
    width = t_ref.shape[-1]
    rows = pltpu.roll(jnp.broadcast_to(t_ref[0], (CA_TQ, width)), 0, 1, stride=1, stride_axis=0)
    rc = lax.broadcasted_iota(jnp.int32, (CA_TQ, CA_BAND_BLK), 0) // CHUNK
    cc = lax.broadcasted_iota(jnp.int32, (CA_TQ, CA_BAND_BLK), 1) // CHUNK
    ok = (cc >= rc) & (cc <= rc + CA_LEFT_CHUNKS)
    o_ref[0] = jnp.where(ok, rows[:, :CA_BAND_BLK], -1e30).T


def _ca_bias_table(rel_bias):
    width = CA_TQ + CA_BAND_BLK
    m = jnp.arange(width)
    delta = jnp.where(m < CA_BAND_BLK, m, m - width)
    idx = jnp.clip(CA_PAD - delta, REL_MIN, REL_MAX) - REL_MIN
    t1 = (rel_bias.astype(F32)[idx].T * LOG2E)[:, None, :]
    return pl.pallas_call(
        _ca_bias_kernel,
        grid=(CA_HEADS,),
        in_specs=[pl.BlockSpec((1, 1, width), lambda h: (h, 0, 0))],
        out_specs=pl.BlockSpec((1, CA_BAND_BLK, CA_TQ), lambda h: (h, 0, 0)),
        out_shape=jax.ShapeDtypeStruct((CA_HEADS, CA_BAND_BLK, CA_TQ), F32),
        compiler_params=_cparams("parallel"),
        name="ca_bias",
    )(t1)


def _split_bf16(x):
    hi = x.astype(BF16)
    lo = (x - hi.astype(F32)).astype(BF16)
    return hi, lo


RW_GROUP = 256


def _rw_kernel(rows, xr_ref, xk_ref, xv_ref, xl_ref, w0_ref, wup_ref, a0_ref, aup_ref, gup_ref, kk_ref, ka_ref,
               rk_ref, lnw_ref, lnb_ref, bd_ref, bdf_ref, o_ref, s_ref):
    L = CHUNK
    W = RW_GROUP
    reps = W // RW_HEAD

    @pl.when(pl.program_id(1) == 0)
    def _():
        s_ref[...] = jnp.zeros(s_ref.shape, F32)

    bdm = bd_ref[...]

    def bd(x):
        return jnp.concatenate([x.astype(BF16)] * reps, axis=0) * bdm

    def head_sums(xs):
        out = _dot(jnp.concatenate([x.astype(BF16) for x in xs], axis=0), bdm)
        offs = [0]
        for x in xs:
            offs.append(offs[-1] + x.shape[0])
        return [out[lo:hi] for lo, hi in zip(offs[:-1], offs[1:])]

    row = lax.broadcasted_iota(jnp.int32, (L, W), 0)
    sub = lax.broadcasted_iota(jnp.int32, (L, W), 1) % RW_HEAD
    strict = sub < row
    incl = sub <= row
    eye = (sub == row).astype(F32)
    tri = (lax.broadcasted_iota(jnp.int32, (L, L), 1)
           <= lax.broadcasted_iota(jnp.int32, (L, L), 0)).astype(BF16)
    inv = 1.0 / RW_HEAD

    def lora_inputs(b):
        xl = xl_ref[b].astype(F32)
        return (jnp.tanh(xl[:, :RW_DECAY_LORA]).astype(BF16),
                xl[:, RW_DECAY_LORA:RW_DECAY_LORA + RW_AAA_LORA].astype(BF16),
                jax.nn.sigmoid(xl[:, RW_DECAY_LORA + RW_AAA_LORA:]).astype(BF16))

    def instance(b, gi, xw, xa, xg):
        sl = slice(gi * W, (gi + 1) * W)
        r = xr_ref[b, :, sl].astype(F32)
        k = xk_ref[b, :, sl].astype(F32)
        v = xv_ref[b, :, sl].astype(F32)
        lw = -math.exp(-0.5) * jax.nn.sigmoid(w0_ref[:, sl] + _dot(xw, wup_ref[:, sl]))
        a = jax.nn.sigmoid(a0_ref[:, sl] + _dot(xa, aup_ref[:, sl]))
        g = _dot(xg, gup_ref[:, sl])
        kk = k * kk_ref[:, sl]
        ss = yield kk * kk
        kk = kk * lax.rsqrt(jnp.maximum(ss, 1e-24))
        k = k * (1.0 + (a - 1.0) * ka_ref[:, sl])
        be = kk * a
        lw_hi, lw_lo = _split_bf16(lw)
        cum = _dot(tri, lw_hi) + _dot(tri, lw_lo)
        yield
        cum_l = cum[L - 1:L, :]
        e_neg = jnp.exp(-cum)
        e_tail = jnp.exp(cum_l - cum)
        ar = jnp.concatenate([-kk * jnp.exp(cum - lw), r * jnp.exp(cum)], axis=0).astype(BF16)
        s0 = s_ref[b, gi]
        ar_s = _dot_nt(ar, s0.astype(BF16))
        a_b = _dot_nt(ar, bd(be * e_neg))
        a_k = _dot_nt(ar, bd(k * e_neg))
        yield
        n = jnp.where(strict, a_b[:L], 0.0)
        a_ak = jnp.where(strict, a_k[:L], 0.0)
        a_rb = jnp.where(incl, a_b[L:], 0.0)
        a_rk = jnp.where(incl, a_k[L:], 0.0)
        p = eye + n
        nk = _dot(n.astype(BF16), bd(n))
        bd_v = bd(v)
        x0 = _dot(a_ak.astype(BF16), bd_v)
        yield
        steps = int(math.log2(L)) - 1
        for it in range(steps):
            m = bd(nk)
            if it + 1 < steps:
                res = _dot(jnp.concatenate([p, nk], axis=0).astype(BF16), m)
                p = p + res[:L]
                nk = res[L:]
            else:
                p = p + _dot(p.astype(BF16), m)
            yield
        u = _dot(p.astype(BF16), bd(ar_s[:L] + x0))
        yield
        y = ar_s[L:] + _dot(jnp.concatenate([a_rb, a_rk], axis=1).astype(BF16),
                            jnp.concatenate([bd(u), bd_v], axis=0))
        uv = jnp.concatenate([u, v], axis=0).astype(BF16)
        bk = jnp.concatenate([be * e_tail, k * e_tail], axis=0).astype(BF16)
        s_ref[b, gi] = s0 * jnp.exp(cum_l) + _dot_tn(uv, bk) * bdf_ref[...]
        yield
        sums = yield jnp.concatenate([y, r * k * rk_ref[:, sl]], axis=0)
        d = y - sums[:L] * inv
        var = (yield d * d) * inv
        yn = d * lax.rsqrt(var + RW_LN_EPS) * lnw_ref[:, sl] + lnb_ref[:, sl]
        o_ref[b, :, sl] = ((yn + sums[L:] * v) * g).astype(o_ref.dtype)

    live = []
    for b in range(rows):
        lora = lora_inputs(b)
        live += [instance(b, gi, *lora) for gi in range(RW_DIM // W)]
    asked = [next(g) for g in live]
    while live:
        answers = head_sums(asked) if asked[0] is not None else asked
        live, asked = _advance(live, answers)


def _rw_mixer(z, b, consts):
    t = z.shape[0]
    s = t // b
    rows = math.gcd(b, 8)
    gid = jnp.arange(RW_GROUP) // RW_HEAD
    ones_f32 = (gid[:, None] == gid[None, :]).astype(F32)
    ones_bd = ones_f32.astype(BF16)
    sec = lambda w, j: pl.BlockSpec((rows, CHUNK, w), lambda bb, c: (bb, c, j))
    z3 = z.reshape(b, s, -1)
    return pl.pallas_call(
        functools.partial(_rw_kernel, rows),
        grid=(b // rows, s // CHUNK),
        in_specs=[sec(RW_DIM, OFF_RW // RW_DIM), sec(RW_DIM, OFF_RW // RW_DIM + 1),
                  sec(RW_DIM, OFF_RW // RW_DIM + 2), sec(RW_LORA, OFF_LORA // RW_LORA)]
        + [_pspec(c) for c in consts] + [_const_spec(ones_bd.shape)] * 2,
        out_specs=sec(RW_DIM, 0),
        out_shape=jax.ShapeDtypeStruct((b, s, RW_DIM), BF16),
        scratch_shapes=[pltpu.VMEM((rows, RW_DIM // RW_GROUP, RW_GROUP, RW_GROUP), F32)],
        compiler_params=_cparams("parallel", "arbitrary"),
        name="rw_mixer",
    )(z3, z3, z3, z3, *_pargs(consts), ones_bd, ones_f32)


FF_SPLIT = 4


def _post_kernel(om_ref, or_ref, oc_ref, g0_ref, g1_ref, g2_ref, h_ref, p_ref, wb_ref, wo_ref, gm_ref,
                 g1n_ref, w1_ref, w2_ref, g2n_ref, wg_ref, wp_ref, o_ref):
    def rows(rs):
        merged = None
        for n, (b_ref, zg_ref) in enumerate(((om_ref, g0_ref), (or_ref, g1_ref), (oc_ref, g2_ref))):
            y = _dot(b_ref[rs, :], wb_ref[n * BRANCH_DIM:(n + 1) * BRANCH_DIM, :])
            yield
            gate = jax.nn.sigmoid(zg_ref[rs, :].astype(F32))
            merged = gate * y if merged is None else merged + gate * y
        out = _dot(merged.astype(BF16), wo_ref[...])
        yield
        h = h_ref[rs, :] + _rms(out, gm_ref[...])
        f = _rms(h, g1n_ref[...]).astype(BF16)
        cw = D_FF // FF_SPLIT
        acc = None
        for c in range(FF_SPLIT):
            a = _dot(f, w1_ref[:, c * cw:(c + 1) * cw])
            yield
            a = jnp.maximum(a, 0.0)
            part = _dot((a * a).astype(BF16), w2_ref[c * cw:(c + 1) * cw, :])
            yield
            acc = part if acc is None else acc + part
        h = h + _rms(acc, g2n_ref[...])
        gate = _dot(h.astype(BF16), wg_ref[...])
        proj = _dot(p_ref[rs, :].astype(BF16), wp_ref[...])
        yield
        o_ref[rs, :] = h + jax.nn.sigmoid(gate) * proj

    half = h_ref.shape[0] // 2
    live = [rows(slice(0, half)), rows(slice(half, 2 * half))]
    while live:
        live = [g for g in live if next(g, True) is None]


def _post(o_mla, o_rw, o_ca, z, h, p, consts):
    t = h.shape[0]
    tm = min(o_rw.shape[1], 512)
    per_seq = o_rw.shape[1] // tm
    row = lambda w: pl.BlockSpec((tm, w), lambda i: (i, 0))
    gate = lambda n: pl.BlockSpec((tm, D_MODEL), lambda i: (i, OFF_GATE // D_MODEL + n))
    p_stack, layer = p
    return pl.pallas_call(
        _post_kernel,
        grid=(t // tm,),
        in_specs=[row(BRANCH_DIM), pl.BlockSpec((None, tm, BRANCH_DIM), lambda i: (i // per_seq, i % per_seq, 0)),
                  row(BRANCH_DIM), gate(0), gate(1), gate(2), row(D_MODEL),
                  pl.BlockSpec((None, tm, D_PLE), lambda i: (layer, i, 0))] + [_pspec(c) for c in consts],
        out_specs=row(D_MODEL),
        out_shape=jax.ShapeDtypeStruct((t, D_MODEL), F32),
        compiler_params=_cparams("parallel"),
        name="post",
    )(o_mla, o_rw, o_ca, z, z, z, h, p_stack, *_pargs(consts))


def _rot_half_cols(w):
    half = w.shape[-1] // 2
    return jnp.concatenate([-w[..., half:], w[..., :half]], axis=-1)


def _pack_w_in(w):
    mla_cols = MLA_Q_RANK + MLA_KV_RANK + MLA_ROPE
    rw_cols = 3 * RW_DIM + RW_LORA
    ca_q = (w[..., mla_cols + rw_cols:mla_cols + rw_cols + CA_DIM] * (CA_HEAD ** -0.5 * LOG2E)).astype(BF16)
    w = w.astype(BF16)
    w_mla = w[..., :mla_cols]
    w_rw = w[..., mla_cols:mla_cols + rw_cols]
    w_ca = w[..., mla_cols + rw_cols:mla_cols + rw_cols + 3 * CA_DIM]
    w_gate = w[..., mla_cols + rw_cols + 3 * CA_DIM:]
    w_kr = w_mla[..., MLA_Q_RANK + MLA_KV_RANK:]
    z64 = jnp.zeros(w.shape[:-1] + (64,), w.dtype)
    packed = jnp.concatenate(
        [w_gate, w_ca[..., CA_DIM:2 * CA_DIM], w_rw, w_mla[..., :MLA_Q_RANK + MLA_KV_RANK], w_kr, z64,
         _rot_half_cols(w_kr), z64], axis=-1)
    wt = jnp.concatenate([ca_q, w_ca[..., 2 * CA_DIM:]], axis=-1)
    return packed, jnp.swapaxes(wt, 1, 2)


def _pack_w_uq(w):
    nl, r, _ = w.shape
    w = w.reshape(nl, r, MLA_HEADS, MLA_QK)
    z64 = jnp.zeros((nl, r, MLA_HEADS, 64), w.dtype)
    rot = _rot_half_cols(w[..., MLA_NOPE:])
    wq = jnp.concatenate([w, z64], axis=-1).reshape(nl, r, MLA_HEADS * MLA_HEAD_PAD)
    wqr = jnp.concatenate([rot, z64], axis=-1).reshape(nl, r, MLA_HEADS * 128)
    return jnp.swapaxes(wq, 1, 2).astype(BF16), jnp.swapaxes(wqr, 1, 2).astype(BF16)


def _pack_w_ukv(w):
    nl, r, _ = w.shape
    w = w.reshape(nl, r, MLA_HEADS, MLA_NOPE + MLA_V)
    wk = w[..., :MLA_NOPE].reshape(nl, r, MLA_HEADS * MLA_NOPE)
    wv = w[..., MLA_NOPE:].reshape(nl, r, MLA_HEADS * MLA_V)
    return wk.astype(BF16), jnp.swapaxes(wv, 1, 2).astype(BF16)


def kernel(x, p, positions, pre_mix_g, w_in, mla_q_norm_g, mla_kv_norm_g, mla_w_uq, mla_w_ukv, rw_mu, rw_w0, rw_w_up, rw_a0, rw_a_up, rw_g_up, rw_k_k, rw_k_a, rw_r_k, rw_ln_w, rw_ln_b, ca_rel_bias, w_branch, w_out, post_mix_g, pre_ff_g, w_ff1, w_ff2, post_ff_g, w_ple_gate, w_ple_proj):
    b, s, d = x.shape
    t = b * s
    depth = w_in.shape[0]
    row = lambda a: a.reshape(depth, 1, -1).astype(F32)
    bf = lambda a: a.astype(BF16)
    w_in_p, w_in_t = _pack_w_in(w_in)
    wqt, wqrt = _pack_w_uq(mla_w_uq)
    wk, wvt = _pack_w_ukv(mla_w_ukv)
    stacks = dict(
        in_proj=[row(pre_mix_g), w_in_p, w_in_t, row(rw_mu),
                 row(mla_q_norm_g), row(mla_kv_norm_g), wqt, wqrt, wk, wvt],
        rw=[row(rw_w0), bf(rw_w_up), row(rw_a0), bf(rw_a_up), bf(rw_g_up), row(rw_k_k), row(rw_k_a),
            row(rw_r_k), row(rw_ln_w), row(rw_ln_b)],
        post=[bf(w_branch), bf(w_out), row(post_mix_g),
              row(pre_ff_g), bf(w_ff1), bf(w_ff2), row(post_ff_g), bf(w_ple_gate), bf(w_ple_proj)],
    )
    p3 = p.reshape(depth, t, -1)
    rope = _rope_table(positions)
    h = x.reshape(t, d)
    for i in range(depth):
        prm = {name: [(a, i) for a in arrs] for name, arrs in stacks.items()}
        z, zt, qt, kn, kr, vt = _in_proj(h, s, rope, prm["in_proj"])
        o_mla = _mla_attn(qt, kn, kr, vt, b)
        o_rw = _rw_mixer(z, b, prm["rw"])
        o_ca = _ca_attn(zt, z, _ca_bias_table(ca_rel_bias[i]), b)
        h = _post(o_mla, o_rw, o_ca, z, h, (p3, i), prm["post"])
    return h.reshape(b, s, d)
```

```python
import functools
import math

import jax
import jax.numpy as jnp
from jax import lax
from jax.experimental import pallas as pl
from jax.experimental.pallas import tpu as pltpu

F32 = jnp.float32
BF16 = jnp.bfloat16

D_MODEL = 1024
D_PLE = 256
D_FF = 4 * D_MODEL
NORM_EPS = 1e-6
CHUNK = 64

MLA_HEADS = 4
MLA_NOPE = 128
MLA_ROPE = 64
MLA_V = 128
MLA_Q_RANK = 256
MLA_KV_RANK = 128
ROPE_THETA = 10000.0
MLA_QK = MLA_NOPE + MLA_ROPE
MLA_HEAD_PAD = 256

RW_HEADS = 8
RW_HEAD = 64
RW_DIM = RW_HEADS * RW_HEAD
RW_DECAY_LORA = 64
RW_AAA_LORA = 64
RW_GATE_LORA = 128
RW_LORA = RW_DECAY_LORA + RW_AAA_LORA + RW_GATE_LORA
RW_LN_EPS = 64e-5

CA_HEADS = 8
CA_HEAD = 64
CA_DIM = CA_HEADS * CA_HEAD
CA_LEFT_CHUNKS = 8
CA_PAD = CA_LEFT_CHUNKS * CHUNK
REL_MIN = -(CHUNK - 1)
REL_MAX = 256

N_BRANCH = 3
BRANCH_DIM = 512
GATE_COLS = N_BRANCH * D_MODEL

OFF_GATE = 0
OFF_CAK = GATE_COLS
OFF_RW = OFF_CAK + CA_DIM
OFF_LORA = OFF_RW + 3 * RW_DIM
OFF_ZQ = OFF_LORA + RW_LORA
OFF_ZKV = OFF_ZQ + MLA_Q_RANK
OFF_KR = OFF_ZKV + MLA_KV_RANK
OFF_KRR = OFF_KR + 128
IN_PACKED = OFF_KRR + 128
LOG2E = math.log2(math.e)

VMEM_LIMIT = 56 * 1024 * 1024


def _cparams(*sem):
    return pltpu.CompilerParams(dimension_semantics=sem, vmem_limit_bytes=VMEM_LIMIT)


def _rms(x, g):
    return x * lax.rsqrt(jnp.mean(x * x, axis=-1, keepdims=True) + NORM_EPS) * g


def _dot(a, b):
    return jnp.dot(a, b, preferred_element_type=F32)


def _dot_nt(a, b):
    return lax.dot_general(a, b, (((1,), (1,)), ((), ())), preferred_element_type=F32)


def _dot_tn(a, b):
    return lax.dot_general(a, b, (((0,), (0,)), ((), ())), preferred_element_type=F32)


def _const_spec(shape):
    nd = len(shape)
    return pl.BlockSpec(shape, lambda *_: (0,) * nd, pipeline_mode=pl.Buffered(1))


def _pshape(param):
    return param[0].shape[1:]


def _pspec(param):
    stack, layer = param
    nd = stack.ndim - 1
    return pl.BlockSpec((None,) + stack.shape[1:], lambda *_: (layer,) + (0,) * nd, pipeline_mode=pl.Buffered(1))


def _pargs(params):
    return [stack for stack, _ in params]


def _advance(gens, answers):
    live, asked = [], []
    for g, ans in zip(gens, answers):
        try:
            asked.append(g.send(ans))
            live.append(g)
        except StopIteration:
            pass
    return live, asked


PIPE_DEPTH = 2
ONES_ROWS = 16


def _pipelined(items, produce, consume):
    pending, outs = [], []
    for it in items:
        pending.append((it, produce(it)))
        if len(pending) > PIPE_DEPTH:
            outs.append(consume(*pending.pop(0)))
    outs.extend(consume(*pc) for pc in pending)
    return outs


def _rope_table_kernel(pos_ref, freq_ref, cs_ref, sn_ref, cst_ref, snt_ref):
    ang = freq_ref[...] * pos_ref[...]
    dead = jnp.zeros((128 - MLA_ROPE, ang.shape[1]), F32)
    cst = jnp.concatenate([jnp.cos(ang), dead], axis=0)
    snt = jnp.concatenate([jnp.sin(ang), dead], axis=0)
    cst_ref[...] = cst
    snt_ref[...] = snt
    cs_ref[...] = cst.T
    sn_ref[...] = snt.T


def _rope_table(positions):
    t = positions.size
    tm = min(t, 2048)
    half = MLA_ROPE // 2
    inv_freq = 1.0 / (ROPE_THETA ** (jnp.arange(half, dtype=F32) / half))
    freq = jnp.concatenate([inv_freq, inv_freq])[:, None]
    return pl.pallas_call(
        _rope_table_kernel,
        grid=(t // tm,),
        in_specs=[pl.BlockSpec((1, tm), lambda i: (0, i)), _const_spec(freq.shape)],
        out_specs=[pl.BlockSpec((tm, 128), lambda i: (i, 0))] * 2 + [pl.BlockSpec((128, tm), lambda i: (0, i))] * 2,
        out_shape=[jax.ShapeDtypeStruct((t, 128), F32)] * 2 + [jax.ShapeDtypeStruct((128, t), F32)] * 2,
        compiler_params=_cparams("parallel"),
        name="rope_table",
    )(positions.astype(F32).reshape(1, t), freq)


IN_RANGES = ((0, 1536), (1536, OFF_CAK), (OFF_CAK, OFF_RW), (OFF_RW, OFF_RW + 1024), (OFF_RW + 1024, OFF_ZQ),
             (OFF_ZQ, IN_PACKED))


def _mla_project(lat, cs, sn, cst, snt, gq_ref, gkv_ref, wqt_ref, wqrt_ref, wk_ref, wvt_ref):
    qn = _rms(lat[:, :MLA_Q_RANK], gq_ref[...]).astype(BF16)
    qt = _dot_nt(wqt_ref[...], qn)
    qrt = _dot_nt(wqrt_ref[...], qn)
    scale = MLA_QK ** -0.5 * LOG2E
    parts = []
    for h in range(MLA_HEADS):
        o = h * MLA_HEAD_PAD
        rope = qt[o + 128:o + 256] * cst + qrt[h * 128:(h + 1) * 128] * snt
        parts += [(qt[o:o + 128] * scale).astype(BF16), (rope * scale).astype(BF16)]
    kvn = _rms(lat[:, MLA_Q_RANK:MLA_Q_RANK + MLA_KV_RANK], gkv_ref[...]).astype(BF16)
    k = _dot(kvn, wk_ref[...]).astype(BF16)
    vt = _dot_nt(wvt_ref[...], kvn).astype(BF16)
    kr = (lat[:, OFF_KR - OFF_ZQ:OFF_KRR - OFF_ZQ] * cs + lat[:, OFF_KRR - OFF_ZQ:] * sn).astype(BF16)
    return jnp.concatenate(parts, axis=0), k, kr, vt


def _in_proj_kernel(seq_len, x_ref, cs_ref, sn_ref, cst_ref, snt_ref, g_ref, w_ref, wt_ref, mu_ref, gq_ref, gkv_ref,
                    wqt_ref, wqrt_ref, wk_ref, wvt_ref, o_ref, ot_ref, qt_ref, k_ref, kr_ref, vt_ref, carry_ref):
    tm = x_ref.shape[0]
    half = tm // 2
    first = (pl.program_id(0) * tm) % seq_len == 0
    row0 = lax.broadcasted_iota(jnp.int32, (half, 1), 0) == 0
    edge = {}

    def rows(idx):
        rs = slice(idx * half, (idx + 1) * half)
        xn = _rms(x_ref[rs, :], g_ref[...]).astype(BF16)
        for c, e in IN_RANGES:
            res = _dot(xn, w_ref[:, c:e])
            yield
            if c >= OFF_ZQ:
                qt, k, kr, vt = _mla_project(res, cs_ref[rs, :], sn_ref[rs, :], cst_ref[:, rs], snt_ref[:, rs],
                                             gq_ref, gkv_ref, wqt_ref, wqrt_ref, wk_ref, wvt_ref)
                qt_ref[:, rs] = qt
                k_ref[rs, :] = k
                kr_ref[rs, :] = kr
                vt_ref[:, rs] = vt
                continue
            if c >= OFF_RW:
                cs = slice(c - OFF_RW, e - OFF_RW)
                if idx == 0:
                    last = jnp.where(first, 0.0, carry_ref[:, cs])
                    edge[c] = res[half - 1:half, :]
                else:
                    last = edge[c]
                    carry_ref[:, cs] = res[half - 1:half, :]
                prev = jnp.where(row0, last, pltpu.roll(res, 1, axis=0))
                res = res + (prev - res) * mu_ref[:, cs]
            o_ref[rs, c:e] = res.astype(o_ref.dtype)
        ot_ref[:, rs] = _dot_nt(wt_ref[...], xn).astype(ot_ref.dtype)

    live = [rows(0), rows(1)]
    while live:
        live = [g for g in live if next(g, True) is None]


def _in_proj(h, seq_len, tables, consts):
    t, d = h.shape
    nt = _pshape(consts[2])[0]
    tm = min(t, 512)
    hv = MLA_HEADS * MLA_V
    hq = MLA_HEADS * MLA_HEAD_PAD
    row = lambda w: pl.BlockSpec((tm, w), lambda i: (i, 0))
    col = lambda w: pl.BlockSpec((w, tm), lambda i: (0, i))
    return pl.pallas_call(
        functools.partial(_in_proj_kernel, seq_len),
        grid=(t // tm,),
        in_specs=[row(d), row(128), row(128), col(128), col(128)] + [_pspec(c) for c in consts],
        out_specs=[row(OFF_ZQ), col(nt), col(hq), row(hv), row(128), col(hv)],
        out_shape=[
            jax.ShapeDtypeStruct((t, OFF_ZQ), BF16),
            jax.ShapeDtypeStruct((nt, t), BF16),
            jax.ShapeDtypeStruct((hq, t), BF16),
            jax.ShapeDtypeStruct((t, hv), BF16),
            jax.ShapeDtypeStruct((t, 128), BF16),
            jax.ShapeDtypeStruct((hv, t), BF16),
        ],
        scratch_shapes=[pltpu.VMEM((1, OFF_ZQ - OFF_RW), F32)],
        compiler_params=_cparams("arbitrary"),
        name="in_proj",
    )(h, *tables, *_pargs(consts))


MLA_TQ = 512


def _mla_attn_kernel(qt_ref, k_ref, kr_ref, vt_ref, o_ref, m_ref, acc_ref):
    i = pl.program_id(1)
    tq = MLA_TQ
    m_ref[...] = jnp.full(m_ref.shape, -1e30, F32)
    acc_ref[...] = jnp.zeros(acc_ref.shape, F32)
    ones = jnp.ones((ONES_ROWS, tq), BF16)

    def run(blocks):
        def scores(j, masked, h):
            start = pl.multiple_of(j * tq, tq)
            kh = jnp.concatenate([k_ref[pl.ds(start, tq), h * 128:(h + 1) * 128],
                                  kr_ref[pl.ds(start, tq), :]], axis=1)
            s = _dot(kh, qt_ref[h * MLA_HEAD_PAD:(h + 1) * MLA_HEAD_PAD, :])
            if masked:
                kc = lax.broadcasted_iota(jnp.int32, s.shape, 0) // CHUNK
                qc = lax.broadcasted_iota(jnp.int32, s.shape, 1) // CHUNK
                s = jnp.where(kc <= qc, s, -1e30)
            return s

        def update(j, h, s):
            start = pl.multiple_of(j * tq, tq)
            m_prev = m_ref[h]
            m_new = jnp.maximum(m_prev, jnp.max(s, axis=0, keepdims=True))
            a = jnp.exp2(m_prev - m_new)
            p = jnp.exp2(s - m_new).astype(BF16)
            vh = jnp.concatenate([vt_ref[h * MLA_V:(h + 1) * MLA_V, pl.ds(start, tq)], ones], axis=0)
            acc_ref[h] = a * acc_ref[h] + _dot(vh, p)
            m_ref[h] = m_new

        items = [(j, masked, h) for j, masked in blocks for h in range(MLA_HEADS)]
        _pipelined(items, lambda it: scores(*it), lambda it, s: update(it[0], it[2], s))

    def body(jj, c):
        run([(2 * jj, False), (2 * jj + 1, False)])
        return c

    lax.fori_loop(0, i // 2, body, 0)
    pl.when(i % 2 == 1)(lambda: run([(i - 1, False), (i, True)]))
    pl.when(i % 2 == 0)(lambda: run([(i, True)]))
    for h in range(MLA_HEADS):
        acc = acc_ref[h]
        o_ref[:, h * MLA_V:(h + 1) * MLA_V] = (acc[:MLA_V] / acc[MLA_V:MLA_V + 1]).T.astype(o_ref.dtype)


def _mla_attn(qt, k, kr, vt, b):
    hq, t = qt.shape
    s = t // b
    tq = MLA_TQ
    nq = s // tq
    hv = MLA_HEADS * MLA_V
    full = lambda w: pl.BlockSpec((s, w), lambda bi, i: (bi, 0))
    return pl.pallas_call(
        _mla_attn_kernel,
        grid=(b, nq),
        in_specs=[
            pl.BlockSpec((hq, tq), lambda bi, i: (0, bi * nq + i)),
            full(hv), full(128),
            pl.BlockSpec((hv, s), lambda bi, i: (0, bi)),
        ],
        out_specs=pl.BlockSpec((tq, hv), lambda bi, i: (bi * nq + i, 0)),
        out_shape=jax.ShapeDtypeStruct((t, hv), BF16),
        scratch_shapes=[pltpu.VMEM((MLA_HEADS, 1, tq), F32),
                        pltpu.VMEM((MLA_HEADS, MLA_V + ONES_ROWS, tq), F32)],
        compiler_params=_cparams("parallel", "arbitrary"),
        name="mla_attn",
    )(qt, k, kr, vt)


CA_TQ = 256
CA_BAND_BLK = CA_TQ + CA_PAD


CA_PER_STEP = 2


def _ca_attn_kernel(qt_ref, k_ref, vt_ref, bias_ref, o_ref):
    i = pl.program_id(1)
    lo = lax.broadcasted_iota(jnp.int32, (128, CA_TQ), 0) < CA_HEAD
    zero = jnp.zeros((), BF16)

    def work(blocks):
        def scores(item):
            (sub, k_start, n_keys, bias_off), h = item
            sl = slice((h // 2) * 128, (h // 2 + 1) * 128)
            k2 = k_ref[pl.ds(k_start, n_keys), sl]
            qh = jnp.where(lo if h % 2 == 0 else ~lo, qt_ref[sl, sub * CA_TQ:(sub + 1) * CA_TQ], zero)
            return _dot(k2, qh) + bias_ref[h, bias_off:bias_off + n_keys, :]

        def attend(item, s):
            (_, k_start, n_keys, _), h = item
            p = jnp.exp2(s - jnp.max(s, axis=0, keepdims=True)).astype(BF16)
            vh = jnp.concatenate([vt_ref[h * CA_HEAD:(h + 1) * CA_HEAD, pl.ds(k_start, n_keys)],
                                  jnp.ones((ONES_ROWS, n_keys), BF16)], axis=0)
            out = _dot(vh, p)
            return out[:CA_HEAD] / out[CA_HEAD:CA_HEAD + 1]

        outs = _pipelined([(blk, h) for blk in blocks for h in range(CA_HEADS)], scores, attend)
        for n, blk in enumerate(blocks):
            rows = slice(blk[0] * CA_TQ, (blk[0] + 1) * CA_TQ)
            for pair in range(CA_HEADS // 2):
                both = jnp.concatenate(outs[n * CA_HEADS + 2 * pair:n * CA_HEADS + 2 * pair + 2], axis=0)
                o_ref[rows, pair * 128:(pair + 1) * 128] = both.T.astype(o_ref.dtype)

    lead = CA_PAD // CA_TQ
    assert lead % CA_PER_STEP == 0
    for j in range(lead // CA_PER_STEP):
        short = [(sub, 0, (j * CA_PER_STEP + sub + 1) * CA_TQ, CA_PAD - (j * CA_PER_STEP + sub) * CA_TQ)
                 for sub in range(CA_PER_STEP)]
        pl.when(i == j)(functools.partial(work, short))
    pl.when(i >= lead // CA_PER_STEP)(lambda: work(
        [(sub, pl.multiple_of((i * CA_PER_STEP + sub) * CA_TQ - CA_PAD, CA_TQ), CA_BAND_BLK, 0)
         for sub in range(CA_PER_STEP)]))


def _ca_attn(zt, z, bias, b):
    t = zt.shape[1]
    s = t // b
    tq = CA_PER_STEP * CA_TQ
    nq = s // tq
    return pl.pallas_call(
        _ca_attn_kernel,
        grid=(b, nq),
        in_specs=[
            pl.BlockSpec((CA_DIM, tq), lambda bi, i: (0, bi * nq + i)),
            pl.BlockSpec((s, CA_DIM), lambda bi, i: (bi, OFF_CAK // CA_DIM)),
            pl.BlockSpec((CA_DIM, s), lambda bi, i: (1, bi)),
            _const_spec(bias.shape),
        ],
        out_specs=pl.BlockSpec((tq, CA_DIM), lambda bi, i: (bi * nq + i, 0)),
        out_shape=jax.ShapeDtypeStruct((t, CA_DIM), BF16),
        compiler_params=_cparams("parallel", "arbitrary"),
        name="ca_attn",
    )(zt, z, zt, bias)


def _ca_bias_kernel(t_ref, o_ref):
    width = t_ref.shape[-1]
    rows = pltpu.roll(jnp.broadcast_to(t_ref[0], (CA_TQ, width)), 0, 1, stride=1, stride_axis=0)
    rc = lax.broadcasted_iota(jnp.int32, (CA_TQ, CA_BAND_BLK), 0) // CHUNK
    cc = lax.broadcasted_iota(jnp.int32, (CA_TQ, CA_BAND_BLK), 1) // CHUNK
    ok = (cc >= rc) & (cc <= rc + CA_LEFT_CHUNKS)
    o_ref[0] = jnp.where(ok, rows[:, :CA_BAND_BLK], -1e30).T


def _ca_bias_table(rel_bias):
    width = CA_TQ + CA_BAND_BLK
    m = jnp.arange(width)
    delta = jnp.where(m < CA_BAND_BLK, m, m - width)
    idx = jnp.clip(CA_PAD - delta, REL_MIN, REL_MAX) - REL_MIN
    t1 = (rel_bias.astype(F32)[idx].T * LOG2E)[:, None, :]
    return pl.pallas_call(
        _ca_bias_kernel,
        grid=(CA_HEADS,),
        in_specs=[pl.BlockSpec((1, 1, width), lambda h: (h, 0, 0))],
        out_specs=pl.BlockSpec((1, CA_BAND_BLK, CA_TQ), lambda h: (h, 0, 0)),
        out_shape=jax.ShapeDtypeStruct((CA_HEADS, CA_BAND_BLK, CA_TQ), F32),
        compiler_params=_cparams("parallel"),
        name="ca_bias",
    )(t1)


def _split_bf16(x):
    hi = x.astype(BF16)
    lo = (x - hi.astype(F32)).astype(BF16)
    return hi, lo


RW_GROUP = 256


def _rw_kernel(rows, xr_ref, xk_ref, xv_ref, xl_ref, w0_ref, wup_ref, a0_ref, aup_ref, gup_ref, kk_ref, ka_ref,
               rk_ref, lnw_ref, lnb_ref, bd_ref, bdf_ref, o_ref, s_ref):
    L = CHUNK
    W = RW_GROUP
    reps = W // RW_HEAD

    @pl.when(pl.program_id(1) == 0)
    def _():
        s_ref[...] = jnp.zeros(s_ref.shape, F32)

    bdm = bd_ref[...]

    def bd(x):
        return jnp.concatenate([x.astype(BF16)] * reps, axis=0) * bdm

    def head_sums(xs):
        out = _dot(jnp.concatenate([x.astype(BF16) for x in xs], axis=0), bdm)
        offs = [0]
        for x in xs:
            offs.append(offs[-1] + x.shape[0])
        return [out[lo:hi] for lo, hi in zip(offs[:-1], offs[1:])]

    row = lax.broadcasted_iota(jnp.int32, (L, W), 0)
    sub = lax.broadcasted_iota(jnp.int32, (L, W), 1) % RW_HEAD
    strict = sub < row
    incl = sub <= row
    eye = (sub == row).astype(F32)
    tri = (lax.broadcasted_iota(jnp.int32, (L, L), 1)
           <= lax.broadcasted_iota(jnp.int32, (L, L), 0)).astype(BF16)
    inv = 1.0 / RW_HEAD

    def lora_inputs(b):
        xl = xl_ref[b].astype(F32)
        return (jnp.tanh(xl[:, :RW_DECAY_LORA]).astype(BF16),
                xl[:, RW_DECAY_LORA:RW_DECAY_LORA + RW_AAA_LORA].astype(BF16),
                jax.nn.sigmoid(xl[:, RW_DECAY_LORA + RW_AAA_LORA:]).astype(BF16))

    def instance(b, gi, xw, xa, xg):
        sl = slice(gi * W, (gi + 1) * W)
        r = xr_ref[b, :, sl].astype(F32)
        k = xk_ref[b, :, sl].astype(F32)
        v = xv_ref[b, :, sl].astype(F32)
        lw = -math.exp(-0.5) * jax.nn.sigmoid(w0_ref[:, sl] + _dot(xw, wup_ref[:, sl]))
        a = jax.nn.sigmoid(a0_ref[:, sl] + _dot(xa, aup_ref[:, sl]))
        g = _dot(xg, gup_ref[:, sl])
        kk = k * kk_ref[:, sl]
        ss = yield kk * kk
        kk = kk * lax.rsqrt(jnp.maximum(ss, 1e-24))
        k = k * (1.0 + (a - 1.0) * ka_ref[:, sl])
        be = kk * a
        lw_hi, lw_lo = _split_bf16(lw)
        cum = _dot(tri, lw_hi) + _dot(tri, lw_lo)
        yield
        cum_l = cum[L - 1:L, :]
        e_neg = jnp.exp(-cum)
        e_tail = jnp.exp(cum_l - cum)
        ar = jnp.concatenate([-kk * jnp.exp(cum - lw), r * jnp.exp(cum)], axis=0).astype(BF16)
        s0 = s_ref[b, gi]
        ar_s = _dot_nt(ar, s0.astype(BF16))
        a_b = _dot_nt(ar, bd(be * e_neg))
        a_k = _dot_nt(ar, bd(k * e_neg))
        yield
        n = jnp.where(strict, a_b[:L], 0.0)
        a_ak = jnp.where(strict, a_k[:L], 0.0)
        a_rb = jnp.where(incl, a_b[L:], 0.0)
        a_rk = jnp.where(incl, a_k[L:], 0.0)
        p = eye + n
        nk = _dot(n.astype(BF16), bd(n))
        bd_v = bd(v)
        x0 = _dot(a_ak.astype(BF16), bd_v)
        yield
        steps = int(math.log2(L)) - 1
        for it in range(steps):
            m = bd(nk)
            if it + 1 < steps:
                res = _dot(jnp.concatenate([p, nk], axis=0).astype(BF16), m)
                p = p + res[:L]
                nk = res[L:]
            else:
                p = p + _dot(p.astype(BF16), m)
            yield
        u = _dot(p.astype(BF16), bd(ar_s[:L] + x0))
        yield
        y = ar_s[L:] + _dot(jnp.concatenate([a_rb, a_rk], axis=1).astype(BF16),
                            jnp.concatenate([bd(u), bd_v], axis=0))
        uv = jnp.concatenate([u, v], axis=0).astype(BF16)
        bk = jnp.concatenate([be * e_tail, k * e_tail], axis=0).astype(BF16)
        s_ref[b, gi] = s0 * jnp.exp(cum_l) + _dot_tn(uv, bk) * bdf_ref[...]
        yield
        sums = yield jnp.concatenate([y, r * k * rk_ref[:, sl]], axis=0)
        d = y - sums[:L] * inv
        var = (yield d * d) * inv
        yn = d * lax.rsqrt(var + RW_LN_EPS) * lnw_ref[:, sl] + lnb_ref[:, sl]
        o_ref[b, :, sl] = ((yn + sums[L:] * v) * g).astype(o_ref.dtype)

    live = []
    for b in range(rows):
        lora = lora_inputs(b)
        live += [instance(b, gi, *lora) for gi in range(RW_DIM // W)]
    asked = [next(g) for g in live]
    while live:
        answers = head_sums(asked) if asked[0] is not None else asked
        live, asked = _advance(live, answers)


def _rw_mixer(z, b, consts):
    t = z.shape[0]
    s = t // b
    rows = math.gcd(b, 8)
    gid = jnp.arange(RW_GROUP) // RW_HEAD
    ones_f32 = (gid[:, None] == gid[None, :]).astype(F32)
    ones_bd = ones_f32.astype(BF16)
    sec = lambda w, j: pl.BlockSpec((rows, CHUNK, w), lambda bb, c: (bb, c, j))
    z3 = z.reshape(b, s, -1)
    return pl.pallas_call(
        functools.partial(_rw_kernel, rows),
        grid=(b // rows, s // CHUNK),
        in_specs=[sec(RW_DIM, OFF_RW // RW_DIM), sec(RW_DIM, OFF_RW // RW_DIM + 1),
                  sec(RW_DIM, OFF_RW // RW_DIM + 2), sec(RW_LORA, OFF_LORA // RW_LORA)]
        + [_pspec(c) for c in consts] + [_const_spec(ones_bd.shape)] * 2,
        out_specs=sec(RW_DIM, 0),
        out_shape=jax.ShapeDtypeStruct((b, s, RW_DIM), BF16),
        scratch_shapes=[pltpu.VMEM((rows, RW_DIM // RW_GROUP, RW_GROUP, RW_GROUP), F32)],
        compiler_params=_cparams("parallel", "arbitrary"),
        name="rw_mixer",
    )(z3, z3, z3, z3, *_pargs(consts), ones_bd, ones_f32)


FF_SPLIT = 4


def _post_kernel(om_ref, or_ref, oc_ref, g0_ref, g1_ref, g2_ref, h_ref, p_ref, wb_ref, wo_ref, gm_ref,
                 g1n_ref, w1_ref, w2_ref, g2n_ref, wg_ref, wp_ref, o_ref):
    def rows(rs):
        merged = None
        for n, (b_ref, zg_ref) in enumerate(((om_ref, g0_ref), (or_ref, g1_ref), (oc_ref, g2_ref))):
            y = _dot(b_ref[rs, :], wb_ref[n * BRANCH_DIM:(n + 1) * BRANCH_DIM, :])
            yield
            gate = jax.nn.sigmoid(zg_ref[rs, :].astype(F32))
            merged = gate * y if merged is None else merged + gate * y
        out = _dot(merged.astype(BF16), wo_ref[...])
        yield
        h = h_ref[rs, :] + _rms(out, gm_ref[...])
        f = _rms(h, g1n_ref[...]).astype(BF16)
        cw = D_FF // FF_SPLIT
        acc = None
        for c in range(FF_SPLIT):
            a = _dot(f, w1_ref[:, c * cw:(c + 1) * cw])
            yield
            a = jnp.maximum(a, 0.0)
            part = _dot((a * a).astype(BF16), w2_ref[c * cw:(c + 1) * cw, :])
            yield
            acc = part if acc is None else acc + part
        h = h + _rms(acc, g2n_ref[...])
        gate = _dot(h.astype(BF16), wg_ref[...])
        proj = _dot(p_ref[rs, :].astype(BF16), wp_ref[...])
        yield
        o_ref[rs, :] = h + jax.nn.sigmoid(gate) * proj

    half = h_ref.shape[0] // 2
    live = [rows(slice(0, half)), rows(slice(half, 2 * half))]
    while live:
        live = [g for g in live if next(g, True) is None]


def _post(o_mla, o_rw, o_ca, z, h, p, consts):
    t = h.shape[0]
    tm = min(o_rw.shape[1], 512)
    per_seq = o_rw.shape[1] // tm
    row = lambda w: pl.BlockSpec((tm, w), lambda i: (i, 0))
    gate = lambda n: pl.BlockSpec((tm, D_MODEL), lambda i: (i, OFF_GATE // D_MODEL + n))
    p_stack, layer = p
    return pl.pallas_call(
        _post_kernel,
        grid=(t // tm,),
        in_specs=[row(BRANCH_DIM), pl.BlockSpec((None, tm, BRANCH_DIM), lambda i: (i // per_seq, i % per_seq, 0)),
                  row(BRANCH_DIM), gate(0), gate(1), gate(2), row(D_MODEL),
                  pl.BlockSpec((None, tm, D_PLE), lambda i: (layer, i, 0))] + [_pspec(c) for c in consts],
        out_specs=row(D_MODEL),
        out_shape=jax.ShapeDtypeStruct((t, D_MODEL), F32),
        compiler_params=_cparams("parallel"),
        name="post",
    )(o_mla, o_rw, o_ca, z, z, z, h, p_stack, *_pargs(consts))


def _rot_half_cols(w):
    half = w.shape[-1] // 2
    return jnp.concatenate([-w[..., half:], w[..., :half]], axis=-1)


def _pack_w_in(w):
    mla_cols = MLA_Q_RANK + MLA_KV_RANK + MLA_ROPE
    rw_cols = 3 * RW_DIM + RW_LORA
    ca_q = (w[..., mla_cols + rw_cols:mla_cols + rw_cols + CA_DIM] * (CA_HEAD ** -0.5 * LOG2E)).astype(BF16)
    w = w.astype(BF16)
    w_mla = w[..., :mla_cols]
    w_rw = w[..., mla_cols:mla_cols + rw_cols]
    w_ca = w[..., mla_cols + rw_cols:mla_cols + rw_cols + 3 * CA_DIM]
    w_gate = w[..., mla_cols + rw_cols + 3 * CA_DIM:]
    w_kr = w_mla[..., MLA_Q_RANK + MLA_KV_RANK:]
    z64 = jnp.zeros(w.shape[:-1] + (64,), w.dtype)
    packed = jnp.concatenate(
        [w_gate, w_ca[..., CA_DIM:2 * CA_DIM], w_rw, w_mla[..., :MLA_Q_RANK + MLA_KV_RANK], w_kr, z64,
         _rot_half_cols(w_kr), z64], axis=-1)
    wt = jnp.concatenate([ca_q, w_ca[..., 2 * CA_DIM:]], axis=-1)
    return packed, jnp.swapaxes(wt, 1, 2)


def _pack_w_uq(w):
    nl, r, _ = w.shape
    w = w.reshape(nl, r, MLA_HEADS, MLA_QK)
    z64 = jnp.zeros((nl, r, MLA_HEADS, 64), w.dtype)
    rot = _rot_half_cols(w[..., MLA_NOPE:])
    wq = jnp.concatenate([w, z64], axis=-1).reshape(nl, r, MLA_HEADS * MLA_HEAD_PAD)
    wqr = jnp.concatenate([rot, z64], axis=-1).reshape(nl, r, MLA_HEADS * 128)
    return jnp.swapaxes(wq, 1, 2).astype(BF16), jnp.swapaxes(wqr, 1, 2).astype(BF16)


def _pack_w_ukv(w):
    nl, r, _ = w.shape
    w = w.reshape(nl, r, MLA_HEADS, MLA_NOPE + MLA_V)
    wk = w[..., :MLA_NOPE].reshape(nl, r, MLA_HEADS * MLA_NOPE)
    wv = w[..., MLA_NOPE:].reshape(nl, r, MLA_HEADS * MLA_V)
    return wk.astype(BF16), jnp.swapaxes(wv, 1, 2).astype(BF16)


def kernel(x, p, positions, pre_mix_g, w_in, mla_q_norm_g, mla_kv_norm_g, mla_w_uq, mla_w_ukv, rw_mu, rw_w0, rw_w_up, rw_a0, rw_a_up, rw_g_up, rw_k_k, rw_k_a, rw_r_k, rw_ln_w, rw_ln_b, ca_rel_bias, w_branch, w_out, post_mix_g, pre_ff_g, w_ff1, w_ff2, post_ff_g, w_ple_gate, w_ple_proj):
    b, s, d = x.shape
    t = b * s
    depth = w_in.shape[0]
    row = lambda a: a.reshape(depth, 1, -1).astype(F32)
    bf = lambda a: a.astype(BF16)
    w_in_p, w_in_t = _pack_w_in(w_in)
    wqt, wqrt = _pack_w_uq(mla_w_uq)
    wk, wvt = _pack_w_ukv(mla_w_ukv)
    stacks = dict(
        in_proj=[row(pre_mix_g), w_in_p, w_in_t, row(rw_mu),
                 row(mla_q_norm_g), row(mla_kv_norm_g), wqt, wqrt, wk, wvt],
        rw=[row(rw_w0), bf(rw_w_up), row(rw_a0), bf(rw_a_up), bf(rw_g_up), row(rw_k_k), row(rw_k_a),
            row(rw_r_k), row(rw_ln_w), row(rw_ln_b)],
        post=[bf(w_branch), bf(w_out), row(post_mix_g),
              row(pre_ff_g), bf(w_ff1), bf(w_ff2), row(post_ff_g), bf(w_ple_gate), bf(w_ple_proj)],
    )
    p3 = p.reshape(depth, t, -1)
    rope = _rope_table(positions)
    h = x.reshape(t, d)
    for i in range(depth):
        prm = {name: [(a, i) for a in arrs] for name, arrs in stacks.items()}
        z, zt, qt, kn, kr, vt = _in_proj(h, s, rope, prm["in_proj"])
        o_mla = _mla_attn(qt, kn, kr, vt, b)
        o_rw = _rw_mixer(z, b, prm["rw"])
        o_ca = _ca_attn(zt, z, _ca_bias_table(ca_rel_bias[i]), b)
        h = _post(o_mla, o_rw, o_ca, z, h, (p3, i), prm["post"])
    return h.reshape(b, s, d)
```

```python
import functools
import math

import jax
import jax.numpy as jnp
from jax import lax
from jax.experimental import pallas as pl
from jax.experimental.pallas import tpu as pltpu

F32 = jnp.float32
BF16 = jnp.bfloat16

D_MODEL = 1024
D_PLE = 256
D_FF = 4 * D_MODEL
NORM_EPS = 1e-6
CHUNK = 64

MLA_HEADS = 4
MLA_NOPE = 128
MLA_ROPE = 64
MLA_V = 128
MLA_Q_RANK = 256
MLA_KV_RANK = 128
ROPE_THETA = 10000.0
MLA_QK = MLA_NOPE + MLA_ROPE
MLA_HEAD_PAD = 256

RW_HEADS = 8
RW_HEAD = 64
RW_DIM = RW_HEADS * RW_HEAD
RW_DECAY_LORA = 64
RW_AAA_LORA = 64
RW_GATE_LORA = 128
RW_LORA = RW_DECAY_LORA + RW_AAA_LORA + RW_GATE_LORA
RW_LN_EPS = 64e-5

CA_HEADS = 8
CA_HEAD = 64
CA_DIM = CA_HEADS * CA_HEAD
CA_LEFT_CHUNKS = 8
CA_PAD = CA_LEFT_CHUNKS * CHUNK
REL_MIN = -(CHUNK - 1)
REL_MAX = 256

N_BRANCH = 3
BRANCH_DIM = 512
GATE_COLS = N_BRANCH * D_MODEL

OFF_GATE = 0
OFF_CAK = GATE_COLS
OFF_RW = OFF_CAK + CA_DIM
OFF_LORA = OFF_RW + 3 * RW_DIM
OFF_ZQ = OFF_LORA + RW_LORA
OFF_ZKV = OFF_ZQ + MLA_Q_RANK
OFF_KR = OFF_ZKV + MLA_KV_RANK
OFF_KRR = OFF_KR + 128
IN_PACKED = OFF_KRR + 128
LOG2E = math.log2(math.e)

VMEM_LIMIT = 56 * 1024 * 1024


def _cparams(*sem):
    return pltpu.CompilerParams(dimension_semantics=sem, vmem_limit_bytes=VMEM_LIMIT)


def _rms(x, g):
    return x * lax.rsqrt(jnp.mean(x * x, axis=-1, keepdims=True) + NORM_EPS) * g


def _dot(a, b):
    return jnp.dot(a, b, preferred_element_type=F32)


def _dot_nt(a, b):
    return lax.dot_general(a, b, (((1,), (1,)), ((), ())), preferred_element_type=F32)


def _dot_tn(a, b):
    return lax.dot_general(a, b, (((0,), (0,)), ((), ())), preferred_element_type=F32)


def _const_spec(shape):
    nd = len(shape)
    return pl.BlockSpec(shape, lambda *_: (0,) * nd, pipeline_mode=pl.Buffered(1))


def _pshape(param):
    return param[0].shape[1:]


def _pspec(param):
    stack, layer = param
    nd = stack.ndim - 1
    return pl.BlockSpec((None,) + stack.shape[1:], lambda *_: (layer,) + (0,) * nd, pipeline_mode=pl.Buffered(1))


def _pargs(params):
    return [stack for stack, _ in params]


def _advance(gens, answers):
    live, asked = [], []
    for g, ans in zip(gens, answers):
        try:
            asked.append(g.send(ans))
            live.append(g)
        except StopIteration:
            pass
    return live, asked


PIPE_DEPTH = 2
ONES_ROWS = 16


def _pipelined(items, produce, consume):
    pending, outs = [], []
    for it in items:
        pending.append((it, produce(it)))
        if len(pending) > PIPE_DEPTH:
            outs.append(consume(*pending.pop(0)))
    outs.extend(consume(*pc) for pc in pending)
    return outs


def _rope_table_kernel(pos_ref, freq_ref, cs_ref, sn_ref, cst_ref, snt_ref):
    ang = freq_ref[...] * pos_ref[...]
    dead = jnp.zeros((128 - MLA_ROPE, ang.shape[1]), F32)
    cst = jnp.concatenate([jnp.cos(ang), dead], axis=0)
    snt = jnp.concatenate([jnp.sin(ang), dead], axis=0)
    cst_ref[...] = cst
    snt_ref[...] = snt
    cs_ref[...] = cst.T
    sn_ref[...] = snt.T


def _rope_table(positions):
    t = positions.size
    tm = min(t, 2048)
    half = MLA_ROPE // 2
    inv_freq = 1.0 / (ROPE_THETA ** (jnp.arange(half, dtype=F32) / half))
    freq = jnp.concatenate([inv_freq, inv_freq])[:, None]
    return pl.pallas_call(
        _rope_table_kernel,
        grid=(t // tm,),
        in_specs=[pl.BlockSpec((1, tm), lambda i: (0, i)), _const_spec(freq.shape)],
        out_specs=[pl.BlockSpec((tm, 128), lambda i: (i, 0))] * 2 + [pl.BlockSpec((128, tm), lambda i: (0, i))] * 2,
        out_shape=[jax.ShapeDtypeStruct((t, 128), F32)] * 2 + [jax.ShapeDtypeStruct((128, t), F32)] * 2,
        compiler_params=_cparams("parallel"),
        name="rope_table",
    )(positions.astype(F32).reshape(1, t), freq)


IN_RANGES = ((0, 1536), (1536, OFF_CAK), (OFF_CAK, OFF_RW), (OFF_RW, OFF_RW + 1024), (OFF_RW + 1024, OFF_ZQ),
             (OFF_ZQ, IN_PACKED))


def _mla_project(lat, cs, sn, cst, snt, gq_ref, gkv_ref, wqt_ref, wqrt_ref, wk_ref, wvt_ref):
    qn = _rms(lat[:, :MLA_Q_RANK], gq_ref[...]).astype(BF16)
    qt = _dot_nt(wqt_ref[...], qn)
    qrt = _dot_nt(wqrt_ref[...], qn)
    scale = MLA_QK ** -0.5 * LOG2E
    parts = []
    for h in range(MLA_HEADS):
        o = h * MLA_HEAD_PAD
        rope = qt[o + 128:o + 256] * cst + qrt[h * 128:(h + 1) * 128] * snt
        parts += [(qt[o:o + 128] * scale).astype(BF16), (rope * scale).astype(BF16)]
    kvn = _rms(lat[:, MLA_Q_RANK:MLA_Q_RANK + MLA_KV_RANK], gkv_ref[...]).astype(BF16)
    k = _dot(kvn, wk_ref[...]).astype(BF16)
    vt = _dot_nt(wvt_ref[...], kvn).astype(BF16)
    kr = (lat[:, OFF_KR - OFF_ZQ:OFF_KRR - OFF_ZQ] * cs + lat[:, OFF_KRR - OFF_ZQ:] * sn).astype(BF16)
    return jnp.concatenate(parts, axis=0), k, kr, vt


def _in_proj_kernel(seq_len, x_ref, cs_ref, sn_ref, cst_ref, snt_ref, g_ref, w_ref, wt_ref, mu_ref, gq_ref, gkv_ref,
                    wqt_ref, wqrt_ref, wk_ref, wvt_ref, o_ref, ot_ref, qt_ref, k_ref, kr_ref, vt_ref, carry_ref):
    tm = x_ref.shape[0]
    half = tm // 2
    first = (pl.program_id(0) * tm) % seq_len == 0
    row0 = lax.broadcasted_iota(jnp.int32, (half, 1), 0) == 0
    edge = {}

    def rows(idx):
        rs = slice(idx * half, (idx + 1) * half)
        xn = _rms(x_ref[rs, :], g_ref[...]).astype(BF16)
        for c, e in IN_RANGES:
            res = _dot(xn, w_ref[:, c:e])
            yield
            if c >= OFF_ZQ:
                qt, k, kr, vt = _mla_project(res, cs_ref[rs, :], sn_ref[rs, :], cst_ref[:, rs], snt_ref[:, rs],
                                             gq_ref, gkv_ref, wqt_ref, wqrt_ref, wk_ref, wvt_ref)
                qt_ref[:, rs] = qt
                k_ref[rs, :] = k
                kr_ref[rs, :] = kr
                vt_ref[:, rs] = vt
                continue
            if c >= OFF_RW:
                cs = slice(c - OFF_RW, e - OFF_RW)
                if idx == 0:
                    last = jnp.where(first, 0.0, carry_ref[:, cs])
                    edge[c] = res[half - 1:half, :]
                else:
                    last = edge[c]
                    carry_ref[:, cs] = res[half - 1:half, :]
                prev = jnp.where(row0, last, pltpu.roll(res, 1, axis=0))
                res = res + (prev - res) * mu_ref[:, cs]
            o_ref[rs, c:e] = res.astype(o_ref.dtype)
        ot_ref[:, rs] = _dot_nt(wt_ref[...], xn).astype(ot_ref.dtype)

    live = [rows(0), rows(1)]
    while live:
        live = [g for g in live if next(g, True) is None]


def _in_proj(h, seq_len, tables, consts):
    t, d = h.shape
    nt = _pshape(consts[2])[0]
    tm = min(t, 512)
    hv = MLA_HEADS * MLA_V
    hq = MLA_HEADS * MLA_HEAD_PAD
    row = lambda w: pl.BlockSpec((tm, w), lambda i: (i, 0))
    col = lambda w: pl.BlockSpec((w, tm), lambda i: (0, i))
    return pl.pallas_call(
        functools.partial(_in_proj_kernel, seq_len),
        grid=(t // tm,),
        in_specs=[row(d), row(128), row(128), col(128), col(128)] + [_pspec(c) for c in consts],
        out_specs=[row(OFF_ZQ), col(nt), col(hq), row(hv), row(128), col(hv)],
        out_shape=[
            jax.ShapeDtypeStruct((t, OFF_ZQ), BF16),
            jax.ShapeDtypeStruct((nt, t), BF16),
            jax.ShapeDtypeStruct((hq, t), BF16),
            jax.ShapeDtypeStruct((t, hv), BF16),
            jax.ShapeDtypeStruct((t, 128), BF16),
            jax.ShapeDtypeStruct((hv, t), BF16),
        ],
        scratch_shapes=[pltpu.VMEM((1, OFF_ZQ - OFF_RW), F32)],
        compiler_params=_cparams("arbitrary"),
        name="in_proj",
    )(h, *tables, *_pargs(consts))


MLA_TQ = 512
MLA_GROUP = 4


def _mla_attn_kernel(qt_ref, k_ref, kr_ref, vt_ref, o_ref, m_ref, acc_ref):
    i = pl.program_id(1)
    tq = MLA_TQ
    m_ref[...] = jnp.full(m_ref.shape, -1e30, F32)
    acc_ref[...] = jnp.zeros(acc_ref.shape, F32)
    ones = jnp.ones((ONES_ROWS, tq), BF16)

    def run(blocks):
        def scores(j, masked, h):
            start = pl.multiple_of(j * tq, tq)
            kh = jnp.concatenate([k_ref[pl.ds(start, tq), h * 128:(h + 1) * 128],
                                  kr_ref[pl.ds(start, tq), :]], axis=1)
            s = _dot(kh, qt_ref[h * MLA_HEAD_PAD:(h + 1) * MLA_HEAD_PAD, :])
            if masked:
                kc = lax.broadcasted_iota(jnp.int32, s.shape, 0) // CHUNK
                qc = lax.broadcasted_iota(jnp.int32, s.shape, 1) // CHUNK
                s = jnp.where(kc <= qc, s, -1e30)
            return s

        def update(j, h, s):
            start = pl.multiple_of(j * tq, tq)
            m_prev = m_ref[h]
            m_new = jnp.maximum(m_prev, jnp.max(s, axis=0, keepdims=True))
            a = jnp.exp2(m_prev - m_new)
            p = jnp.exp2(s - m_new).astype(BF16)
            vh = jnp.concatenate([vt_ref[h * MLA_V:(h + 1) * MLA_V, pl.ds(start, tq)], ones], axis=0)
            acc_ref[h] = a * acc_ref[h] + _dot(vh, p)
            m_ref[h] = m_new

        items = [(j, masked, h) for j, masked in blocks for h in range(MLA_HEADS)]
        _pipelined(items, lambda it: scores(*it), lambda it, s: update(it[0], it[2], s))

    def body(jj, c):
        run([(MLA_GROUP * jj + n, False) for n in range(MLA_GROUP)])
        return c

    lax.fori_loop(0, i // MLA_GROUP, body, 0)
    for r in range(MLA_GROUP):
        pl.when(i % MLA_GROUP == r)(
            functools.partial(run, [(i - r + n, False) for n in range(r)] + [(i, True)]))
    for h in range(MLA_HEADS):
        acc = acc_ref[h]
        o_ref[:, h * MLA_V:(h + 1) * MLA_V] = (acc[:MLA_V] / acc[MLA_V:MLA_V + 1]).T.astype(o_ref.dtype)


def _mla_attn(qt, k, kr, vt, b):
    hq, t = qt.shape
    s = t // b
    tq = MLA_TQ
    nq = s // tq
    hv = MLA_HEADS * MLA_V
    full = lambda w: pl.BlockSpec((s, w), lambda bi, i: (bi, 0))
    return pl.pallas_call(
        _mla_attn_kernel,
        grid=(b, nq),
        in_specs=[
            pl.BlockSpec((hq, tq), lambda bi, i: (0, bi * nq + i)),
            full(hv), full(128),
            pl.BlockSpec((hv, s), lambda bi, i: (0, bi)),
        ],
        out_specs=pl.BlockSpec((tq, hv), lambda bi, i: (bi * nq + i, 0)),
        out_shape=jax.ShapeDtypeStruct((t, hv), BF16),
        scratch_shapes=[pltpu.VMEM((MLA_HEADS, 1, tq), F32),
                        pltpu.VMEM((MLA_HEADS, MLA_V + ONES_ROWS, tq), F32)],
        compiler_params=_cparams("parallel", "arbitrary"),
        name="mla_attn",
    )(qt, k, kr, vt)


CA_TQ = 256
CA_BAND_BLK = CA_TQ + CA_PAD


CA_PER_STEP = 4


def _ca_attn_kernel(qt_ref, k_ref, vt_ref, bias_ref, o_ref):
    i = pl.program_id(1)
    lo = lax.broadcasted_iota(jnp.int32, (128, CA_TQ), 0) < CA_HEAD
    zero = jnp.zeros((), BF16)

    def work(blocks):
        def scores(item):
            (sub, k_start, n_keys, bias_off), h = item
            sl = slice((h // 2) * 128, (h // 2 + 1) * 128)
            k2 = k_ref[pl.ds(k_start, n_keys), sl]
            qh = jnp.where(lo if h % 2 == 0 else ~lo, qt_ref[sl, sub * CA_TQ:(sub + 1) * CA_TQ], zero)
            return _dot(k2, qh) + bias_ref[h, bias_off:bias_off + n_keys, :]

        def attend(item, s):
            (_, k_start, n_keys, _), h = item
            p = jnp.exp2(s - jnp.max(s, axis=0, keepdims=True)).astype(BF16)
            vh = jnp.concatenate([vt_ref[h * CA_HEAD:(h + 1) * CA_HEAD, pl.ds(k_start, n_keys)],
                                  jnp.ones((ONES_ROWS, n_keys), BF16)], axis=0)
            out = _dot(vh, p)
            return out[:CA_HEAD] / out[CA_HEAD:CA_HEAD + 1]

        outs = _pipelined([(blk, h) for blk in blocks for h in range(CA_HEADS)], scores, attend)
        for n, blk in enumerate(blocks):
            rows = slice(blk[0] * CA_TQ, (blk[0] + 1) * CA_TQ)
            for pair in range(CA_HEADS // 2):
                both = jnp.concatenate(outs[n * CA_HEADS + 2 * pair:n * CA_HEADS + 2 * pair + 2], axis=0)
                o_ref[rows, pair * 128:(pair + 1) * 128] = both.T.astype(o_ref.dtype)

    lead = CA_PAD // CA_TQ
    lead_steps = -(-lead // CA_PER_STEP)
    for j in range(lead_steps):
        blocks = []
        for sub in range(CA_PER_STEP):
            q = j * CA_PER_STEP + sub
            blocks.append((sub, 0, (q + 1) * CA_TQ, CA_PAD - q * CA_TQ) if q < lead
                          else (sub, q * CA_TQ - CA_PAD, CA_BAND_BLK, 0))
        pl.when(i == j)(functools.partial(work, blocks))
    pl.when(i >= lead_steps)(lambda: work(
        [(sub, pl.multiple_of((i * CA_PER_STEP + sub) * CA_TQ - CA_PAD, CA_TQ), CA_BAND_BLK, 0)
         for sub in range(CA_PER_STEP)]))


def _ca_attn(zt, z, bias, b):
    t = zt.shape[1]
    s = t // b
    tq = CA_PER_STEP * CA_TQ
    nq = s // tq
    return pl.pallas_call(
        _ca_attn_kernel,
        grid=(b, nq),
        in_specs=[
            pl.BlockSpec((CA_DIM, tq), lambda bi, i: (0, bi * nq + i)),
            pl.BlockSpec((s, CA_DIM), lambda bi, i: (bi, OFF_CAK // CA_DIM)),
            pl.BlockSpec((CA_DIM, s), lambda bi, i: (1, bi)),
            _const_spec(bias.shape),
        ],
        out_specs=pl.BlockSpec((tq, CA_DIM), lambda bi, i: (bi * nq + i, 0)),
        out_shape=jax.ShapeDtypeStruct((t, CA_DIM), BF16),
        compiler_params=_cparams("parallel", "arbitrary"),
        name="ca_attn",
    )(zt, z, zt, bias)


def _ca_bias_kernel(t_ref, o_ref):
    width = t_ref.shape[-1]
    rows = pltpu.roll(jnp.broadcast_to(t_ref[0], (CA_TQ, width)), 0, 1, stride=1, stride_axis=0)
    rc = lax.broadcasted_iota(jnp.int32, (CA_TQ, CA_BAND_BLK), 0) // CHUNK
    cc = lax.broadcasted_iota(jnp.int32, (CA_TQ, CA_BAND_BLK), 1) // CHUNK
    ok = (cc >= rc) & (cc <= rc + CA_LEFT_CHUNKS)
    o_ref[0] = jnp.where(ok, rows[:, :CA_BAND_BLK], -1e30).T


def _ca_bias_table(rel_bias):
    width = CA_TQ + CA_BAND_BLK
    m = jnp.arange(width)
    delta = jnp.where(m < CA_BAND_BLK, m, m - width)
    idx = jnp.clip(CA_PAD - delta, REL_MIN, REL_MAX) - REL_MIN
    t1 = (rel_bias.astype(F32)[idx].T * LOG2E)[:, None, :]
    return pl.pallas_call(
        _ca_bias_kernel,
        grid=(CA_HEADS,),
        in_specs=[pl.BlockSpec((1, 1, width), lambda h: (h, 0, 0))],
        out_specs=pl.BlockSpec((1, CA_BAND_BLK, CA_TQ), lambda h: (h, 0, 0)),
        out_shape=jax.ShapeDtypeStruct((CA_HEADS, CA_BAND_BLK, CA_TQ), F32),
        compiler_params=_cparams("parallel"),
        name="ca_bias",
    )(t1)


def _split_bf16(x):
    hi = x.astype(BF16)
    lo = (x - hi.astype(F32)).astype(BF16)
    return hi, lo


RW_GROUP = 256


def _rw_kernel(rows, xr_ref, xk_ref, xv_ref, xl_ref, w0_ref, wup_ref, a0_ref, aup_ref, gup_ref, kk_ref, ka_ref,
               rk_ref, lnw_ref, lnb_ref, bd_ref, bdf_ref, o_ref, s_ref):
    L = CHUNK
    W = RW_GROUP
    reps = W // RW_HEAD

    @pl.when(pl.program_id(1) == 0)
    def _():
        s_ref[...] = jnp.zeros(s_ref.shape, F32)

    bdm = bd_ref[...]

    def bd(x):
        return jnp.concatenate([x.astype(BF16)] * reps, axis=0) * bdm

    def head_sums(xs):
        out = _dot(jnp.concatenate([x.astype(BF16) for x in xs], axis=0), bdm)
        offs = [0]
        for x in xs:
            offs.append(offs[-1] + x.shape[0])
        return [out[lo:hi] for lo, hi in zip(offs[:-1], offs[1:])]

    row = lax.broadcasted_iota(jnp.int32, (L, W), 0)
    sub = lax.broadcasted_iota(jnp.int32, (L, W), 1) % RW_HEAD
    strict = sub < row
    incl = sub <= row
    eye = (sub == row).astype(F32)
    tri = (lax.broadcasted_iota(jnp.int32, (L, L), 1)
           <= lax.broadcasted_iota(jnp.int32, (L, L), 0)).astype(BF16)
    inv = 1.0 / RW_HEAD

    def lora_inputs(b):
        xl = xl_ref[b].astype(F32)
        return (jnp.tanh(xl[:, :RW_DECAY_LORA]).astype(BF16),
                xl[:, RW_DECAY_LORA:RW_DECAY_LORA + RW_AAA_LORA].astype(BF16),
                jax.nn.sigmoid(xl[:, RW_DECAY_LORA + RW_AAA_LORA:]).astype(BF16))

    def instance(b, gi, xw, xa, xg):
        sl = slice(gi * W, (gi + 1) * W)
        r = xr_ref[b, :, sl].astype(F32)
        k = xk_ref[b, :, sl].astype(F32)
        v = xv_ref[b, :, sl].astype(F32)
        lw = -math.exp(-0.5) * jax.nn.sigmoid(w0_ref[:, sl] + _dot(xw, wup_ref[:, sl]))
        a = jax.nn.sigmoid(a0_ref[:, sl] + _dot(xa, aup_ref[:, sl]))
        g = _dot(xg, gup_ref[:, sl])
        kk = k * kk_ref[:, sl]
        ss = yield kk * kk
        kk = kk * lax.rsqrt(jnp.maximum(ss, 1e-24))
        k = k * (1.0 + (a - 1.0) * ka_ref[:, sl])
        be = kk * a
        lw_hi, lw_lo = _split_bf16(lw)
        cum = _dot(tri, lw_hi) + _dot(tri, lw_lo)
        yield
        cum_l = cum[L - 1:L, :]
        e_neg = jnp.exp(-cum)
        e_tail = jnp.exp(cum_l - cum)
        ar = jnp.concatenate([-kk * jnp.exp(cum - lw), r * jnp.exp(cum)], axis=0).astype(BF16)
        s0 = s_ref[b, gi]
        ar_s = _dot_nt(ar, s0.astype(BF16))
        a_b = _dot_nt(ar, bd(be * e_neg))
        a_k = _dot_nt(ar, bd(k * e_neg))
        yield
        n = jnp.where(strict, a_b[:L], 0.0)
        a_ak = jnp.where(strict, a_k[:L], 0.0)
        a_rb = jnp.where(incl, a_b[L:], 0.0)
        a_rk = jnp.where(incl, a_k[L:], 0.0)
        p = eye + n
        nk = _dot(n.astype(BF16), bd(n))
        bd_v = bd(v)
        x0 = _dot(a_ak.astype(BF16), bd_v)
        yield
        steps = int(math.log2(L)) - 1
        for it in range(steps):
            m = bd(nk)
            if it + 1 < steps:
                res = _dot(jnp.concatenate([p, nk], axis=0).astype(BF16), m)
                p = p + res[:L]
                nk = res[L:]
            else:
                p = p + _dot(p.astype(BF16), m)
            yield
        u = _dot(p.astype(BF16), bd(ar_s[:L] + x0))
        yield
        y = ar_s[L:] + _dot(jnp.concatenate([a_rb, a_rk], axis=1).astype(BF16),
                            jnp.concatenate([bd(u), bd_v], axis=0))
        uv = jnp.concatenate([u, v], axis=0).astype(BF16)
        bk = jnp.concatenate([be * e_tail, k * e_tail], axis=0).astype(BF16)
        s_ref[b, gi] = s0 * jnp.exp(cum_l) + _dot_tn(uv, bk) * bdf_ref[...]
        yield
        sums = yield jnp.concatenate([y, r * k * rk_ref[:, sl]], axis=0)
        d = y - sums[:L] * inv
        var = (yield d * d) * inv
        yn = d * lax.rsqrt(var + RW_LN_EPS) * lnw_ref[:, sl] + lnb_ref[:, sl]
        o_ref[b, :, sl] = ((yn + sums[L:] * v) * g).astype(o_ref.dtype)

    live = []
    for b in range(rows):
        lora = lora_inputs(b)
        live += [instance(b, gi, *lora) for gi in range(RW_DIM // W)]
    asked = [next(g) for g in live]
    while live:
        answers = head_sums(asked) if asked[0] is not None else asked
        live, asked = _advance(live, answers)


def _rw_mixer(z, b, consts):
    t = z.shape[0]
    s = t // b
    rows = math.gcd(b, 8)
    gid = jnp.arange(RW_GROUP) // RW_HEAD
    ones_f32 = (gid[:, None] == gid[None, :]).astype(F32)
    ones_bd = ones_f32.astype(BF16)
    sec = lambda w, j: pl.BlockSpec((rows, CHUNK, w), lambda bb, c: (bb, c, j))
    z3 = z.reshape(b, s, -1)
    return pl.pallas_call(
        functools.partial(_rw_kernel, rows),
        grid=(b // rows, s // CHUNK),
        in_specs=[sec(RW_DIM, OFF_RW // RW_DIM), sec(RW_DIM, OFF_RW // RW_DIM + 1),
                  sec(RW_DIM, OFF_RW // RW_DIM + 2), sec(RW_LORA, OFF_LORA // RW_LORA)]
        + [_pspec(c) for c in consts] + [_const_spec(ones_bd.shape)] * 2,
        out_specs=sec(RW_DIM, 0),
        out_shape=jax.ShapeDtypeStruct((b, s, RW_DIM), BF16),
        scratch_shapes=[pltpu.VMEM((rows, RW_DIM // RW_GROUP, RW_GROUP, RW_GROUP), F32)],
        compiler_params=_cparams("parallel", "arbitrary"),
        name="rw_mixer",
    )(z3, z3, z3, z3, *_pargs(consts), ones_bd, ones_f32)


FF_SPLIT = 4


def _post_kernel(om_ref, or_ref, oc_ref, g0_ref, g1_ref, g2_ref, h_ref, p_ref, wb_ref, wo_ref, gm_ref,
                 g1n_ref, w1_ref, w2_ref, g2n_ref, wg_ref, wp_ref, o_ref):
    def rows(rs):
        merged = None
        for n, (b_ref, zg_ref) in enumerate(((om_ref, g0_ref), (or_ref, g1_ref), (oc_ref, g2_ref))):
            y = _dot(b_ref[rs, :], wb_ref[n * BRANCH_DIM:(n + 1) * BRANCH_DIM, :])
            yield
            gate = jax.nn.sigmoid(zg_ref[rs, :].astype(F32))
            merged = gate * y if merged is None else merged + gate * y
        out = _dot(merged.astype(BF16), wo_ref[...])
        yield
        h = h_ref[rs, :] + _rms(out, gm_ref[...])
        f = _rms(h, g1n_ref[...]).astype(BF16)
        cw = D_FF // FF_SPLIT
        acc = None
        for c in range(FF_SPLIT):
            a = _dot(f, w1_ref[:, c * cw:(c + 1) * cw])
            yield
            a = jnp.maximum(a, 0.0)
            part = _dot((a * a).astype(BF16), w2_ref[c * cw:(c + 1) * cw, :])
            yield
            acc = part if acc is None else acc + part
        h = h + _rms(acc, g2n_ref[...])
        gate = _dot(h.astype(BF16), wg_ref[...])
        proj = _dot(p_ref[rs, :].astype(BF16), wp_ref[...])
        yield
        o_ref[rs, :] = h + jax.nn.sigmoid(gate) * proj

    half = h_ref.shape[0] // 2
    live = [rows(slice(0, half)), rows(slice(half, 2 * half))]
    while live:
        live = [g for g in live if next(g, True) is None]


def _post(o_mla, o_rw, o_ca, z, h, p, consts):
    t = h.shape[0]
    tm = min(o_rw.shape[1], 512)
    per_seq = o_rw.shape[1] // tm
    row = lambda w: pl.BlockSpec((tm, w), lambda i: (i, 0))
    gate = lambda n: pl.BlockSpec((tm, D_MODEL), lambda i: (i, OFF_GATE // D_MODEL + n))
    p_stack, layer = p
    return pl.pallas_call(
        _post_kernel,
        grid=(t // tm,),
        in_specs=[row(BRANCH_DIM), pl.BlockSpec((None, tm, BRANCH_DIM), lambda i: (i // per_seq, i % per_seq, 0)),
                  row(BRANCH_DIM), gate(0), gate(1), gate(2), row(D_MODEL),
                  pl.BlockSpec((None, tm, D_PLE), lambda i: (layer, i, 0))] + [_pspec(c) for c in consts],
        out_specs=row(D_MODEL),
        out_shape=jax.ShapeDtypeStruct((t, D_MODEL), F32),
        compiler_params=_cparams("parallel"),
        name="post",
    )(o_mla, o_rw, o_ca, z, z, z, h, p_stack, *_pargs(consts))


def _rot_half_cols(w):
    half = w.shape[-1] // 2
    return jnp.concatenate([-w[..., half:], w[..., :half]], axis=-1)


def _pack_w_in(w):
    mla_cols = MLA_Q_RANK + MLA_KV_RANK + MLA_ROPE
    rw_cols = 3 * RW_DIM + RW_LORA
    ca_q = (w[..., mla_cols + rw_cols:mla_cols + rw_cols + CA_DIM] * (CA_HEAD ** -0.5 * LOG2E)).astype(BF16)
    w = w.astype(BF16)
    w_mla = w[..., :mla_cols]
    w_rw = w[..., mla_cols:mla_cols + rw_cols]
    w_ca = w[..., mla_cols + rw_cols:mla_cols + rw_cols + 3 * CA_DIM]
    w_gate = w[..., mla_cols + rw_cols + 3 * CA_DIM:]
    w_kr = w_mla[..., MLA_Q_RANK + MLA_KV_RANK:]
    z64 = jnp.zeros(w.shape[:-1] + (64,), w.dtype)
    packed = jnp.concatenate(
        [w_gate, w_ca[..., CA_DIM:2 * CA_DIM], w_rw, w_mla[..., :MLA_Q_RANK + MLA_KV_RANK], w_kr, z64,
         _rot_half_cols(w_kr), z64], axis=-1)
    wt = jnp.concatenate([ca_q, w_ca[..., 2 * CA_DIM:]], axis=-1)
    return packed, jnp.swapaxes(wt, 1, 2)


def _pack_w_uq(w):
    nl, r, _ = w.shape
    w = w.reshape(nl, r, MLA_HEADS, MLA_QK)
    z64 = jnp.zeros((nl, r, MLA_HEADS, 64), w.dtype)
    rot = _rot_half_cols(w[..., MLA_NOPE:])
    wq = jnp.concatenate([w, z64], axis=-1).reshape(nl, r, MLA_HEADS * MLA_HEAD_PAD)
    wqr = jnp.concatenate([rot, z64], axis=-1).reshape(nl, r, MLA_HEADS * 128)
    return jnp.swapaxes(wq, 1, 2).astype(BF16), jnp.swapaxes(wqr, 1, 2).astype(BF16)


def _pack_w_ukv(w):
    nl, r, _ = w.shape
    w = w.reshape(nl, r, MLA_HEADS, MLA_NOPE + MLA_V)
    wk = w[..., :MLA_NOPE].reshape(nl, r, MLA_HEADS * MLA_NOPE)
    wv = w[..., MLA_NOPE:].reshape(nl, r, MLA_HEADS * MLA_V)
    return wk.astype(BF16), jnp.swapaxes(wv, 1, 2).astype(BF16)


def kernel(x, p, positions, pre_mix_g, w_in, mla_q_norm_g, mla_kv_norm_g, mla_w_uq, mla_w_ukv, rw_mu, rw_w0, rw_w_up, rw_a0, rw_a_up, rw_g_up, rw_k_k, rw_k_a, rw_r_k, rw_ln_w, rw_ln_b, ca_rel_bias, w_branch, w_out, post_mix_g, pre_ff_g, w_ff1, w_ff2, post_ff_g, w_ple_gate, w_ple_proj):
    b, s, d = x.shape
    t = b * s
    depth = w_in.shape[0]
    row = lambda a: a.reshape(depth, 1, -1).astype(F32)
    bf = lambda a: a.astype(BF16)
    w_in_p, w_in_t = _pack_w_in(w_in)
    wqt, wqrt = _pack_w_uq(mla_w_uq)
    wk, wvt = _pack_w_ukv(mla_w_ukv)
    stacks = dict(
        in_proj=[row(pre_mix_g), w_in_p, w_in_t, row(rw_mu),
                 row(mla_q_norm_g), row(mla_kv_norm_g), wqt, wqrt, wk, wvt],
        rw=[row(rw_w0), bf(rw_w_up), row(rw_a0), bf(rw_a_up), bf(rw_g_up), row(rw_k_k), row(rw_k_a),
            row(rw_r_k), row(rw_ln_w), row(rw_ln_b)],
        post=[bf(w_branch), bf(w_out), row(post_mix_g),
              row(pre_ff_g), bf(w_ff1), bf(w_ff2), row(post_ff_g), bf(w_ple_gate), bf(w_ple_proj)],
    )
    p3 = p.reshape(depth, t, -1)
    rope = _rope_table(positions)
    h = x.reshape(t, d)
    for i in range(depth):
        prm = {name: [(a, i) for a in arrs] for name, arrs in stacks.items()}
        z, zt, qt, kn, kr, vt = _in_proj(h, s, rope, prm["in_proj"])
        o_mla = _mla_attn(qt, kn, kr, vt, b)
        o_rw = _rw_mixer(z, b, prm["rw"])
        o_ca = _ca_attn(zt, z, _ca_bias_table(ca_rel_bias[i]), b)
        h = _post(o_mla, o_rw, o_ca, z, h, (p3, i), prm["post"])
    return h.reshape(b, s, d)
```

```python
import functools
import math

import jax
import jax.numpy as jnp
from jax import lax
from jax.experimental import pallas as pl
from jax.experimental.pallas import tpu as pltpu

F32 = jnp.float32
BF16 = jnp.bfloat16

D_MODEL = 1024
D_PLE = 256
D_FF = 4 * D_MODEL
NORM_EPS = 1e-6
CHUNK = 64

MLA_HEADS = 4
MLA_NOPE = 128
MLA_ROPE = 64
MLA_V = 128
MLA_Q_RANK = 256
MLA_KV_RANK = 128
ROPE_THETA = 10000.0
MLA_QK = MLA_NOPE + MLA_ROPE
MLA_HEAD_PAD = 256

RW_HEADS = 8
RW_HEAD = 64
RW_DIM = RW_HEADS * RW_HEAD
RW_DECAY_LORA = 64
RW_AAA_LORA = 64
RW_GATE_LORA = 128
RW_LORA = RW_DECAY_LORA + RW_AAA_LORA + RW_GATE_LORA
RW_LN_EPS = 64e-5

CA_HEADS = 8
CA_HEAD = 64
CA_DIM = CA_HEADS * CA_HEAD
CA_LEFT_CHUNKS = 8
CA_PAD = CA_LEFT_CHUNKS * CHUNK
REL_MIN = -(CHUNK - 1)
REL_MAX = 256

N_BRANCH = 3
BRANCH_DIM = 512
GATE_COLS = N_BRANCH * D_MODEL

OFF_GATE = 0
OFF_CAK = GATE_COLS
OFF_RW = OFF_CAK + CA_DIM
OFF_LORA = OFF_RW + 3 * RW_DIM
OFF_ZQ = OFF_LORA + RW_LORA
OFF_ZKV = OFF_ZQ + MLA_Q_RANK
OFF_KR = OFF_ZKV + MLA_KV_RANK
OFF_KRR = OFF_KR + 128
IN_PACKED = OFF_KRR + 128
LOG2E = math.log2(math.e)

VMEM_LIMIT = 56 * 1024 * 1024


def _cparams(*sem):
    return pltpu.CompilerParams(dimension_semantics=sem, vmem_limit_bytes=VMEM_LIMIT)


def _rms(x, g):
    return x * lax.rsqrt(jnp.mean(x * x, axis=-1, keepdims=True) + NORM_EPS) * g


def _dot(a, b):
    return jnp.dot(a, b, preferred_element_type=F32)


def _dot_nt(a, b):
    return lax.dot_general(a, b, (((1,), (1,)), ((), ())), preferred_element_type=F32)


def _dot_tn(a, b):
    return lax.dot_general(a, b, (((0,), (0,)), ((), ())), preferred_element_type=F32)


def _const_spec(shape):
    nd = len(shape)
    return pl.BlockSpec(shape, lambda *_: (0,) * nd, pipeline_mode=pl.Buffered(1))


def _pshape(param):
    return param[0].shape[1:]


def _pspec(param):
    stack, layer = param
    nd = stack.ndim - 1
    return pl.BlockSpec((None,) + stack.shape[1:], lambda *_: (layer,) + (0,) * nd, pipeline_mode=pl.Buffered(1))


def _pargs(params):
    return [stack for stack, _ in params]


def _advance(gens, answers):
    live, asked = [], []
    for g, ans in zip(gens, answers):
        try:
            asked.append(g.send(ans))
            live.append(g)
        except StopIteration:
            pass
    return live, asked


PIPE_DEPTH = 2
ONES_ROWS = 16


def _pipelined(items, produce, consume):
    pending, outs = [], []
    for it in items:
        pending.append((it, produce(it)))
        if len(pending) > PIPE_DEPTH:
            outs.append(consume(*pending.pop(0)))
    outs.extend(consume(*pc) for pc in pending)
    return outs


def _rope_table_kernel(pos_ref, freq_ref, cs_ref, sn_ref, cst_ref, snt_ref):
    ang = freq_ref[...] * pos_ref[...]
    dead = jnp.zeros((128 - MLA_ROPE, ang.shape[1]), F32)
    cst = jnp.concatenate([jnp.cos(ang), dead], axis=0)
    snt = jnp.concatenate([jnp.sin(ang), dead], axis=0)
    cst_ref[...] = cst
    snt_ref[...] = snt
    cs_ref[...] = cst.T
    sn_ref[...] = snt.T


def _rope_table(positions):
    t = positions.size
    tm = min(t, 2048)
    half = MLA_ROPE // 2
    inv_freq = 1.0 / (ROPE_THETA ** (jnp.arange(half, dtype=F32) / half))
    freq = jnp.concatenate([inv_freq, inv_freq])[:, None]
    return pl.pallas_call(
        _rope_table_kernel,
        grid=(t // tm,),
        in_specs=[pl.BlockSpec((1, tm), lambda i: (0, i)), _const_spec(freq.shape)],
        out_specs=[pl.BlockSpec((tm, 128), lambda i: (i, 0))] * 2 + [pl.BlockSpec((128, tm), lambda i: (0, i))] * 2,
        out_shape=[jax.ShapeDtypeStruct((t, 128), F32)] * 2 + [jax.ShapeDtypeStruct((128, t), F32)] * 2,
        compiler_params=_cparams("parallel"),
        name="rope_table",
    )(positions.astype(F32).reshape(1, t), freq)


IN_RANGES = ((0, 1536), (1536, OFF_CAK), (OFF_CAK, OFF_RW), (OFF_RW, OFF_RW + 1024), (OFF_RW + 1024, OFF_ZQ),
             (OFF_ZQ, IN_PACKED))


def _mla_project(lat, cs, sn, cst, snt, gq_ref, gkv_ref, wqt_ref, wqrt_ref, wk_ref, wvt_ref):
    qn = _rms(lat[:, :MLA_Q_RANK], gq_ref[...]).astype(BF16)
    qt = _dot_nt(wqt_ref[...], qn)
    qrt = _dot_nt(wqrt_ref[...], qn)
    scale = MLA_QK ** -0.5 * LOG2E
    parts = []
    for h in range(MLA_HEADS):
        o = h * MLA_HEAD_PAD
        rope = qt[o + 128:o + 256] * cst + qrt[h * 128:(h + 1) * 128] * snt
        parts += [(qt[o:o + 128] * scale).astype(BF16), (rope * scale).astype(BF16)]
    kvn = _rms(lat[:, MLA_Q_RANK:MLA_Q_RANK + MLA_KV_RANK], gkv_ref[...]).astype(BF16)
    k = _dot(kvn, wk_ref[...]).astype(BF16)
    vt = _dot_nt(wvt_ref[...], kvn).astype(BF16)
    kr = (lat[:, OFF_KR - OFF_ZQ:OFF_KRR - OFF_ZQ] * cs + lat[:, OFF_KRR - OFF_ZQ:] * sn).astype(BF16)
    return jnp.concatenate(parts, axis=0), k, kr, vt


def _in_proj_kernel(seq_len, x_ref, cs_ref, sn_ref, cst_ref, snt_ref, g_ref, w_ref, wt_ref, mu_ref, gq_ref, gkv_ref,
                    wqt_ref, wqrt_ref, wk_ref, wvt_ref, o_ref, ot_ref, qt_ref, k_ref, kr_ref, vt_ref, carry_ref):
    tm = x_ref.shape[0]
    half = tm // 2
    first = (pl.program_id(0) * tm) % seq_len == 0
    row0 = lax.broadcasted_iota(jnp.int32, (half, 1), 0) == 0
    edge = {}

    def rows(idx):
        rs = slice(idx * half, (idx + 1) * half)
        xn = _rms(x_ref[rs, :], g_ref[...]).astype(BF16)
        for c, e in IN_RANGES:
            res = _dot(xn, w_ref[:, c:e])
            yield
            if c >= OFF_ZQ:
                qt, k, kr, vt = _mla_project(res, cs_ref[rs, :], sn_ref[rs, :], cst_ref[:, rs], snt_ref[:, rs],
                                             gq_ref, gkv_ref, wqt_ref, wqrt_ref, wk_ref, wvt_ref)
                qt_ref[:, rs] = qt
                k_ref[rs, :] = k
                kr_ref[rs, :] = kr
                vt_ref[:, rs] = vt
                continue
            if c >= OFF_RW:
                cs = slice(c - OFF_RW, e - OFF_RW)
                if idx == 0:
                    last = jnp.where(first, 0.0, carry_ref[:, cs])
                    edge[c] = res[half - 1:half, :]
                else:
                    last = edge[c]
                    carry_ref[:, cs] = res[half - 1:half, :]
                prev = jnp.where(row0, last, pltpu.roll(res, 1, axis=0))
                res = res + (prev - res) * mu_ref[:, cs]
            o_ref[rs, c:e] = res.astype(o_ref.dtype)
        ot_ref[:, rs] = _dot_nt(wt_ref[...], xn).astype(ot_ref.dtype)

    live = [rows(0), rows(1)]
    while live:
        live = [g for g in live if next(g, True) is None]


def _in_proj(h, seq_len, tables, consts):
    t, d = h.shape
    nt = _pshape(consts[2])[0]
    tm = min(t, 512)
    hv = MLA_HEADS * MLA_V
    hq = MLA_HEADS * MLA_HEAD_PAD
    row = lambda w: pl.BlockSpec((tm, w), lambda i: (i, 0))
    col = lambda w: pl.BlockSpec((w, tm), lambda i: (0, i))
    return pl.pallas_call(
        functools.partial(_in_proj_kernel, seq_len),
        grid=(t // tm,),
        in_specs=[row(d), row(128), row(128), col(128), col(128)] + [_pspec(c) for c in consts],
        out_specs=[row(OFF_ZQ), col(nt), col(hq), row(hv), row(128), col(hv)],
        out_shape=[
            jax.ShapeDtypeStruct((t, OFF_ZQ), BF16),
            jax.ShapeDtypeStruct((nt, t), BF16),
            jax.ShapeDtypeStruct((hq, t), BF16),
            jax.ShapeDtypeStruct((t, hv), BF16),
            jax.ShapeDtypeStruct((t, 128), BF16),
            jax.ShapeDtypeStruct((hv, t), BF16),
        ],
        scratch_shapes=[pltpu.VMEM((1, OFF_ZQ - OFF_RW), F32)],
        compiler_params=_cparams("arbitrary"),
        name="in_proj",
    )(h, *tables, *_pargs(consts))


MLA_TQ = 512
MLA_GROUP = 4


def _mla_attn_kernel(qt_ref, k_ref, kr_ref, vt_ref, o_ref, m_ref, acc_ref):
    i = pl.program_id(1)
    tq = MLA_TQ
    m_ref[...] = jnp.full(m_ref.shape, -1e30, F32)
    acc_ref[...] = jnp.zeros(acc_ref.shape, F32)
    ones = jnp.ones((ONES_ROWS, tq), BF16)

    def run(blocks):
        def scores(j, masked, h):
            start = pl.multiple_of(j * tq, tq)
            kh = jnp.concatenate([k_ref[pl.ds(start, tq), h * 128:(h + 1) * 128],
                                  kr_ref[pl.ds(start, tq), :]], axis=1)
            s = _dot(kh, qt_ref[h * MLA_HEAD_PAD:(h + 1) * MLA_HEAD_PAD, :])
            if masked:
                kc = lax.broadcasted_iota(jnp.int32, s.shape, 0) // CHUNK
                qc = lax.broadcasted_iota(jnp.int32, s.shape, 1) // CHUNK
                s = jnp.where(kc <= qc, s, -1e30)
            return s

        def update(j, h, s):
            start = pl.multiple_of(j * tq, tq)
            m_prev = m_ref[h]
            m_new = jnp.maximum(m_prev, jnp.max(s, axis=0, keepdims=True))
            a = jnp.exp2(m_prev - m_new)
            p = jnp.exp2(s - m_new).astype(BF16)
            vh = jnp.concatenate([vt_ref[h * MLA_V:(h + 1) * MLA_V, pl.ds(start, tq)], ones], axis=0)
            acc_ref[h] = a * acc_ref[h] + _dot(vh, p)
            m_ref[h] = m_new

        items = [(j, masked, h) for j, masked in blocks for h in range(MLA_HEADS)]
        _pipelined(items, lambda it: scores(*it), lambda it, s: update(it[0], it[2], s))

    def body(jj, c):
        run([(MLA_GROUP * jj + n, False) for n in range(MLA_GROUP)])
        return c

    lax.fori_loop(0, i // MLA_GROUP, body, 0)
    for r in range(MLA_GROUP):
        pl.when(i % MLA_GROUP == r)(
            functools.partial(run, [(i - r + n, False) for n in range(r)] + [(i, True)]))
    for h in range(MLA_HEADS):
        acc = acc_ref[h]
        o_ref[:, h * MLA_V:(h + 1) * MLA_V] = (acc[:MLA_V] / acc[MLA_V:MLA_V + 1]).T.astype(o_ref.dtype)


def _mla_attn(qt, k, kr, vt, b):
    hq, t = qt.shape
    s = t // b
    tq = MLA_TQ
    nq = s // tq
    hv = MLA_HEADS * MLA_V
    full = lambda w: pl.BlockSpec((s, w), lambda bi, i: (bi, 0))
    return pl.pallas_call(
        _mla_attn_kernel,
        grid=(b, nq),
        in_specs=[
            pl.BlockSpec((hq, tq), lambda bi, i: (0, bi * nq + i)),
            full(hv), full(128),
            pl.BlockSpec((hv, s), lambda bi, i: (0, bi)),
        ],
        out_specs=pl.BlockSpec((tq, hv), lambda bi, i: (bi * nq + i, 0)),
        out_shape=jax.ShapeDtypeStruct((t, hv), BF16),
        scratch_shapes=[pltpu.VMEM((MLA_HEADS, 1, tq), F32),
                        pltpu.VMEM((MLA_HEADS, MLA_V + ONES_ROWS, tq), F32)],
        compiler_params=_cparams("parallel", "arbitrary"),
        name="mla_attn",
    )(qt, k, kr, vt)


CA_TQ = 256
CA_BAND_BLK = CA_TQ + CA_PAD


CA_PER_STEP = 8


def _ca_attn_kernel(qt_ref, k_ref, vt_ref, bias_ref, o_ref):
    i = pl.program_id(1)
    lo = lax.broadcasted_iota(jnp.int32, (128, CA_TQ), 0) < CA_HEAD
    zero = jnp.zeros((), BF16)

    def work(blocks):
        def scores(item):
            (sub, k_start, n_keys, bias_off), h = item
            sl = slice((h // 2) * 128, (h // 2 + 1) * 128)
            k2 = k_ref[pl.ds(k_start, n_keys), sl]
            qh = jnp.where(lo if h % 2 == 0 else ~lo, qt_ref[sl, sub * CA_TQ:(sub + 1) * CA_TQ], zero)
            return _dot(k2, qh) + bias_ref[h, bias_off:bias_off + n_keys, :]

        def attend(item, s):
            (_, k_start, n_keys, _), h = item
            p = jnp.exp2(s - jnp.max(s, axis=0, keepdims=True)).astype(BF16)
            vh = jnp.concatenate([vt_ref[h * CA_HEAD:(h + 1) * CA_HEAD, pl.ds(k_start, n_keys)],
                                  jnp.ones((ONES_ROWS, n_keys), BF16)], axis=0)
            out = _dot(vh, p)
            return out[:CA_HEAD] / out[CA_HEAD:CA_HEAD + 1]

        outs = _pipelined([(blk, h) for blk in blocks for h in range(CA_HEADS)], scores, attend)
        for n, blk in enumerate(blocks):
            rows = slice(blk[0] * CA_TQ, (blk[0] + 1) * CA_TQ)
            for pair in range(CA_HEADS // 2):
                both = jnp.concatenate(outs[n * CA_HEADS + 2 * pair:n * CA_HEADS + 2 * pair + 2], axis=0)
                o_ref[rows, pair * 128:(pair + 1) * 128] = both.T.astype(o_ref.dtype)

    lead = CA_PAD // CA_TQ
    lead_steps = -(-lead // CA_PER_STEP)
    for j in range(lead_steps):
        blocks = []
        for sub in range(CA_PER_STEP):
            q = j * CA_PER_STEP + sub
            blocks.append((sub, 0, (q + 1) * CA_TQ, CA_PAD - q * CA_TQ) if q < lead
                          else (sub, q * CA_TQ - CA_PAD, CA_BAND_BLK, 0))
        pl.when(i == j)(functools.partial(work, blocks))
    pl.when(i >= lead_steps)(lambda: work(
        [(sub, pl.multiple_of((i * CA_PER_STEP + sub) * CA_TQ - CA_PAD, CA_TQ), CA_BAND_BLK, 0)
         for sub in range(CA_PER_STEP)]))


def _ca_attn(zt, z, bias, b):
    t = zt.shape[1]
    s = t // b
    tq = CA_PER_STEP * CA_TQ
    nq = s // tq
    return pl.pallas_call(
        _ca_attn_kernel,
        grid=(b, nq),
        in_specs=[
            pl.BlockSpec((CA_DIM, tq), lambda bi, i: (0, bi * nq + i)),
            pl.BlockSpec((s, CA_DIM), lambda bi, i: (bi, OFF_CAK // CA_DIM)),
            pl.BlockSpec((CA_DIM, s), lambda bi, i: (1, bi)),
            _const_spec(bias.shape),
        ],
        out_specs=pl.BlockSpec((tq, CA_DIM), lambda bi, i: (bi * nq + i, 0)),
        out_shape=jax.ShapeDtypeStruct((t, CA_DIM), BF16),
        compiler_params=_cparams("parallel", "arbitrary"),
        name="ca_attn",
    )(zt, z, zt, bias)


def _ca_bias_kernel(t_ref, o_ref):
    width = t_ref.shape[-1]
    rows = pltpu.roll(jnp.broadcast_to(t_ref[0], (CA_TQ, width)), 0, 1, stride=1, stride_axis=0)
    rc = lax.broadcasted_iota(jnp.int32, (CA_TQ, CA_BAND_BLK), 0) // CHUNK
    cc = lax.broadcasted_iota(jnp.int32, (CA_TQ, CA_BAND_BLK), 1) // CHUNK
    ok = (cc >= rc) & (cc <= rc + CA_LEFT_CHUNKS)
    o_ref[0] = jnp.where(ok, rows[:, :CA_BAND_BLK], -1e30).T


def _ca_bias_table(rel_bias):
    width = CA_TQ + CA_BAND_BLK
    m = jnp.arange(width)
    delta = jnp.where(m < CA_BAND_BLK, m, m - width)
    idx = jnp.clip(CA_PAD - delta, REL_MIN, REL_MAX) - REL_MIN
    t1 = (rel_bias.astype(F32)[idx].T * LOG2E)[:, None, :]
    return pl.pallas_call(
        _ca_bias_kernel,
        grid=(CA_HEADS,),
        in_specs=[pl.BlockSpec((1, 1, width), lambda h: (h, 0, 0))],
        out_specs=pl.BlockSpec((1, CA_BAND_BLK, CA_TQ), lambda h: (h, 0, 0)),
        out_shape=jax.ShapeDtypeStruct((CA_HEADS, CA_BAND_BLK, CA_TQ), F32),
        compiler_params=_cparams("parallel"),
        name="ca_bias",
    )(t1)


def _split_bf16(x):
    hi = x.astype(BF16)
    lo = (x - hi.astype(F32)).astype(BF16)
    return hi, lo


RW_GROUP = 256


def _rw_kernel(rows, xr_ref, xk_ref, xv_ref, xl_ref, w0_ref, wup_ref, a0_ref, aup_ref, gup_ref, kk_ref, ka_ref,
               rk_ref, lnw_ref, lnb_ref, bd_ref, bdf_ref, o_ref, s_ref):
    L = CHUNK
    W = RW_GROUP
    reps = W // RW_HEAD

    @pl.when(pl.program_id(1) == 0)
    def _():
        s_ref[...] = jnp.zeros(s_ref.shape, F32)

    bdm = bd_ref[...]

    def bd(x):
        return jnp.concatenate([x.astype(BF16)] * reps, axis=0) * bdm

    def head_sums(xs):
        out = _dot(jnp.concatenate([x.astype(BF16) for x in xs], axis=0), bdm)
        offs = [0]
        for x in xs:
            offs.append(offs[-1] + x.shape[0])
        return [out[lo:hi] for lo, hi in zip(offs[:-1], offs[1:])]

    row = lax.broadcasted_iota(jnp.int32, (L, W), 0)
    sub = lax.broadcasted_iota(jnp.int32, (L, W), 1) % RW_HEAD
    strict = sub < row
    incl = sub <= row
    eye = (sub == row).astype(F32)
    tri = (lax.broadcasted_iota(jnp.int32, (L, L), 1)
           <= lax.broadcasted_iota(jnp.int32, (L, L), 0)).astype(BF16)
    inv = 1.0 / RW_HEAD

    def lora_inputs(b):
        xl = xl_ref[b].astype(F32)
        return (jnp.tanh(xl[:, :RW_DECAY_LORA]).astype(BF16),
                xl[:, RW_DECAY_LORA:RW_DECAY_LORA + RW_AAA_LORA].astype(BF16),
                jax.nn.sigmoid(xl[:, RW_DECAY_LORA + RW_AAA_LORA:]).astype(BF16))

    def instance(b, gi, xw, xa, xg):
        sl = slice(gi * W, (gi + 1) * W)
        r = xr_ref[b, :, sl].astype(F32)
        k = xk_ref[b, :, sl].astype(F32)
        v = xv_ref[b, :, sl].astype(F32)
        lw = -math.exp(-0.5) * jax.nn.sigmoid(w0_ref[:, sl] + _dot(xw, wup_ref[:, sl]))
        a = jax.nn.sigmoid(a0_ref[:, sl] + _dot(xa, aup_ref[:, sl]))
        g = _dot(xg, gup_ref[:, sl])
        kk = k * kk_ref[:, sl]
        ss = yield kk * kk
        kk = kk * lax.rsqrt(jnp.maximum(ss, 1e-24))
        k = k * (1.0 + (a - 1.0) * ka_ref[:, sl])
        be = kk * a
        lw_hi, lw_lo = _split_bf16(lw)
        cum = _dot(tri, lw_hi) + _dot(tri, lw_lo)
        yield
        cum_l = cum[L - 1:L, :]
        e_neg = jnp.exp(-cum)
        e_tail = jnp.exp(cum_l - cum)
        ar = jnp.concatenate([-kk * jnp.exp(cum - lw), r * jnp.exp(cum)], axis=0).astype(BF16)
        s0 = s_ref[b, gi]
        ar_s = _dot_nt(ar, s0.astype(BF16))
        a_b = _dot_nt(ar, bd(be * e_neg))
        a_k = _dot_nt(ar, bd(k * e_neg))
        yield
        n = jnp.where(strict, a_b[:L], 0.0)
        a_ak = jnp.where(strict, a_k[:L], 0.0)
        a_rb = jnp.where(incl, a_b[L:], 0.0)
        a_rk = jnp.where(incl, a_k[L:], 0.0)
        p = eye + n
        nk = _dot(n.astype(BF16), bd(n))
        bd_v = bd(v)
        x0 = _dot(a_ak.astype(BF16), bd_v)
        yield
        steps = int(math.log2(L)) - 1
        for it in range(steps):
            m = bd(nk)
            if it + 1 < steps:
                res = _dot(jnp.concatenate([p, nk], axis=0).astype(BF16), m)
                p = p + res[:L]
                nk = res[L:]
            else:
                p = p + _dot(p.astype(BF16), m)
            yield
        u = _dot(p.astype(BF16), bd(ar_s[:L] + x0))
        yield
        y = ar_s[L:] + _dot(jnp.concatenate([a_rb, a_rk], axis=1).astype(BF16),
                            jnp.concatenate([bd(u), bd_v], axis=0))
        uv = jnp.concatenate([u, v], axis=0).astype(BF16)
        bk = jnp.concatenate([be * e_tail, k * e_tail], axis=0).astype(BF16)
        s_ref[b, gi] = s0 * jnp.exp(cum_l) + _dot_tn(uv, bk) * bdf_ref[...]
        yield
        sums = yield jnp.concatenate([y, r * k * rk_ref[:, sl]], axis=0)
        d = y - sums[:L] * inv
        var = (yield d * d) * inv
        yn = d * lax.rsqrt(var + RW_LN_EPS) * lnw_ref[:, sl] + lnb_ref[:, sl]
        o_ref[b, :, sl] = ((yn + sums[L:] * v) * g).astype(o_ref.dtype)

    live = []
    for b in range(rows):
        lora = lora_inputs(b)
        live += [instance(b, gi, *lora) for gi in range(RW_DIM // W)]
    asked = [next(g) for g in live]
    while live:
        answers = head_sums(asked) if asked[0] is not None else asked
        live, asked = _advance(live, answers)


def _rw_mixer(z, b, consts):
    t = z.shape[0]
    s = t // b
    rows = math.gcd(b, 8)
    gid = jnp.arange(RW_GROUP) // RW_HEAD
    ones_f32 = (gid[:, None] == gid[None, :]).astype(F32)
    ones_bd = ones_f32.astype(BF16)
    sec = lambda w, j: pl.BlockSpec((rows, CHUNK, w), lambda bb, c: (bb, c, j))
    z3 = z.reshape(b, s, -1)
    return pl.pallas_call(
        functools.partial(_rw_kernel, rows),
        grid=(b // rows, s // CHUNK),
        in_specs=[sec(RW_DIM, OFF_RW // RW_DIM), sec(RW_DIM, OFF_RW // RW_DIM + 1),
                  sec(RW_DIM, OFF_RW // RW_DIM + 2), sec(RW_LORA, OFF_LORA // RW_LORA)]
        + [_pspec(c) for c in consts] + [_const_spec(ones_bd.shape)] * 2,
        out_specs=sec(RW_DIM, 0),
        out_shape=jax.ShapeDtypeStruct((b, s, RW_DIM), BF16),
        scratch_shapes=[pltpu.VMEM((rows, RW_DIM // RW_GROUP, RW_GROUP, RW_GROUP), F32)],
        compiler_params=_cparams("parallel", "arbitrary"),
        name="rw_mixer",
    )(z3, z3, z3, z3, *_pargs(consts), ones_bd, ones_f32)


FF_SPLIT = 4


def _post_kernel(om_ref, or_ref, oc_ref, g0_ref, g1_ref, g2_ref, h_ref, p_ref, wb_ref, wo_ref, gm_ref,
                 g1n_ref, w1_ref, w2_ref, g2n_ref, wg_ref, wp_ref, o_ref):
    def rows(rs):
        merged = None
        for n, (b_ref, zg_ref) in enumerate(((om_ref, g0_ref), (or_ref, g1_ref), (oc_ref, g2_ref))):
            y = _dot(b_ref[rs, :], wb_ref[n * BRANCH_DIM:(n + 1) * BRANCH_DIM, :])
            yield
            gate = jax.nn.sigmoid(zg_ref[rs, :].astype(F32))
            merged = gate * y if merged is None else merged + gate * y
        out = _dot(merged.astype(BF16), wo_ref[...])
        yield
        h = h_ref[rs, :] + _rms(out, gm_ref[...])
        f = _rms(h, g1n_ref[...]).astype(BF16)
        cw = D_FF // FF_SPLIT
        acc = None
        for c in range(FF_SPLIT):
            a = _dot(f, w1_ref[:, c * cw:(c + 1) * cw])
            yield
            a = jnp.maximum(a, 0.0)
            part = _dot((a * a).astype(BF16), w2_ref[c * cw:(c + 1) * cw, :])
            yield
            acc = part if acc is None else acc + part
        h = h + _rms(acc, g2n_ref[...])
        gate = _dot(h.astype(BF16), wg_ref[...])
        proj = _dot(p_ref[rs, :].astype(BF16), wp_ref[...])
        yield
        o_ref[rs, :] = h + jax.nn.sigmoid(gate) * proj

    half = h_ref.shape[0] // 2
    live = [rows(slice(0, half)), rows(slice(half, 2 * half))]
    while live:
        live = [g for g in live if next(g, True) is None]


def _post(o_mla, o_rw, o_ca, z, h, p, consts):
    t = h.shape[0]
    tm = min(o_rw.shape[1], 512)
    per_seq = o_rw.shape[1] // tm
    row = lambda w: pl.BlockSpec((tm, w), lambda i: (i, 0))
    gate = lambda n: pl.BlockSpec((tm, D_MODEL), lambda i: (i, OFF_GATE // D_MODEL + n))
    p_stack, layer = p
    return pl.pallas_call(
        _post_kernel,
        grid=(t // tm,),
        in_specs=[row(BRANCH_DIM), pl.BlockSpec((None, tm, BRANCH_DIM), lambda i: (i // per_seq, i % per_seq, 0)),
                  row(BRANCH_DIM), gate(0), gate(1), gate(2), row(D_MODEL),
                  pl.BlockSpec((None, tm, D_PLE), lambda i: (layer, i, 0))] + [_pspec(c) for c in consts],
        out_specs=row(D_MODEL),
        out_shape=jax.ShapeDtypeStruct((t, D_MODEL), F32),
        compiler_params=_cparams("parallel"),
        name="post",
    )(o_mla, o_rw, o_ca, z, z, z, h, p_stack, *_pargs(consts))


def _rot_half_cols(w):
    half = w.shape[-1] // 2
    return jnp.concatenate([-w[..., half:], w[..., :half]], axis=-1)


def _pack_w_in(w):
    mla_cols = MLA_Q_RANK + MLA_KV_RANK + MLA_ROPE
    rw_cols = 3 * RW_DIM + RW_LORA
    ca_q = (w[..., mla_cols + rw_cols:mla_cols + rw_cols + CA_DIM] * (CA_HEAD ** -0.5 * LOG2E)).astype(BF16)
    w = w.astype(BF16)
    w_mla = w[..., :mla_cols]
    w_rw = w[..., mla_cols:mla_cols + rw_cols]
    w_ca = w[..., mla_cols + rw_cols:mla_cols + rw_cols + 3 * CA_DIM]
    w_gate = w[..., mla_cols + rw_cols + 3 * CA_DIM:]
    w_kr = w_mla[..., MLA_Q_RANK + MLA_KV_RANK:]
    z64 = jnp.zeros(w.shape[:-1] + (64,), w.dtype)
    packed = jnp.concatenate(
        [w_gate, w_ca[..., CA_DIM:2 * CA_DIM], w_rw, w_mla[..., :MLA_Q_RANK + MLA_KV_RANK], w_kr, z64,
         _rot_half_cols(w_kr), z64], axis=-1)
    wt = jnp.concatenate([ca_q, w_ca[..., 2 * CA_DIM:]], axis=-1)
    return packed, jnp.swapaxes(wt, 1, 2)


def _pack_w_uq(w):
    nl, r, _ = w.shape
    w = w.reshape(nl, r, MLA_HEADS, MLA_QK)
    z64 = jnp.zeros((nl, r, MLA_HEADS, 64), w.dtype)
    rot = _rot_half_cols(w[..., MLA_NOPE:])
    wq = jnp.concatenate([w, z64], axis=-1).reshape(nl, r, MLA_HEADS * MLA_HEAD_PAD)
    wqr = jnp.concatenate([rot, z64], axis=-1).reshape(nl, r, MLA_HEADS * 128)
    return jnp.swapaxes(wq, 1, 2).astype(BF16), jnp.swapaxes(wqr, 1, 2).astype(BF16)


def _pack_w_ukv(w):
    nl, r, _ = w.shape
    w = w.reshape(nl, r, MLA_HEADS, MLA_NOPE + MLA_V)
    wk = w[..., :MLA_NOPE].reshape(nl, r, MLA_HEADS * MLA_NOPE)
    wv = w[..., MLA_NOPE:].reshape(nl, r, MLA_HEADS * MLA_V)
    return wk.astype(BF16), jnp.swapaxes(wv, 1, 2).astype(BF16)


def kernel(x, p, positions, pre_mix_g, w_in, mla_q_norm_g, mla_kv_norm_g, mla_w_uq, mla_w_ukv, rw_mu, rw_w0, rw_w_up, rw_a0, rw_a_up, rw_g_up, rw_k_k, rw_k_a, rw_r_k, rw_ln_w, rw_ln_b, ca_rel_bias, w_branch, w_out, post_mix_g, pre_ff_g, w_ff1, w_ff2, post_ff_g, w_ple_gate, w_ple_proj):
    b, s, d = x.shape
    t = b * s
    depth = w_in.shape[0]
    row = lambda a: a.reshape(depth, 1, -1).astype(F32)
    bf = lambda a: a.astype(BF16)
    w_in_p, w_in_t = _pack_w_in(w_in)
    wqt, wqrt = _pack_w_uq(mla_w_uq)
    wk, wvt = _pack_w_ukv(mla_w_ukv)
    stacks = dict(
        in_proj=[row(pre_mix_g), w_in_p, w_in_t, row(rw_mu),
                 row(mla_q_norm_g), row(mla_kv_norm_g), wqt, wqrt, wk, wvt],
        rw=[row(rw_w0), bf(rw_w_up), row(rw_a0), bf(rw_a_up), bf(rw_g_up), row(rw_k_k), row(rw_k_a),
            row(rw_r_k), row(rw_ln_w), row(rw_ln_b)],
        post=[bf(w_branch), bf(w_out), row(post_mix_g),
              row(pre_ff_g), bf(w_ff1), bf(w_ff2), row(post_ff_g), bf(w_ple_gate), bf(w_ple_proj)],
    )
    p3 = p.reshape(depth, t, -1)
    rope = _rope_table(positions)
    h = x.reshape(t, d)
    for i in range(depth):
        prm = {name: [(a, i) for a in arrs] for name, arrs in stacks.items()}
        z, zt, qt, kn, kr, vt = _in_proj(h, s, rope, prm["in_proj"])
        o_mla = _mla_attn(qt, kn, kr, vt, b)
        o_rw = _rw_mixer(z, b, prm["rw"])
        o_ca = _ca_attn(zt, z, _ca_bias_table(ca_rel_bias[i]), b)
        h = _post(o_mla, o_rw, o_ca, z, h, (p3, i), prm["post"])
    return h.reshape(b, s, d)
```

```python
import functools
import math

import jax
import jax.numpy as jnp
from jax import lax
from jax.experimental import pallas as pl
from jax.experimental.pallas import tpu as pltpu

F32 = jnp.float32
BF16 = jnp.bfloat16

D_MODEL = 1024
D_PLE = 256
D_FF = 4 * D_MODEL
NORM_EPS = 1e-6
CHUNK = 64

MLA_HEADS = 4
MLA_NOPE = 128
MLA_ROPE = 64
MLA_V = 128
MLA_Q_RANK = 256
MLA_KV_RANK = 128
ROPE_THETA = 10000.0
MLA_QK = MLA_NOPE + MLA_ROPE
MLA_HEAD_PAD = 256

RW_HEADS = 8
RW_HEAD = 64
RW_DIM = RW_HEADS * RW_HEAD
RW_DECAY_LORA = 64
RW_AAA_LORA = 64
RW_GATE_LORA = 128
RW_LORA = RW_DECAY_LORA + RW_AAA_LORA + RW_GATE_LORA
RW_LN_EPS = 64e-5

CA_HEADS = 8
CA_HEAD = 64
CA_DIM = CA_HEADS * CA_HEAD
CA_LEFT_CHUNKS = 8
CA_PAD = CA_LEFT_CHUNKS * CHUNK
REL_MIN = -(CHUNK - 1)
REL_MAX = 256

N_BRANCH = 3
BRANCH_DIM = 512
GATE_COLS = N_BRANCH * D_MODEL

OFF_GATE = 0
OFF_CAK = GATE_COLS
OFF_RW = OFF_CAK + CA_DIM
OFF_LORA = OFF_RW + 3 * RW_DIM
OFF_ZQ = OFF_LORA + RW_LORA
OFF_ZKV = OFF_ZQ + MLA_Q_RANK
OFF_KR = OFF_ZKV + MLA_KV_RANK
OFF_KRR = OFF_KR + 128
IN_PACKED = OFF_KRR + 128
LOG2E = math.log2(math.e)
MASKED = -1e30

VMEM_LIMIT = 56 * 1024 * 1024


def _cparams(*sem):
    return pltpu.CompilerParams(dimension_semantics=sem, vmem_limit_bytes=VMEM_LIMIT)


def _rms(x, g):
    return x * lax.rsqrt(jnp.mean(x * x, axis=-1, keepdims=True) + NORM_EPS) * g


def _dot(a, b):
    return jnp.dot(a, b, preferred_element_type=F32)


def _dot_nt(a, b):
    return lax.dot_general(a, b, (((1,), (1,)), ((), ())), preferred_element_type=F32)


def _dot_tn(a, b):
    return lax.dot_general(a, b, (((0,), (0,)), ((), ())), preferred_element_type=F32)


def _const_spec(shape):
    nd = len(shape)
    return pl.BlockSpec(shape, lambda *_: (0,) * nd, pipeline_mode=pl.Buffered(1))


def _pshape(param):
    return param[0].shape[1:]


def _pspec(param):
    stack, layer = param
    nd = stack.ndim - 1
    return pl.BlockSpec((None,) + stack.shape[1:], lambda *_: (layer,) + (0,) * nd, pipeline_mode=pl.Buffered(1))


def _pargs(params):
    return [stack for stack, _ in params]


def _advance(gens, answers):
    live, asked = [], []
    for g, ans in zip(gens, answers):
        try:
            asked.append(g.send(ans))
            live.append(g)
        except StopIteration:
            pass
    return live, asked


PIPE_DEPTH = 2
ONES_ROWS = 16


def _pipelined(items, produce, consume):
    pending, outs = [], []
    for it in items:
        pending.append((it, produce(it)))
        if len(pending) > PIPE_DEPTH:
            outs.append(consume(*pending.pop(0)))
    outs.extend(consume(*pc) for pc in pending)
    return outs


def _rope_table_kernel(pos_ref, freq_ref, cs_ref, sn_ref, cst_ref, snt_ref):
    ang = freq_ref[...] * pos_ref[...]
    dead = jnp.zeros((128 - MLA_ROPE, ang.shape[1]), F32)
    cst = jnp.concatenate([jnp.cos(ang), dead], axis=0)
    snt = jnp.concatenate([jnp.sin(ang), dead], axis=0)
    cst_ref[...] = cst
    snt_ref[...] = snt
    cs_ref[...] = cst.T
    sn_ref[...] = snt.T


def _rope_table(positions):
    t = positions.size
    tm = min(t, 2048)
    half = MLA_ROPE // 2
    inv_freq = 1.0 / (ROPE_THETA ** (jnp.arange(half, dtype=F32) / half))
    freq = jnp.concatenate([inv_freq, inv_freq])[:, None]
    return pl.pallas_call(
        _rope_table_kernel,
        grid=(t // tm,),
        in_specs=[pl.BlockSpec((1, tm), lambda i: (0, i)), _const_spec(freq.shape)],
        out_specs=[pl.BlockSpec((tm, 128), lambda i: (i, 0))] * 2 + [pl.BlockSpec((128, tm), lambda i: (0, i))] * 2,
        out_shape=[jax.ShapeDtypeStruct((t, 128), F32)] * 2 + [jax.ShapeDtypeStruct((128, t), F32)] * 2,
        compiler_params=_cparams("parallel"),
        name="rope_table",
    )(positions.astype(F32).reshape(1, t), freq)


IN_RANGES = ((0, 1536), (1536, OFF_CAK), (OFF_CAK, OFF_RW), (OFF_RW, OFF_RW + 1024), (OFF_RW + 1024, OFF_ZQ),
             (OFF_ZQ, IN_PACKED))


def _mla_project(lat, cs, sn, cst, snt, gq_ref, gkv_ref, wqt_ref, wqrt_ref, wk_ref, wvt_ref):
    qn = _rms(lat[:, :MLA_Q_RANK], gq_ref[...]).astype(BF16)
    qt = _dot_nt(wqt_ref[...], qn)
    qrt = _dot_nt(wqrt_ref[...], qn)
    scale = MLA_QK ** -0.5 * LOG2E
    parts = []
    for h in range(MLA_HEADS):
        o = h * MLA_HEAD_PAD
        rope = qt[o + 128:o + 256] * cst + qrt[h * 128:(h + 1) * 128] * snt
        parts += [(qt[o:o + 128] * scale).astype(BF16), (rope * scale).astype(BF16)]
    kvn = _rms(lat[:, MLA_Q_RANK:MLA_Q_RANK + MLA_KV_RANK], gkv_ref[...]).astype(BF16)
    k = _dot(kvn, wk_ref[...]).astype(BF16)
    vt = _dot_nt(wvt_ref[...], kvn).astype(BF16)
    kr = (lat[:, OFF_KR - OFF_ZQ:OFF_KRR - OFF_ZQ] * cs + lat[:, OFF_KRR - OFF_ZQ:] * sn).astype(BF16)
    return jnp.concatenate(parts, axis=0), k, kr, vt


def _in_proj_kernel(seq_len, x_ref, cs_ref, sn_ref, cst_ref, snt_ref, g_ref, w_ref, wt_ref, mu_ref, gq_ref, gkv_ref,
                    wqt_ref, wqrt_ref, wk_ref, wvt_ref, o_ref, ot_ref, qt_ref, k_ref, kr_ref, vt_ref, carry_ref):
    tm = x_ref.shape[0]
    half = tm // 2
    first = (pl.program_id(0) * tm) % seq_len == 0
    row0 = lax.broadcasted_iota(jnp.int32, (half, 1), 0) == 0
    edge = {}

    def rows(idx):
        rs = slice(idx * half, (idx + 1) * half)
        xn = _rms(x_ref[rs, :], g_ref[...]).astype(BF16)
        for c, e in IN_RANGES:
            res = _dot(xn, w_ref[:, c:e])
            yield
            if c >= OFF_ZQ:
                qt, k, kr, vt = _mla_project(res, cs_ref[rs, :], sn_ref[rs, :], cst_ref[:, rs], snt_ref[:, rs],
                                             gq_ref, gkv_ref, wqt_ref, wqrt_ref, wk_ref, wvt_ref)
                qt_ref[:, rs] = qt
                k_ref[rs, :] = k
                kr_ref[rs, :] = kr
                vt_ref[:, rs] = vt
                continue
            if c >= OFF_RW:
                cs = slice(c - OFF_RW, e - OFF_RW)
                if idx == 0:
                    last = jnp.where(first, 0.0, carry_ref[:, cs])
                    edge[c] = res[half - 1:half, :]
                else:
                    last = edge[c]
                    carry_ref[:, cs] = res[half - 1:half, :]
                prev = jnp.where(row0, last, pltpu.roll(res, 1, axis=0))
                res = res + (prev - res) * mu_ref[:, cs]
            o_ref[rs, c:e] = res.astype(o_ref.dtype)
        ot_ref[:, rs] = _dot_nt(wt_ref[...], xn).astype(ot_ref.dtype)

    live = [rows(0), rows(1)]
    while live:
        live = [g for g in live if next(g, True) is None]


def _in_proj(h, seq_len, tables, consts):
    t, d = h.shape
    nt = _pshape(consts[2])[0]
    tm = min(t, 512)
    hv = MLA_HEADS * MLA_V
    hq = MLA_HEADS * MLA_HEAD_PAD
    row = lambda w: pl.BlockSpec((tm, w), lambda i: (i, 0))
    col = lambda w: pl.BlockSpec((w, tm), lambda i: (0, i))
    return pl.pallas_call(
        functools.partial(_in_proj_kernel, seq_len),
        grid=(t // tm,),
        in_specs=[row(d), row(128), row(128), col(128), col(128)] + [_pspec(c) for c in consts],
        out_specs=[row(OFF_ZQ), col(nt), col(hq), row(hv), row(128), col(hv)],
        out_shape=[
            jax.ShapeDtypeStruct((t, OFF_ZQ), BF16),
            jax.ShapeDtypeStruct((nt, t), BF16),
            jax.ShapeDtypeStruct((hq, t), BF16),
            jax.ShapeDtypeStruct((t, hv), BF16),
            jax.ShapeDtypeStruct((t, 128), BF16),
            jax.ShapeDtypeStruct((hv, t), BF16),
        ],
        scratch_shapes=[pltpu.VMEM((1, OFF_ZQ - OFF_RW), F32)],
        compiler_params=_cparams("arbitrary"),
        name="in_proj",
    )(h, *tables, *_pargs(consts))


MLA_TQ = 512
MLA_GROUP = 4


def _mla_attn_kernel(qt_ref, k_ref, kr_ref, vt_ref, o_ref, m_ref, acc_ref):
    i = pl.program_id(1)
    tq = MLA_TQ
    m_ref[...] = jnp.full(m_ref.shape, MASKED, F32)
    acc_ref[...] = jnp.zeros(acc_ref.shape, F32)
    ones = jnp.ones((ONES_ROWS, tq), BF16)

    def run(blocks):
        def scores(j, masked, h):
            start = pl.multiple_of(j * tq, tq)
            kh = jnp.concatenate([k_ref[pl.ds(start, tq), h * 128:(h + 1) * 128],
                                  kr_ref[pl.ds(start, tq), :]], axis=1)
            s = _dot(kh, qt_ref[h * MLA_HEAD_PAD:(h + 1) * MLA_HEAD_PAD, :])
            if masked:
                kc = lax.broadcasted_iota(jnp.int32, s.shape, 0) // CHUNK
                qc = lax.broadcasted_iota(jnp.int32, s.shape, 1) // CHUNK
                s = jnp.where(kc <= qc, s, MASKED)
            return s

        def update(j, h, s):
            start = pl.multiple_of(j * tq, tq)
            m_prev = m_ref[h]
            m_new = jnp.maximum(m_prev, jnp.max(s, axis=0, keepdims=True))
            a = jnp.exp2(m_prev - m_new)
            p = jnp.exp2(s - m_new).astype(BF16)
            vh = jnp.concatenate([vt_ref[h * MLA_V:(h + 1) * MLA_V, pl.ds(start, tq)], ones], axis=0)
            acc_ref[h] = a * acc_ref[h] + _dot(vh, p)
            m_ref[h] = m_new

        items = [(j, masked, h) for j, masked in blocks for h in range(MLA_HEADS)]
        _pipelined(items, lambda it: scores(*it), lambda it, s: update(it[0], it[2], s))

    def body(jj, c):
        run([(MLA_GROUP * jj + n, False) for n in range(MLA_GROUP)])
        return c

    lax.fori_loop(0, i // MLA_GROUP, body, 0)
    for r in range(MLA_GROUP):
        pl.when(i % MLA_GROUP == r)(
            functools.partial(run, [(i - r + n, False) for n in range(r)] + [(i, True)]))
    for h in range(MLA_HEADS):
        acc = acc_ref[h]
        o_ref[:, h * MLA_V:(h + 1) * MLA_V] = (acc[:MLA_V] / acc[MLA_V:MLA_V + 1]).T.astype(o_ref.dtype)


def _mla_attn(qt, k, kr, vt, b):
    hq, t = qt.shape
    s = t // b
    tq = MLA_TQ
    nq = s // tq
    hv = MLA_HEADS * MLA_V
    full = lambda w: pl.BlockSpec((s, w), lambda bi, i: (bi, 0))
    return pl.pallas_call(
        _mla_attn_kernel,
        grid=(b, nq),
        in_specs=[
            pl.BlockSpec((hq, tq), lambda bi, i: (0, bi * nq + i)),
            full(hv), full(128),
            pl.BlockSpec((hv, s), lambda bi, i: (0, bi)),
        ],
        out_specs=pl.BlockSpec((tq, hv), lambda bi, i: (bi * nq + i, 0)),
        out_shape=jax.ShapeDtypeStruct((t, hv), BF16),
        scratch_shapes=[pltpu.VMEM((MLA_HEADS, 1, tq), F32),
                        pltpu.VMEM((MLA_HEADS, MLA_V + ONES_ROWS, tq), F32)],
        compiler_params=_cparams("parallel", "arbitrary"),
        name="mla_attn",
    )(qt, k, kr, vt)


CA_TQ = 256
CA_BAND_BLK = CA_TQ + CA_PAD


CA_PER_STEP = 4


def _ca_attn_kernel(qt_ref, k_ref, vt_ref, bias_ref, o_ref):
    i = pl.program_id(1)
    lo = lax.broadcasted_iota(jnp.int32, (128, CA_TQ), 0) < CA_HEAD
    zero = jnp.zeros((), BF16)

    def work(blocks):
        def scores(item):
            (sub, k_start, n_keys, bias_off), h = item
            sl = slice((h // 2) * 128, (h // 2 + 1) * 128)
            k2 = k_ref[pl.ds(k_start, n_keys), sl]
            qh = jnp.where(lo if h % 2 == 0 else ~lo, qt_ref[sl, sub * CA_TQ:(sub + 1) * CA_TQ], zero)
            return _dot(k2, qh) + bias_ref[h, bias_off:bias_off + n_keys, :]

        def attend(item, s):
            (_, k_start, n_keys, _), h = item
            p = jnp.exp2(s - jnp.max(s, axis=0, keepdims=True)).astype(BF16)
            vh = jnp.concatenate([vt_ref[h * CA_HEAD:(h + 1) * CA_HEAD, pl.ds(k_start, n_keys)],
                                  jnp.ones((ONES_ROWS, n_keys), BF16)], axis=0)
            out = _dot(vh, p)
            return out[:CA_HEAD] / out[CA_HEAD:CA_HEAD + 1]

        outs = _pipelined([(blk, h) for blk in blocks for h in range(CA_HEADS)], scores, attend)
        for n, blk in enumerate(blocks):
            rows = slice(blk[0] * CA_TQ, (blk[0] + 1) * CA_TQ)
            for pair in range(CA_HEADS // 2):
                both = jnp.concatenate(outs[n * CA_HEADS + 2 * pair:n * CA_HEADS + 2 * pair + 2], axis=0)
                o_ref[rows, pair * 128:(pair + 1) * 128] = both.T.astype(o_ref.dtype)

    lead = CA_PAD // CA_TQ
    lead_steps = -(-lead // CA_PER_STEP)
    for j in range(lead_steps):
        blocks = []
        for sub in range(CA_PER_STEP):
            q = j * CA_PER_STEP + sub
            blocks.append((sub, 0, (q + 1) * CA_TQ, CA_PAD - q * CA_TQ) if q < lead
                          else (sub, q * CA_TQ - CA_PAD, CA_BAND_BLK, 0))
        pl.when(i == j)(functools.partial(work, blocks))
    pl.when(i >= lead_steps)(lambda: work(
        [(sub, pl.multiple_of((i * CA_PER_STEP + sub) * CA_TQ - CA_PAD, CA_TQ), CA_BAND_BLK, 0)
         for sub in range(CA_PER_STEP)]))


def _ca_attn(zt, z, bias, b):
    t = zt.shape[1]
    s = t // b
    tq = CA_PER_STEP * CA_TQ
    nq = s // tq
    return pl.pallas_call(
        _ca_attn_kernel,
        grid=(b, nq),
        in_specs=[
            pl.BlockSpec((CA_DIM, tq), lambda bi, i: (0, bi * nq + i)),
            pl.BlockSpec((s, CA_DIM), lambda bi, i: (bi, OFF_CAK // CA_DIM)),
            pl.BlockSpec((CA_DIM, s), lambda bi, i: (1, bi)),
            _const_spec(bias.shape),
        ],
        out_specs=pl.BlockSpec((tq, CA_DIM), lambda bi, i: (bi * nq + i, 0)),
        out_shape=jax.ShapeDtypeStruct((t, CA_DIM), BF16),
        compiler_params=_cparams("parallel", "arbitrary"),
        name="ca_attn",
    )(zt, z, zt, bias)


def _ca_bias_kernel(t_ref, o_ref):
    width = t_ref.shape[-1]
    rows = pltpu.roll(jnp.broadcast_to(t_ref[0], (CA_TQ, width)), 0, 1, stride=1, stride_axis=0)
    rc = lax.broadcasted_iota(jnp.int32, (CA_TQ, CA_BAND_BLK), 0) // CHUNK
    cc = lax.broadcasted_iota(jnp.int32, (CA_TQ, CA_BAND_BLK), 1) // CHUNK
    ok = (cc >= rc) & (cc <= rc + CA_LEFT_CHUNKS)
    o_ref[0] = jnp.where(ok, rows[:, :CA_BAND_BLK], MASKED).T


def _ca_bias_table(rel_bias):
    width = CA_TQ + CA_BAND_BLK
    m = jnp.arange(width)
    delta = jnp.where(m < CA_BAND_BLK, m, m - width)
    idx = jnp.clip(CA_PAD - delta, REL_MIN, REL_MAX) - REL_MIN
    t1 = (rel_bias.astype(F32)[idx].T * LOG2E)[:, None, :]
    return pl.pallas_call(
        _ca_bias_kernel,
        grid=(CA_HEADS,),
        in_specs=[pl.BlockSpec((1, 1, width), lambda h: (h, 0, 0))],
        out_specs=pl.BlockSpec((1, CA_BAND_BLK, CA_TQ), lambda h: (h, 0, 0)),
        out_shape=jax.ShapeDtypeStruct((CA_HEADS, CA_BAND_BLK, CA_TQ), F32),
        compiler_params=_cparams("parallel"),
        name="ca_bias",
    )(t1)


def _split_bf16(x):
    hi = x.astype(BF16)
    lo = (x - hi.astype(F32)).astype(BF16)
    return hi, lo


RW_GROUP = 256


def _rw_kernel(rows, xr_ref, xk_ref, xv_ref, xl_ref, w0_ref, wup_ref, a0_ref, aup_ref, gup_ref, kk_ref, ka_ref,
               rk_ref, lnw_ref, lnb_ref, bd_ref, bdf_ref, o_ref, s_ref):
    L = CHUNK
    W = RW_GROUP
    reps = W // RW_HEAD

    @pl.when(pl.program_id(1) == 0)
    def _():
        s_ref[...] = jnp.zeros(s_ref.shape, F32)

    bdm = bd_ref[...]

    def bd(x):
        return jnp.concatenate([x.astype(BF16)] * reps, axis=0) * bdm

    def head_sums(xs):
        out = _dot(jnp.concatenate([x.astype(BF16) for x in xs], axis=0), bdm)
        offs = [0]
        for x in xs:
            offs.append(offs[-1] + x.shape[0])
        return [out[lo:hi] for lo, hi in zip(offs[:-1], offs[1:])]

    row = lax.broadcasted_iota(jnp.int32, (L, W), 0)
    sub = lax.broadcasted_iota(jnp.int32, (L, W), 1) % RW_HEAD
    strict = sub < row
    incl = sub <= row
    eye = (sub == row).astype(F32)
    tri = (lax.broadcasted_iota(jnp.int32, (L, L), 1)
           <= lax.broadcasted_iota(jnp.int32, (L, L), 0)).astype(BF16)
    inv = 1.0 / RW_HEAD

    def lora_inputs(b):
        xl = xl_ref[b].astype(F32)
        return (jnp.tanh(xl[:, :RW_DECAY_LORA]).astype(BF16),
                xl[:, RW_DECAY_LORA:RW_DECAY_LORA + RW_AAA_LORA].astype(BF16),
                jax.nn.sigmoid(xl[:, RW_DECAY_LORA + RW_AAA_LORA:]).astype(BF16))

    def instance(b, gi, xw, xa, xg):
        sl = slice(gi * W, (gi + 1) * W)
        r = xr_ref[b, :, sl].astype(F32)
        k = xk_ref[b, :, sl].astype(F32)
        v = xv_ref[b, :, sl].astype(F32)
        lw = -math.exp(-0.5) * jax.nn.sigmoid(w0_ref[:, sl] + _dot(xw, wup_ref[:, sl]))
        a = jax.nn.sigmoid(a0_ref[:, sl] + _dot(xa, aup_ref[:, sl]))
        g = _dot(xg, gup_ref[:, sl])
        kk = k * kk_ref[:, sl]
        ss = yield kk * kk
        kk = kk * lax.rsqrt(jnp.maximum(ss, 1e-24))
        k = k * (1.0 + (a - 1.0) * ka_ref[:, sl])
        be = kk * a
        lw_hi, lw_lo = _split_bf16(lw)
        cum = _dot(tri, lw_hi) + _dot(tri, lw_lo)
        yield
        cum_l = cum[L - 1:L, :]
        e_neg = jnp.exp(-cum)
        e_tail = jnp.exp(cum_l - cum)
        ar = jnp.concatenate([-kk * jnp.exp(cum - lw), r * jnp.exp(cum)], axis=0).astype(BF16)
        s0 = s_ref[b, gi]
        ar_s = _dot_nt(ar, s0.astype(BF16))
        a_b = _dot_nt(ar, bd(be * e_neg))
        a_k = _dot_nt(ar, bd(k * e_neg))
        yield
        n = jnp.where(strict, a_b[:L], 0.0)
        a_ak = jnp.where(strict, a_k[:L], 0.0)
        a_rb = jnp.where(incl, a_b[L:], 0.0)
        a_rk = jnp.where(incl, a_k[L:], 0.0)
        p = eye + n
        nk = _dot(n.astype(BF16), bd(n))
        bd_v = bd(v)
        x0 = _dot(a_ak.astype(BF16), bd_v)
        yield
        steps = int(math.log2(L)) - 1
        for it in range(steps):
            m = bd(nk)
            if it + 1 < steps:
                res = _dot(jnp.concatenate([p, nk], axis=0).astype(BF16), m)
                p = p + res[:L]
                nk = res[L:]
            else:
                p = p + _dot(p.astype(BF16), m)
            yield
        u = _dot(p.astype(BF16), bd(ar_s[:L] + x0))
        yield
        y = ar_s[L:] + _dot(jnp.concatenate([a_rb, a_rk], axis=1).astype(BF16),
                            jnp.concatenate([bd(u), bd_v], axis=0))
        uv = jnp.concatenate([u, v], axis=0).astype(BF16)
        bk = jnp.concatenate([be * e_tail, k * e_tail], axis=0).astype(BF16)
        s_ref[b, gi] = s0 * jnp.exp(cum_l) + _dot_tn(uv, bk) * bdf_ref[...]
        yield
        sums = yield jnp.concatenate([y, r * k * rk_ref[:, sl]], axis=0)
        d = y - sums[:L] * inv
        var = (yield d * d) * inv
        yn = d * lax.rsqrt(var + RW_LN_EPS) * lnw_ref[:, sl] + lnb_ref[:, sl]
        o_ref[b, :, sl] = ((yn + sums[L:] * v) * g).astype(o_ref.dtype)

    live = []
    for b in range(rows):
        lora = lora_inputs(b)
        live += [instance(b, gi, *lora) for gi in range(RW_DIM // W)]
    asked = [next(g) for g in live]
    while live:
        answers = head_sums(asked) if asked[0] is not None else asked
        live, asked = _advance(live, answers)


def _rw_mixer(z, b, consts):
    t = z.shape[0]
    s = t // b
    rows = math.gcd(b, 8)
    gid = jnp.arange(RW_GROUP) // RW_HEAD
    ones_f32 = (gid[:, None] == gid[None, :]).astype(F32)
    ones_bd = ones_f32.astype(BF16)
    sec = lambda w, j: pl.BlockSpec((rows, CHUNK, w), lambda bb, c: (bb, c, j))
    z3 = z.reshape(b, s, -1)
    return pl.pallas_call(
        functools.partial(_rw_kernel, rows),
        grid=(b // rows, s // CHUNK),
        in_specs=[sec(RW_DIM, OFF_RW // RW_DIM), sec(RW_DIM, OFF_RW // RW_DIM + 1),
                  sec(RW_DIM, OFF_RW // RW_DIM + 2), sec(RW_LORA, OFF_LORA // RW_LORA)]
        + [_pspec(c) for c in consts] + [_const_spec(ones_bd.shape)] * 2,
        out_specs=sec(RW_DIM, 0),
        out_shape=jax.ShapeDtypeStruct((b, s, RW_DIM), BF16),
        scratch_shapes=[pltpu.VMEM((rows, RW_DIM // RW_GROUP, RW_GROUP, RW_GROUP), F32)],
        compiler_params=_cparams("parallel", "arbitrary"),
        name="rw_mixer",
    )(z3, z3, z3, z3, *_pargs(consts), ones_bd, ones_f32)


FF_SPLIT = 4


def _post_kernel(om_ref, or_ref, oc_ref, g0_ref, g1_ref, g2_ref, h_ref, p_ref, wb_ref, wo_ref, gm_ref,
                 g1n_ref, w1_ref, w2_ref, g2n_ref, wg_ref, wp_ref, o_ref):
    def rows(rs):
        merged = None
        for n, (b_ref, zg_ref) in enumerate(((om_ref, g0_ref), (or_ref, g1_ref), (oc_ref, g2_ref))):
            y = _dot(b_ref[rs, :], wb_ref[n * BRANCH_DIM:(n + 1) * BRANCH_DIM, :])
            yield
            gate = jax.nn.sigmoid(zg_ref[rs, :].astype(F32))
            merged = gate * y if merged is None else merged + gate * y
        out = _dot(merged.astype(BF16), wo_ref[...])
        yield
        h = h_ref[rs, :] + _rms(out, gm_ref[...])
        f = _rms(h, g1n_ref[...]).astype(BF16)
        cw = D_FF // FF_SPLIT
        acc = None
        for c in range(FF_SPLIT):
            a = _dot(f, w1_ref[:, c * cw:(c + 1) * cw])
            yield
            a = jnp.maximum(a, 0.0)
            part = _dot((a * a).astype(BF16), w2_ref[c * cw:(c + 1) * cw, :])
            yield
            acc = part if acc is None else acc + part
        h = h + _rms(acc, g2n_ref[...])
        gate = _dot(h.astype(BF16), wg_ref[...])
        proj = _dot(p_ref[rs, :].astype(BF16), wp_ref[...])
        yield
        o_ref[rs, :] = h + jax.nn.sigmoid(gate) * proj

    half = h_ref.shape[0] // 2
    live = [rows(slice(0, half)), rows(slice(half, 2 * half))]
    while live:
        live = [g for g in live if next(g, True) is None]


def _post(o_mla, o_rw, o_ca, z, h, p, consts):
    t = h.shape[0]
    tm = min(o_rw.shape[1], 512)
    per_seq = o_rw.shape[1] // tm
    row = lambda w: pl.BlockSpec((tm, w), lambda i: (i, 0))
    gate = lambda n: pl.BlockSpec((tm, D_MODEL), lambda i: (i, OFF_GATE // D_MODEL + n))
    p_stack, layer = p
    return pl.pallas_call(
        _post_kernel,
        grid=(t // tm,),
        in_specs=[row(BRANCH_DIM), pl.BlockSpec((None, tm, BRANCH_DIM), lambda i: (i // per_seq, i % per_seq, 0)),
                  row(BRANCH_DIM), gate(0), gate(1), gate(2), row(D_MODEL),
                  pl.BlockSpec((None, tm, D_PLE), lambda i: (layer, i, 0))] + [_pspec(c) for c in consts],
        out_specs=row(D_MODEL),
        out_shape=jax.ShapeDtypeStruct((t, D_MODEL), F32),
        compiler_params=_cparams("parallel"),
        name="post",
    )(o_mla, o_rw, o_ca, z, z, z, h, p_stack, *_pargs(consts))


def _rot_half_cols(w):
    half = w.shape[-1] // 2
    return jnp.concatenate([-w[..., half:], w[..., :half]], axis=-1)


def _pack_w_in(w):
    mla_cols = MLA_Q_RANK + MLA_KV_RANK + MLA_ROPE
    rw_cols = 3 * RW_DIM + RW_LORA
    ca_q = (w[..., mla_cols + rw_cols:mla_cols + rw_cols + CA_DIM] * (CA_HEAD ** -0.5 * LOG2E)).astype(BF16)
    w = w.astype(BF16)
    w_mla = w[..., :mla_cols]
    w_rw = w[..., mla_cols:mla_cols + rw_cols]
    w_ca = w[..., mla_cols + rw_cols:mla_cols + rw_cols + 3 * CA_DIM]
    w_gate = w[..., mla_cols + rw_cols + 3 * CA_DIM:]
    w_kr = w_mla[..., MLA_Q_RANK + MLA_KV_RANK:]
    z64 = jnp.zeros(w.shape[:-1] + (64,), w.dtype)
    packed = jnp.concatenate(
        [w_gate, w_ca[..., CA_DIM:2 * CA_DIM], w_rw, w_mla[..., :MLA_Q_RANK + MLA_KV_RANK], w_kr, z64,
         _rot_half_cols(w_kr), z64], axis=-1)
    wt = jnp.concatenate([ca_q, w_ca[..., 2 * CA_DIM:]], axis=-1)
    return packed, jnp.swapaxes(wt, 1, 2)


def _pack_w_uq(w):
    nl, r, _ = w.shape
    w = w.reshape(nl, r, MLA_HEADS, MLA_QK)
    z64 = jnp.zeros((nl, r, MLA_HEADS, 64), w.dtype)
    rot = _rot_half_cols(w[..., MLA_NOPE:])
    wq = jnp.concatenate([w, z64], axis=-1).reshape(nl, r, MLA_HEADS * MLA_HEAD_PAD)
    wqr = jnp.concatenate([rot, z64], axis=-1).reshape(nl, r, MLA_HEADS * 128)
    return jnp.swapaxes(wq, 1, 2).astype(BF16), jnp.swapaxes(wqr, 1, 2).astype(BF16)


def _pack_w_ukv(w):
    nl, r, _ = w.shape
    w = w.reshape(nl, r, MLA_HEADS, MLA_NOPE + MLA_V)
    wk = w[..., :MLA_NOPE].reshape(nl, r, MLA_HEADS * MLA_NOPE)
    wv = w[..., MLA_NOPE:].reshape(nl, r, MLA_HEADS * MLA_V)
    return wk.astype(BF16), jnp.swapaxes(wv, 1, 2).astype(BF16)


def kernel(x, p, positions, pre_mix_g, w_in, mla_q_norm_g, mla_kv_norm_g, mla_w_uq, mla_w_ukv, rw_mu, rw_w0, rw_w_up, rw_a0, rw_a_up, rw_g_up, rw_k_k, rw_k_a, rw_r_k, rw_ln_w, rw_ln_b, ca_rel_bias, w_branch, w_out, post_mix_g, pre_ff_g, w_ff1, w_ff2, post_ff_g, w_ple_gate, w_ple_proj):
    b, s, d = x.shape
    t = b * s
    depth = w_in.shape[0]
    row = lambda a: a.reshape(depth, 1, -1).astype(F32)
    bf = lambda a: a.astype(BF16)
    w_in_p, w_in_t = _pack_w_in(w_in)
    wqt, wqrt = _pack_w_uq(mla_w_uq)
    wk, wvt = _pack_w_ukv(mla_w_ukv)
    stacks = dict(
        in_proj=[row(pre_mix_g), w_in_p, w_in_t, row(rw_mu),
                 row(mla_q_norm_g), row(mla_kv_norm_g), wqt, wqrt, wk, wvt],
        rw=[row(rw_w0), bf(rw_w_up), row(rw_a0), bf(rw_a_up), bf(rw_g_up), row(rw_k_k), row(rw_k_a),
            row(rw_r_k), row(rw_ln_w), row(rw_ln_b)],
        post=[bf(w_branch), bf(w_out), row(post_mix_g),
              row(pre_ff_g), bf(w_ff1), bf(w_ff2), row(post_ff_g), bf(w_ple_gate), bf(w_ple_proj)],
    )
    p3 = p.reshape(depth, t, -1)
    rope = _rope_table(positions)
    h = x.reshape(t, d)
    for i in range(depth):
        prm = {name: [(a, i) for a in arrs] for name, arrs in stacks.items()}
        z, zt, qt, kn, kr, vt = _in_proj(h, s, rope, prm["in_proj"])
        o_mla = _mla_attn(qt, kn, kr, vt, b)
        o_rw = _rw_mixer(z, b, prm["rw"])
        o_ca = _ca_attn(zt, z, _ca_bias_table(ca_rel_bias[i]), b)
        h = _post(o_mla, o_rw, o_ca, z, h, (p3, i), prm["post"])
    return h.reshape(b, s, d)
```

```python
import functools
import math

import jax
import jax.numpy as jnp
from jax import lax
from jax.experimental import pallas as pl
from jax.experimental.pallas import tpu as pltpu

F32 = jnp.float32
BF16 = jnp.bfloat16

D_MODEL = 1024
D_PLE = 256
D_FF = 4 * D_MODEL
NORM_EPS = 1e-6
CHUNK = 64

MLA_HEADS = 4
MLA_NOPE = 128
MLA_ROPE = 64
MLA_V = 128
MLA_Q_RANK = 256
MLA_KV_RANK = 128
ROPE_THETA = 10000.0
MLA_QK = MLA_NOPE + MLA_ROPE
MLA_HEAD_PAD = 256

RW_HEADS = 8
RW_HEAD = 64
RW_DIM = RW_HEADS * RW_HEAD
RW_DECAY_LORA = 64
RW_AAA_LORA = 64
RW_GATE_LORA = 128
RW_LORA = RW_DECAY_LORA + RW_AAA_LORA + RW_GATE_LORA
RW_LN_EPS = 64e-5

CA_HEADS = 8
CA_HEAD = 64
CA_DIM = CA_HEADS * CA_HEAD
CA_LEFT_CHUNKS = 8
CA_PAD = CA_LEFT_CHUNKS * CHUNK
REL_MIN = -(CHUNK - 1)
REL_MAX = 256

N_BRANCH = 3
BRANCH_DIM = 512
GATE_COLS = N_BRANCH * D_MODEL

OFF_GATE = 0
OFF_CAK = GATE_COLS
OFF_RW = OFF_CAK + CA_DIM
OFF_LORA = OFF_RW + 3 * RW_DIM
OFF_ZQ = OFF_LORA + RW_LORA
OFF_ZKV = OFF_ZQ + MLA_Q_RANK
OFF_KR = OFF_ZKV + MLA_KV_RANK
OFF_KRR = OFF_KR + 128
IN_PACKED = OFF_KRR + 128
LOG2E = math.log2(math.e)
MASKED = -1e30

VMEM_LIMIT = 56 * 1024 * 1024


def _cparams(*sem):
    return pltpu.CompilerParams(dimension_semantics=sem, vmem_limit_bytes=VMEM_LIMIT)


def _rms(x, g):
    return x * lax.rsqrt(jnp.mean(x * x, axis=-1, keepdims=True) + NORM_EPS) * g


def _dot(a, b):
    return jnp.dot(a, b, preferred_element_type=F32)


def _dot_nt(a, b):
    return lax.dot_general(a, b, (((1,), (1,)), ((), ())), preferred_element_type=F32)


def _dot_tn(a, b):
    return lax.dot_general(a, b, (((0,), (0,)), ((), ())), preferred_element_type=F32)


def _const_spec(shape):
    nd = len(shape)
    return pl.BlockSpec(shape, lambda *_: (0,) * nd, pipeline_mode=pl.Buffered(1))


def _pshape(param):
    return param[0].shape[1:]


def _pspec(param):
    stack, layer = param
    nd = stack.ndim - 1
    return pl.BlockSpec((None,) + stack.shape[1:], lambda *_: (layer,) + (0,) * nd, pipeline_mode=pl.Buffered(1))


def _pargs(params):
    return [stack for stack, _ in params]


def _advance(gens, answers):
    live, asked = [], []
    for g, ans in zip(gens, answers):
        try:
            asked.append(g.send(ans))
            live.append(g)
        except StopIteration:
            pass
    return live, asked


MLA_PIPE = 2
CA_PIPE = 3
ONES_ROWS = 16


def _pipelined(items, produce, consume, depth):
    pending, outs = [], []
    for it in items:
        pending.append((it, produce(it)))
        if len(pending) > depth:
            outs.append(consume(*pending.pop(0)))
    outs.extend(consume(*pc) for pc in pending)
    return outs


def _rope_table_kernel(pos_ref, freq_ref, cs_ref, sn_ref, cst_ref, snt_ref):
    ang = freq_ref[...] * pos_ref[...]
    dead = jnp.zeros((128 - MLA_ROPE, ang.shape[1]), F32)
    cst = jnp.concatenate([jnp.cos(ang), dead], axis=0)
    snt = jnp.concatenate([jnp.sin(ang), dead], axis=0)
    cst_ref[...] = cst
    snt_ref[...] = snt
    cs_ref[...] = cst.T
    sn_ref[...] = snt.T


def _rope_table(positions):
    t = positions.size
    tm = min(t, 2048)
    half = MLA_ROPE // 2
    inv_freq = 1.0 / (ROPE_THETA ** (jnp.arange(half, dtype=F32) / half))
    freq = jnp.concatenate([inv_freq, inv_freq])[:, None]
    return pl.pallas_call(
        _rope_table_kernel,
        grid=(t // tm,),
        in_specs=[pl.BlockSpec((1, tm), lambda i: (0, i)), _const_spec(freq.shape)],
        out_specs=[pl.BlockSpec((tm, 128), lambda i: (i, 0))] * 2 + [pl.BlockSpec((128, tm), lambda i: (0, i))] * 2,
        out_shape=[jax.ShapeDtypeStruct((t, 128), F32)] * 2 + [jax.ShapeDtypeStruct((128, t), F32)] * 2,
        compiler_params=_cparams("parallel"),
        name="rope_table",
    )(positions.astype(F32).reshape(1, t), freq)


IN_RANGES = ((0, 1536), (1536, OFF_CAK), (OFF_CAK, OFF_RW), (OFF_RW, OFF_RW + 1024), (OFF_RW + 1024, OFF_ZQ),
             (OFF_ZQ, IN_PACKED))


def _mla_project(lat, cs, sn, cst, snt, gq_ref, gkv_ref, wqt_ref, wqrt_ref, wk_ref, wvt_ref):
    qn = _rms(lat[:, :MLA_Q_RANK], gq_ref[...]).astype(BF16)
    qt = _dot_nt(wqt_ref[...], qn)
    qrt = _dot_nt(wqrt_ref[...], qn)
    scale = MLA_QK ** -0.5 * LOG2E
    parts = []
    for h in range(MLA_HEADS):
        o = h * MLA_HEAD_PAD
        rope = qt[o + 128:o + 256] * cst + qrt[h * 128:(h + 1) * 128] * snt
        parts += [(qt[o:o + 128] * scale).astype(BF16), (rope * scale).astype(BF16)]
    kvn = _rms(lat[:, MLA_Q_RANK:MLA_Q_RANK + MLA_KV_RANK], gkv_ref[...]).astype(BF16)
    k = _dot(kvn, wk_ref[...]).astype(BF16)
    vt = _dot_nt(wvt_ref[...], kvn).astype(BF16)
    kr = (lat[:, OFF_KR - OFF_ZQ:OFF_KRR - OFF_ZQ] * cs + lat[:, OFF_KRR - OFF_ZQ:] * sn).astype(BF16)
    return jnp.concatenate(parts, axis=0), k, kr, vt


def _in_proj_kernel(seq_len, x_ref, cs_ref, sn_ref, cst_ref, snt_ref, g_ref, w_ref, wt_ref, mu_ref, gq_ref, gkv_ref,
                    wqt_ref, wqrt_ref, wk_ref, wvt_ref, o_ref, ot_ref, qt_ref, k_ref, kr_ref, vt_ref, carry_ref):
    tm = x_ref.shape[0]
    half = tm // 2
    first = (pl.program_id(0) * tm) % seq_len == 0
    row0 = lax.broadcasted_iota(jnp.int32, (half, 1), 0) == 0
    edge = {}

    def rows(idx):
        rs = slice(idx * half, (idx + 1) * half)
        xn = _rms(x_ref[rs, :], g_ref[...]).astype(BF16)
        for c, e in IN_RANGES:
            res = _dot(xn, w_ref[:, c:e])
            yield
            if c >= OFF_ZQ:
                qt, k, kr, vt = _mla_project(res, cs_ref[rs, :], sn_ref[rs, :], cst_ref[:, rs], snt_ref[:, rs],
                                             gq_ref, gkv_ref, wqt_ref, wqrt_ref, wk_ref, wvt_ref)
                qt_ref[:, rs] = qt
                k_ref[rs, :] = k
                kr_ref[rs, :] = kr
                vt_ref[:, rs] = vt
                continue
            if c >= OFF_RW:
                cs = slice(c - OFF_RW, e - OFF_RW)
                if idx == 0:
                    last = jnp.where(first, 0.0, carry_ref[:, cs])
                    edge[c] = res[half - 1:half, :]
                else:
                    last = edge[c]
                    carry_ref[:, cs] = res[half - 1:half, :]
                prev = jnp.where(row0, last, pltpu.roll(res, 1, axis=0))
                res = res + (prev - res) * mu_ref[:, cs]
            o_ref[rs, c:e] = res.astype(o_ref.dtype)
        ot_ref[:, rs] = _dot_nt(wt_ref[...], xn).astype(ot_ref.dtype)

    live = [rows(0), rows(1)]
    while live:
        live = [g for g in live if next(g, True) is None]


def _in_proj(h, seq_len, tables, consts):
    t, d = h.shape
    nt = _pshape(consts[2])[0]
    tm = min(t, 512)
    hv = MLA_HEADS * MLA_V
    hq = MLA_HEADS * MLA_HEAD_PAD
    row = lambda w: pl.BlockSpec((tm, w), lambda i: (i, 0))
    col = lambda w: pl.BlockSpec((w, tm), lambda i: (0, i))
    return pl.pallas_call(
        functools.partial(_in_proj_kernel, seq_len),
        grid=(t // tm,),
        in_specs=[row(d), row(128), row(128), col(128), col(128)] + [_pspec(c) for c in consts],
        out_specs=[row(OFF_ZQ), col(nt), col(hq), row(hv), row(128), col(hv)],
        out_shape=[
            jax.ShapeDtypeStruct((t, OFF_ZQ), BF16),
            jax.ShapeDtypeStruct((nt, t), BF16),
            jax.ShapeDtypeStruct((hq, t), BF16),
            jax.ShapeDtypeStruct((t, hv), BF16),
            jax.ShapeDtypeStruct((t, 128), BF16),
            jax.ShapeDtypeStruct((hv, t), BF16),
        ],
        scratch_shapes=[pltpu.VMEM((1, OFF_ZQ - OFF_RW), F32)],
        compiler_params=_cparams("arbitrary"),
        name="in_proj",
    )(h, *tables, *_pargs(consts))


MLA_TQ = 512
MLA_GROUP = 4


def _mla_attn_kernel(qt_ref, k_ref, kr_ref, vt_ref, o_ref, m_ref, acc_ref):
    i = pl.program_id(1)
    tq = MLA_TQ
    m_ref[...] = jnp.full(m_ref.shape, MASKED, F32)
    acc_ref[...] = jnp.zeros(acc_ref.shape, F32)
    ones = jnp.ones((ONES_ROWS, tq), BF16)

    def run(blocks):
        def scores(j, masked, h):
            start = pl.multiple_of(j * tq, tq)
            kh = jnp.concatenate([k_ref[pl.ds(start, tq), h * 128:(h + 1) * 128],
                                  kr_ref[pl.ds(start, tq), :]], axis=1)
            s = _dot(kh, qt_ref[h * MLA_HEAD_PAD:(h + 1) * MLA_HEAD_PAD, :])
            if masked:
                kc = lax.broadcasted_iota(jnp.int32, s.shape, 0) // CHUNK
                qc = lax.broadcasted_iota(jnp.int32, s.shape, 1) // CHUNK
                s = jnp.where(kc <= qc, s, MASKED)
            return s

        def update(j, h, s):
            start = pl.multiple_of(j * tq, tq)
            m_prev = m_ref[h]
            m_new = jnp.maximum(m_prev, jnp.max(s, axis=0, keepdims=True))
            a = jnp.exp2(m_prev - m_new)
            p = jnp.exp2(s - m_new).astype(BF16)
            vh = jnp.concatenate([vt_ref[h * MLA_V:(h + 1) * MLA_V, pl.ds(start, tq)], ones], axis=0)
            acc_ref[h] = a * acc_ref[h] + _dot(vh, p)
            m_ref[h] = m_new

        items = [(j, masked, h) for j, masked in blocks for h in range(MLA_HEADS)]
        _pipelined(items, lambda it: scores(*it), lambda it, s: update(it[0], it[2], s), MLA_PIPE)

    def body(jj, c):
        run([(MLA_GROUP * jj + n, False) for n in range(MLA_GROUP)])
        return c

    lax.fori_loop(0, i // MLA_GROUP, body, 0)
    for r in range(MLA_GROUP):
        pl.when(i % MLA_GROUP == r)(
            functools.partial(run, [(i - r + n, False) for n in range(r)] + [(i, True)]))
    for h in range(MLA_HEADS):
        acc = acc_ref[h]
        o_ref[:, h * MLA_V:(h + 1) * MLA_V] = (acc[:MLA_V] / acc[MLA_V:MLA_V + 1]).T.astype(o_ref.dtype)


def _mla_attn(qt, k, kr, vt, b):
    hq, t = qt.shape
    s = t // b
    tq = MLA_TQ
    nq = s // tq
    hv = MLA_HEADS * MLA_V
    full = lambda w: pl.BlockSpec((s, w), lambda bi, i: (bi, 0))
    return pl.pallas_call(
        _mla_attn_kernel,
        grid=(b, nq),
        in_specs=[
            pl.BlockSpec((hq, tq), lambda bi, i: (0, bi * nq + i)),
            full(hv), full(128),
            pl.BlockSpec((hv, s), lambda bi, i: (0, bi)),
        ],
        out_specs=pl.BlockSpec((tq, hv), lambda bi, i: (bi * nq + i, 0)),
        out_shape=jax.ShapeDtypeStruct((t, hv), BF16),
        scratch_shapes=[pltpu.VMEM((MLA_HEADS, 1, tq), F32),
                        pltpu.VMEM((MLA_HEADS, MLA_V + ONES_ROWS, tq), F32)],
        compiler_params=_cparams("parallel", "arbitrary"),
        name="mla_attn",
    )(qt, k, kr, vt)


CA_TQ = 256
CA_BAND_BLK = CA_TQ + CA_PAD


CA_PER_STEP = 4


def _ca_attn_kernel(qt_ref, k_ref, vt_ref, bias_ref, o_ref):
    i = pl.program_id(1)
    lo = lax.broadcasted_iota(jnp.int32, (128, CA_TQ), 0) < CA_HEAD
    zero = jnp.zeros((), BF16)

    def work(blocks):
        def scores(item):
            (sub, k_start, n_keys, bias_off), h = item
            sl = slice((h // 2) * 128, (h // 2 + 1) * 128)
            k2 = k_ref[pl.ds(k_start, n_keys), sl]
            qh = jnp.where(lo if h % 2 == 0 else ~lo, qt_ref[sl, sub * CA_TQ:(sub + 1) * CA_TQ], zero)
            return _dot(k2, qh) + bias_ref[h, bias_off:bias_off + n_keys, :]

        def attend(item, s):
            (_, k_start, n_keys, _), h = item
            p = jnp.exp2(s - jnp.max(s, axis=0, keepdims=True)).astype(BF16)
            vh = jnp.concatenate([vt_ref[h * CA_HEAD:(h + 1) * CA_HEAD, pl.ds(k_start, n_keys)],
                                  jnp.ones((ONES_ROWS, n_keys), BF16)], axis=0)
            out = _dot(vh, p)
            return out[:CA_HEAD] / out[CA_HEAD:CA_HEAD + 1]

        outs = _pipelined([(blk, h) for blk in blocks for h in range(CA_HEADS)], scores, attend, CA_PIPE)
        for n, blk in enumerate(blocks):
            rows = slice(blk[0] * CA_TQ, (blk[0] + 1) * CA_TQ)
            for pair in range(CA_HEADS // 2):
                both = jnp.concatenate(outs[n * CA_HEADS + 2 * pair:n * CA_HEADS + 2 * pair + 2], axis=0)
                o_ref[rows, pair * 128:(pair + 1) * 128] = both.T.astype(o_ref.dtype)

    lead = CA_PAD // CA_TQ
    lead_steps = -(-lead // CA_PER_STEP)
    for j in range(lead_steps):
        blocks = []
        for sub in range(CA_PER_STEP):
            q = j * CA_PER_STEP + sub
            blocks.append((sub, 0, (q + 1) * CA_TQ, CA_PAD - q * CA_TQ) if q < lead
                          else (sub, q * CA_TQ - CA_PAD, CA_BAND_BLK, 0))
        pl.when(i == j)(functools.partial(work, blocks))
    pl.when(i >= lead_steps)(lambda: work(
        [(sub, pl.multiple_of((i * CA_PER_STEP + sub) * CA_TQ - CA_PAD, CA_TQ), CA_BAND_BLK, 0)
         for sub in range(CA_PER_STEP)]))


def _ca_attn(zt, z, bias, b):
    t = zt.shape[1]
    s = t // b
    tq = CA_PER_STEP * CA_TQ
    nq = s // tq
    return pl.pallas_call(
        _ca_attn_kernel,
        grid=(b, nq),
        in_specs=[
            pl.BlockSpec((CA_DIM, tq), lambda bi, i: (0, bi * nq + i)),
            pl.BlockSpec((s, CA_DIM), lambda bi, i: (bi, OFF_CAK // CA_DIM)),
            pl.BlockSpec((CA_DIM, s), lambda bi, i: (1, bi)),
            _const_spec(bias.shape),
        ],
        out_specs=pl.BlockSpec((tq, CA_DIM), lambda bi, i: (bi * nq + i, 0)),
        out_shape=jax.ShapeDtypeStruct((t, CA_DIM), BF16),
        compiler_params=_cparams("parallel", "arbitrary"),
        name="ca_attn",
    )(zt, z, zt, bias)


def _ca_bias_kernel(t_ref, o_ref):
    width = t_ref.shape[-1]
    rows = pltpu.roll(jnp.broadcast_to(t_ref[0], (CA_TQ, width)), 0, 1, stride=1, stride_axis=0)
    rc = lax.broadcasted_iota(jnp.int32, (CA_TQ, CA_BAND_BLK), 0) // CHUNK
    cc = lax.broadcasted_iota(jnp.int32, (CA_TQ, CA_BAND_BLK), 1) // CHUNK
    ok = (cc >= rc) & (cc <= rc + CA_LEFT_CHUNKS)
    o_ref[0] = jnp.where(ok, rows[:, :CA_BAND_BLK], MASKED).T


def _ca_bias_table(rel_bias):
    width = CA_TQ + CA_BAND_BLK
    m = jnp.arange(width)
    delta = jnp.where(m < CA_BAND_BLK, m, m - width)
    idx = jnp.clip(CA_PAD - delta, REL_MIN, REL_MAX) - REL_MIN
    t1 = (rel_bias.astype(F32)[idx].T * LOG2E)[:, None, :]
    return pl.pallas_call(
        _ca_bias_kernel,
        grid=(CA_HEADS,),
        in_specs=[pl.BlockSpec((1, 1, width), lambda h: (h, 0, 0))],
        out_specs=pl.BlockSpec((1, CA_BAND_BLK, CA_TQ), lambda h: (h, 0, 0)),
        out_shape=jax.ShapeDtypeStruct((CA_HEADS, CA_BAND_BLK, CA_TQ), F32),
        compiler_params=_cparams("parallel"),
        name="ca_bias",
    )(t1)


def _split_bf16(x):
    hi = x.astype(BF16)
    lo = (x - hi.astype(F32)).astype(BF16)
    return hi, lo


RW_GROUP = 256


def _rw_kernel(rows, xr_ref, xk_ref, xv_ref, xl_ref, w0_ref, wup_ref, a0_ref, aup_ref, gup_ref, kk_ref, ka_ref,
               rk_ref, lnw_ref, lnb_ref, bd_ref, bdf_ref, o_ref, s_ref):
    L = CHUNK
    W = RW_GROUP
    reps = W // RW_HEAD

    @pl.when(pl.program_id(1) == 0)
    def _():
        s_ref[...] = jnp.zeros(s_ref.shape, F32)

    bdm = bd_ref[...]

    def bd(x):
        return jnp.concatenate([x.astype(BF16)] * reps, axis=0) * bdm

    def head_sums(xs):
        out = _dot(jnp.concatenate([x.astype(BF16) for x in xs], axis=0), bdm)
        offs = [0]
        for x in xs:
            offs.append(offs[-1] + x.shape[0])
        return [out[lo:hi] for lo, hi in zip(offs[:-1], offs[1:])]

    row = lax.broadcasted_iota(jnp.int32, (L, W), 0)
    sub = lax.broadcasted_iota(jnp.int32, (L, W), 1) % RW_HEAD
    strict = sub < row
    incl = sub <= row
    eye = (sub == row).astype(F32)
    tri = (lax.broadcasted_iota(jnp.int32, (L, L), 1)
           <= lax.broadcasted_iota(jnp.int32, (L, L), 0)).astype(BF16)
    inv = 1.0 / RW_HEAD

    def lora_inputs(b):
        xl = xl_ref[b].astype(F32)
        return (jnp.tanh(xl[:, :RW_DECAY_LORA]).astype(BF16),
                xl[:, RW_DECAY_LORA:RW_DECAY_LORA + RW_AAA_LORA].astype(BF16),
                jax.nn.sigmoid(xl[:, RW_DECAY_LORA + RW_AAA_LORA:]).astype(BF16))

    def instance(b, gi, xw, xa, xg):
        sl = slice(gi * W, (gi + 1) * W)
        r = xr_ref[b, :, sl].astype(F32)
        k = xk_ref[b, :, sl].astype(F32)
        v = xv_ref[b, :, sl].astype(F32)
        lw = -math.exp(-0.5) * jax.nn.sigmoid(w0_ref[:, sl] + _dot(xw, wup_ref[:, sl]))
        a = jax.nn.sigmoid(a0_ref[:, sl] + _dot(xa, aup_ref[:, sl]))
        g = _dot(xg, gup_ref[:, sl])
        kk = k * kk_ref[:, sl]
        ss = yield kk * kk
        kk = kk * lax.rsqrt(jnp.maximum(ss, 1e-24))
        k = k * (1.0 + (a - 1.0) * ka_ref[:, sl])
        be = kk * a
        lw_hi, lw_lo = _split_bf16(lw)
        cum = _dot(tri, lw_hi) + _dot(tri, lw_lo)
        yield
        cum_l = cum[L - 1:L, :]
        e_neg = jnp.exp(-cum)
        e_tail = jnp.exp(cum_l - cum)
        ar = jnp.concatenate([-kk * jnp.exp(cum - lw), r * jnp.exp(cum)], axis=0).astype(BF16)
        s0 = s_ref[b, gi]
        ar_s = _dot_nt(ar, s0.astype(BF16))
        a_b = _dot_nt(ar, bd(be * e_neg))
        a_k = _dot_nt(ar, bd(k * e_neg))
        yield
        n = jnp.where(strict, a_b[:L], 0.0)
        a_ak = jnp.where(strict, a_k[:L], 0.0)
        a_rb = jnp.where(incl, a_b[L:], 0.0)
        a_rk = jnp.where(incl, a_k[L:], 0.0)
        p = eye + n
        nk = _dot(n.astype(BF16), bd(n))
        bd_v = bd(v)
        x0 = _dot(a_ak.astype(BF16), bd_v)
        yield
        steps = int(math.log2(L)) - 1
        for it in range(steps):
            m = bd(nk)
            if it + 1 < steps:
                res = _dot(jnp.concatenate([p, nk], axis=0).astype(BF16), m)
                p = p + res[:L]
                nk = res[L:]
            else:
                p = p + _dot(p.astype(BF16), m)
            yield
        u = _dot(p.astype(BF16), bd(ar_s[:L] + x0))
        yield
        y = ar_s[L:] + _dot(jnp.concatenate([a_rb, a_rk], axis=1).astype(BF16),
                            jnp.concatenate([bd(u), bd_v], axis=0))
        uv = jnp.concatenate([u, v], axis=0).astype(BF16)
        bk = jnp.concatenate([be * e_tail, k * e_tail], axis=0).astype(BF16)
        s_ref[b, gi] = s0 * jnp.exp(cum_l) + _dot_tn(uv, bk) * bdf_ref[...]
        yield
        sums = yield jnp.concatenate([y, r * k * rk_ref[:, sl]], axis=0)
        d = y - sums[:L] * inv
        var = (yield d * d) * inv
        yn = d * lax.rsqrt(var + RW_LN_EPS) * lnw_ref[:, sl] + lnb_ref[:, sl]
        o_ref[b, :, sl] = ((yn + sums[L:] * v) * g).astype(o_ref.dtype)

    live = []
    for b in range(rows):
        lora = lora_inputs(b)
        live += [instance(b, gi, *lora) for gi in range(RW_DIM // W)]
    asked = [next(g) for g in live]
    while live:
        answers = head_sums(asked) if asked[0] is not None else asked
        live, asked = _advance(live, answers)


def _rw_mixer(z, b, consts):
    t = z.shape[0]
    s = t // b
    rows = math.gcd(b, 8)
    gid = jnp.arange(RW_GROUP) // RW_HEAD
    ones_f32 = (gid[:, None] == gid[None, :]).astype(F32)
    ones_bd = ones_f32.astype(BF16)
    sec = lambda w, j: pl.BlockSpec((rows, CHUNK, w), lambda bb, c: (bb, c, j))
    z3 = z.reshape(b, s, -1)
    return pl.pallas_call(
        functools.partial(_rw_kernel, rows),
        grid=(b // rows, s // CHUNK),
        in_specs=[sec(RW_DIM, OFF_RW // RW_DIM), sec(RW_DIM, OFF_RW // RW_DIM + 1),
                  sec(RW_DIM, OFF_RW // RW_DIM + 2), sec(RW_LORA, OFF_LORA // RW_LORA)]
        + [_pspec(c) for c in consts] + [_const_spec(ones_bd.shape)] * 2,
        out_specs=sec(RW_DIM, 0),
        out_shape=jax.ShapeDtypeStruct((b, s, RW_DIM), BF16),
        scratch_shapes=[pltpu.VMEM((rows, RW_DIM // RW_GROUP, RW_GROUP, RW_GROUP), F32)],
        compiler_params=_cparams("parallel", "arbitrary"),
        name="rw_mixer",
    )(z3, z3, z3, z3, *_pargs(consts), ones_bd, ones_f32)


FF_SPLIT = 4


def _post_kernel(om_ref, or_ref, oc_ref, g0_ref, g1_ref, g2_ref, h_ref, p_ref, wb_ref, wo_ref, gm_ref,
                 g1n_ref, w1_ref, w2_ref, g2n_ref, wg_ref, wp_ref, o_ref):
    def rows(rs):
        merged = None
        for n, (b_ref, zg_ref) in enumerate(((om_ref, g0_ref), (or_ref, g1_ref), (oc_ref, g2_ref))):
            y = _dot(b_ref[rs, :], wb_ref[n * BRANCH_DIM:(n + 1) * BRANCH_DIM, :])
            yield
            gate = jax.nn.sigmoid(zg_ref[rs, :].astype(F32))
            merged = gate * y if merged is None else merged + gate * y
        out = _dot(merged.astype(BF16), wo_ref[...])
        yield
        h = h_ref[rs, :] + _rms(out, gm_ref[...])
        f = _rms(h, g1n_ref[...]).astype(BF16)
        cw = D_FF // FF_SPLIT
        acc = None
        for c in range(FF_SPLIT):
            a = _dot(f, w1_ref[:, c * cw:(c + 1) * cw])
            yield
            a = jnp.maximum(a, 0.0)
            part = _dot((a * a).astype(BF16), w2_ref[c * cw:(c + 1) * cw, :])
            yield
            acc = part if acc is None else acc + part
        h = h + _rms(acc, g2n_ref[...])
        gate = _dot(h.astype(BF16), wg_ref[...])
        proj = _dot(p_ref[rs, :].astype(BF16), wp_ref[...])
        yield
        o_ref[rs, :] = h + jax.nn.sigmoid(gate) * proj

    half = h_ref.shape[0] // 2
    live = [rows(slice(0, half)), rows(slice(half, 2 * half))]
    while live:
        live = [g for g in live if next(g, True) is None]


def _post(o_mla, o_rw, o_ca, z, h, p, consts):
    t = h.shape[0]
    tm = min(o_rw.shape[1], 512)
    per_seq = o_rw.shape[1] // tm
    row = lambda w: pl.BlockSpec((tm, w), lambda i: (i, 0))
    gate = lambda n: pl.BlockSpec((tm, D_MODEL), lambda i: (i, OFF_GATE // D_MODEL + n))
    p_stack, layer = p
    return pl.pallas_call(
        _post_kernel,
        grid=(t // tm,),
        in_specs=[row(BRANCH_DIM), pl.BlockSpec((None, tm, BRANCH_DIM), lambda i: (i // per_seq, i % per_seq, 0)),
                  row(BRANCH_DIM), gate(0), gate(1), gate(2), row(D_MODEL),
                  pl.BlockSpec((None, tm, D_PLE), lambda i: (layer, i, 0))] + [_pspec(c) for c in consts],
        out_specs=row(D_MODEL),
        out_shape=jax.ShapeDtypeStruct((t, D_MODEL), F32),
        compiler_params=_cparams("parallel"),
        name="post",
    )(o_mla, o_rw, o_ca, z, z, z, h, p_stack, *_pargs(consts))


def _rot_half_cols(w):
    half = w.shape[-1] // 2
    return jnp.concatenate([-w[..., half:], w[..., :half]], axis=-1)


def _pack_w_in(w):
    mla_cols = MLA_Q_RANK + MLA_KV_RANK + MLA_ROPE
    rw_cols = 3 * RW_DIM + RW_LORA
    ca_q = (w[..., mla_cols + rw_cols:mla_cols + rw_cols + CA_DIM] * (CA_HEAD ** -0.5 * LOG2E)).astype(BF16)
    w = w.astype(BF16)
    w_mla = w[..., :mla_cols]
    w_rw = w[..., mla_cols:mla_cols + rw_cols]
    w_ca = w[..., mla_cols + rw_cols:mla_cols + rw_cols + 3 * CA_DIM]
    w_gate = w[..., mla_cols + rw_cols + 3 * CA_DIM:]
    w_kr = w_mla[..., MLA_Q_RANK + MLA_KV_RANK:]
    z64 = jnp.zeros(w.shape[:-1] + (64,), w.dtype)
    packed = jnp.concatenate(
        [w_gate, w_ca[..., CA_DIM:2 * CA_DIM], w_rw, w_mla[..., :MLA_Q_RANK + MLA_KV_RANK], w_kr, z64,
         _rot_half_cols(w_kr), z64], axis=-1)
    wt = jnp.concatenate([ca_q, w_ca[..., 2 * CA_DIM:]], axis=-1)
    return packed, jnp.swapaxes(wt, 1, 2)


def _pack_w_uq(w):
    nl, r, _ = w.shape
    w = w.reshape(nl, r, MLA_HEADS, MLA_QK)
    z64 = jnp.zeros((nl, r, MLA_HEADS, 64), w.dtype)
    rot = _rot_half_cols(w[..., MLA_NOPE:])
    wq = jnp.concatenate([w, z64], axis=-1).reshape(nl, r, MLA_HEADS * MLA_HEAD_PAD)
    wqr = jnp.concatenate([rot, z64], axis=-1).reshape(nl, r, MLA_HEADS * 128)
    return jnp.swapaxes(wq, 1, 2).astype(BF16), jnp.swapaxes(wqr, 1, 2).astype(BF16)


def _pack_w_ukv(w):
    nl, r, _ = w.shape
    w = w.reshape(nl, r, MLA_HEADS, MLA_NOPE + MLA_V)
    wk = w[..., :MLA_NOPE].reshape(nl, r, MLA_HEADS * MLA_NOPE)
    wv = w[..., MLA_NOPE:].reshape(nl, r, MLA_HEADS * MLA_V)
    return wk.astype(BF16), jnp.swapaxes(wv, 1, 2).astype(BF16)


def kernel(x, p, positions, pre_mix_g, w_in, mla_q_norm_g, mla_kv_norm_g, mla_w_uq, mla_w_ukv, rw_mu, rw_w0, rw_w_up, rw_a0, rw_a_up, rw_g_up, rw_k_k, rw_k_a, rw_r_k, rw_ln_w, rw_ln_b, ca_rel_bias, w_branch, w_out, post_mix_g, pre_ff_g, w_ff1, w_ff2, post_ff_g, w_ple_gate, w_ple_proj):
    b, s, d = x.shape
    t = b * s
    depth = w_in.shape[0]
    row = lambda a: a.reshape(depth, 1, -1).astype(F32)
    bf = lambda a: a.astype(BF16)
    w_in_p, w_in_t = _pack_w_in(w_in)
    wqt, wqrt = _pack_w_uq(mla_w_uq)
    wk, wvt = _pack_w_ukv(mla_w_ukv)
    stacks = dict(
        in_proj=[row(pre_mix_g), w_in_p, w_in_t, row(rw_mu),
                 row(mla_q_norm_g), row(mla_kv_norm_g), wqt, wqrt, wk, wvt],
        rw=[row(rw_w0), bf(rw_w_up), row(rw_a0), bf(rw_a_up), bf(rw_g_up), row(rw_k_k), row(rw_k_a),
            row(rw_r_k), row(rw_ln_w), row(rw_ln_b)],
        post=[bf(w_branch), bf(w_out), row(post_mix_g),
              row(pre_ff_g), bf(w_ff1), bf(w_ff2), row(post_ff_g), bf(w_ple_gate), bf(w_ple_proj)],
    )
    p3 = p.reshape(depth, t, -1)
    rope = _rope_table(positions)
    h = x.reshape(t, d)
    for i in range(depth):
        prm = {name: [(a, i) for a in arrs] for name, arrs in stacks.items()}
        z, zt, qt, kn, kr, vt = _in_proj(h, s, rope, prm["in_proj"])
        o_mla = _mla_attn(qt, kn, kr, vt, b)
        o_rw = _rw_mixer(z, b, prm["rw"])
        o_ca = _ca_attn(zt, z, _ca_bias_table(ca_rel_bias[i]), b)
        h = _post(o_mla, o_rw, o_ca, z, h, (p3, i), prm["post"])
    return h.reshape(b, s, d)
```

```python
import functools
import math

import jax
import jax.numpy as jnp
from jax import lax
from jax.experimental import pallas as pl
from jax.experimental.pallas import tpu as pltpu

F32 = jnp.float32
BF16 = jnp.bfloat16

D_MODEL = 1024
D_PLE = 256
D_FF = 4 * D_MODEL
NORM_EPS = 1e-6
CHUNK = 64

MLA_HEADS = 4
MLA_NOPE = 128
MLA_ROPE = 64
MLA_V = 128
MLA_Q_RANK = 256
MLA_KV_RANK = 128
ROPE_THETA = 10000.0
MLA_QK = MLA_NOPE + MLA_ROPE
MLA_HEAD_PAD = 256

RW_HEADS = 8
RW_HEAD = 64
RW_DIM = RW_HEADS * RW_HEAD
RW_DECAY_LORA = 64
RW_AAA_LORA = 64
RW_GATE_LORA = 128
RW_LORA = RW_DECAY_LORA + RW_AAA_LORA + RW_GATE_LORA
RW_LN_EPS = 64e-5

CA_HEADS = 8
CA_HEAD = 64
CA_DIM = CA_HEADS * CA_HEAD
CA_LEFT_CHUNKS = 8
CA_PAD = CA_LEFT_CHUNKS * CHUNK
REL_MIN = -(CHUNK - 1)
REL_MAX = 256

N_BRANCH = 3
BRANCH_DIM = 512
GATE_COLS = N_BRANCH * D_MODEL

OFF_GATE = 0
OFF_CAK = GATE_COLS
OFF_RW = OFF_CAK + CA_DIM
OFF_LORA = OFF_RW + 3 * RW_DIM
OFF_ZQ = OFF_LORA + RW_LORA
OFF_ZKV = OFF_ZQ + MLA_Q_RANK
OFF_KR = OFF_ZKV + MLA_KV_RANK
OFF_KRR = OFF_KR + 128
IN_PACKED = OFF_KRR + 128
LOG2E = math.log2(math.e)
MASKED = -1e30

VMEM_LIMIT = 56 * 1024 * 1024


def _cparams(*sem):
    return pltpu.CompilerParams(dimension_semantics=sem, vmem_limit_bytes=VMEM_LIMIT)


def _rms(x, g):
    return x * lax.rsqrt(jnp.mean(x * x, axis=-1, keepdims=True) + NORM_EPS) * g


def _dot(a, b):
    return jnp.dot(a, b, preferred_element_type=F32)


def _dot_nt(a, b):
    return lax.dot_general(a, b, (((1,), (1,)), ((), ())), preferred_element_type=F32)


def _dot_tn(a, b):
    return lax.dot_general(a, b, (((0,), (0,)), ((), ())), preferred_element_type=F32)


def _const_spec(shape):
    nd = len(shape)
    return pl.BlockSpec(shape, lambda *_: (0,) * nd, pipeline_mode=pl.Buffered(1))


def _pshape(param):
    return param[0].shape[1:]


def _pspec(param):
    stack, layer = param
    nd = stack.ndim - 1
    return pl.BlockSpec((None,) + stack.shape[1:], lambda *_: (layer,) + (0,) * nd, pipeline_mode=pl.Buffered(1))


def _pargs(params):
    return [stack for stack, _ in params]


def _advance(gens, answers):
    live, asked = [], []
    for g, ans in zip(gens, answers):
        try:
            asked.append(g.send(ans))
            live.append(g)
        except StopIteration:
            pass
    return live, asked


MLA_PIPE = 2
CA_PIPE = 3
ONES_ROWS = 16


def _pipelined(items, produce, consume, depth):
    pending, outs = [], []
    for it in items:
        pending.append((it, produce(it)))
        if len(pending) > depth:
            outs.append(consume(*pending.pop(0)))
    outs.extend(consume(*pc) for pc in pending)
    return outs


def _rope_table_kernel(pos_ref, freq_ref, cs_ref, sn_ref, cst_ref, snt_ref):
    ang = freq_ref[...] * pos_ref[...]
    dead = jnp.zeros((128 - MLA_ROPE, ang.shape[1]), F32)
    cst = jnp.concatenate([jnp.cos(ang), dead], axis=0)
    snt = jnp.concatenate([jnp.sin(ang), dead], axis=0)
    cst_ref[...] = cst
    snt_ref[...] = snt
    cs_ref[...] = cst.T
    sn_ref[...] = snt.T


def _rope_table(positions):
    t = positions.size
    tm = min(t, 2048)
    half = MLA_ROPE // 2
    inv_freq = 1.0 / (ROPE_THETA ** (jnp.arange(half, dtype=F32) / half))
    freq = jnp.concatenate([inv_freq, inv_freq])[:, None]
    return pl.pallas_call(
        _rope_table_kernel,
        grid=(t // tm,),
        in_specs=[pl.BlockSpec((1, tm), lambda i: (0, i)), _const_spec(freq.shape)],
        out_specs=[pl.BlockSpec((tm, 128), lambda i: (i, 0))] * 2 + [pl.BlockSpec((128, tm), lambda i: (0, i))] * 2,
        out_shape=[jax.ShapeDtypeStruct((t, 128), F32)] * 2 + [jax.ShapeDtypeStruct((128, t), F32)] * 2,
        compiler_params=_cparams("parallel"),
        name="rope_table",
    )(positions.astype(F32).reshape(1, t), freq)


IN_RANGES = ((0, 1536), (1536, OFF_CAK), (OFF_CAK, OFF_RW), (OFF_RW, OFF_RW + 1024), (OFF_RW + 1024, OFF_ZQ),
             (OFF_ZQ, IN_PACKED))


def _mla_project(lat, cs, sn, cst, snt, gq_ref, gkv_ref, wqt_ref, wqrt_ref, wk_ref, wvt_ref):
    qn = _rms(lat[:, :MLA_Q_RANK], gq_ref[...]).astype(BF16)
    qt = _dot_nt(wqt_ref[...], qn)
    qrt = _dot_nt(wqrt_ref[...], qn)
    scale = MLA_QK ** -0.5 * LOG2E
    parts = []
    for h in range(MLA_HEADS):
        o = h * MLA_HEAD_PAD
        rope = qt[o + 128:o + 256] * cst + qrt[h * 128:(h + 1) * 128] * snt
        parts += [(qt[o:o + 128] * scale).astype(BF16), (rope * scale).astype(BF16)]
    kvn = _rms(lat[:, MLA_Q_RANK:MLA_Q_RANK + MLA_KV_RANK], gkv_ref[...]).astype(BF16)
    k = _dot(kvn, wk_ref[...]).astype(BF16)
    vt = _dot_nt(wvt_ref[...], kvn).astype(BF16)
    kr = (lat[:, OFF_KR - OFF_ZQ:OFF_KRR - OFF_ZQ] * cs + lat[:, OFF_KRR - OFF_ZQ:] * sn).astype(BF16)
    return jnp.concatenate(parts, axis=0), k, kr, vt


def _in_proj_kernel(seq_len, x_ref, cs_ref, sn_ref, cst_ref, snt_ref, g_ref, w_ref, wt_ref, mu_ref, gq_ref, gkv_ref,
                    wqt_ref, wqrt_ref, wk_ref, wvt_ref, o_ref, ot_ref, qt_ref, k_ref, kr_ref, vt_ref, kc_ref,
                    carry_ref):
    tm = x_ref.shape[0]
    half = tm // 2
    first = (pl.program_id(0) * tm) % seq_len == 0
    row0 = lax.broadcasted_iota(jnp.int32, (half, 1), 0) == 0
    edge = {}

    def rows(idx):
        rs = slice(idx * half, (idx + 1) * half)
        xn = _rms(x_ref[rs, :], g_ref[...]).astype(BF16)
        for c, e in IN_RANGES:
            res = _dot(xn, w_ref[:, c:e])
            yield
            if c >= OFF_ZQ:
                qt, k, kr, vt = _mla_project(res, cs_ref[rs, :], sn_ref[rs, :], cst_ref[:, rs], snt_ref[:, rs],
                                             gq_ref, gkv_ref, wqt_ref, wqrt_ref, wk_ref, wvt_ref)
                qt_ref[:, rs] = qt
                k_ref[rs, :] = k
                kr_ref[rs, :] = kr
                vt_ref[:, rs] = vt
                continue
            if c >= OFF_RW:
                cs = slice(c - OFF_RW, e - OFF_RW)
                if idx == 0:
                    last = jnp.where(first, 0.0, carry_ref[:, cs])
                    edge[c] = res[half - 1:half, :]
                else:
                    last = edge[c]
                    carry_ref[:, cs] = res[half - 1:half, :]
                prev = jnp.where(row0, last, pltpu.roll(res, 1, axis=0))
                res = res + (prev - res) * mu_ref[:, cs]
            o_ref[rs, c:e] = res.astype(o_ref.dtype)
            if c == OFF_CAK:
                kc_ref[rs, :] = res.astype(kc_ref.dtype)
        ot_ref[:, rs] = _dot_nt(wt_ref[...], xn).astype(ot_ref.dtype)

    live = [rows(0), rows(1)]
    while live:
        live = [g for g in live if next(g, True) is None]


def _in_proj(h, seq_len, tables, consts):
    t, d = h.shape
    nt = _pshape(consts[2])[0]
    tm = min(t, 512)
    hv = MLA_HEADS * MLA_V
    hq = MLA_HEADS * MLA_HEAD_PAD
    row = lambda w: pl.BlockSpec((tm, w), lambda i: (i, 0))
    col = lambda w: pl.BlockSpec((w, tm), lambda i: (0, i))
    return pl.pallas_call(
        functools.partial(_in_proj_kernel, seq_len),
        grid=(t // tm,),
        in_specs=[row(d), row(128), row(128), col(128), col(128)] + [_pspec(c) for c in consts],
        out_specs=[row(OFF_ZQ), col(nt), col(hq), row(hv), row(128), col(hv), row(CA_DIM)],
        out_shape=[
            jax.ShapeDtypeStruct((t, OFF_ZQ), BF16),
            jax.ShapeDtypeStruct((nt, t), BF16),
            jax.ShapeDtypeStruct((hq, t), BF16),
            jax.ShapeDtypeStruct((t, hv), BF16),
            jax.ShapeDtypeStruct((t, 128), BF16),
            jax.ShapeDtypeStruct((hv, t), BF16),
            jax.ShapeDtypeStruct((t, CA_DIM), BF16),
        ],
        scratch_shapes=[pltpu.VMEM((1, OFF_ZQ - OFF_RW), F32)],
        compiler_params=_cparams("arbitrary"),
        name="in_proj",
    )(h, *tables, *_pargs(consts))


MLA_TQ = 512
MLA_GROUP = 4


def _mla_attn_kernel(qt_ref, k_ref, kr_ref, vt_ref, o_ref, m_ref, acc_ref):
    i = pl.program_id(1)
    tq = MLA_TQ
    m_ref[...] = jnp.full(m_ref.shape, MASKED, F32)
    acc_ref[...] = jnp.zeros(acc_ref.shape, F32)
    ones = jnp.ones((ONES_ROWS, tq), BF16)

    def run(blocks):
        def scores(j, masked, h):
            start = pl.multiple_of(j * tq, tq)
            kh = jnp.concatenate([k_ref[pl.ds(start, tq), h * 128:(h + 1) * 128],
                                  kr_ref[pl.ds(start, tq), :]], axis=1)
            s = _dot(kh, qt_ref[h * MLA_HEAD_PAD:(h + 1) * MLA_HEAD_PAD, :])
            if masked:
                kc = lax.broadcasted_iota(jnp.int32, s.shape, 0) // CHUNK
                qc = lax.broadcasted_iota(jnp.int32, s.shape, 1) // CHUNK
                s = jnp.where(kc <= qc, s, MASKED)
            return s

        def update(j, h, s):
            start = pl.multiple_of(j * tq, tq)
            m_prev = m_ref[h]
            m_new = jnp.maximum(m_prev, jnp.max(s, axis=0, keepdims=True))
            a = jnp.exp2(m_prev - m_new)
            p = jnp.exp2(s - m_new).astype(BF16)
            vh = jnp.concatenate([vt_ref[h * MLA_V:(h + 1) * MLA_V, pl.ds(start, tq)], ones], axis=0)
            acc_ref[h] = a * acc_ref[h] + _dot(vh, p)
            m_ref[h] = m_new

        items = [(j, masked, h) for j, masked in blocks for h in range(MLA_HEADS)]
        _pipelined(items, lambda it: scores(*it), lambda it, s: update(it[0], it[2], s), MLA_PIPE)

    def body(jj, c):
        run([(MLA_GROUP * jj + n, False) for n in range(MLA_GROUP)])
        return c

    lax.fori_loop(0, i // MLA_GROUP, body, 0)
    for r in range(MLA_GROUP):
        pl.when(i % MLA_GROUP == r)(
            functools.partial(run, [(i - r + n, False) for n in range(r)] + [(i, True)]))
    for h in range(MLA_HEADS):
        acc = acc_ref[h]
        o_ref[:, h * MLA_V:(h + 1) * MLA_V] = (acc[:MLA_V] / acc[MLA_V:MLA_V + 1]).T.astype(o_ref.dtype)


def _mla_attn(qt, k, kr, vt, b):
    hq, t = qt.shape
    s = t // b
    tq = MLA_TQ
    nq = s // tq
    hv = MLA_HEADS * MLA_V
    full = lambda w: pl.BlockSpec((s, w), lambda bi, i: (bi, 0))
    return pl.pallas_call(
        _mla_attn_kernel,
        grid=(b, nq),
        in_specs=[
            pl.BlockSpec((hq, tq), lambda bi, i: (0, bi * nq + i)),
            full(hv), full(128),
            pl.BlockSpec((hv, s), lambda bi, i: (0, bi)),
        ],
        out_specs=pl.BlockSpec((tq, hv), lambda bi, i: (bi * nq + i, 0)),
        out_shape=jax.ShapeDtypeStruct((t, hv), BF16),
        scratch_shapes=[pltpu.VMEM((MLA_HEADS, 1, tq), F32),
                        pltpu.VMEM((MLA_HEADS, MLA_V + ONES_ROWS, tq), F32)],
        compiler_params=_cparams("parallel", "arbitrary"),
        name="mla_attn",
    )(qt, k, kr, vt)


CA_TQ = 256
CA_BAND_BLK = CA_TQ + CA_PAD


CA_PER_STEP = 4


def _ca_attn_kernel(qt_ref, k_ref, vt_ref, bias_ref, o_ref):
    i = pl.program_id(1)
    lo = lax.broadcasted_iota(jnp.int32, (128, CA_TQ), 0) < CA_HEAD
    zero = jnp.zeros((), BF16)

    def work(blocks):
        def scores(item):
            (sub, k_start, n_keys, bias_off), h = item
            sl = slice((h // 2) * 128, (h // 2 + 1) * 128)
            k2 = k_ref[pl.ds(k_start, n_keys), sl]
            qh = jnp.where(lo if h % 2 == 0 else ~lo, qt_ref[sl, sub * CA_TQ:(sub + 1) * CA_TQ], zero)
            return _dot(k2, qh) + bias_ref[h, bias_off:bias_off + n_keys, :]

        def attend(item, s):
            (_, k_start, n_keys, _), h = item
            p = jnp.exp2(s - jnp.max(s, axis=0, keepdims=True)).astype(BF16)
            vh = jnp.concatenate([vt_ref[h * CA_HEAD:(h + 1) * CA_HEAD, pl.ds(k_start, n_keys)],
                                  jnp.ones((ONES_ROWS, n_keys), BF16)], axis=0)
            out = _dot(vh, p)
            return out[:CA_HEAD] / out[CA_HEAD:CA_HEAD + 1]

        outs = _pipelined([(blk, h) for blk in blocks for h in range(CA_HEADS)], scores, attend, CA_PIPE)
        for n, blk in enumerate(blocks):
            rows = slice(blk[0] * CA_TQ, (blk[0] + 1) * CA_TQ)
            for pair in range(CA_HEADS // 2):
                both = jnp.concatenate(outs[n * CA_HEADS + 2 * pair:n * CA_HEADS + 2 * pair + 2], axis=0)
                o_ref[rows, pair * 128:(pair + 1) * 128] = both.T.astype(o_ref.dtype)

    lead = CA_PAD // CA_TQ
    lead_steps = -(-lead // CA_PER_STEP)
    for j in range(lead_steps):
        blocks = []
        for sub in range(CA_PER_STEP):
            q = j * CA_PER_STEP + sub
            blocks.append((sub, 0, (q + 1) * CA_TQ, CA_PAD - q * CA_TQ) if q < lead
                          else (sub, q * CA_TQ - CA_PAD, CA_BAND_BLK, 0))
        pl.when(i == j)(functools.partial(work, blocks))
    pl.when(i >= lead_steps)(lambda: work(
        [(sub, pl.multiple_of((i * CA_PER_STEP + sub) * CA_TQ - CA_PAD, CA_TQ), CA_BAND_BLK, 0)
         for sub in range(CA_PER_STEP)]))


def _ca_attn(zt, kc, bias, b):
    t = zt.shape[1]
    s = t // b
    tq = CA_PER_STEP * CA_TQ
    nq = s // tq
    return pl.pallas_call(
        _ca_attn_kernel,
        grid=(b, nq),
        in_specs=[
            pl.BlockSpec((CA_DIM, tq), lambda bi, i: (0, bi * nq + i)),
            pl.BlockSpec((s, CA_DIM), lambda bi, i: (bi, 0)),
            pl.BlockSpec((CA_DIM, s), lambda bi, i: (1, bi)),
            _const_spec(bias.shape),
        ],
        out_specs=pl.BlockSpec((tq, CA_DIM), lambda bi, i: (bi * nq + i, 0)),
        out_shape=jax.ShapeDtypeStruct((t, CA_DIM), BF16),
        compiler_params=_cparams("parallel", "arbitrary"),
        name="ca_attn",
    )(zt, kc, zt, bias)


def _ca_bias_kernel(t_ref, o_ref):
    width = t_ref.shape[-1]
    rows = pltpu.roll(jnp.broadcast_to(t_ref[0], (CA_TQ, width)), 0, 1, stride=1, stride_axis=0)
    rc = lax.broadcasted_iota(jnp.int32, (CA_TQ, CA_BAND_BLK), 0) // CHUNK
    cc = lax.broadcasted_iota(jnp.int32, (CA_TQ, CA_BAND_BLK), 1) // CHUNK
    ok = (cc >= rc) & (cc <= rc + CA_LEFT_CHUNKS)
    o_ref[0] = jnp.where(ok, rows[:, :CA_BAND_BLK], MASKED).T


def _ca_bias_table(rel_bias):
    width = CA_TQ + CA_BAND_BLK
    m = jnp.arange(width)
    delta = jnp.where(m < CA_BAND_BLK, m, m - width)
    idx = jnp.clip(CA_PAD - delta, REL_MIN, REL_MAX) - REL_MIN
    t1 = (rel_bias.astype(F32)[idx].T * LOG2E)[:, None, :]
    return pl.pallas_call(
        _ca_bias_kernel,
        grid=(CA_HEADS,),
        in_specs=[pl.BlockSpec((1, 1, width), lambda h: (h, 0, 0))],
        out_specs=pl.BlockSpec((1, CA_BAND_BLK, CA_TQ), lambda h: (h, 0, 0)),
        out_shape=jax.ShapeDtypeStruct((CA_HEADS, CA_BAND_BLK, CA_TQ), F32),
        compiler_params=_cparams("parallel"),
        name="ca_bias",
    )(t1)


def _split_bf16(x):
    hi = x.astype(BF16)
    lo = (x - hi.astype(F32)).astype(BF16)
    return hi, lo


RW_GROUP = 256


def _rw_kernel(rows, xr_ref, xk_ref, xv_ref, xl_ref, w0_ref, wup_ref, a0_ref, aup_ref, gup_ref, kk_ref, ka_ref,
               rk_ref, lnw_ref, lnb_ref, bd_ref, bdf_ref, o_ref, s_ref):
    L = CHUNK
    W = RW_GROUP
    reps = W // RW_HEAD

    @pl.when(pl.program_id(1) == 0)
    def _():
        s_ref[...] = jnp.zeros(s_ref.shape, F32)

    bdm = bd_ref[...]

    def bd(x):
        return jnp.concatenate([x.astype(BF16)] * reps, axis=0) * bdm

    def head_sums(xs):
        out = _dot(jnp.concatenate([x.astype(BF16) for x in xs], axis=0), bdm)
        offs = [0]
        for x in xs:
            offs.append(offs[-1] + x.shape[0])
        return [out[lo:hi] for lo, hi in zip(offs[:-1], offs[1:])]

    row = lax.broadcasted_iota(jnp.int32, (L, W), 0)
    sub = lax.broadcasted_iota(jnp.int32, (L, W), 1) % RW_HEAD
    strict = sub < row
    incl = sub <= row
    eye = (sub == row).astype(F32)
    tri = (lax.broadcasted_iota(jnp.int32, (L, L), 1)
           <= lax.broadcasted_iota(jnp.int32, (L, L), 0)).astype(BF16)
    inv = 1.0 / RW_HEAD

    def lora_inputs(b):
        xl = xl_ref[b].astype(F32)
        return (jnp.tanh(xl[:, :RW_DECAY_LORA]).astype(BF16),
                xl[:, RW_DECAY_LORA:RW_DECAY_LORA + RW_AAA_LORA].astype(BF16),
                jax.nn.sigmoid(xl[:, RW_DECAY_LORA + RW_AAA_LORA:]).astype(BF16))

    def instance(b, gi, xw, xa, xg):
        sl = slice(gi * W, (gi + 1) * W)
        r = xr_ref[b, :, sl].astype(F32)
        k = xk_ref[b, :, sl].astype(F32)
        v = xv_ref[b, :, sl].astype(F32)
        lw = -math.exp(-0.5) * jax.nn.sigmoid(w0_ref[:, sl] + _dot(xw, wup_ref[:, sl]))
        a = jax.nn.sigmoid(a0_ref[:, sl] + _dot(xa, aup_ref[:, sl]))
        g = _dot(xg, gup_ref[:, sl])
        kk = k * kk_ref[:, sl]
        ss = yield kk * kk
        kk = kk * lax.rsqrt(jnp.maximum(ss, 1e-24))
        k = k * (1.0 + (a - 1.0) * ka_ref[:, sl])
        be = kk * a
        lw_hi, lw_lo = _split_bf16(lw)
        cum = _dot(tri, lw_hi) + _dot(tri, lw_lo)
        yield
        cum_l = cum[L - 1:L, :]
        e_neg = jnp.exp(-cum)
        e_tail = jnp.exp(cum_l - cum)
        ar = jnp.concatenate([-kk * jnp.exp(cum - lw), r * jnp.exp(cum)], axis=0).astype(BF16)
        s0 = s_ref[b, gi]
        ar_s = _dot_nt(ar, s0.astype(BF16))
        a_b = _dot_nt(ar, bd(be * e_neg))
        a_k = _dot_nt(ar, bd(k * e_neg))
        yield
        n = jnp.where(strict, a_b[:L], 0.0)
        a_ak = jnp.where(strict, a_k[:L], 0.0)
        a_rb = jnp.where(incl, a_b[L:], 0.0)
        a_rk = jnp.where(incl, a_k[L:], 0.0)
        p = eye + n
        nk = _dot(n.astype(BF16), bd(n))
        bd_v = bd(v)
        x0 = _dot(a_ak.astype(BF16), bd_v)
        yield
        steps = int(math.log2(L)) - 1
        for it in range(steps):
            m = bd(nk)
            if it + 1 < steps:
                res = _dot(jnp.concatenate([p, nk], axis=0).astype(BF16), m)
                p = p + res[:L]
                nk = res[L:]
            else:
                p = p + _dot(p.astype(BF16), m)
            yield
        u = _dot(p.astype(BF16), bd(ar_s[:L] + x0))
        yield
        y = ar_s[L:] + _dot(jnp.concatenate([a_rb, a_rk], axis=1).astype(BF16),
                            jnp.concatenate([bd(u), bd_v], axis=0))
        uv = jnp.concatenate([u, v], axis=0).astype(BF16)
        bk = jnp.concatenate([be * e_tail, k * e_tail], axis=0).astype(BF16)
        s_ref[b, gi] = s0 * jnp.exp(cum_l) + _dot_tn(uv, bk) * bdf_ref[...]
        yield
        sums = yield jnp.concatenate([y, r * k * rk_ref[:, sl]], axis=0)
        d = y - sums[:L] * inv
        var = (yield d * d) * inv
        yn = d * lax.rsqrt(var + RW_LN_EPS) * lnw_ref[:, sl] + lnb_ref[:, sl]
        o_ref[b, :, sl] = ((yn + sums[L:] * v) * g).astype(o_ref.dtype)

    live = []
    for b in range(rows):
        lora = lora_inputs(b)
        live += [instance(b, gi, *lora) for gi in range(RW_DIM // W)]
    asked = [next(g) for g in live]
    while live:
        answers = head_sums(asked) if asked[0] is not None else asked
        live, asked = _advance(live, answers)


def _rw_mixer(z, b, consts):
    t = z.shape[0]
    s = t // b
    rows = math.gcd(b, 8)
    gid = jnp.arange(RW_GROUP) // RW_HEAD
    ones_f32 = (gid[:, None] == gid[None, :]).astype(F32)
    ones_bd = ones_f32.astype(BF16)
    sec = lambda w, j: pl.BlockSpec((rows, CHUNK, w), lambda bb, c: (bb, c, j))
    z3 = z.reshape(b, s, -1)
    return pl.pallas_call(
        functools.partial(_rw_kernel, rows),
        grid=(b // rows, s // CHUNK),
        in_specs=[sec(RW_DIM, OFF_RW // RW_DIM), sec(RW_DIM, OFF_RW // RW_DIM + 1),
                  sec(RW_DIM, OFF_RW // RW_DIM + 2), sec(RW_LORA, OFF_LORA // RW_LORA)]
        + [_pspec(c) for c in consts] + [_const_spec(ones_bd.shape)] * 2,
        out_specs=sec(RW_DIM, 0),
        out_shape=jax.ShapeDtypeStruct((b, s, RW_DIM), BF16),
        scratch_shapes=[pltpu.VMEM((rows, RW_DIM // RW_GROUP, RW_GROUP, RW_GROUP), F32)],
        compiler_params=_cparams("parallel", "arbitrary"),
        name="rw_mixer",
    )(z3, z3, z3, z3, *_pargs(consts), ones_bd, ones_f32)


FF_SPLIT = 4


def _post_kernel(om_ref, or_ref, oc_ref, g0_ref, g1_ref, g2_ref, h_ref, p_ref, wb_ref, wo_ref, gm_ref,
                 g1n_ref, w1_ref, w2_ref, g2n_ref, wg_ref, wp_ref, o_ref):
    def rows(rs):
        merged = None
        for n, (b_ref, zg_ref) in enumerate(((om_ref, g0_ref), (or_ref, g1_ref), (oc_ref, g2_ref))):
            y = _dot(b_ref[rs, :], wb_ref[n * BRANCH_DIM:(n + 1) * BRANCH_DIM, :])
            yield
            gate = jax.nn.sigmoid(zg_ref[rs, :].astype(F32))
            merged = gate * y if merged is None else merged + gate * y
        out = _dot(merged.astype(BF16), wo_ref[...])
        yield
        h = h_ref[rs, :] + _rms(out, gm_ref[...])
        f = _rms(h, g1n_ref[...]).astype(BF16)
        cw = D_FF // FF_SPLIT
        acc = None
        for c in range(FF_SPLIT):
            a = _dot(f, w1_ref[:, c * cw:(c + 1) * cw])
            yield
            a = jnp.maximum(a, 0.0)
            part = _dot((a * a).astype(BF16), w2_ref[c * cw:(c + 1) * cw, :])
            yield
            acc = part if acc is None else acc + part
        h = h + _rms(acc, g2n_ref[...])
        gate = _dot(h.astype(BF16), wg_ref[...])
        proj = _dot(p_ref[rs, :].astype(BF16), wp_ref[...])
        yield
        o_ref[rs, :] = h + jax.nn.sigmoid(gate) * proj

    half = h_ref.shape[0] // 2
    live = [rows(slice(0, half)), rows(slice(half, 2 * half))]
    while live:
        live = [g for g in live if next(g, True) is None]


def _post(o_mla, o_rw, o_ca, z, h, p, consts):
    t = h.shape[0]
    tm = min(o_rw.shape[1], 512)
    per_seq = o_rw.shape[1] // tm
    row = lambda w: pl.BlockSpec((tm, w), lambda i: (i, 0))
    gate = lambda n: pl.BlockSpec((tm, D_MODEL), lambda i: (i, OFF_GATE // D_MODEL + n))
    p_stack, layer = p
    return pl.pallas_call(
        _post_kernel,
        grid=(t // tm,),
        in_specs=[row(BRANCH_DIM), pl.BlockSpec((None, tm, BRANCH_DIM), lambda i: (i // per_seq, i % per_seq, 0)),
                  row(BRANCH_DIM), gate(0), gate(1), gate(2), row(D_MODEL),
                  pl.BlockSpec((None, tm, D_PLE), lambda i: (layer, i, 0))] + [_pspec(c) for c in consts],
        out_specs=row(D_MODEL),
        out_shape=jax.ShapeDtypeStruct((t, D_MODEL), F32),
        compiler_params=_cparams("parallel"),
        name="post",
    )(o_mla, o_rw, o_ca, z, z, z, h, p_stack, *_pargs(consts))


def _rot_half_cols(w):
    half = w.shape[-1] // 2
    return jnp.concatenate([-w[..., half:], w[..., :half]], axis=-1)


def _pack_w_in(w):
    mla_cols = MLA_Q_RANK + MLA_KV_RANK + MLA_ROPE
    rw_cols = 3 * RW_DIM + RW_LORA
    ca_q = (w[..., mla_cols + rw_cols:mla_cols + rw_cols + CA_DIM] * (CA_HEAD ** -0.5 * LOG2E)).astype(BF16)
    w = w.astype(BF16)
    w_mla = w[..., :mla_cols]
    w_rw = w[..., mla_cols:mla_cols + rw_cols]
    w_ca = w[..., mla_cols + rw_cols:mla_cols + rw_cols + 3 * CA_DIM]
    w_gate = w[..., mla_cols + rw_cols + 3 * CA_DIM:]
    w_kr = w_mla[..., MLA_Q_RANK + MLA_KV_RANK:]
    z64 = jnp.zeros(w.shape[:-1] + (64,), w.dtype)
    packed = jnp.concatenate(
        [w_gate, w_ca[..., CA_DIM:2 * CA_DIM], w_rw, w_mla[..., :MLA_Q_RANK + MLA_KV_RANK], w_kr, z64,
         _rot_half_cols(w_kr), z64], axis=-1)
    wt = jnp.concatenate([ca_q, w_ca[..., 2 * CA_DIM:]], axis=-1)
    return packed, jnp.swapaxes(wt, 1, 2)


def _pack_w_uq(w):
    nl, r, _ = w.shape
    w = w.reshape(nl, r, MLA_HEADS, MLA_QK)
    z64 = jnp.zeros((nl, r, MLA_HEADS, 64), w.dtype)
    rot = _rot_half_cols(w[..., MLA_NOPE:])
    wq = jnp.concatenate([w, z64], axis=-1).reshape(nl, r, MLA_HEADS * MLA_HEAD_PAD)
    wqr = jnp.concatenate([rot, z64], axis=-1).reshape(nl, r, MLA_HEADS * 128)
    return jnp.swapaxes(wq, 1, 2).astype(BF16), jnp.swapaxes(wqr, 1, 2).astype(BF16)


def _pack_w_ukv(w):
    nl, r, _ = w.shape
    w = w.reshape(nl, r, MLA_HEADS, MLA_NOPE + MLA_V)
    wk = w[..., :MLA_NOPE].reshape(nl, r, MLA_HEADS * MLA_NOPE)
    wv = w[..., MLA_NOPE:].reshape(nl, r, MLA_HEADS * MLA_V)
    return wk.astype(BF16), jnp.swapaxes(wv, 1, 2).astype(BF16)


def kernel(x, p, positions, pre_mix_g, w_in, mla_q_norm_g, mla_kv_norm_g, mla_w_uq, mla_w_ukv, rw_mu, rw_w0, rw_w_up, rw_a0, rw_a_up, rw_g_up, rw_k_k, rw_k_a, rw_r_k, rw_ln_w, rw_ln_b, ca_rel_bias, w_branch, w_out, post_mix_g, pre_ff_g, w_ff1, w_ff2, post_ff_g, w_ple_gate, w_ple_proj):
    b, s, d = x.shape
    t = b * s
    depth = w_in.shape[0]
    row = lambda a: a.reshape(depth, 1, -1).astype(F32)
    bf = lambda a: a.astype(BF16)
    w_in_p, w_in_t = _pack_w_in(w_in)
    wqt, wqrt = _pack_w_uq(mla_w_uq)
    wk, wvt = _pack_w_ukv(mla_w_ukv)
    stacks = dict(
        in_proj=[row(pre_mix_g), w_in_p, w_in_t, row(rw_mu),
                 row(mla_q_norm_g), row(mla_kv_norm_g), wqt, wqrt, wk, wvt],
        rw=[row(rw_w0), bf(rw_w_up), row(rw_a0), bf(rw_a_up), bf(rw_g_up), row(rw_k_k), row(rw_k_a),
            row(rw_r_k), row(rw_ln_w), row(rw_ln_b)],
        post=[bf(w_branch), bf(w_out), row(post_mix_g),
              row(pre_ff_g), bf(w_ff1), bf(w_ff2), row(post_ff_g), bf(w_ple_gate), bf(w_ple_proj)],
    )
    p3 = p.reshape(depth, t, -1)
    rope = _rope_table(positions)
    h = x.reshape(t, d)
    for i in range(depth):
        prm = {name: [(a, i) for a in arrs] for name, arrs in stacks.items()}
        z, zt, qt, kn, kr, vt, kc = _in_proj(h, s, rope, prm["in_proj"])
        o_mla = _mla_attn(qt, kn, kr, vt, b)
        o_rw = _rw_mixer(z, b, prm["rw"])
        o_ca = _ca_attn(zt, kc, _ca_bias_table(ca_rel_bias[i]), b)
        h = _post(o_mla, o_rw, o_ca, z, h, (p3, i), prm["post"])
    return h.reshape(b, s, d)
```

```python
import functools
import math

import jax
import jax.numpy as jnp
from jax import lax
from jax.experimental import pallas as pl
from jax.experimental.pallas import tpu as pltpu

F32 = jnp.float32
BF16 = jnp.bfloat16

D_MODEL = 1024
D_PLE = 256
D_FF = 4 * D_MODEL
NORM_EPS = 1e-6
CHUNK = 64

MLA_HEADS = 4
MLA_NOPE = 128
MLA_ROPE = 64
MLA_V = 128
MLA_Q_RANK = 256
MLA_KV_RANK = 128
ROPE_THETA = 10000.0
MLA_QK = MLA_NOPE + MLA_ROPE
MLA_HEAD_PAD = 256

RW_HEADS = 8
RW_HEAD = 64
RW_DIM = RW_HEADS * RW_HEAD
RW_DECAY_LORA = 64
RW_AAA_LORA = 64
RW_GATE_LORA = 128
RW_LORA = RW_DECAY_LORA + RW_AAA_LORA + RW_GATE_LORA
RW_LN_EPS = 64e-5

CA_HEADS = 8
CA_HEAD = 64
CA_DIM = CA_HEADS * CA_HEAD
CA_LEFT_CHUNKS = 8
CA_PAD = CA_LEFT_CHUNKS * CHUNK
REL_MIN = -(CHUNK - 1)
REL_MAX = 256

N_BRANCH = 3
BRANCH_DIM = 512
GATE_COLS = N_BRANCH * D_MODEL

OFF_GATE = 0
OFF_CAK = GATE_COLS
OFF_RW = OFF_CAK + CA_DIM
OFF_LORA = OFF_RW + 3 * RW_DIM
OFF_ZQ = OFF_LORA + RW_LORA
OFF_ZKV = OFF_ZQ + MLA_Q_RANK
OFF_KR = OFF_ZKV + MLA_KV_RANK
OFF_KRR = OFF_KR + 128
IN_PACKED = OFF_KRR + 128
LOG2E = math.log2(math.e)
MASKED = -1e30

VMEM_LIMIT = 56 * 1024 * 1024


def _cparams(*sem):
    return pltpu.CompilerParams(dimension_semantics=sem, vmem_limit_bytes=VMEM_LIMIT)


def _rms(x, g):
    return x * lax.rsqrt(jnp.mean(x * x, axis=-1, keepdims=True) + NORM_EPS) * g


def _dot(a, b):
    return jnp.dot(a, b, preferred_element_type=F32)


def _dot_nt(a, b):
    return lax.dot_general(a, b, (((1,), (1,)), ((), ())), preferred_element_type=F32)


def _dot_tn(a, b):
    return lax.dot_general(a, b, (((0,), (0,)), ((), ())), preferred_element_type=F32)


def _const_spec(shape):
    nd = len(shape)
    return pl.BlockSpec(shape, lambda *_: (0,) * nd, pipeline_mode=pl.Buffered(1))


def _pshape(param):
    return param[0].shape[1:]


def _pspec(param):
    stack, layer = param
    nd = stack.ndim - 1
    return pl.BlockSpec((None,) + stack.shape[1:], lambda *_: (layer,) + (0,) * nd, pipeline_mode=pl.Buffered(1))


def _pargs(params):
    return [stack for stack, _ in params]


def _advance(gens, answers):
    live, asked = [], []
    for g, ans in zip(gens, answers):
        try:
            asked.append(g.send(ans))
            live.append(g)
        except StopIteration:
            pass
    return live, asked


MLA_PIPE = 2
CA_PIPE = 3
ONES_ROWS = 16


def _pipelined(items, produce, consume, depth):
    pending, outs = [], []
    for it in items:
        pending.append((it, produce(it)))
        if len(pending) > depth:
            outs.append(consume(*pending.pop(0)))
    outs.extend(consume(*pc) for pc in pending)
    return outs


def _rope_table_kernel(pos_ref, freq_ref, cs_ref, sn_ref, cst_ref, snt_ref):
    ang = freq_ref[...] * pos_ref[...]
    dead = jnp.zeros((128 - MLA_ROPE, ang.shape[1]), F32)
    cst = jnp.concatenate([jnp.cos(ang), dead], axis=0)
    snt = jnp.concatenate([jnp.sin(ang), dead], axis=0)
    cst_ref[...] = cst
    snt_ref[...] = snt
    cs_ref[...] = cst.T
    sn_ref[...] = snt.T


def _rope_table(positions):
    t = positions.size
    tm = min(t, 2048)
    half = MLA_ROPE // 2
    inv_freq = 1.0 / (ROPE_THETA ** (jnp.arange(half, dtype=F32) / half))
    freq = jnp.concatenate([inv_freq, inv_freq])[:, None]
    return pl.pallas_call(
        _rope_table_kernel,
        grid=(t // tm,),
        in_specs=[pl.BlockSpec((1, tm), lambda i: (0, i)), _const_spec(freq.shape)],
        out_specs=[pl.BlockSpec((tm, 128), lambda i: (i, 0))] * 2 + [pl.BlockSpec((128, tm), lambda i: (0, i))] * 2,
        out_shape=[jax.ShapeDtypeStruct((t, 128), F32)] * 2 + [jax.ShapeDtypeStruct((128, t), F32)] * 2,
        compiler_params=_cparams("parallel"),
        name="rope_table",
    )(positions.astype(F32).reshape(1, t), freq)


IN_RANGES = ((0, 1536), (1536, OFF_CAK), (OFF_CAK, OFF_RW), (OFF_RW, OFF_RW + 1024), (OFF_RW + 1024, OFF_ZQ),
             (OFF_ZQ, IN_PACKED))


def _mla_project(lat, cs, sn, cst, snt, gq_ref, gkv_ref, wqt_ref, wqrt_ref, wk_ref, wvt_ref):
    qn = _rms(lat[:, :MLA_Q_RANK], gq_ref[...]).astype(BF16)
    qt = _dot_nt(wqt_ref[...], qn)
    qrt = _dot_nt(wqrt_ref[...], qn)
    scale = MLA_QK ** -0.5 * LOG2E
    parts = []
    for h in range(MLA_HEADS):
        o = h * MLA_HEAD_PAD
        rope = qt[o + 128:o + 256] * cst + qrt[h * 128:(h + 1) * 128] * snt
        parts += [(qt[o:o + 128] * scale).astype(BF16), (rope * scale).astype(BF16)]
    kvn = _rms(lat[:, MLA_Q_RANK:MLA_Q_RANK + MLA_KV_RANK], gkv_ref[...]).astype(BF16)
    k = _dot(kvn, wk_ref[...]).astype(BF16)
    vt = _dot_nt(wvt_ref[...], kvn).astype(BF16)
    kr = (lat[:, OFF_KR - OFF_ZQ:OFF_KRR - OFF_ZQ] * cs + lat[:, OFF_KRR - OFF_ZQ:] * sn).astype(BF16)
    return jnp.concatenate(parts, axis=0), k, kr, vt


def _in_proj_kernel(seq_len, x_ref, cs_ref, sn_ref, cst_ref, snt_ref, g_ref, w_ref, wt_ref, mu_ref, gq_ref, gkv_ref,
                    wqt_ref, wqrt_ref, wk_ref, wvt_ref, o_ref, ot_ref, qt_ref, k_ref, kr_ref, vt_ref, kc_ref,
                    carry_ref):
    tm = x_ref.shape[0]
    half = tm // 2
    first = (pl.program_id(0) * tm) % seq_len == 0
    row0 = lax.broadcasted_iota(jnp.int32, (half, 1), 0) == 0
    edge = {}

    def rows(idx):
        rs = slice(idx * half, (idx + 1) * half)
        xn = _rms(x_ref[rs, :], g_ref[...]).astype(BF16)
        for c, e in IN_RANGES:
            res = _dot(xn, w_ref[:, c:e])
            yield
            if c >= OFF_ZQ:
                qt, k, kr, vt = _mla_project(res, cs_ref[rs, :], sn_ref[rs, :], cst_ref[:, rs], snt_ref[:, rs],
                                             gq_ref, gkv_ref, wqt_ref, wqrt_ref, wk_ref, wvt_ref)
                qt_ref[:, rs] = qt
                k_ref[rs, :] = k
                kr_ref[rs, :] = kr
                vt_ref[:, rs] = vt
                continue
            if c >= OFF_RW:
                cs = slice(c - OFF_RW, e - OFF_RW)
                if idx == 0:
                    last = jnp.where(first, 0.0, carry_ref[:, cs])
                    edge[c] = res[half - 1:half, :]
                else:
                    last = edge[c]
                    carry_ref[:, cs] = res[half - 1:half, :]
                prev = jnp.where(row0, last, pltpu.roll(res, 1, axis=0))
                res = res + (prev - res) * mu_ref[:, cs]
            o_ref[rs, c:e] = res.astype(o_ref.dtype)
            if c == OFF_CAK:
                kc_ref[rs, :] = res.astype(kc_ref.dtype)
        ot_ref[:, rs] = _dot_nt(wt_ref[...], xn).astype(ot_ref.dtype)

    live = [rows(0), rows(1)]
    while live:
        live = [g for g in live if next(g, True) is None]


def _in_proj(h, seq_len, tables, consts):
    t, d = h.shape
    nt = _pshape(consts[2])[0]
    tm = min(t, 512)
    hv = MLA_HEADS * MLA_V
    hq = MLA_HEADS * MLA_HEAD_PAD
    row = lambda w: pl.BlockSpec((tm, w), lambda i: (i, 0))
    col = lambda w: pl.BlockSpec((w, tm), lambda i: (0, i))
    return pl.pallas_call(
        functools.partial(_in_proj_kernel, seq_len),
        grid=(t // tm,),
        in_specs=[row(d), row(128), row(128), col(128), col(128)] + [_pspec(c) for c in consts],
        out_specs=[row(OFF_ZQ), col(nt), pl.BlockSpec((None, hq, tm), lambda i: (i, 0, 0)), row(hv), row(128),
                   col(hv), row(CA_DIM)],
        out_shape=[
            jax.ShapeDtypeStruct((t, OFF_ZQ), BF16),
            jax.ShapeDtypeStruct((nt, t), BF16),
            jax.ShapeDtypeStruct((t // tm, hq, tm), BF16),
            jax.ShapeDtypeStruct((t, hv), BF16),
            jax.ShapeDtypeStruct((t, 128), BF16),
            jax.ShapeDtypeStruct((hv, t), BF16),
            jax.ShapeDtypeStruct((t, CA_DIM), BF16),
        ],
        scratch_shapes=[pltpu.VMEM((1, OFF_ZQ - OFF_RW), F32)],
        compiler_params=_cparams("arbitrary"),
        name="in_proj",
    )(h, *tables, *_pargs(consts))


MLA_TQ = 512
MLA_GROUP = 4


def _mla_attn_kernel(qt_ref, k_ref, kr_ref, vt_ref, o_ref, m_ref, acc_ref):
    i = pl.program_id(1)
    tq = MLA_TQ
    m_ref[...] = jnp.full(m_ref.shape, MASKED, F32)
    acc_ref[...] = jnp.zeros(acc_ref.shape, F32)
    ones = jnp.ones((ONES_ROWS, tq), BF16)

    def run(blocks):
        def scores(j, masked, h):
            start = pl.multiple_of(j * tq, tq)
            kh = jnp.concatenate([k_ref[pl.ds(start, tq), h * 128:(h + 1) * 128],
                                  kr_ref[pl.ds(start, tq), :]], axis=1)
            s = _dot(kh, qt_ref[h * MLA_HEAD_PAD:(h + 1) * MLA_HEAD_PAD, :])
            if masked:
                kc = lax.broadcasted_iota(jnp.int32, s.shape, 0) // CHUNK
                qc = lax.broadcasted_iota(jnp.int32, s.shape, 1) // CHUNK
                s = jnp.where(kc <= qc, s, MASKED)
            return s

        def update(j, h, s):
            start = pl.multiple_of(j * tq, tq)
            m_prev = m_ref[h]
            m_new = jnp.maximum(m_prev, jnp.max(s, axis=0, keepdims=True))
            a = jnp.exp2(m_prev - m_new)
            p = jnp.exp2(s - m_new).astype(BF16)
            vh = jnp.concatenate([vt_ref[h * MLA_V:(h + 1) * MLA_V, pl.ds(start, tq)], ones], axis=0)
            acc_ref[h] = a * acc_ref[h] + _dot(vh, p)
            m_ref[h] = m_new

        items = [(j, masked, h) for j, masked in blocks for h in range(MLA_HEADS)]
        _pipelined(items, lambda it: scores(*it), lambda it, s: update(it[0], it[2], s), MLA_PIPE)

    def body(jj, c):
        run([(MLA_GROUP * jj + n, False) for n in range(MLA_GROUP)])
        return c

    lax.fori_loop(0, i // MLA_GROUP, body, 0)
    for r in range(MLA_GROUP):
        pl.when(i % MLA_GROUP == r)(
            functools.partial(run, [(i - r + n, False) for n in range(r)] + [(i, True)]))
    for h in range(MLA_HEADS):
        acc = acc_ref[h]
        o_ref[:, h * MLA_V:(h + 1) * MLA_V] = (acc[:MLA_V] / acc[MLA_V:MLA_V + 1]).T.astype(o_ref.dtype)


def _mla_attn(qt, k, kr, vt, b):
    tiles, hq, tq = qt.shape
    assert tq == MLA_TQ
    t = tiles * tq
    s = t // b
    nq = s // tq
    hv = MLA_HEADS * MLA_V
    full = lambda w: pl.BlockSpec((s, w), lambda bi, i: (bi, 0))
    return pl.pallas_call(
        _mla_attn_kernel,
        grid=(b, nq),
        in_specs=[
            pl.BlockSpec((None, hq, tq), lambda bi, i: (bi * nq + i, 0, 0)),
            full(hv), full(128),
            pl.BlockSpec((hv, s), lambda bi, i: (0, bi)),
        ],
        out_specs=pl.BlockSpec((tq, hv), lambda bi, i: (bi * nq + i, 0)),
        out_shape=jax.ShapeDtypeStruct((t, hv), BF16),
        scratch_shapes=[pltpu.VMEM((MLA_HEADS, 1, tq), F32),
                        pltpu.VMEM((MLA_HEADS, MLA_V + ONES_ROWS, tq), F32)],
        compiler_params=_cparams("parallel", "arbitrary"),
        name="mla_attn",
    )(qt, k, kr, vt)


CA_TQ = 256
CA_BAND_BLK = CA_TQ + CA_PAD


CA_PER_STEP = 4


def _ca_attn_kernel(qt_ref, k_ref, vt_ref, bias_ref, o_ref):
    i = pl.program_id(1)
    lo = lax.broadcasted_iota(jnp.int32, (128, CA_TQ), 0) < CA_HEAD
    zero = jnp.zeros((), BF16)

    def work(blocks):
        def scores(item):
            (sub, k_start, n_keys, bias_off), h = item
            sl = slice((h // 2) * 128, (h // 2 + 1) * 128)
            k2 = k_ref[pl.ds(k_start, n_keys), sl]
            qh = jnp.where(lo if h % 2 == 0 else ~lo, qt_ref[sl, sub * CA_TQ:(sub + 1) * CA_TQ], zero)
            return _dot(k2, qh) + bias_ref[h, bias_off:bias_off + n_keys, :]

        def attend(item, s):
            (_, k_start, n_keys, _), h = item
            p = jnp.exp2(s - jnp.max(s, axis=0, keepdims=True)).astype(BF16)
            vh = jnp.concatenate([vt_ref[h * CA_HEAD:(h + 1) * CA_HEAD, pl.ds(k_start, n_keys)],
                                  jnp.ones((ONES_ROWS, n_keys), BF16)], axis=0)
            out = _dot(vh, p)
            return out[:CA_HEAD] / out[CA_HEAD:CA_HEAD + 1]

        outs = _pipelined([(blk, h) for blk in blocks for h in range(CA_HEADS)], scores, attend, CA_PIPE)
        for n, blk in enumerate(blocks):
            rows = slice(blk[0] * CA_TQ, (blk[0] + 1) * CA_TQ)
            for pair in range(CA_HEADS // 2):
                both = jnp.concatenate(outs[n * CA_HEADS + 2 * pair:n * CA_HEADS + 2 * pair + 2], axis=0)
                o_ref[rows, pair * 128:(pair + 1) * 128] = both.T.astype(o_ref.dtype)

    lead = CA_PAD // CA_TQ
    lead_steps = -(-lead // CA_PER_STEP)
    for j in range(lead_steps):
        blocks = []
        for sub in range(CA_PER_STEP):
            q = j * CA_PER_STEP + sub
            blocks.append((sub, 0, (q + 1) * CA_TQ, CA_PAD - q * CA_TQ) if q < lead
                          else (sub, q * CA_TQ - CA_PAD, CA_BAND_BLK, 0))
        pl.when(i == j)(functools.partial(work, blocks))
    pl.when(i >= lead_steps)(lambda: work(
        [(sub, pl.multiple_of((i * CA_PER_STEP + sub) * CA_TQ - CA_PAD, CA_TQ), CA_BAND_BLK, 0)
         for sub in range(CA_PER_STEP)]))


def _ca_attn(zt, kc, bias, b):
    t = zt.shape[1]
    s = t // b
    tq = CA_PER_STEP * CA_TQ
    nq = s // tq
    return pl.pallas_call(
        _ca_attn_kernel,
        grid=(b, nq),
        in_specs=[
            pl.BlockSpec((CA_DIM, tq), lambda bi, i: (0, bi * nq + i)),
            pl.BlockSpec((s, CA_DIM), lambda bi, i: (bi, 0)),
            pl.BlockSpec((CA_DIM, s), lambda bi, i: (1, bi)),
            _const_spec(bias.shape),
        ],
        out_specs=pl.BlockSpec((tq, CA_DIM), lambda bi, i: (bi * nq + i, 0)),
        out_shape=jax.ShapeDtypeStruct((t, CA_DIM), BF16),
        compiler_params=_cparams("parallel", "arbitrary"),
        name="ca_attn",
    )(zt, kc, zt, bias)


def _ca_bias_kernel(t_ref, o_ref):
    width = t_ref.shape[-1]
    rows = pltpu.roll(jnp.broadcast_to(t_ref[0], (CA_TQ, width)), 0, 1, stride=1, stride_axis=0)
    rc = lax.broadcasted_iota(jnp.int32, (CA_TQ, CA_BAND_BLK), 0) // CHUNK
    cc = lax.broadcasted_iota(jnp.int32, (CA_TQ, CA_BAND_BLK), 1) // CHUNK
    ok = (cc >= rc) & (cc <= rc + CA_LEFT_CHUNKS)
    o_ref[0] = jnp.where(ok, rows[:, :CA_BAND_BLK], MASKED).T


def _ca_bias_table(rel_bias):
    width = CA_TQ + CA_BAND_BLK
    m = jnp.arange(width)
    delta = jnp.where(m < CA_BAND_BLK, m, m - width)
    idx = jnp.clip(CA_PAD - delta, REL_MIN, REL_MAX) - REL_MIN
    t1 = (rel_bias.astype(F32)[idx].T * LOG2E)[:, None, :]
    return pl.pallas_call(
        _ca_bias_kernel,
        grid=(CA_HEADS,),
        in_specs=[pl.BlockSpec((1, 1, width), lambda h: (h, 0, 0))],
        out_specs=pl.BlockSpec((1, CA_BAND_BLK, CA_TQ), lambda h: (h, 0, 0)),
        out_shape=jax.ShapeDtypeStruct((CA_HEADS, CA_BAND_BLK, CA_TQ), F32),
        compiler_params=_cparams("parallel"),
        name="ca_bias",
    )(t1)


def _split_bf16(x):
    hi = x.astype(BF16)
    lo = (x - hi.astype(F32)).astype(BF16)
    return hi, lo


RW_GROUP = 256


def _rw_kernel(rows, xr_ref, xk_ref, xv_ref, xl_ref, w0_ref, wup_ref, a0_ref, aup_ref, gup_ref, kk_ref, ka_ref,
               rk_ref, lnw_ref, lnb_ref, bd_ref, bdf_ref, o_ref, s_ref):
    L = CHUNK
    W = RW_GROUP
    reps = W // RW_HEAD

    @pl.when(pl.program_id(1) == 0)
    def _():
        s_ref[...] = jnp.zeros(s_ref.shape, F32)

    bdm = bd_ref[...]

    def bd(x):
        return jnp.concatenate([x.astype(BF16)] * reps, axis=0) * bdm

    def head_sums(xs):
        out = _dot(jnp.concatenate([x.astype(BF16) for x in xs], axis=0), bdm)
        offs = [0]
        for x in xs:
            offs.append(offs[-1] + x.shape[0])
        return [out[lo:hi] for lo, hi in zip(offs[:-1], offs[1:])]

    row = lax.broadcasted_iota(jnp.int32, (L, W), 0)
    sub = lax.broadcasted_iota(jnp.int32, (L, W), 1) % RW_HEAD
    strict = sub < row
    incl = sub <= row
    eye = (sub == row).astype(F32)
    tri = (lax.broadcasted_iota(jnp.int32, (L, L), 1)
           <= lax.broadcasted_iota(jnp.int32, (L, L), 0)).astype(BF16)
    inv = 1.0 / RW_HEAD

    def lora_inputs(b):
        xl = xl_ref[b].astype(F32)
        return (jnp.tanh(xl[:, :RW_DECAY_LORA]).astype(BF16),
                xl[:, RW_DECAY_LORA:RW_DECAY_LORA + RW_AAA_LORA].astype(BF16),
                jax.nn.sigmoid(xl[:, RW_DECAY_LORA + RW_AAA_LORA:]).astype(BF16))

    def instance(b, gi, xw, xa, xg):
        sl = slice(gi * W, (gi + 1) * W)
        r = xr_ref[b, :, sl].astype(F32)
        k = xk_ref[b, :, sl].astype(F32)
        v = xv_ref[b, :, sl].astype(F32)
        lw = -math.exp(-0.5) * jax.nn.sigmoid(w0_ref[:, sl] + _dot(xw, wup_ref[:, sl]))
        a = jax.nn.sigmoid(a0_ref[:, sl] + _dot(xa, aup_ref[:, sl]))
        g = _dot(xg, gup_ref[:, sl])
        kk = k * kk_ref[:, sl]
        ss = yield kk * kk
        kk = kk * lax.rsqrt(jnp.maximum(ss, 1e-24))
        k = k * (1.0 + (a - 1.0) * ka_ref[:, sl])
        be = kk * a
        lw_hi, lw_lo = _split_bf16(lw)
        cum = _dot(tri, lw_hi) + _dot(tri, lw_lo)
        yield
        cum_l = cum[L - 1:L, :]
        e_neg = jnp.exp(-cum)
        e_tail = jnp.exp(cum_l - cum)
        ar = jnp.concatenate([-kk * jnp.exp(cum - lw), r * jnp.exp(cum)], axis=0).astype(BF16)
        s0 = s_ref[b, gi]
        ar_s = _dot_nt(ar, s0.astype(BF16))
        a_b = _dot_nt(ar, bd(be * e_neg))
        a_k = _dot_nt(ar, bd(k * e_neg))
        yield
        n = jnp.where(strict, a_b[:L], 0.0)
        a_ak = jnp.where(strict, a_k[:L], 0.0)
        a_rb = jnp.where(incl, a_b[L:], 0.0)
        a_rk = jnp.where(incl, a_k[L:], 0.0)
        p = eye + n
        nk = _dot(n.astype(BF16), bd(n))
        bd_v = bd(v)
        x0 = _dot(a_ak.astype(BF16), bd_v)
        yield
        steps = int(math.log2(L)) - 1
        for it in range(steps):
            m = bd(nk)
            if it + 1 < steps:
                res = _dot(jnp.concatenate([p, nk], axis=0).astype(BF16), m)
                p = p + res[:L]
                nk = res[L:]
            else:
                p = p + _dot(p.astype(BF16), m)
            yield
        u = _dot(p.astype(BF16), bd(ar_s[:L] + x0))
        yield
        y = ar_s[L:] + _dot(jnp.concatenate([a_rb, a_rk], axis=1).astype(BF16),
                            jnp.concatenate([bd(u), bd_v], axis=0))
        uv = jnp.concatenate([u, v], axis=0).astype(BF16)
        bk = jnp.concatenate([be * e_tail, k * e_tail], axis=0).astype(BF16)
        s_ref[b, gi] = s0 * jnp.exp(cum_l) + _dot_tn(uv, bk) * bdf_ref[...]
        yield
        sums = yield jnp.concatenate([y, r * k * rk_ref[:, sl]], axis=0)
        d = y - sums[:L] * inv
        var = (yield d * d) * inv
        yn = d * lax.rsqrt(var + RW_LN_EPS) * lnw_ref[:, sl] + lnb_ref[:, sl]
        o_ref[b, :, sl] = ((yn + sums[L:] * v) * g).astype(o_ref.dtype)

    live = []
    for b in range(rows):
        lora = lora_inputs(b)
        live += [instance(b, gi, *lora) for gi in range(RW_DIM // W)]
    asked = [next(g) for g in live]
    while live:
        answers = head_sums(asked) if asked[0] is not None else asked
        live, asked = _advance(live, answers)


def _rw_mixer(z, b, consts):
    t = z.shape[0]
    s = t // b
    rows = math.gcd(b, 8)
    gid = jnp.arange(RW_GROUP) // RW_HEAD
    ones_f32 = (gid[:, None] == gid[None, :]).astype(F32)
    ones_bd = ones_f32.astype(BF16)
    sec = lambda w, j: pl.BlockSpec((rows, CHUNK, w), lambda bb, c: (bb, c, j))
    z3 = z.reshape(b, s, -1)
    return pl.pallas_call(
        functools.partial(_rw_kernel, rows),
        grid=(b // rows, s // CHUNK),
        in_specs=[sec(RW_DIM, OFF_RW // RW_DIM), sec(RW_DIM, OFF_RW // RW_DIM + 1),
                  sec(RW_DIM, OFF_RW // RW_DIM + 2), sec(RW_LORA, OFF_LORA // RW_LORA)]
        + [_pspec(c) for c in consts] + [_const_spec(ones_bd.shape)] * 2,
        out_specs=sec(RW_DIM, 0),
        out_shape=jax.ShapeDtypeStruct((b, s, RW_DIM), BF16),
        scratch_shapes=[pltpu.VMEM((rows, RW_DIM // RW_GROUP, RW_GROUP, RW_GROUP), F32)],
        compiler_params=_cparams("parallel", "arbitrary"),
        name="rw_mixer",
    )(z3, z3, z3, z3, *_pargs(consts), ones_bd, ones_f32)


FF_SPLIT = 4


def _post_kernel(om_ref, or_ref, oc_ref, g0_ref, g1_ref, g2_ref, h_ref, p_ref, wb_ref, wo_ref, gm_ref,
                 g1n_ref, w1_ref, w2_ref, g2n_ref, wg_ref, wp_ref, o_ref):
    def rows(rs):
        merged = None
        for n, (b_ref, zg_ref) in enumerate(((om_ref, g0_ref), (or_ref, g1_ref), (oc_ref, g2_ref))):
            y = _dot(b_ref[rs, :], wb_ref[n * BRANCH_DIM:(n + 1) * BRANCH_DIM, :])
            yield
            gate = jax.nn.sigmoid(zg_ref[rs, :].astype(F32))
            merged = gate * y if merged is None else merged + gate * y
        out = _dot(merged.astype(BF16), wo_ref[...])
        yield
        h = h_ref[rs, :] + _rms(out, gm_ref[...])
        f = _rms(h, g1n_ref[...]).astype(BF16)
        cw = D_FF // FF_SPLIT
        acc = None
        for c in range(FF_SPLIT):
            a = _dot(f, w1_ref[:, c * cw:(c + 1) * cw])
            yield
            a = jnp.maximum(a, 0.0)
            part = _dot((a * a).astype(BF16), w2_ref[c * cw:(c + 1) * cw, :])
            yield
            acc = part if acc is None else acc + part
        h = h + _rms(acc, g2n_ref[...])
        gate = _dot(h.astype(BF16), wg_ref[...])
        proj = _dot(p_ref[rs, :].astype(BF16), wp_ref[...])
        yield
        o_ref[rs, :] = h + jax.nn.sigmoid(gate) * proj

    half = h_ref.shape[0] // 2
    live = [rows(slice(0, half)), rows(slice(half, 2 * half))]
    while live:
        live = [g for g in live if next(g, True) is None]


def _post(o_mla, o_rw, o_ca, z, h, p, consts):
    t = h.shape[0]
    tm = min(o_rw.shape[1], 512)
    per_seq = o_rw.shape[1] // tm
    row = lambda w: pl.BlockSpec((tm, w), lambda i: (i, 0))
    gate = lambda n: pl.BlockSpec((tm, D_MODEL), lambda i: (i, OFF_GATE // D_MODEL + n))
    p_stack, layer = p
    return pl.pallas_call(
        _post_kernel,
        grid=(t // tm,),
        in_specs=[row(BRANCH_DIM), pl.BlockSpec((None, tm, BRANCH_DIM), lambda i: (i // per_seq, i % per_seq, 0)),
                  row(BRANCH_DIM), gate(0), gate(1), gate(2), row(D_MODEL),
                  pl.BlockSpec((None, tm, D_PLE), lambda i: (layer, i, 0))] + [_pspec(c) for c in consts],
        out_specs=row(D_MODEL),
        out_shape=jax.ShapeDtypeStruct((t, D_MODEL), F32),
        compiler_params=_cparams("parallel"),
        name="post",
    )(o_mla, o_rw, o_ca, z, z, z, h, p_stack, *_pargs(consts))


def _rot_half_cols(w):
    half = w.shape[-1] // 2
    return jnp.concatenate([-w[..., half:], w[..., :half]], axis=-1)


def _pack_w_in(w):
    mla_cols = MLA_Q_RANK + MLA_KV_RANK + MLA_ROPE
    rw_cols = 3 * RW_DIM + RW_LORA
    ca_q = (w[..., mla_cols + rw_cols:mla_cols + rw_cols + CA_DIM] * (CA_HEAD ** -0.5 * LOG2E)).astype(BF16)
    w = w.astype(BF16)
    w_mla = w[..., :mla_cols]
    w_rw = w[..., mla_cols:mla_cols + rw_cols]
    w_ca = w[..., mla_cols + rw_cols:mla_cols + rw_cols + 3 * CA_DIM]
    w_gate = w[..., mla_cols + rw_cols + 3 * CA_DIM:]
    w_kr = w_mla[..., MLA_Q_RANK + MLA_KV_RANK:]
    z64 = jnp.zeros(w.shape[:-1] + (64,), w.dtype)
    packed = jnp.concatenate(
        [w_gate, w_ca[..., CA_DIM:2 * CA_DIM], w_rw, w_mla[..., :MLA_Q_RANK + MLA_KV_RANK], w_kr, z64,
         _rot_half_cols(w_kr), z64], axis=-1)
    wt = jnp.concatenate([ca_q, w_ca[..., 2 * CA_DIM:]], axis=-1)
    return packed, jnp.swapaxes(wt, 1, 2)


def _pack_w_uq(w):
    nl, r, _ = w.shape
    w = w.reshape(nl, r, MLA_HEADS, MLA_QK)
    z64 = jnp.zeros((nl, r, MLA_HEADS, 64), w.dtype)
    rot = _rot_half_cols(w[..., MLA_NOPE:])
    wq = jnp.concatenate([w, z64], axis=-1).reshape(nl, r, MLA_HEADS * MLA_HEAD_PAD)
    wqr = jnp.concatenate([rot, z64], axis=-1).reshape(nl, r, MLA_HEADS * 128)
    return jnp.swapaxes(wq, 1, 2).astype(BF16), jnp.swapaxes(wqr, 1, 2).astype(BF16)


def _pack_w_ukv(w):
    nl, r, _ = w.shape
    w = w.reshape(nl, r, MLA_HEADS, MLA_NOPE + MLA_V)
    wk = w[..., :MLA_NOPE].reshape(nl, r, MLA_HEADS * MLA_NOPE)
    wv = w[..., MLA_NOPE:].reshape(nl, r, MLA_HEADS * MLA_V)
    return wk.astype(BF16), jnp.swapaxes(wv, 1, 2).astype(BF16)


def kernel(x, p, positions, pre_mix_g, w_in, mla_q_norm_g, mla_kv_norm_g, mla_w_uq, mla_w_ukv, rw_mu, rw_w0, rw_w_up, rw_a0, rw_a_up, rw_g_up, rw_k_k, rw_k_a, rw_r_k, rw_ln_w, rw_ln_b, ca_rel_bias, w_branch, w_out, post_mix_g, pre_ff_g, w_ff1, w_ff2, post_ff_g, w_ple_gate, w_ple_proj):
    b, s, d = x.shape
    t = b * s
    depth = w_in.shape[0]
    row = lambda a: a.reshape(depth, 1, -1).astype(F32)
    bf = lambda a: a.astype(BF16)
    w_in_p, w_in_t = _pack_w_in(w_in)
    wqt, wqrt = _pack_w_uq(mla_w_uq)
    wk, wvt = _pack_w_ukv(mla_w_ukv)
    stacks = dict(
        in_proj=[row(pre_mix_g), w_in_p, w_in_t, row(rw_mu),
                 row(mla_q_norm_g), row(mla_kv_norm_g), wqt, wqrt, wk, wvt],
        rw=[row(rw_w0), bf(rw_w_up), row(rw_a0), bf(rw_a_up), bf(rw_g_up), row(rw_k_k), row(rw_k_a),
            row(rw_r_k), row(rw_ln_w), row(rw_ln_b)],
        post=[bf(w_branch), bf(w_out), row(post_mix_g),
              row(pre_ff_g), bf(w_ff1), bf(w_ff2), row(post_ff_g), bf(w_ple_gate), bf(w_ple_proj)],
    )
    p3 = p.reshape(depth, t, -1)
    rope = _rope_table(positions)
    h = x.reshape(t, d)
    for i in range(depth):
        prm = {name: [(a, i) for a in arrs] for name, arrs in stacks.items()}
        z, zt, qt, kn, kr, vt, kc = _in_proj(h, s, rope, prm["in_proj"])
        o_mla = _mla_attn(qt, kn, kr, vt, b)
        o_rw = _rw_mixer(z, b, prm["rw"])
        o_ca = _ca_attn(zt, kc, _ca_bias_table(ca_rel_bias[i]), b)
        h = _post(o_mla, o_rw, o_ca, z, h, (p3, i), prm["post"])
    return h.reshape(b, s, d)
```
